```python
import math
import jax
import jax.numpy as jnp
from jax import lax
import numpy as np

D_MODEL = 1024
BATCH = 4
SEQ = 4096
DEPTH = 4

GRID_W = 64
CTX_LEN = 256
N_MIXERS = 2
NORM_EPS = 1e-6
N_MOD = 6

SSD_D_INNER = 2 * D_MODEL
SSD_HEAD_DIM = 64
SSD_N_HEADS = SSD_D_INNER // SSD_HEAD_DIM
SSD_N_GROUPS = 4
SSD_D_STATE = 128
SSD_CONV = 5
SSD_CHUNK = 128
SSD_CONV_DIM = SSD_D_INNER + 2 * SSD_N_GROUPS * SSD_D_STATE
SSD_IN_DIM = SSD_D_INNER + SSD_CONV_DIM + 2 * SSD_N_HEADS

ATT_HEAD_DIM = 64
ATT_Q_HEADS = D_MODEL // ATT_HEAD_DIM
ATT_KV_HEADS = 4
ATT_GROUP = ATT_Q_HEADS // ATT_KV_HEADS
ATT_Q_BLOCK = 128
ATT_SCALE = ATT_HEAD_DIM ** -0.5
ROPE_THETA = 10000.0
ATT_QKV_DIM = (ATT_Q_HEADS + 2 * ATT_KV_HEADS) * ATT_HEAD_DIM

MOE_GROUPS = 4
MOE_EXPERTS_PER_GROUP = 8
MOE_EXPERTS = MOE_GROUPS * MOE_EXPERTS_PER_GROUP
MOE_TOP_K = 2
MOE_D_FF = D_MODEL // 2
MOE_ROW_BLOCK = 256

N_SSD_LAYERS = (DEPTH + 1) // 2
N_ATT_LAYERS = DEPTH // 2

kernel_name = "hybrid_ssd_gqa_hmoe_dit"


def rms_norm(x, g):
    xf = x.astype(jnp.float32)
    y = xf * lax.rsqrt(jnp.mean(xf * xf, axis=-1, keepdims=True) + NORM_EPS)
    return (y * g.astype(jnp.float32)).astype(x.dtype)


def modulate(x, g, shift, scale):
    return rms_norm(x, g) * (1 + scale) + shift


def axial_rope_tables(n_tokens):
    rows = n_tokens // GRID_W
    row = jnp.repeat(jnp.arange(rows), GRID_W).astype(jnp.float32)
    col = (jnp.arange(n_tokens) % GRID_W).astype(jnp.float32)
    half = ATT_HEAD_DIM // 2
    freqs = ROPE_THETA ** (-jnp.arange(0, half, 2, dtype=jnp.float32) / half)
    ang = jnp.concatenate([row[:, None] * freqs, col[:, None] * freqs], axis=-1)
    return jnp.cos(ang), jnp.sin(ang)


def apply_rope(x, cos, sin):
    xf = x.astype(jnp.float32).reshape(*x.shape[:-1], ATT_HEAD_DIM // 2, 2)
    x0, x1 = xf[..., 0], xf[..., 1]
    c = cos[None, :, None, :]
    s = sin[None, :, None, :]
    out = jnp.stack([x0 * c - x1 * s, x0 * s + x1 * c], axis=-1)
    return out.reshape(x.shape).astype(x.dtype)


def gqa_attend(q, k, v):
    b, lq = q.shape[:2]
    qg = q.reshape(b, lq, ATT_KV_HEADS, ATT_GROUP, ATT_HEAD_DIM)
    s = jnp.einsum("bqkgd,bskd->bkgqs", qg, k, preferred_element_type=jnp.float32) * ATT_SCALE
    p = jax.nn.softmax(s, axis=-1).astype(v.dtype)
    o = jnp.einsum("bkgqs,bskd->bqkgd", p, v)
    return o.reshape(b, lq, ATT_Q_HEADS, ATT_HEAD_DIM)


def gqa_mixer(h_lat, h_ctx, w_qkv, q_gain, k_gain, w_o, ctx_out):
    b, n_lat = h_lat.shape[:2]

    def project(h):
        qkv = h @ w_qkv
        q, k, v = jnp.split(qkv, [ATT_Q_HEADS * ATT_HEAD_DIM, (ATT_Q_HEADS + ATT_KV_HEADS) * ATT_HEAD_DIM], axis=-1)
        lead = h.shape[:2]
        q = rms_norm(q.reshape(*lead, ATT_Q_HEADS, ATT_HEAD_DIM), q_gain)
        k = rms_norm(k.reshape(*lead, ATT_KV_HEADS, ATT_HEAD_DIM), k_gain)
        v = v.reshape(*lead, ATT_KV_HEADS, ATT_HEAD_DIM)
        return q, k, v

    q_l, k_l, v_l = project(h_lat)
    q_c, k_c, v_c = project(h_ctx)
    cos, sin = axial_rope_tables(n_lat)
    q_l = apply_rope(q_l, cos, sin)
    k_l = apply_rope(k_l, cos, sin)
    k_all = jnp.concatenate([k_l, k_c], axis=1)
    v_all = jnp.concatenate([v_l, v_c], axis=1)
    n_blk = n_lat // ATT_Q_BLOCK
    q_blocks = jnp.moveaxis(q_l.reshape(b, n_blk, ATT_Q_BLOCK, ATT_Q_HEADS, ATT_HEAD_DIM), 1, 0)
    o_blocks = lax.map(lambda qb: gqa_attend(qb, k_all, v_all), q_blocks)
    o_l = jnp.moveaxis(o_blocks, 0, 1).reshape(b, n_lat, ATT_Q_HEADS * ATT_HEAD_DIM)
    out_l = o_l @ w_o
    out_c = None
    if ctx_out:
        o_c = gqa_attend(q_c, k_c, v_c)
        out_c = o_c.reshape(*h_ctx.shape[:2], ATT_Q_HEADS * ATT_HEAD_DIM) @ w_o
    return out_l, out_c


def depthwise_conv_centered(u, w, bias):
    ch = u.shape[-1]
    out = lax.conv_general_dilated(
        u, w[:, None, :].astype(u.dtype), window_strides=(1,),
        padding=[(SSD_CONV // 2, SSD_CONV // 2)],
        dimension_numbers=("NWC", "WIO", "NWC"), feature_group_count=ch)
    return out + bias


def ssd_chunked(x, dt, a, bm, cm, h0):
    f32 = jnp.float32
    bsz, n, nh, p = x.shape
    g, ns = bm.shape[-2:]
    r = nh // g
    q = SSD_CHUNK
    nc = n // q
    x = x.astype(f32).reshape(bsz, nc, q, g, r, p)
    dt = dt.astype(f32).reshape(bsz, nc, q, g, r)
    bm = bm.astype(f32).reshape(bsz, nc, q, g, ns)
    cm = cm.astype(f32).reshape(bsz, nc, q, g, ns)
    a_cum = jnp.cumsum(dt * a.astype(f32).reshape(g, r), axis=2)
    seg = a_cum[:, :, :, None] - a_cum[:, :, None, :]
    mask = jnp.tril(jnp.ones((q, q), dtype=bool))[:, :, None, None]
    decay = jnp.exp(jnp.where(mask, seg, -jnp.inf))
    cb = jnp.einsum("bcign,bcjgn->bcijg", cm, bm)
    mix = cb[..., None] * decay * dt[:, :, None]
    y_diag = jnp.einsum("bcijgr,bcjgrp->bcigrp", mix, x)
    w_end = jnp.exp(a_cum[:, :, -1:] - a_cum) * dt
    states = jnp.einsum("bcjgn,bcjgrp->bcgrpn", bm, x * w_end[..., None])
    chunk_decay = jnp.exp(a_cum[:, :, -1])

    def step(h, inp):
        s_c, d_c = inp
        return h * d_c[..., None, None] + s_c, h

    h_last, h_starts = lax.scan(step, h0.astype(f32).reshape(bsz, g, r, p, ns),
                                (jnp.moveaxis(states, 1, 0), jnp.moveaxis(chunk_decay, 1, 0)))
    h_starts = jnp.moveaxis(h_starts, 0, 1)
    y_off = jnp.einsum("bcign,bcgrpn->bcigrp", cm, h_starts) * jnp.exp(a_cum)[..., None]
    y = (y_diag + y_off).reshape(bsz, n, nh, p)
    return y, h_last.reshape(bsz, nh, p, ns)


def ssd_mixer(h_lat, h_ctx, w_in, conv_w, conv_b, dt_bias, a_log, d_skip, norm_g, w_out, ctx_out):
    gn = SSD_N_GROUPS * SSD_D_STATE

    def project(h):
        bb, ll = h.shape[:2]
        zxbcdt = h @ w_in
        z, xbc, dt = jnp.split(zxbcdt, [SSD_D_INNER, SSD_D_INNER + SSD_CONV_DIM], axis=-1)
        xbc = jax.nn.silu(depthwise_conv_centered(xbc, conv_w, conv_b))
        xs, bm, cm = jnp.split(xbc, [SSD_D_INNER, SSD_D_INNER + gn], axis=-1)
        xs = xs.reshape(bb, ll, SSD_N_HEADS, SSD_HEAD_DIM)
        bm = bm.reshape(bb, ll, SSD_N_GROUPS, SSD_D_STATE)
        cm = cm.reshape(bb, ll, SSD_N_GROUPS, SSD_D_STATE)
        dt = jax.nn.softplus(dt.astype(jnp.float32).reshape(bb, ll, 2, SSD_N_HEADS)
                             + dt_bias.astype(jnp.float32))
        return z, xs, bm, cm, dt

    z_l, x_l, b_l, c_l, dt_l = project(h_lat)
    z_c, x_c, b_c, c_c, dt_c = project(h_ctx)
    a = -jnp.exp(a_log.astype(jnp.float32))
    h0 = jnp.zeros((h_lat.shape[0], SSD_N_HEADS, SSD_HEAD_DIM, SSD_D_STATE), jnp.float32)
    flip = lambda t: jnp.flip(t, axis=1)
    yc_f, hc_f = ssd_chunked(x_c, dt_c[:, :, 0], a[0], b_c, c_c, h0)
    yl_f, _ = ssd_chunked(x_l, dt_l[:, :, 0], a[0], b_l, c_l, hc_f)
    yc_b, hc_b = ssd_chunked(flip(x_c), flip(dt_c[:, :, 1]), a[1], flip(b_c), flip(c_c), h0)
    yl_b, _ = ssd_chunked(flip(x_l), flip(dt_l[:, :, 1]), a[1], flip(b_l), flip(c_l), hc_b)

    def finish(yf, yb, xs, z):
        bb, ll = xs.shape[:2]
        y = (yf + flip(yb)).astype(xs.dtype) + xs * d_skip[:, None]
        y = y.reshape(bb, ll, SSD_D_INNER) * jax.nn.silu(z)
        y = rms_norm(y.reshape(bb, ll, SSD_N_GROUPS, SSD_D_INNER // SSD_N_GROUPS),
                     norm_g.reshape(SSD_N_GROUPS, SSD_D_INNER // SSD_N_GROUPS))
        return y.reshape(bb, ll, SSD_D_INNER) @ w_out

    out_l = finish(yl_f, yl_b, x_l, z_l)
    out_c = finish(yc_f, yc_b, x_c, z_c) if ctx_out else None
    return out_l, out_c


def hier_moe(h, w_group, b_group, w_router, b_router, w_gate, w_up, w_down):
    t, d = h.shape
    hf = h.astype(jnp.float32)
    g_logits = hf @ w_group.astype(jnp.float32) + b_group.astype(jnp.float32)
    g_prob = jax.nn.softmax(g_logits, axis=-1)
    g_sel = jnp.argmax(g_logits, axis=-1)
    g_w = jnp.take_along_axis(g_prob, g_sel[:, None], axis=-1)[:, 0]
    e_logits = (hf @ w_router.astype(jnp.float32) + b_router.astype(jnp.float32)).reshape(
        t, MOE_GROUPS, MOE_EXPERTS_PER_GROUP)
    e_logits = jnp.take_along_axis(e_logits, g_sel[:, None, None], axis=1)[:, 0]
    top_v, top_i = lax.top_k(e_logits, MOE_TOP_K)
    e_w = jax.nn.softmax(top_v, axis=-1) * g_w[:, None]
    expert_id = g_sel[:, None] * MOE_EXPERTS_PER_GROUP + top_i

    n_assign = t * MOE_TOP_K
    flat_e = expert_id.reshape(-1)
    flat_tok = jnp.repeat(jnp.arange(t, dtype=jnp.int32), MOE_TOP_K)
    flat_w = e_w.reshape(-1)
    order = jnp.argsort(flat_e)
    sorted_e = flat_e[order]
    counts = jnp.bincount(flat_e, length=MOE_EXPERTS)
    starts = jnp.cumsum(counts) - counts
    padded = (counts + MOE_ROW_BLOCK - 1) // MOE_ROW_BLOCK * MOE_ROW_BLOCK
    pad_ends = jnp.cumsum(padded)
    pad_starts = pad_ends - padded
    dest = pad_starts[sorted_e] + (jnp.arange(n_assign) - starts[sorted_e])
    n_blocks = -(-n_assign // MOE_ROW_BLOCK) + MOE_EXPERTS
    cap = n_blocks * MOE_ROW_BLOCK
    buf_tok = jnp.zeros((cap,), jnp.int32).at[dest].set(flat_tok[order])
    buf_w = jnp.zeros((cap,), jnp.float32).at[dest].set(flat_w[order])
    block_e = jnp.minimum(
        jnp.searchsorted(pad_ends, jnp.arange(n_blocks) * MOE_ROW_BLOCK, side="right"), MOE_EXPERTS - 1)
    xb = h[buf_tok].reshape(n_blocks, MOE_ROW_BLOCK, d)

    def run_block(args):
        xblk, e = args
        return (jax.nn.silu(xblk @ w_gate[e]) * (xblk @ w_up[e])) @ w_down[e]

    yb = lax.map(run_block, (xb, block_e)).reshape(cap, d)
    return jax.ops.segment_sum(yb * buf_w[:, None].astype(yb.dtype), buf_tok, num_segments=t)


def setup_inputs(seed: int = 0) -> dict:
    key = jax.random.key(seed)
    ks = jax.random.split(key, 32)
    f32 = jnp.float32
    nrm = lambda k, shape, s: jax.random.normal(k, shape, f32) * s
    D = D_MODEL
    H = SSD_N_HEADS
    dt0 = jnp.exp(jax.random.uniform(ks[10], (N_SSD_LAYERS, 2, H), f32,
                                     minval=math.log(1e-3), maxval=math.log(1e-1)))
    dt_bias = dt0 + jnp.log(-jnp.expm1(-dt0))
    a_log = jnp.log(jax.random.uniform(ks[11], (N_SSD_LAYERS, 2, H), f32, minval=1.0, maxval=16.0))
    return {
        "x": nrm(ks[0], (BATCH, SEQ, D), 1.0),
        "c": nrm(ks[1], (BATCH, D), 1.0),
        "ctx": nrm(ks[2], (BATCH, CTX_LEN, D), 1.0),
        "c_ctx": nrm(ks[3], (D,), 1.0),
        "mod_w": nrm(ks[4], (DEPTH, D, N_MOD * D), 0.5 * D ** -0.5),
        "mod_b": nrm(ks[5], (DEPTH, N_MOD * D), 0.01),
        "norm1_g": 1.0 + nrm(ks[6], (DEPTH, D), 0.02),
        "norm2_g": 1.0 + nrm(ks[7], (DEPTH, D), 0.02),
        "ssd_w_in": nrm(ks[8], (N_SSD_LAYERS, D, SSD_IN_DIM), D ** -0.5),
        "ssd_conv_w": nrm(ks[9], (N_SSD_LAYERS, SSD_CONV, SSD_CONV_DIM), SSD_CONV ** -0.5),
        "ssd_conv_b": nrm(ks[12], (N_SSD_LAYERS, SSD_CONV_DIM), 0.01),
        "ssd_dt_bias": dt_bias,
        "ssd_a_log": a_log,
        "ssd_d": 1.0 + nrm(ks[13], (N_SSD_LAYERS, H), 0.02),
        "ssd_norm_g": 1.0 + nrm(ks[14], (N_SSD_LAYERS, SSD_D_INNER), 0.02),
        "ssd_w_out": nrm(ks[15], (N_SSD_LAYERS, SSD_D_INNER, D), SSD_D_INNER ** -0.5),
        "att_w_qkv": nrm(ks[16], (N_ATT_LAYERS, D, ATT_QKV_DIM), D ** -0.5),
        "att_q_gain": 1.0 + nrm(ks[17], (N_ATT_LAYERS, ATT_HEAD_DIM), 0.02),
        "att_k_gain": 1.0 + nrm(ks[18], (N_ATT_LAYERS, ATT_HEAD_DIM), 0.02),
        "att_w_o": nrm(ks[19], (N_ATT_LAYERS, D, D), D ** -0.5),
        "moe_w_group": nrm(ks[20], (DEPTH, D, MOE_GROUPS), D ** -0.5),
        "moe_b_group": nrm(ks[21], (DEPTH, MOE_GROUPS), 0.01),
        "moe_w_router": nrm(ks[22], (DEPTH, D, MOE_EXPERTS), D ** -0.5),
        "moe_b_router": nrm(ks[23], (DEPTH, MOE_EXPERTS), 0.01),
        "moe_w_gate": nrm(ks[24], (DEPTH, MOE_EXPERTS, D, MOE_D_FF), D ** -0.5),
        "moe_w_up": nrm(ks[25], (DEPTH, MOE_EXPERTS, D, MOE_D_FF), D ** -0.5),
        "moe_w_down": nrm(ks[26], (DEPTH, MOE_EXPERTS, MOE_D_FF, D), MOE_D_FF ** -0.5),
    }


def reference(x, c, ctx, c_ctx, mod_w, mod_b, norm1_g, norm2_g,
              ssd_w_in, ssd_conv_w, ssd_conv_b, ssd_dt_bias, ssd_a_log, ssd_d, ssd_norm_g, ssd_w_out,
              att_w_qkv, att_q_gain, att_k_gain, att_w_o,
              moe_w_group, moe_b_group, moe_w_router, moe_b_router, moe_w_gate, moe_w_up, moe_w_down):
    b, n_lat, d = x.shape
    n_ctx = ctx.shape[1]
    silu_c = jax.nn.silu(c)
    silu_cc = jax.nn.silu(c_ctx)[None]
    x_lat, x_ctx = x, ctx
    for layer in range(DEPTH):
        last = layer == DEPTH - 1
        mod_l = jnp.split((silu_c @ mod_w[layer] + mod_b[layer])[:, None, :], N_MOD, axis=-1)
        mod_c = jnp.split((silu_cc @ mod_w[layer] + mod_b[layer])[:, None, :], N_MOD, axis=-1)
        sh1, sc1, g1, sh2, sc2, g2 = mod_l
        csh1, csc1, cg1, csh2, csc2, cg2 = mod_c
        a_l = modulate(x_lat, norm1_g[layer], sh1, sc1)
        a_c = modulate(x_ctx, norm1_g[layer], csh1, csc1)
        j = layer // N_MIXERS
        if layer % N_MIXERS == 0:
            m_l, m_c = ssd_mixer(a_l, a_c, ssd_w_in[j], ssd_conv_w[j], ssd_conv_b[j], ssd_dt_bias[j],
                                 ssd_a_log[j], ssd_d[j], ssd_norm_g[j], ssd_w_out[j], not last)
        else:
            m_l, m_c = gqa_mixer(a_l, a_c, att_w_qkv[j], att_q_gain[j], att_k_gain[j], att_w_o[j], not last)
        x_lat = x_lat + g1 * m_l
        f_l = modulate(x_lat, norm2_g[layer], sh2, sc2).reshape(b * n_lat, d)
        moe_args = (moe_w_group[layer], moe_b_group[layer], moe_w_router[layer], moe_b_router[layer],
                    moe_w_gate[layer], moe_w_up[layer], moe_w_down[layer])
        if last:
            y = hier_moe(f_l, *moe_args)
            x_lat = x_lat + g2 * y.reshape(b, n_lat, d)
        else:
            x_ctx = x_ctx + cg1 * m_c
            f_c = modulate(x_ctx, norm2_g[layer], csh2, csc2).reshape(b * n_ctx, d)
            y = hier_moe(jnp.concatenate([f_l, f_c], axis=0), *moe_args)
            x_lat = x_lat + g2 * y[: b * n_lat].reshape(b, n_lat, d)
            x_ctx = x_ctx + cg2 * y[b * n_lat:].reshape(b, n_ctx, d)
    return x_lat
```

```python
import functools

import numpy as np
import jax
import jax.numpy as jnp
from jax import lax
from jax.experimental import pallas as pl
from jax.experimental.pallas import tpu as pltpu

F32 = jnp.float32
BF16 = jnp.bfloat16
I32 = jnp.int32
HIGHEST = lax.Precision.HIGHEST

NORM_EPS = 1e-6
N_MOD = 6
GRID_W = 64
ROPE_THETA = 10000.0

SSD_HEAD_DIM = 64
SSD_N_GROUPS = 4
SSD_D_STATE = 128
SSD_CONV = 5
SSD_CHUNK = 128

ATT_HEAD_DIM = 64
ATT_KV_HEADS = 4

MOE_GROUPS = 4
MOE_EPG = 8
MOE_EXPERTS = MOE_GROUPS * MOE_EPG
MOE_ROW_BLOCK = 256
TAB_SLOTS = 4

TM = 256
LANES = 128
HALO = 16
VMEM_LIMIT = 56 * 1024 * 1024


def _cparams(sem):
    return pltpu.CompilerParams(dimension_semantics=sem, vmem_limit_bytes=VMEM_LIMIT)


def _silu(v):
    return v / (1.0 + jnp.exp(-v))


def _softplus(v):
    return jnp.maximum(v, 0.0) + jnp.log1p(jnp.exp(-jnp.abs(v)))


def _norm_mod(x, g, shift, scale):
    ms = jnp.mean(x * x, axis=-1, keepdims=True)
    y = x * lax.rsqrt(ms + NORM_EPS) * g
    return y * (1.0 + scale) + shift


class _Layout:
    def __init__(self, batch, seq, n_ctx):
        assert seq % TM == 0 and n_ctx % TM == 0
        self.batch, self.seq, self.n_ctx = batch, seq, n_ctx
        self.t_lat = batch * seq
        self.t_ctx = batch * n_ctx
        self.t = self.t_lat + self.t_ctx
        self.lat_tiles = self.t_lat // TM
        self.seq_tiles = seq // TM
        self.ctx_tiles = n_ctx // TM
        self.n_tiles = self.t // TM
        self.nk = seq + n_ctx

    def mod_row(self, i):
        return jnp.where(i < self.lat_tiles, i // self.seq_tiles, self.batch)

    def tile_batch(self, i):
        return jnp.where(i < self.lat_tiles, i // self.seq_tiles, (i - self.lat_tiles) // self.ctx_tiles)

    def tile_pos(self, i):
        return jnp.where(i < self.lat_tiles, i % self.seq_tiles,
                         self.seq_tiles + (i - self.lat_tiles) % self.ctx_tiles)

    def seg_first(self, i):
        return jnp.where(i < self.lat_tiles, i % self.seq_tiles == 0, (i - self.lat_tiles) % self.ctx_tiles == 0)

    def seg_last(self, i):
        return jnp.where(i < self.lat_tiles, i % self.seq_tiles == self.seq_tiles - 1,
                         (i - self.lat_tiles) % self.ctx_tiles == self.ctx_tiles - 1)


def _mod_kernel(c_ref, w_ref, b_ref, o_ref):
    s = _silu(c_ref[...])
    o_ref[...] = jnp.dot(s, w_ref[...], precision=HIGHEST, preferred_element_type=F32) + b_ref[...]


def _mod_table(c8, mod_w, mod_b):
    depth, d, n = mod_w.shape
    tn = 1536
    assert n % tn == 0
    return pl.pallas_call(
        _mod_kernel,
        grid=(depth, n // tn),
        in_specs=[pl.BlockSpec((8, d), lambda l, j: (0, 0)),
                  pl.BlockSpec((None, d, tn), lambda l, j: (l, 0, j)),
                  pl.BlockSpec((None, 1, tn), lambda l, j: (l, 0, j))],
        out_specs=pl.BlockSpec((None, 8, tn), lambda l, j: (l, 0, j)),
        out_shape=jax.ShapeDtypeStruct((depth, 8, n), F32),
        compiler_params=_cparams(("parallel", "parallel")),
        name="mod_table",
    )(c8, mod_w, mod_b.reshape(depth, 1, n))


def _ssd_inproj_kernel(x_ref, mod_ref, g_ref, wz_ref, wx_ref, wdt_ref, wdtt_ref, dtb_ref, dtbt_ref,
                       z_ref, xbc_ref, dt_ref, dtt_ref):
    h = _norm_mod(x_ref[...], g_ref[...], mod_ref[0:1, :], mod_ref[1:2, :]).astype(BF16)
    z_ref[...] = jnp.dot(h, wz_ref[...], preferred_element_type=F32).astype(BF16)
    xbc_ref[...] = jnp.dot(h, wx_ref[...], preferred_element_type=F32).astype(BF16)
    nh = dtb_ref.shape[1] // 2
    dt = _softplus(jnp.dot(h, wdt_ref[...], preferred_element_type=F32) + dtb_ref[...])
    dt_ref[0] = dt[:, :nh]
    dt_ref[1] = dt[:, nh:]
    dtt = lax.dot_general(wdtt_ref[...], h, (((1,), (1,)), ((), ())), preferred_element_type=F32)
    dtt = _softplus(dtt + dtbt_ref[...])
    dtt_ref[0] = dtt[:nh, :]
    dtt_ref[1] = dtt[nh:, :]


def _ssd_inproj(lay, x, mod_l, g, wz, wx, wdt, wdtt, dtb, dtbt):
    t, d = x.shape
    di, dc, nh2 = wz.shape[1], wx.shape[1], wdt.shape[1]
    nh = nh2 // 2
    full = lambda a: pl.BlockSpec(a.shape, lambda i: (0,) * a.ndim)
    return pl.pallas_call(
        _ssd_inproj_kernel,
        grid=(lay.n_tiles,),
        in_specs=[pl.BlockSpec((TM, d), lambda i: (i, 0)),
                  pl.BlockSpec((None, N_MOD, d), lambda i: (lay.mod_row(i), 0, 0)),
                  full(g), full(wz), full(wx), full(wdt), full(wdtt), full(dtb), full(dtbt)],
        out_specs=[pl.BlockSpec((TM, di), lambda i: (i, 0)),
                   pl.BlockSpec((TM, dc), lambda i: (i, 0)),
                   pl.BlockSpec((2, TM, nh), lambda i: (0, i, 0)),
                   pl.BlockSpec((2, nh, TM), lambda i: (0, 0, i))],
        out_shape=[jax.ShapeDtypeStruct((t, di), BF16),
                   jax.ShapeDtypeStruct((t, dc), BF16),
                   jax.ShapeDtypeStruct((2, t, nh), F32),
                   jax.ShapeDtypeStruct((2, nh, t), F32)],
        compiler_params=_cparams(("parallel",)),
        name="ssd_inproj",
    )(x, mod_l, g, wz, wx, wdt, wdtt, dtb, dtbt)


def _ssd_conv_kernel(lay, di, gn, prev_ref, main_ref, next_ref, w_ref, b_ref, xs_ref, bm_ref, cm_ref, ext_ref):
    i = pl.program_id(0)
    dc = main_ref.shape[1]
    prev = jnp.where(lay.seg_first(i), 0.0, prev_ref[...].astype(F32))
    nxt = jnp.where(lay.seg_last(i), 0.0, next_ref[...].astype(F32))
    ext_ref[0:HALO, :] = prev
    ext_ref[HALO:HALO + TM, :] = main_ref[...].astype(F32)
    ext_ref[HALO + TM:HALO + TM + HALO, :] = nxt
    half = SSD_CONV // 2
    cw = 512
    for c0 in range(0, dc, cw):
        acc = jnp.zeros((TM, cw), F32) + b_ref[:, c0:c0 + cw]
        for k in range(SSD_CONV):
            acc = acc + ext_ref[HALO + k - half:HALO + k - half + TM, c0:c0 + cw] * w_ref[k:k + 1, c0:c0 + cw]
        y = _silu(acc).astype(BF16)
        if c0 < di:
            xs_ref[:, c0:c0 + cw] = y
        elif c0 < di + gn:
            bm_ref[:, c0 - di:c0 - di + cw] = y
        else:
            cm_ref[:, c0 - di - gn:c0 - di - gn + cw] = y


def _ssd_conv(lay, xbc, conv_w, conv_b, di, gn):
    t, dc = xbc.shape
    hb = TM // HALO
    nhb = t // HALO
    assert gn == 512 and di % 512 == 0
    return pl.pallas_call(
        functools.partial(_ssd_conv_kernel, lay, di, gn),
        grid=(lay.n_tiles,),
        in_specs=[pl.BlockSpec((HALO, dc), lambda i: (jnp.maximum(i * hb - 1, 0), 0)),
                  pl.BlockSpec((TM, dc), lambda i: (i, 0)),
                  pl.BlockSpec((HALO, dc), lambda i: (jnp.minimum((i + 1) * hb, nhb - 1), 0)),
                  pl.BlockSpec(conv_w.shape, lambda i: (0, 0)),
                  pl.BlockSpec(conv_b.shape, lambda i: (0, 0))],
        out_specs=[pl.BlockSpec((TM, di), lambda i: (i, 0)),
                   pl.BlockSpec((TM, gn), lambda i: (i, 0)),
                   pl.BlockSpec((TM, gn), lambda i: (i, 0))],
        out_shape=[jax.ShapeDtypeStruct((t, di), BF16),
                   jax.ShapeDtypeStruct((t, gn), BF16),
                   jax.ShapeDtypeStruct((t, gn), BF16)],
        scratch_shapes=[pltpu.VMEM((TM + 2 * HALO, dc), F32)],
        compiler_params=_cparams(("parallel",)),
        name="ssd_conv",
    )(xbc, xbc, xbc, conv_w, conv_b)


def _ssd_scan_kernel(xs_ref, bm_ref, cm_ref, dt_ref, dtt_ref, alr_ref, alc_ref, e_ref, y_ref, state_ref):
    d = pl.program_id(0)
    step = pl.program_id(2)
    q = SSD_CHUNK
    ng = SSD_N_GROUPS
    ns = SSD_D_STATE
    gw = xs_ref.shape[1] // ng
    hpg = gw // SSD_HEAD_DIM

    @pl.when(step == 0)
    def _():
        state_ref[...] = jnp.zeros_like(state_ref)

    sgn = jnp.where(d == 0, 1, -1)
    row = lax.broadcasted_iota(I32, (q, q), 0)
    col = lax.broadcasted_iota(I32, (q, q), 1)
    lmask = (row - col) * sgn >= 0
    lmask_t = (col - row) * sgn >= 0

    dt = dt_ref[...]
    dtt = dtt_ref[...]
    da = dt * (-jnp.exp(alr_ref[...]))
    dat = dtt * (-jnp.exp(alc_ref[...]))
    a_cum = jnp.dot(lmask.astype(F32), da, precision=HIGHEST, preferred_element_type=F32)
    a_cum_t = jnp.dot(dat, lmask_t.astype(F32), precision=HIGHEST, preferred_element_type=F32)
    a_end = jnp.sum(da, axis=0, keepdims=True)
    w_end = jnp.exp(a_end - a_cum) * dt

    e = e_ref[...]
    a_exp = jnp.dot(a_cum, e, precision=HIGHEST, preferred_element_type=F32)
    w_exp = jnp.dot(w_end, e, precision=HIGHEST, preferred_element_type=F32)
    end_exp = jnp.dot(jnp.broadcast_to(a_end, (8, a_end.shape[1])), e, precision=HIGHEST,
                      preferred_element_type=F32)[0:1, :]
    decay_in = jnp.exp(a_exp)
    chunk_decay = jnp.exp(end_exp)
    xw = (xs_ref[...].astype(F32) * w_exp).astype(BF16)

    lane = lax.broadcasted_iota(I32, (q, LANES), 1)
    for g in range(ng):
        bg = bm_ref[:, g * ns:(g + 1) * ns]
        cg = cm_ref[:, g * ns:(g + 1) * ns]
        cb = lax.dot_general(cg, bg, (((1,), (1,)), ((), ())), preferred_element_type=F32)
        s_in = state_ref[g]
        y_off = jnp.dot(cg, s_in.astype(BF16), preferred_element_type=F32)
        new_s = lax.dot_general(bg, xw[:, g * gw:(g + 1) * gw], (((0,), (0,)), ((), ())),
                                preferred_element_type=F32)
        state_ref[g] = s_in * chunk_decay[:, g * gw:(g + 1) * gw] + new_s
        for pr in range(hpg // 2):
            mixes = []
            for hh in range(2):
                h = g * hpg + 2 * pr + hh
                seg = a_cum[:, h:h + 1] - a_cum_t[h:h + 1, :]
                lh = jnp.where(lmask, jnp.exp(jnp.minimum(seg, 0.0)), 0.0)
                mixes.append((cb * lh * dtt[h:h + 1, :]).astype(BF16))
            lhs = jnp.concatenate(mixes, axis=1)
            l0 = g * gw + pr * LANES
            xp = xs_ref[:, l0:l0 + LANES]
            zero = jnp.zeros_like(xp)
            rhs = jnp.concatenate([jnp.where(lane < SSD_HEAD_DIM, xp, zero),
                                   jnp.where(lane >= SSD_HEAD_DIM, xp, zero)], axis=0)
            y_pair = jnp.dot(lhs, rhs, preferred_element_type=F32)
            y_pair = y_pair + y_off[:, pr * LANES:(pr + 1) * LANES] * decay_in[:, l0:l0 + LANES]
            y_ref[:, l0:l0 + LANES] = y_pair.astype(BF16)


def _ssd_scan(lay, xs, bm, cm, dt, dtt, a_log, expand):
    t, di = xs.shape
    gn = bm.shape[1]
    nh = dt.shape[2]
    q = SSD_CHUNK
    nct, nlt = lay.n_ctx // q, lay.seq // q
    nc = nct + nlt
    ctx_base = lay.t_lat // q

    def blk(d, b, s):
        j_ctx = jnp.where(d == 0, s, nct - 1 - s)
        j_lat = jnp.where(d == 0, s - nct, nlt - 1 - (s - nct))
        return jnp.where(s < nct, ctx_base + b * nct + j_ctx, b * nlt + j_lat)

    return pl.pallas_call(
        _ssd_scan_kernel,
        grid=(2, lay.batch, nc),
        in_specs=[pl.BlockSpec((q, di), lambda d, b, s: (blk(d, b, s), 0)),
                  pl.BlockSpec((q, gn), lambda d, b, s: (blk(d, b, s), 0)),
                  pl.BlockSpec((q, gn), lambda d, b, s: (blk(d, b, s), 0)),
                  pl.BlockSpec((None, q, nh), lambda d, b, s: (d, blk(d, b, s), 0)),
                  pl.BlockSpec((None, nh, q), lambda d, b, s: (d, 0, blk(d, b, s))),
                  pl.BlockSpec((None, 1, nh), lambda d, b, s: (d, 0, 0)),
                  pl.BlockSpec((None, nh, 1), lambda d, b, s: (d, 0, 0)),
                  pl.BlockSpec(expand.shape, lambda d, b, s: (0, 0))],
        out_specs=pl.BlockSpec((None, q, di), lambda d, b, s: (d, blk(d, b, s), 0)),
        out_shape=jax.ShapeDtypeStruct((2, t, di), BF16),
        scratch_shapes=[pltpu.VMEM((SSD_N_GROUPS, SSD_D_STATE, di // SSD_N_GROUPS), F32)],
        compiler_params=_cparams(("arbitrary", "arbitrary", "arbitrary")),
        name="ssd_scan",
    )(xs, bm, cm, dt, dtt, a_log.reshape(2, 1, nh), a_log.reshape(2, nh, 1), expand)


def _ssd_outproj_kernel(y_ref, xs_ref, z_ref, dexp_ref, ng_ref, w_ref, x_ref, mod_ref, o_ref):
    y = y_ref[0].astype(F32) + y_ref[1].astype(F32) + xs_ref[...].astype(F32) * dexp_ref[...]
    y = y * _silu(z_ref[...].astype(F32))
    gw = y.shape[1] // SSD_N_GROUPS
    parts = []
    for g in range(SSD_N_GROUPS):
        yg = y[:, g * gw:(g + 1) * gw]
        ms = jnp.mean(yg * yg, axis=-1, keepdims=True)
        parts.append((yg * lax.rsqrt(ms + NORM_EPS) * ng_ref[:, g * gw:(g + 1) * gw]).astype(BF16))
    yn = jnp.concatenate(parts, axis=1)
    m = jnp.dot(yn, w_ref[...], preferred_element_type=F32)
    o_ref[...] = x_ref[...] + mod_ref[2:3, :] * m


def _ssd_outproj(lay, y, xs, z, dexp, ng, w, x, mod_l):
    t, d = x.shape
    di = xs.shape[1]
    full = lambda a: pl.BlockSpec(a.shape, lambda i: (0,) * a.ndim)
    return pl.pallas_call(
        _ssd_outproj_kernel,
        grid=(lay.n_tiles,),
        in_specs=[pl.BlockSpec((2, TM, di), lambda i: (0, i, 0)),
                  pl.BlockSpec((TM, di), lambda i: (i, 0)),
                  pl.BlockSpec((TM, di), lambda i: (i, 0)),
                  full(dexp), full(ng), full(w),
                  pl.BlockSpec((TM, d), lambda i: (i, 0)),
                  pl.BlockSpec((None, N_MOD, d), lambda i: (lay.mod_row(i), 0, 0))],
        out_specs=pl.BlockSpec((TM, d), lambda i: (i, 0)),
        out_shape=jax.ShapeDtypeStruct((t, d), F32),
        compiler_params=_cparams(("parallel",)),
        name="ssd_outproj",
    )(y, xs, z, dexp, ng, w, x, mod_l)


def _att_inproj_kernel(x_ref, mod_ref, g_ref, wq_ref, wkt_ref, wv_ref, qg_ref, kgt_ref, gsum_ref, gexp_ref,
                       cos_ref, sin_ref, cost_ref, sint_ref, q_ref, kt_ref, v_ref):
    h = _norm_mod(x_ref[...], g_ref[...], mod_ref[0:1, :], mod_ref[1:2, :]).astype(BF16)
    hd = ATT_HEAD_DIM
    q = jnp.dot(h, wq_ref[...], preferred_element_type=F32)
    ssum = jnp.dot((q * q).astype(BF16), gsum_ref[...], preferred_element_type=F32)
    r = lax.rsqrt(ssum * (1.0 / hd) + NORM_EPS)
    r_hi = r.astype(BF16)
    r_lo = (r - r_hi.astype(F32)).astype(BF16)
    r_exp = jnp.dot(jnp.concatenate([r_hi, r_lo], axis=1), gexp_ref[...], preferred_element_type=F32)
    qn = q * r_exp * qg_ref[...]
    nl = qn.shape[1]
    lane = lax.broadcasted_iota(I32, qn.shape, 1)
    partner = jnp.where((lane & (hd // 2)) == 0, pltpu.roll(qn, nl - hd // 2, 1), pltpu.roll(qn, hd // 2, 1))
    reps = nl // LANES
    cos = jnp.concatenate([cos_ref[...]] * reps, axis=1)
    sin = jnp.concatenate([sin_ref[...]] * reps, axis=1)
    q_ref[...] = (qn * cos + partner * sin).astype(BF16)
    kt = lax.dot_general(wkt_ref[...], h, (((1,), (1,)), ((), ())), preferred_element_type=F32)
    ct, st = cost_ref[...], sint_ref[...]
    for kh in range(ATT_KV_HEADS):
        x0 = kt[kh * hd:kh * hd + hd // 2]
        x1 = kt[kh * hd + hd // 2:(kh + 1) * hd]
        ms = (jnp.sum(x0 * x0, axis=0, keepdims=True) + jnp.sum(x1 * x1, axis=0, keepdims=True)) * (1.0 / hd)
        rk = lax.rsqrt(ms + NORM_EPS)
        x0 = x0 * rk * kgt_ref[0:hd // 2, :]
        x1 = x1 * rk * kgt_ref[hd // 2:hd, :]
        kt_ref[kh, 0:hd // 2, :] = (x0 * ct - x1 * st).astype(BF16)
        kt_ref[kh, hd // 2:hd, :] = (x0 * st + x1 * ct).astype(BF16)
    v = jnp.dot(h, wv_ref[...], preferred_element_type=F32).astype(BF16)
    for kh in range(ATT_KV_HEADS):
        v_ref[kh] = v[:, kh * LANES:(kh + 1) * LANES]


def _att_inproj(lay, x, mod_l, g, wq, wkt, wv2, qg, kgt, gsum, gexp, cos, sin, cost, sint):
    t, d = x.shape
    dq = wq.shape[1]
    hd = ATT_HEAD_DIM
    full = lambda a: pl.BlockSpec(a.shape, lambda i: (0,) * a.ndim)
    return pl.pallas_call(
        _att_inproj_kernel,
        grid=(lay.n_tiles,),
        in_specs=[pl.BlockSpec((TM, d), lambda i: (i, 0)),
                  pl.BlockSpec((None, N_MOD, d), lambda i: (lay.mod_row(i), 0, 0)),
                  full(g), full(wq), full(wkt), full(wv2), full(qg), full(kgt), full(gsum), full(gexp),
                  pl.BlockSpec((TM, LANES), lambda i: (lay.tile_pos(i), 0)),
                  pl.BlockSpec((TM, LANES), lambda i: (lay.tile_pos(i), 0)),
                  pl.BlockSpec((hd // 2, TM), lambda i: (0, lay.tile_pos(i))),
                  pl.BlockSpec((hd // 2, TM), lambda i: (0, lay.tile_pos(i)))],
        out_specs=[pl.BlockSpec((TM, dq), lambda i: (i, 0)),
                   pl.BlockSpec((None, ATT_KV_HEADS, hd, TM), lambda i: (lay.tile_batch(i), 0, 0, lay.tile_pos(i))),
                   pl.BlockSpec((None, ATT_KV_HEADS, TM, LANES), lambda i: (lay.tile_batch(i), 0, lay.tile_pos(i), 0))],
        out_shape=[jax.ShapeDtypeStruct((t, dq), BF16),
                   jax.ShapeDtypeStruct((lay.batch, ATT_KV_HEADS, hd, lay.nk), BF16),
                   jax.ShapeDtypeStruct((lay.batch, ATT_KV_HEADS, lay.nk, LANES), BF16)],
        compiler_params=_cparams(("parallel",)),
        name="att_inproj",
    )(x, mod_l, g, wq, wkt, wv2, qg, kgt, gsum, gexp, cos, sin, cost, sint)


def _attend(q_ref, kt, v, o_ref):
    hd = ATT_HEAD_DIM
    zk = jnp.zeros_like(kt)
    lane = lax.broadcasted_iota(I32, v.shape, 1)
    zv = jnp.zeros_like(v)
    halves = ((jnp.concatenate([kt, zk], axis=0), jnp.where(lane < hd, v, zv)),
              (jnp.concatenate([zk, kt], axis=0), jnp.where(lane >= hd, v, zv)))
    for pr in range(q_ref.shape[1] // LANES):
        qp = q_ref[:, pr * LANES:(pr + 1) * LANES]
        acc = jnp.zeros((qp.shape[0], LANES), F32)
        for kth, vh in halves:
            s = jnp.dot(qp, kth, preferred_element_type=F32)
            m = jnp.max(s, axis=-1, keepdims=True)
            p = jnp.exp(s - m)
            l = jnp.sum(p, axis=-1, keepdims=True)
            acc = acc + jnp.dot(p.astype(BF16), vh, preferred_element_type=F32) / l
        o_ref[:, pr * LANES:(pr + 1) * LANES] = acc.astype(BF16)


def _att_core_kernel(lay, q_ref, kt_ref, v_ref, o_ref):
    qi = pl.program_id(2)

    @pl.when(qi < lay.seq_tiles)
    def _():
        _attend(q_ref, kt_ref[...], v_ref[...], o_ref)

    @pl.when(qi >= lay.seq_tiles)
    def _():
        _attend(q_ref, kt_ref[:, lay.seq:], v_ref[lay.seq:, :], o_ref)


def _att_core(lay, qn, kt, v):
    t, dq = qn.shape
    hd = ATT_HEAD_DIM
    qw = dq // ATT_KV_HEADS
    per_b = lay.seq_tiles + lay.ctx_tiles

    def row_tile(b, qi):
        return jnp.where(qi < lay.seq_tiles, b * lay.seq_tiles + qi,
                         lay.lat_tiles + b * lay.ctx_tiles + (qi - lay.seq_tiles))

    return pl.pallas_call(
        functools.partial(_att_core_kernel, lay),
        grid=(lay.batch, ATT_KV_HEADS, per_b),
        in_specs=[pl.BlockSpec((TM, qw), lambda b, kh, qi: (row_tile(b, qi), kh)),
                  pl.BlockSpec((None, None, hd, lay.nk), lambda b, kh, qi: (b, kh, 0, 0)),
                  pl.BlockSpec((None, None, lay.nk, LANES), lambda b, kh, qi: (b, kh, 0, 0))],
        out_specs=pl.BlockSpec((TM, qw), lambda b, kh, qi: (row_tile(b, qi), kh)),
        out_shape=jax.ShapeDtypeStruct((t, dq), BF16),
        compiler_params=_cparams(("parallel", "parallel", "arbitrary")),
        name="att_core",
    )(qn, kt, v)


def _att_outproj_kernel(o_ref, w_ref, x_ref, mod_ref, out_ref):
    m = jnp.dot(o_ref[...], w_ref[...], preferred_element_type=F32)
    out_ref[...] = x_ref[...] + mod_ref[2:3, :] * m


def _att_outproj(lay, o, w, x, mod_l):
    t, d = x.shape
    return pl.pallas_call(
        _att_outproj_kernel,
        grid=(lay.n_tiles,),
        in_specs=[pl.BlockSpec((TM, o.shape[1]), lambda i: (i, 0)),
                  pl.BlockSpec(w.shape, lambda i: (0, 0)),
                  pl.BlockSpec((TM, d), lambda i: (i, 0)),
                  pl.BlockSpec((None, N_MOD, d), lambda i: (lay.mod_row(i), 0, 0))],
        out_specs=pl.BlockSpec((TM, d), lambda i: (i, 0)),
        out_shape=jax.ShapeDtypeStruct((t, d), F32),
        compiler_params=_cparams(("parallel",)),
        name="att_outproj",
    )(o, w, x, mod_l)


R_EID, R_RANK, R_W = 0, 2, 4
GROUP_LANE0 = MOE_EXPERTS


def _router_kernel(x_ref, mod_ref, g_ref, wr_ref, br_ref, f_ref, r_ref, cnt_ref, base_ref):
    i = pl.program_id(0)

    @pl.when(i == 0)
    def _():
        base_ref[...] = jnp.zeros_like(base_ref)

    f = _norm_mod(x_ref[...], g_ref[...], mod_ref[3:4, :], mod_ref[4:5, :])
    f_ref[...] = f
    logits = jnp.dot(f, wr_ref[...], precision=HIGHEST, preferred_element_type=F32) + br_ref[...]
    lane = lax.broadcasted_iota(I32, logits.shape, 1)
    neg = jnp.float32(-jnp.inf)
    big = jnp.int32(LANES)

    def first_max(vals):
        top = jnp.max(vals, axis=-1, keepdims=True)
        idx = jnp.min(jnp.where(vals == top, lane, big), axis=-1, keepdims=True)
        return top, idx

    g_mask = (lane >= GROUP_LANE0) & (lane < GROUP_LANE0 + MOE_GROUPS)
    glog = jnp.where(g_mask, logits, neg)
    g_top, g_idx = first_max(glog)
    g_w = 1.0 / jnp.sum(jnp.exp(glog - g_top), axis=-1, keepdims=True)
    e0 = (g_idx - GROUP_LANE0) * MOE_EPG
    elog = jnp.where((lane >= e0) & (lane < e0 + MOE_EPG), logits, neg)
    v1, i1 = first_max(elog)
    v2, i2 = first_max(jnp.where(lane == i1, neg, elog))
    ex = jnp.exp(v2 - v1)
    w1 = g_w / (1.0 + ex)
    w2 = g_w * ex / (1.0 + ex)

    oh1 = (lane == i1).astype(F32)
    oh2 = (lane == i2).astype(F32)
    tm = logits.shape[0]
    rr = lax.broadcasted_iota(I32, (tm, tm), 0)
    cc = lax.broadcasted_iota(I32, (tm, tm), 1)
    before = (cc < rr).astype(BF16)
    cum1 = jnp.dot(before, oh1.astype(BF16), preferred_element_type=F32)
    cum2 = jnp.dot(before, oh2.astype(BF16), preferred_element_type=F32)
    tot1 = jnp.sum(oh1, axis=0, keepdims=True)
    tot2 = jnp.sum(oh2, axis=0, keepdims=True)
    base = base_ref[...]
    rank1 = jnp.sum(oh1 * (cum1 + base), axis=-1, keepdims=True)
    rank2 = jnp.sum(oh2 * (cum2 + base + tot1), axis=-1, keepdims=True)
    base = base + tot1 + tot2
    base_ref[...] = base
    cnt_ref[...] = base

    rec = jnp.zeros(logits.shape, F32)
    for k, val in ((R_EID, i1.astype(F32)), (R_EID + 1, i2.astype(F32)), (R_RANK, rank1), (R_RANK + 1, rank2),
                   (R_W, w1), (R_W + 1, w2)):
        rec = jnp.where(lane == k, val, rec)
    r_ref[...] = rec


def _router(lay, n_tiles, x, mod_l, g, wr, br):
    t, d = x.shape
    return pl.pallas_call(
        _router_kernel,
        grid=(n_tiles,),
        in_specs=[pl.BlockSpec((TM, d), lambda i: (i, 0)),
                  pl.BlockSpec((None, N_MOD, d), lambda i: (lay.mod_row(i), 0, 0)),
                  pl.BlockSpec(g.shape, lambda i: (0, 0)),
                  pl.BlockSpec(wr.shape, lambda i: (0, 0)),
                  pl.BlockSpec(br.shape, lambda i: (0, 0))],
        out_specs=[pl.BlockSpec((TM, d), lambda i: (i, 0)),
                   pl.BlockSpec((TM, LANES), lambda i: (i, 0)),
                   pl.BlockSpec((1, LANES), lambda i: (0, 0))],
        out_shape=[jax.ShapeDtypeStruct((n_tiles * TM, d), F32),
                   jax.ShapeDtypeStruct((n_tiles * TM, LANES), F32),
                   jax.ShapeDtypeStruct((1, LANES), F32)],
        scratch_shapes=[pltpu.VMEM((1, LANES), F32)],
        compiler_params=_cparams(("arbitrary",)),
        name="moe_router",
    )(x, mod_l, g, wr, br)


def _expert_kernel(be_ref, nv_ref, tab_hbm, f_hbm, wg_ref, wu_ref, wd_ref, y_hbm,
                   tab_ref, xbuf, ybuf, wgb, wub, wdb, tsem, gsem, ssem):
    i = pl.program_id(0)
    n = pl.num_programs(0)
    t_tok = f_hbm.shape[0]

    def tab_copy(j):
        return pltpu.make_async_copy(tab_hbm.at[j], tab_ref.at[j % TAB_SLOTS], tsem.at[j % TAB_SLOTS])

    def gather_row(j, r):
        tok = tab_ref[j % TAB_SLOTS, r] >> 1
        return pltpu.make_async_copy(f_hbm.at[pl.ds(tok, 1), :], xbuf.at[j % 2, pl.ds(r, 1), :], gsem.at[j % 2])

    def scatter_row(j, r):
        code = tab_ref[j % TAB_SLOTS, r]
        dst = (code & 1) * t_tok + (code >> 1)
        return pltpu.make_async_copy(ybuf.at[pl.ds(r, 1), :], y_hbm.at[pl.ds(dst, 1), :], ssem.at[0])

    def for_rows(j, fn):
        def body(r, c):
            fn(j, r)
            return c
        lax.fori_loop(0, nv_ref[j], body, 0)

    @pl.when(i == 0)
    def _():
        xbuf[...] = jnp.zeros_like(xbuf)
        tab_copy(0).start()
        tab_copy(0).wait()
        for_rows(0, lambda j, r: gather_row(j, r).start())

        @pl.when(n > 1)
        def _():
            tab_copy(1).start()

    @pl.when(i + 1 < n)
    def _():
        tab_copy(i + 1).wait()

        @pl.when(i + 2 < n)
        def _():
            tab_copy(i + 2).start()

        for_rows(i + 1, lambda j, r: gather_row(j, r).start())

    for_rows(i, lambda j, r: gather_row(j, r).wait())

    changed = jnp.logical_or(i == 0, be_ref[i] != be_ref[jnp.maximum(i - 1, 0)])

    @pl.when(changed)
    def _():
        wgb[...] = wg_ref[...].astype(BF16)
        wub[...] = wu_ref[...].astype(BF16)
        wdb[...] = wd_ref[...].astype(BF16)

    @pl.when(i > 0)
    def _():
        for_rows(i - 1, lambda j, r: scatter_row(j, r).wait())

    @pl.when(nv_ref[i] > 0)
    def _():
        xb = xbuf[i % 2].astype(BF16)
        hg = jnp.dot(xb, wgb[...], preferred_element_type=F32)
        hu = jnp.dot(xb, wub[...], preferred_element_type=F32)
        act = (_silu(hg) * hu).astype(BF16)
        ybuf[...] = jnp.dot(act, wdb[...], preferred_element_type=F32)
        for_rows(i, lambda j, r: scatter_row(j, r).start())

    @pl.when(i == n - 1)
    def _():
        for_rows(i, lambda j, r: scatter_row(j, r).wait())


def _experts(block_e, block_nv, tab, f, wg, wu, wd):
    n_blocks = tab.shape[0]
    t, d = f.shape
    ne, _, dff = wg.shape
    rb = MOE_ROW_BLOCK
    grid_spec = pltpu.PrefetchScalarGridSpec(
        num_scalar_prefetch=2,
        grid=(n_blocks,),
        in_specs=[pl.BlockSpec(memory_space=pl.ANY),
                  pl.BlockSpec(memory_space=pl.ANY),
                  pl.BlockSpec((None, d, dff), lambda i, be, nv: (be[i], 0, 0)),
                  pl.BlockSpec((None, d, dff), lambda i, be, nv: (be[i], 0, 0)),
                  pl.BlockSpec((None, dff, d), lambda i, be, nv: (be[i], 0, 0))],
        out_specs=pl.BlockSpec(memory_space=pl.ANY),
        scratch_shapes=[pltpu.SMEM((TAB_SLOTS, rb), I32),
                        pltpu.VMEM((2, rb, d), F32),
                        pltpu.VMEM((rb, d), F32),
                        pltpu.VMEM((d, dff), BF16),
                        pltpu.VMEM((d, dff), BF16),
                        pltpu.VMEM((dff, d), BF16),
                        pltpu.SemaphoreType.DMA((TAB_SLOTS,)),
                        pltpu.SemaphoreType.DMA((2,)),
                        pltpu.SemaphoreType.DMA((1,))],
    )
    return pl.pallas_call(
        _expert_kernel,
        grid_spec=grid_spec,
        out_shape=jax.ShapeDtypeStruct((2 * t, d), F32),
        compiler_params=_cparams(("arbitrary",)),
        name="moe_experts",
    )(block_e, block_nv, tab, f, wg, wu, wd)


def _combine_kernel(x_ref, y_ref, r_ref, mod_ref, o_ref):
    w1 = r_ref[:, R_W:R_W + 1]
    w2 = r_ref[:, R_W + 1:R_W + 2]
    o_ref[...] = x_ref[...] + mod_ref[5:6, :] * (y_ref[0] * w1 + y_ref[1] * w2)


def _combine(lay, n_tiles, x, y, rec, mod_l):
    t, d = x.shape
    return pl.pallas_call(
        _combine_kernel,
        grid=(n_tiles,),
        in_specs=[pl.BlockSpec((TM, d), lambda i: (i, 0)),
                  pl.BlockSpec((2, TM, d), lambda i: (0, i, 0)),
                  pl.BlockSpec((TM, LANES), lambda i: (i, 0)),
                  pl.BlockSpec((None, N_MOD, d), lambda i: (lay.mod_row(i), 0, 0))],
        out_specs=pl.BlockSpec((TM, d), lambda i: (i, 0)),
        out_shape=jax.ShapeDtypeStruct((n_tiles * TM, d), F32),
        compiler_params=_cparams(("parallel",)),
        name="moe_combine",
    )(x, y, rec, mod_l)


def _moe(lay, n_tiles, x, mod_l, g2, w_group, b_group, w_router, b_router, wg, wu, wd):
    d = x.shape[1]
    t_tok = n_tiles * TM
    pad = LANES - MOE_EXPERTS - MOE_GROUPS
    wr = jnp.concatenate([w_router, w_group, jnp.zeros((d, pad), F32)], axis=1)
    br = jnp.concatenate([b_router, b_group, jnp.zeros((pad,), F32)])[None, :]
    f, rec, cnt = _router(lay, n_tiles, x, mod_l, g2, wr, br)

    rb = MOE_ROW_BLOCK
    eid = rec[:, R_EID:R_EID + 2].astype(I32)
    rank = rec[:, R_RANK:R_RANK + 2].astype(I32)
    counts = cnt[0, :MOE_EXPERTS].astype(I32)
    padded = (counts + rb - 1) // rb * rb
    pad_ends = jnp.cumsum(padded)
    pad_starts = pad_ends - padded
    dest = pad_starts[eid] + rank
    n_blocks = -(-(2 * t_tok) // rb) + MOE_EXPERTS
    blk_start = jnp.arange(n_blocks, dtype=I32) * rb
    block_e = jnp.minimum(jnp.searchsorted(pad_ends, blk_start, side="right"), MOE_EXPERTS - 1).astype(I32)
    block_nv = jnp.clip(counts[block_e] - (blk_start - pad_starts[block_e]), 0, rb).astype(I32)
    code = 2 * jnp.arange(t_tok, dtype=I32)[:, None] + jnp.arange(2, dtype=I32)[None, :]
    tab = jnp.zeros((n_blocks * rb,), I32).at[dest.reshape(-1)].set(code.reshape(-1)).reshape(n_blocks, rb)

    y = _experts(block_e, block_nv, tab, f, wg, wu, wd)
    return _combine(lay, n_tiles, x, y.reshape(2, t_tok, d), rec, mod_l)


def _rope_tables(lay):
    half = ATT_HEAD_DIM // 2
    pos = jnp.arange(lay.seq)
    rowp = (pos // GRID_W).astype(F32)
    colp = (pos % GRID_W).astype(F32)
    freqs = ROPE_THETA ** (-jnp.arange(0, half, 2, dtype=F32) / half)
    ang = jnp.concatenate([rowp[:, None] * freqs, colp[:, None] * freqs], axis=-1)
    cos = jnp.concatenate([jnp.cos(ang), jnp.ones((lay.n_ctx, half), F32)], axis=0)
    sin = jnp.concatenate([jnp.sin(ang), jnp.zeros((lay.n_ctx, half), F32)], axis=0)
    cos128 = jnp.concatenate([cos] * (LANES // half), axis=1)
    sin128 = jnp.concatenate([-sin, sin] * (LANES // (2 * half)), axis=1)
    return cos128, sin128, cos.T, sin.T


def kernel(x, c, ctx, c_ctx, mod_w, mod_b, norm1_g, norm2_g, ssd_w_in, ssd_conv_w, ssd_conv_b, ssd_dt_bias,
           ssd_a_log, ssd_d, ssd_norm_g, ssd_w_out, att_w_qkv, att_q_gain, att_k_gain, att_w_o, moe_w_group,
           moe_b_group, moe_w_router, moe_b_router, moe_w_gate, moe_w_up, moe_w_down):
    b, n_lat, d = x.shape
    n_ctx = ctx.shape[1]
    depth = mod_w.shape[0]
    lay = _Layout(b, n_lat, n_ctx)
    assert b + 1 <= 8

    xs = jnp.concatenate([x.reshape(lay.t_lat, d), ctx.reshape(lay.t_ctx, d)], axis=0)
    c8 = jnp.concatenate([c, c_ctx[None, :], jnp.zeros((8 - b - 1, d), F32)], axis=0)
    mods = _mod_table(c8, mod_w, mod_b).reshape(depth, 8, N_MOD, d)

    nh = ssd_dt_bias.shape[2]
    di = nh * SSD_HEAD_DIM
    gn = SSD_N_GROUPS * SSD_D_STATE
    dc = di + 2 * gn
    expand = jnp.repeat(jnp.eye(nh, dtype=F32), SSD_HEAD_DIM, axis=1)

    hq = att_w_qkv.shape[2] // ATT_HEAD_DIM - 2 * ATT_KV_HEADS
    dq = hq * ATT_HEAD_DIM
    dkv = ATT_KV_HEADS * ATT_HEAD_DIM
    perm = np.concatenate([np.arange(0, ATT_HEAD_DIM, 2), np.arange(1, ATT_HEAD_DIM, 2)])
    q_cols = (np.arange(hq)[:, None] * ATT_HEAD_DIM + perm[None, :]).reshape(-1)
    k_cols = dq + (np.arange(ATT_KV_HEADS)[:, None] * ATT_HEAD_DIM + perm[None, :]).reshape(-1)
    v_cols = dq + dkv + (np.arange(ATT_KV_HEADS)[:, None, None] * ATT_HEAD_DIM
                         + np.zeros((1, 2, 1), np.int64) + np.arange(ATT_HEAD_DIM)[None, None, :]).reshape(-1)
    gsum = np.zeros((dq, LANES), np.float32)
    gsum[np.arange(dq), np.arange(dq) // ATT_HEAD_DIM] = 1.0
    gexp = np.concatenate([gsum.T, gsum.T], axis=0)
    cos128, sin128, cos_t, sin_t = _rope_tables(lay)

    for layer in range(depth):
        last = layer == depth - 1
        mod_l = mods[layer]
        j = layer // 2
        g1 = norm1_g[layer][None, :]
        if layer % 2 == 0:
            w_in = ssd_w_in[j].astype(BF16)
            wz, wx, wdt = w_in[:, :di], w_in[:, di:di + dc], w_in[:, di + dc:]
            dtb = ssd_dt_bias[j].reshape(1, 2 * nh)
            z, xbc, dt, dtt = _ssd_inproj(lay, xs, mod_l, g1, wz, wx, wdt, wdt.T, dtb, dtb.T)
            xc, bm, cm = _ssd_conv(lay, xbc, ssd_conv_w[j], ssd_conv_b[j][None, :], di, gn)
            y = _ssd_scan(lay, xc, bm, cm, dt, dtt, ssd_a_log[j], expand)
            dexp = jnp.repeat(ssd_d[j], SSD_HEAD_DIM)[None, :]
            xs = _ssd_outproj(lay, y, xc, z, dexp, ssd_norm_g[j][None, :], ssd_w_out[j].astype(BF16), xs, mod_l)
        else:
            w = att_w_qkv[j]
            wq = w[:, q_cols].astype(BF16)
            wkt = w[:, k_cols].T.astype(BF16)
            wv2 = w[:, v_cols].astype(BF16)
            qg = (jnp.tile(att_q_gain[j][perm], hq) * (ATT_HEAD_DIM ** -0.5))[None, :]
            kgt = att_k_gain[j][perm][:, None]
            qn, kt, v = _att_inproj(lay, xs, mod_l, g1, wq, wkt, wv2, qg, kgt, jnp.asarray(gsum, BF16),
                                    jnp.asarray(gexp, BF16), cos128, sin128, cos_t, sin_t)
            o = _att_core(lay, qn, kt, v)
            xs = _att_outproj(lay, o, att_w_o[j].astype(BF16), xs, mod_l)
        n_tiles = lay.lat_tiles if last else lay.n_tiles
        xs = _moe(lay, n_tiles, xs, mod_l, norm2_g[layer][None, :], moe_w_group[layer], moe_b_group[layer],
                  moe_w_router[layer], moe_b_router[layer], moe_w_gate[layer], moe_w_up[layer], moe_w_down[layer])
    return xs[:lay.t_lat].reshape(b, n_lat, d)
```

```python
import functools

import numpy as np
import jax
import jax.numpy as jnp
from jax import lax
from jax.experimental import pallas as pl
from jax.experimental.pallas import tpu as pltpu

F32 = jnp.float32
BF16 = jnp.bfloat16
I32 = jnp.int32
HIGHEST = lax.Precision.HIGHEST

NORM_EPS = 1e-6
N_MOD = 6
GRID_W = 64
ROPE_THETA = 10000.0

SSD_HEAD_DIM = 64
SSD_N_GROUPS = 4
SSD_D_STATE = 128
SSD_CONV = 5
SSD_CHUNK = 128

ATT_HEAD_DIM = 64
ATT_KV_HEADS = 4
ATT_KEY_BLOCK = 4352
LOG2E = 1.4426950408889634

MOE_GROUPS = 4
MOE_EPG = 8
MOE_EXPERTS = MOE_GROUPS * MOE_EPG
MOE_ROW_BLOCK = 256
ZROWS = MOE_ROW_BLOCK // 2

TM = 256
LANES = 128
SUBLANES = 8
HALO = 16
VMEM_LIMIT = 56 * 1024 * 1024


def _cparams(sem):
    return pltpu.CompilerParams(dimension_semantics=sem, vmem_limit_bytes=VMEM_LIMIT)


def _silu(v):
    return v / (1.0 + jnp.exp(-v))


def _softplus(v):
    return jnp.maximum(v, 0.0) + jnp.log1p(jnp.exp(-jnp.abs(v)))


def _norm_mod(x, g, shift, scale):
    ms = jnp.mean(x * x, axis=-1, keepdims=True)
    y = x * lax.rsqrt(ms + NORM_EPS) * g
    return y * (1.0 + scale) + shift


class _Layout:
    def __init__(self, batch, seq, n_ctx):
        assert seq % TM == 0 and n_ctx % TM == 0
        self.batch, self.seq, self.n_ctx = batch, seq, n_ctx
        self.t_lat = batch * seq
        self.t_ctx = batch * n_ctx
        self.t = self.t_lat + self.t_ctx
        self.lat_tiles = self.t_lat // TM
        self.seq_tiles = seq // TM
        self.ctx_tiles = n_ctx // TM
        self.n_tiles = self.t // TM
        self.nk = seq + n_ctx

    def mod_row(self, i):
        return jnp.where(i < self.lat_tiles, i // self.seq_tiles, self.batch)

    def tile_batch(self, i):
        return jnp.where(i < self.lat_tiles, i // self.seq_tiles, (i - self.lat_tiles) // self.ctx_tiles)

    def tile_pos(self, i):
        return jnp.where(i < self.lat_tiles, i % self.seq_tiles,
                         self.seq_tiles + (i - self.lat_tiles) % self.ctx_tiles)

    def seg_first(self, i):
        return jnp.where(i < self.lat_tiles, i % self.seq_tiles == 0, (i - self.lat_tiles) % self.ctx_tiles == 0)

    def seg_last(self, i):
        return jnp.where(i < self.lat_tiles, i % self.seq_tiles == self.seq_tiles - 1,
                         (i - self.lat_tiles) % self.ctx_tiles == self.ctx_tiles - 1)


def _mod_kernel(c_ref, w_ref, b_ref, o_ref):
    s = _silu(c_ref[...])
    o_ref[...] = jnp.dot(s, w_ref[...], precision=HIGHEST, preferred_element_type=F32) + b_ref[...]


def _mod_table(c8, mod_w, mod_b):
    depth, d, n = mod_w.shape
    tn = 1536
    assert n % tn == 0
    return pl.pallas_call(
        _mod_kernel,
        grid=(depth, n // tn),
        in_specs=[pl.BlockSpec((8, d), lambda l, j: (0, 0)),
                  pl.BlockSpec((None, d, tn), lambda l, j: (l, 0, j)),
                  pl.BlockSpec((None, 1, tn), lambda l, j: (l, 0, j))],
        out_specs=pl.BlockSpec((None, 8, tn), lambda l, j: (l, 0, j)),
        out_shape=jax.ShapeDtypeStruct((depth, 8, n), F32),
        compiler_params=_cparams(("parallel", "parallel")),
        name="mod_table",
    )(c8, mod_w, mod_b.reshape(depth, 1, n))


def _ssd_inproj_kernel(x_ref, mod_ref, g_ref, wz_ref, wx_ref, wdt_ref, wdtt_ref, dtb_ref, dtbt_ref,
                       z_ref, xbc_ref, dt_ref, dtt_ref):
    h = _norm_mod(x_ref[...], g_ref[...], mod_ref[0:1, :], mod_ref[1:2, :]).astype(BF16)
    z_ref[...] = jnp.dot(h, wz_ref[...], preferred_element_type=F32).astype(BF16)
    xbc_ref[...] = jnp.dot(h, wx_ref[...], preferred_element_type=F32).astype(BF16)
    nh = dtb_ref.shape[1] // 2
    dt = _softplus(jnp.dot(h, wdt_ref[...], preferred_element_type=F32) + dtb_ref[...])
    dt_ref[0] = dt[:, :nh]
    dt_ref[1] = dt[:, nh:]
    dtt = lax.dot_general(wdtt_ref[...], h, (((1,), (1,)), ((), ())), preferred_element_type=F32)
    dtt = _softplus(dtt + dtbt_ref[...])
    dtt_ref[0] = dtt[:nh, :]
    dtt_ref[1] = dtt[nh:, :]


def _ssd_inproj(lay, x, mod_l, g, wz, wx, wdt, wdtt, dtb, dtbt):
    t, d = x.shape
    di, dc, nh2 = wz.shape[1], wx.shape[1], wdt.shape[1]
    nh = nh2 // 2
    full = lambda a: pl.BlockSpec(a.shape, lambda i: (0,) * a.ndim)
    return pl.pallas_call(
        _ssd_inproj_kernel,
        grid=(lay.n_tiles,),
        in_specs=[pl.BlockSpec((TM, d), lambda i: (i, 0)),
                  pl.BlockSpec((None, N_MOD, d), lambda i: (lay.mod_row(i), 0, 0)),
                  full(g), full(wz), full(wx), full(wdt), full(wdtt), full(dtb), full(dtbt)],
        out_specs=[pl.BlockSpec((TM, di), lambda i: (i, 0)),
                   pl.BlockSpec((TM, dc), lambda i: (i, 0)),
                   pl.BlockSpec((2, TM, nh), lambda i: (0, i, 0)),
                   pl.BlockSpec((2, nh, TM), lambda i: (0, 0, i))],
        out_shape=[jax.ShapeDtypeStruct((t, di), BF16),
                   jax.ShapeDtypeStruct((t, dc), BF16),
                   jax.ShapeDtypeStruct((2, t, nh), F32),
                   jax.ShapeDtypeStruct((2, nh, t), F32)],
        compiler_params=_cparams(("parallel",)),
        name="ssd_inproj",
    )(x, mod_l, g, wz, wx, wdt, wdtt, dtb, dtbt)


def _ssd_conv_kernel(lay, di, gn, prev_ref, main_ref, next_ref, w_ref, b_ref, xs_ref, bm_ref, cm_ref, ext_ref):
    i = pl.program_id(0)
    dc = main_ref.shape[1]
    prev = jnp.where(lay.seg_first(i), 0.0, prev_ref[...].astype(F32))
    nxt = jnp.where(lay.seg_last(i), 0.0, next_ref[...].astype(F32))
    ext_ref[0:HALO, :] = prev
    ext_ref[HALO:HALO + TM, :] = main_ref[...].astype(F32)
    ext_ref[HALO + TM:HALO + TM + HALO, :] = nxt
    half = SSD_CONV // 2
    cw = 512
    for c0 in range(0, dc, cw):
        acc = jnp.zeros((TM, cw), F32) + b_ref[:, c0:c0 + cw]
        for k in range(SSD_CONV):
            acc = acc + ext_ref[HALO + k - half:HALO + k - half + TM, c0:c0 + cw] * w_ref[k:k + 1, c0:c0 + cw]
        y = _silu(acc).astype(BF16)
        if c0 < di:
            xs_ref[:, c0:c0 + cw] = y
        elif c0 < di + gn:
            bm_ref[:, c0 - di:c0 - di + cw] = y
        else:
            cm_ref[:, c0 - di - gn:c0 - di - gn + cw] = y


def _ssd_conv(lay, xbc, conv_w, conv_b, di, gn):
    t, dc = xbc.shape
    hb = TM // HALO
    nhb = t // HALO
    assert gn == 512 and di % 512 == 0
    return pl.pallas_call(
        functools.partial(_ssd_conv_kernel, lay, di, gn),
        grid=(lay.n_tiles,),
        in_specs=[pl.BlockSpec((HALO, dc), lambda i: (jnp.maximum(i * hb - 1, 0), 0)),
                  pl.BlockSpec((TM, dc), lambda i: (i, 0)),
                  pl.BlockSpec((HALO, dc), lambda i: (jnp.minimum((i + 1) * hb, nhb - 1), 0)),
                  pl.BlockSpec(conv_w.shape, lambda i: (0, 0)),
                  pl.BlockSpec(conv_b.shape, lambda i: (0, 0))],
        out_specs=[pl.BlockSpec((TM, di), lambda i: (i, 0)),
                   pl.BlockSpec((TM, gn), lambda i: (i, 0)),
                   pl.BlockSpec((TM, gn), lambda i: (i, 0))],
        out_shape=[jax.ShapeDtypeStruct((t, di), BF16),
                   jax.ShapeDtypeStruct((t, gn), BF16),
                   jax.ShapeDtypeStruct((t, gn), BF16)],
        scratch_shapes=[pltpu.VMEM((TM + 2 * HALO, dc), F32)],
        compiler_params=_cparams(("parallel",)),
        name="ssd_conv",
    )(xbc, xbc, xbc, conv_w, conv_b)


def _ssd_scan_kernel(xs_ref, bm_ref, cm_ref, dt_ref, dtt_ref, alr_ref, alc_ref, e_ref, y_ref, state_ref):
    d = pl.program_id(0)
    step = pl.program_id(2)
    q = SSD_CHUNK
    ng = SSD_N_GROUPS
    ns = SSD_D_STATE
    gw = xs_ref.shape[1] // ng
    hpg = gw // SSD_HEAD_DIM

    @pl.when(step == 0)
    def _():
        state_ref[...] = jnp.zeros_like(state_ref)

    sgn = jnp.where(d == 0, 1, -1)
    row = lax.broadcasted_iota(I32, (q, q), 0)
    col = lax.broadcasted_iota(I32, (q, q), 1)
    lmask = (row - col) * sgn >= 0
    lmask_t = (col - row) * sgn >= 0

    dt = dt_ref[...]
    dtt = dtt_ref[...]
    da = dt * (-jnp.exp(alr_ref[...]))
    dat = dtt * (-jnp.exp(alc_ref[...]))
    a_cum = jnp.dot(lmask.astype(F32), da, precision=HIGHEST, preferred_element_type=F32)
    a_cum_t = jnp.dot(dat, lmask_t.astype(F32), precision=HIGHEST, preferred_element_type=F32)
    a_end = jnp.sum(da, axis=0, keepdims=True)
    w_end = jnp.exp(a_end - a_cum) * dt

    e = e_ref[...]
    a_exp = jnp.dot(a_cum, e, precision=HIGHEST, preferred_element_type=F32)
    w_exp = jnp.dot(w_end, e, precision=HIGHEST, preferred_element_type=F32)
    end_exp = jnp.dot(jnp.broadcast_to(a_end, (8, a_end.shape[1])), e, precision=HIGHEST,
                      preferred_element_type=F32)[0:1, :]
    decay_in = jnp.exp(a_exp)
    chunk_decay = jnp.exp(end_exp)
    xw = (xs_ref[...].astype(F32) * w_exp).astype(BF16)

    lane = lax.broadcasted_iota(I32, (q, LANES), 1)
    for g in range(ng):
        bg = bm_ref[:, g * ns:(g + 1) * ns]
        cg = cm_ref[:, g * ns:(g + 1) * ns]
        cb = lax.dot_general(cg, bg, (((1,), (1,)), ((), ())), preferred_element_type=F32)
        s_in = state_ref[g]
        y_off = jnp.dot(cg, s_in.astype(BF16), preferred_element_type=F32)
        new_s = lax.dot_general(bg, xw[:, g * gw:(g + 1) * gw], (((0,), (0,)), ((), ())),
                                preferred_element_type=F32)
        state_ref[g] = s_in * chunk_decay[:, g * gw:(g + 1) * gw] + new_s
        for pr in range(hpg // 2):
            mixes = []
            for hh in range(2):
                h = g * hpg + 2 * pr + hh
                seg = a_cum[:, h:h + 1] - a_cum_t[h:h + 1, :]
                lh = jnp.where(lmask, jnp.exp(jnp.minimum(seg, 0.0)), 0.0)
                mixes.append((cb * lh * dtt[h:h + 1, :]).astype(BF16))
            lhs = jnp.concatenate(mixes, axis=1)
            l0 = g * gw + pr * LANES
            xp = xs_ref[:, l0:l0 + LANES]
            zero = jnp.zeros_like(xp)
            rhs = jnp.concatenate([jnp.where(lane < SSD_HEAD_DIM, xp, zero),
                                   jnp.where(lane >= SSD_HEAD_DIM, xp, zero)], axis=0)
            y_pair = jnp.dot(lhs, rhs, preferred_element_type=F32)
            y_pair = y_pair + y_off[:, pr * LANES:(pr + 1) * LANES] * decay_in[:, l0:l0 + LANES]
            y_ref[:, l0:l0 + LANES] = y_pair.astype(BF16)


def _ssd_scan(lay, xs, bm, cm, dt, dtt, a_log, expand):
    t, di = xs.shape
    gn = bm.shape[1]
    nh = dt.shape[2]
    q = SSD_CHUNK
    nct, nlt = lay.n_ctx // q, lay.seq // q
    nc = nct + nlt
    ctx_base = lay.t_lat // q

    def blk(d, b, s):
        j_ctx = jnp.where(d == 0, s, nct - 1 - s)
        j_lat = jnp.where(d == 0, s - nct, nlt - 1 - (s - nct))
        return jnp.where(s < nct, ctx_base + b * nct + j_ctx, b * nlt + j_lat)

    return pl.pallas_call(
        _ssd_scan_kernel,
        grid=(2, lay.batch, nc),
        in_specs=[pl.BlockSpec((q, di), lambda d, b, s: (blk(d, b, s), 0)),
                  pl.BlockSpec((q, gn), lambda d, b, s: (blk(d, b, s), 0)),
                  pl.BlockSpec((q, gn), lambda d, b, s: (blk(d, b, s), 0)),
                  pl.BlockSpec((None, q, nh), lambda d, b, s: (d, blk(d, b, s), 0)),
                  pl.BlockSpec((None, nh, q), lambda d, b, s: (d, 0, blk(d, b, s))),
                  pl.BlockSpec((None, 1, nh), lambda d, b, s: (d, 0, 0)),
                  pl.BlockSpec((None, nh, 1), lambda d, b, s: (d, 0, 0)),
                  pl.BlockSpec(expand.shape, lambda d, b, s: (0, 0))],
        out_specs=pl.BlockSpec((None, q, di), lambda d, b, s: (d, blk(d, b, s), 0)),
        out_shape=jax.ShapeDtypeStruct((2, t, di), BF16),
        scratch_shapes=[pltpu.VMEM((SSD_N_GROUPS, SSD_D_STATE, di // SSD_N_GROUPS), F32)],
        compiler_params=_cparams(("arbitrary", "arbitrary", "arbitrary")),
        name="ssd_scan",
    )(xs, bm, cm, dt, dtt, a_log.reshape(2, 1, nh), a_log.reshape(2, nh, 1), expand)


def _ssd_outproj_kernel(y_ref, xs_ref, z_ref, dexp_ref, ng_ref, w_ref, x_ref, mod_ref, o_ref):
    y = y_ref[0].astype(F32) + y_ref[1].astype(F32) + xs_ref[...].astype(F32) * dexp_ref[...]
    y = y * _silu(z_ref[...].astype(F32))
    gw = y.shape[1] // SSD_N_GROUPS
    parts = []
    for g in range(SSD_N_GROUPS):
        yg = y[:, g * gw:(g + 1) * gw]
        ms = jnp.mean(yg * yg, axis=-1, keepdims=True)
        parts.append((yg * lax.rsqrt(ms + NORM_EPS) * ng_ref[:, g * gw:(g + 1) * gw]).astype(BF16))
    yn = jnp.concatenate(parts, axis=1)
    m = jnp.dot(yn, w_ref[...], preferred_element_type=F32)
    o_ref[...] = x_ref[...] + mod_ref[2:3, :] * m


def _ssd_outproj(lay, y, xs, z, dexp, ng, w, x, mod_l):
    t, d = x.shape
    di = xs.shape[1]
    full = lambda a: pl.BlockSpec(a.shape, lambda i: (0,) * a.ndim)
    return pl.pallas_call(
        _ssd_outproj_kernel,
        grid=(lay.n_tiles,),
        in_specs=[pl.BlockSpec((2, TM, di), lambda i: (0, i, 0)),
                  pl.BlockSpec((TM, di), lambda i: (i, 0)),
                  pl.BlockSpec((TM, di), lambda i: (i, 0)),
                  full(dexp), full(ng), full(w),
                  pl.BlockSpec((TM, d), lambda i: (i, 0)),
                  pl.BlockSpec((None, N_MOD, d), lambda i: (lay.mod_row(i), 0, 0))],
        out_specs=pl.BlockSpec((TM, d), lambda i: (i, 0)),
        out_shape=jax.ShapeDtypeStruct((t, d), F32),
        compiler_params=_cparams(("parallel",)),
        name="ssd_outproj",
    )(y, xs, z, dexp, ng, w, x, mod_l)


def _att_inproj_kernel(x_ref, mod_ref, g_ref, wq_ref, wkt_ref, wv_ref, vone_ref, qg_ref, kgt_ref, gsum_ref, gexp_ref,
                       cos_ref, sin_ref, cost_ref, sint_ref, q_ref, kt_ref, v_ref):
    h = _norm_mod(x_ref[...], g_ref[...], mod_ref[0:1, :], mod_ref[1:2, :]).astype(BF16)
    hd = ATT_HEAD_DIM
    q = jnp.dot(h, wq_ref[...], preferred_element_type=F32)
    ssum = jnp.dot((q * q).astype(BF16), gsum_ref[...], preferred_element_type=F32)
    r = lax.rsqrt(ssum * (1.0 / hd) + NORM_EPS)
    r_hi = r.astype(BF16)
    r_lo = (r - r_hi.astype(F32)).astype(BF16)
    r_exp = jnp.dot(jnp.concatenate([r_hi, r_lo], axis=1), gexp_ref[...], preferred_element_type=F32)
    qn = q * r_exp * qg_ref[...]
    nl = qn.shape[1]
    lane = lax.broadcasted_iota(I32, qn.shape, 1)
    partner = jnp.where((lane & 1) == 0, pltpu.roll(qn, nl - 1, 1), pltpu.roll(qn, 1, 1))
    reps = nl // LANES
    cos = jnp.concatenate([cos_ref[...]] * reps, axis=1)
    sin = jnp.concatenate([sin_ref[...]] * reps, axis=1)
    q_ref[...] = (qn * cos + partner * sin).astype(BF16)
    kt = lax.dot_general(wkt_ref[...], h, (((1,), (1,)), ((), ())), preferred_element_type=F32)
    ct, st = cost_ref[...], sint_ref[...]
    sub = lax.broadcasted_iota(I32, ct.shape, 0)
    for kh in range(ATT_KV_HEADS):
        blk = kt[kh * hd:(kh + 1) * hd]
        rk = lax.rsqrt(jnp.sum(blk * blk, axis=0, keepdims=True) * (1.0 / hd) + NORM_EPS)
        kn = blk * rk * kgt_ref[...]
        kpart = jnp.where((sub & 1) == 0, pltpu.roll(kn, hd - 1, 0), pltpu.roll(kn, 1, 0))
        kt_ref[kh] = (kn * ct + kpart * st).astype(BF16)
    v = (jnp.dot(h, wv_ref[...], preferred_element_type=F32) + vone_ref[...]).astype(BF16)
    for kh in range(ATT_KV_HEADS):
        v_ref[kh] = v[:, kh * LANES:(kh + 1) * LANES]


def _att_inproj(lay, x, mod_l, g, wq, wkt, wv2, vone, qg, kgt, gsum, gexp, cos, sin, cost, sint):
    t, d = x.shape
    dq = wq.shape[1]
    hd = ATT_HEAD_DIM
    full = lambda a: pl.BlockSpec(a.shape, lambda i: (0,) * a.ndim)
    return pl.pallas_call(
        _att_inproj_kernel,
        grid=(lay.n_tiles,),
        in_specs=[pl.BlockSpec((TM, d), lambda i: (i, 0)),
                  pl.BlockSpec((None, N_MOD, d), lambda i: (lay.mod_row(i), 0, 0)),
                  full(g), full(wq), full(wkt), full(wv2), full(vone), full(qg), full(kgt), full(gsum), full(gexp),
                  pl.BlockSpec((TM, LANES), lambda i: (lay.tile_pos(i), 0)),
                  pl.BlockSpec((TM, LANES), lambda i: (lay.tile_pos(i), 0)),
                  pl.BlockSpec((hd, TM), lambda i: (0, lay.tile_pos(i))),
                  pl.BlockSpec((hd, TM), lambda i: (0, lay.tile_pos(i)))],
        out_specs=[pl.BlockSpec((TM, dq), lambda i: (i, 0)),
                   pl.BlockSpec((None, ATT_KV_HEADS, hd, TM), lambda i: (lay.tile_batch(i), 0, 0, lay.tile_pos(i))),
                   pl.BlockSpec((None, ATT_KV_HEADS, TM, LANES), lambda i: (lay.tile_batch(i), 0, lay.tile_pos(i), 0))],
        out_shape=[jax.ShapeDtypeStruct((t, dq), BF16),
                   jax.ShapeDtypeStruct((lay.batch, ATT_KV_HEADS, hd, lay.nk), BF16),
                   jax.ShapeDtypeStruct((lay.batch, ATT_KV_HEADS, lay.nk, LANES), BF16)],
        compiler_params=_cparams(("parallel",)),
        name="att_inproj",
    )(x, mod_l, g, wq, wkt, wv2, vone, qg, kgt, gsum, gexp, cos, sin, cost, sint)


def _attend(q_ref, kt_ref, v_ref, o_ref, k0, n_keys):
    hd = ATT_HEAD_DIM
    tm = q_ref.shape[0]
    heads = q_ref.shape[1] // hd
    qs = [q_ref[:, u * hd:(u + 1) * hd] for u in range(heads)]

    bk = min(ATT_KEY_BLOCK, n_keys)
    assert n_keys % bk == 0 and k0 % LANES == 0 and bk % LANES == 0

    def body(j, carry):
        start = pl.multiple_of(k0 + j * bk, LANES)
        kb = kt_ref[:, pl.ds(start, bk)]
        vb = v_ref[pl.ds(start, bk), :]
        out = []
        for u in range(heads):
            m_old, acc = carry[u]
            s = jnp.dot(qs[u], kb, preferred_element_type=F32)
            m_new = jnp.maximum(m_old, jnp.max(s, axis=-1, keepdims=True))
            p = jnp.exp2((s - m_new).astype(BF16))
            acc = jnp.exp2(m_old - m_new) * acc + jnp.dot(p, vb, preferred_element_type=F32)
            out.append((m_new, acc))
        return tuple(out)

    init = tuple((jnp.full((tm, 1), -jnp.inf, F32), jnp.zeros((tm, LANES), F32)) for _ in range(heads))
    final = lax.fori_loop(0, n_keys // bk, body, init) if n_keys > bk else body(0, init)
    outs = [acc[:, :hd] / acc[:, hd:hd + 1] for _, acc in final]
    o_ref[...] = jnp.concatenate(outs, axis=1).astype(BF16)


def _att_core_kernel(lay, q_ref, kt_ref, v_ref, o_ref):
    qi = pl.program_id(2)

    @pl.when(qi < lay.seq_tiles)
    def _():
        _attend(q_ref, kt_ref, v_ref, o_ref, 0, lay.nk)

    @pl.when(qi >= lay.seq_tiles)
    def _():
        _attend(q_ref, kt_ref, v_ref, o_ref, lay.seq, lay.n_ctx)


def _att_core(lay, qn, kt, v):
    t, dq = qn.shape
    hd = ATT_HEAD_DIM
    qw = dq // ATT_KV_HEADS
    per_b = lay.seq_tiles + lay.ctx_tiles

    def row_tile(b, qi):
        return jnp.where(qi < lay.seq_tiles, b * lay.seq_tiles + qi,
                         lay.lat_tiles + b * lay.ctx_tiles + (qi - lay.seq_tiles))

    return pl.pallas_call(
        functools.partial(_att_core_kernel, lay),
        grid=(lay.batch, ATT_KV_HEADS, per_b),
        in_specs=[pl.BlockSpec((TM, qw), lambda b, kh, qi: (row_tile(b, qi), kh)),
                  pl.BlockSpec((None, None, hd, lay.nk), lambda b, kh, qi: (b, kh, 0, 0)),
                  pl.BlockSpec((None, None, lay.nk, LANES), lambda b, kh, qi: (b, kh, 0, 0))],
        out_specs=pl.BlockSpec((TM, qw), lambda b, kh, qi: (row_tile(b, qi), kh)),
        out_shape=jax.ShapeDtypeStruct((t, dq), BF16),
        compiler_params=_cparams(("parallel", "parallel", "arbitrary")),
        name="att_core",
    )(qn, kt, v)


def _att_outproj_kernel(o_ref, w_ref, x_ref, mod_ref, out_ref):
    m = jnp.dot(o_ref[...], w_ref[...], preferred_element_type=F32)
    out_ref[...] = x_ref[...] + mod_ref[2:3, :] * m


def _att_outproj(lay, o, w, x, mod_l):
    t, d = x.shape
    return pl.pallas_call(
        _att_outproj_kernel,
        grid=(lay.n_tiles,),
        in_specs=[pl.BlockSpec((TM, o.shape[1]), lambda i: (i, 0)),
                  pl.BlockSpec(w.shape, lambda i: (0, 0)),
                  pl.BlockSpec((TM, d), lambda i: (i, 0)),
                  pl.BlockSpec((None, N_MOD, d), lambda i: (lay.mod_row(i), 0, 0))],
        out_specs=pl.BlockSpec((TM, d), lambda i: (i, 0)),
        out_shape=jax.ShapeDtypeStruct((t, d), F32),
        compiler_params=_cparams(("parallel",)),
        name="att_outproj",
    )(o, w, x, mod_l)


R_EID, R_RANK, R_W = 0, 2, 4
GROUP_LANE0 = MOE_EXPERTS


def _router_kernel(x_ref, mod_ref, g_ref, wr_ref, br_ref, f_ref, r_ref, cnt_ref, base_ref):
    i = pl.program_id(0)

    @pl.when(i == 0)
    def _():
        base_ref[...] = jnp.zeros_like(base_ref)

    f = _norm_mod(x_ref[...], g_ref[...], mod_ref[3:4, :], mod_ref[4:5, :])
    f_ref[...] = f
    logits = jnp.dot(f, wr_ref[...], precision=HIGHEST, preferred_element_type=F32) + br_ref[...]
    lane = lax.broadcasted_iota(I32, logits.shape, 1)
    neg = jnp.float32(-jnp.inf)
    big = jnp.int32(LANES)

    def first_max(vals):
        top = jnp.max(vals, axis=-1, keepdims=True)
        idx = jnp.min(jnp.where(vals == top, lane, big), axis=-1, keepdims=True)
        return top, idx

    g_mask = (lane >= GROUP_LANE0) & (lane < GROUP_LANE0 + MOE_GROUPS)
    glog = jnp.where(g_mask, logits, neg)
    g_top, g_idx = first_max(glog)
    g_w = 1.0 / jnp.sum(jnp.exp(glog - g_top), axis=-1, keepdims=True)
    e0 = (g_idx - GROUP_LANE0) * MOE_EPG
    elog = jnp.where((lane >= e0) & (lane < e0 + MOE_EPG), logits, neg)
    v1, i1 = first_max(elog)
    v2, i2 = first_max(jnp.where(lane == i1, neg, elog))
    ex = jnp.exp(v2 - v1)
    w1 = g_w / (1.0 + ex)
    w2 = g_w * ex / (1.0 + ex)

    oh1 = (lane == i1).astype(F32)
    oh2 = (lane == i2).astype(F32)
    tm = logits.shape[0]
    rr = lax.broadcasted_iota(I32, (tm, tm), 0)
    cc = lax.broadcasted_iota(I32, (tm, tm), 1)
    before = (cc < rr).astype(BF16)
    cum1 = jnp.dot(before, oh1.astype(BF16), preferred_element_type=F32)
    cum2 = jnp.dot(before, oh2.astype(BF16), preferred_element_type=F32)
    tot1 = jnp.sum(oh1, axis=0, keepdims=True)
    tot2 = jnp.sum(oh2, axis=0, keepdims=True)
    base = base_ref[...]
    rank1 = jnp.sum(oh1 * (cum1 + base), axis=-1, keepdims=True)
    rank2 = jnp.sum(oh2 * (cum2 + base + tot1), axis=-1, keepdims=True)
    base = base + tot1 + tot2
    base_ref[...] = base
    cnt_ref[...] = base

    rec = jnp.zeros(logits.shape, F32)
    for k, val in ((R_EID, i1.astype(F32)), (R_EID + 1, i2.astype(F32)), (R_RANK, rank1), (R_RANK + 1, rank2),
                   (R_W, w1), (R_W + 1, w2)):
        rec = jnp.where(lane == k, val, rec)
    r_ref[...] = rec


def _router(lay, n_tiles, x, mod_l, g, wr, br):
    t, d = x.shape
    return pl.pallas_call(
        _router_kernel,
        grid=(n_tiles,),
        in_specs=[pl.BlockSpec((TM, d), lambda i: (i, 0)),
                  pl.BlockSpec((None, N_MOD, d), lambda i: (lay.mod_row(i), 0, 0)),
                  pl.BlockSpec(g.shape, lambda i: (0, 0)),
                  pl.BlockSpec(wr.shape, lambda i: (0, 0)),
                  pl.BlockSpec(br.shape, lambda i: (0, 0))],
        out_specs=[pl.BlockSpec((TM, d), lambda i: (i, 0)),
                   pl.BlockSpec((TM, LANES), lambda i: (i, 0)),
                   pl.BlockSpec((1, LANES), lambda i: (0, 0))],
        out_shape=[jax.ShapeDtypeStruct((n_tiles * TM, d), F32),
                   jax.ShapeDtypeStruct((n_tiles * TM, LANES), F32),
                   jax.ShapeDtypeStruct((1, LANES), F32)],
        scratch_shapes=[pltpu.VMEM((1, LANES), F32)],
        compiler_params=_cparams(("arbitrary",)),
        name="moe_router",
    )(x, mod_l, g, wr, br)


def _dispatch_kernel(fill_ref, plen_ref, dest_hbm, f_ref, xs_hbm, dest_s, zbuf, dsem, fsem, ssem):
    i = pl.program_id(0)
    n = pl.num_programs(0)

    def dest_copy(j):
        return pltpu.make_async_copy(dest_hbm.at[j], dest_s.at[j % 2], dsem.at[j % 2])

    def pad_pieces(e, fn):
        first = fill_ref[e]
        head = jnp.minimum(plen_ref[e], (-first) & (SUBLANES - 1))
        for h in range(SUBLANES - 1):
            @pl.when(h < head)
            def _():
                fn(pltpu.make_async_copy(zbuf.at[pl.ds(0, 1), :], xs_hbm.at[pl.ds(first + h, 1), :], fsem.at[0]))

        base = first + head
        rest = plen_ref[e] - head
        for bit in range(ZROWS.bit_length() - 1, SUBLANES.bit_length() - 2, -1):
            size = 1 << bit

            @pl.when((rest & size) != 0)
            def _():
                start = pl.multiple_of(base + ((rest >> (bit + 1)) << (bit + 1)), SUBLANES)
                fn(pltpu.make_async_copy(zbuf.at[pl.ds(0, size), :], xs_hbm.at[pl.ds(start, size), :], fsem.at[0]))

    @pl.when(i == 0)
    def _():
        zbuf[...] = jnp.zeros_like(zbuf)
        dest_copy(0).start()

        def start_e(e, c):
            pad_pieces(e, lambda cp: cp.start())
            return c

        def wait_e(e, c):
            pad_pieces(e, lambda cp: cp.wait())
            return c

        def tail_pieces(fn):
            def piece(t, c):
                start = pl.multiple_of(fill_ref[MOE_EXPERTS] + t * ZROWS, ZROWS)
                fn(pltpu.make_async_copy(zbuf, xs_hbm.at[pl.ds(start, ZROWS), :], fsem.at[0]))
                return c
            lax.fori_loop(0, plen_ref[MOE_EXPERTS] // ZROWS, piece, 0)

        lax.fori_loop(0, MOE_EXPERTS, start_e, 0)
        tail_pieces(lambda cp: cp.start())
        lax.fori_loop(0, MOE_EXPERTS, wait_e, 0)
        tail_pieces(lambda cp: cp.wait())

    @pl.when(i + 1 < n)
    def _():
        dest_copy(i + 1).start()

    dest_copy(i).wait()
    slot = i % 2

    def body(r, c):
        for k in range(2):
            row = dest_s[slot, 2 * r + k]
            pltpu.make_async_copy(f_ref.at[pl.ds(r, 1), :], xs_hbm.at[pl.ds(row, 1), :], ssem.at[0]).start()
        return c

    lax.fori_loop(0, TM, body, 0, unroll=8)
    for k in range(2):
        pltpu.make_async_copy(f_ref, xs_hbm.at[pl.ds(0, TM), :], ssem.at[0]).wait()


def _dispatch(fill, plen, dest, f, cap):
    n_tiles = dest.shape[0]
    d = f.shape[1]
    grid_spec = pltpu.PrefetchScalarGridSpec(
        num_scalar_prefetch=2,
        grid=(n_tiles,),
        in_specs=[pl.BlockSpec(memory_space=pl.ANY),
                  pl.BlockSpec((TM, d), lambda i, fill, plen: (i, 0))],
        out_specs=pl.BlockSpec(memory_space=pl.ANY),
        scratch_shapes=[pltpu.SMEM((2, 2 * TM), I32),
                        pltpu.VMEM((ZROWS, d), F32),
                        pltpu.SemaphoreType.DMA((2,)),
                        pltpu.SemaphoreType.DMA((1,)),
                        pltpu.SemaphoreType.DMA((1,))],
    )
    return pl.pallas_call(
        _dispatch_kernel,
        grid_spec=grid_spec,
        out_shape=jax.ShapeDtypeStruct((cap, d), F32),
        compiler_params=_cparams(("arbitrary",)),
        name="moe_dispatch",
    )(fill, plen, dest, f)


def _expert_kernel(be_ref, nv_ref, blk_ref, x_ref, wg_ref, wu_ref, wd_ref, y_ref, wgb, wub, wdb):
    i = pl.program_id(0)
    live = nv_ref[i] > 0
    changed = jnp.logical_or(i == 0, be_ref[i] != be_ref[jnp.maximum(i - 1, 0)])

    @pl.when(jnp.logical_and(live, changed))
    def _():
        wgb[...] = wg_ref[...].astype(BF16)
        wub[...] = wu_ref[...].astype(BF16)
        wdb[...] = wd_ref[...].astype(BF16)

    @pl.when(live)
    def _():
        xb = x_ref[...].astype(BF16)
        hg = jnp.dot(xb, wgb[...], preferred_element_type=F32)
        hu = jnp.dot(xb, wub[...], preferred_element_type=F32)
        act = (_silu(hg) * hu).astype(BF16)
        y_ref[...] = jnp.dot(act, wdb[...], preferred_element_type=F32)

    @pl.when(jnp.logical_not(live))
    def _():
        y_ref[...] = jnp.zeros_like(y_ref)


def _experts(layer, block_e, block_nv, block_src, xs, wg, wu, wd):
    n_blocks = block_e.shape[0]
    d = xs.shape[1]
    dff = wg.shape[3]
    rb = MOE_ROW_BLOCK
    grid_spec = pltpu.PrefetchScalarGridSpec(
        num_scalar_prefetch=3,
        grid=(n_blocks,),
        in_specs=[pl.BlockSpec((rb, d), lambda i, be, nv, src: (src[i], 0)),
                  pl.BlockSpec((None, None, d, dff), lambda i, be, nv, src: (layer, be[i], 0, 0)),
                  pl.BlockSpec((None, None, d, dff), lambda i, be, nv, src: (layer, be[i], 0, 0)),
                  pl.BlockSpec((None, None, dff, d), lambda i, be, nv, src: (layer, be[i], 0, 0))],
        out_specs=pl.BlockSpec((rb, d), lambda i, be, nv, src: (i, 0)),
        scratch_shapes=[pltpu.VMEM((d, dff), BF16),
                        pltpu.VMEM((d, dff), BF16),
                        pltpu.VMEM((dff, d), BF16)],
    )
    return pl.pallas_call(
        _expert_kernel,
        grid_spec=grid_spec,
        out_shape=jax.ShapeDtypeStruct(xs.shape, F32),
        compiler_params=_cparams(("arbitrary",)),
        name="moe_experts",
    )(block_e, block_nv, block_src, xs, wg, wu, wd)


def _combine_kernel(dest_hbm, x_ref, r_ref, mod_ref, ys_hbm, o_ref, dest_s, gbuf, dsem, gsem):
    i = pl.program_id(0)
    n = pl.num_programs(0)

    def dest_copy(j):
        return pltpu.make_async_copy(dest_hbm.at[j], dest_s.at[j % 3], dsem.at[j % 3])

    def gather_tile(j):
        def body(r, c):
            for k in range(2):
                row = dest_s[j % 3, 2 * r + k]
                pltpu.make_async_copy(ys_hbm.at[pl.ds(row, 1), :], gbuf.at[j % 2, k, pl.ds(r, 1), :],
                                      gsem.at[j % 2]).start()
            return c
        lax.fori_loop(0, TM, body, 0, unroll=8)

    @pl.when(i == 0)
    def _():
        dest_copy(0).start()
        dest_copy(0).wait()
        gather_tile(0)

        @pl.when(n > 1)
        def _():
            dest_copy(1).start()

    @pl.when(i + 1 < n)
    def _():
        dest_copy(i + 1).wait()

        @pl.when(i + 2 < n)
        def _():
            dest_copy(i + 2).start()

        gather_tile(i + 1)

    slot = i % 2
    for k in range(2):
        pltpu.make_async_copy(ys_hbm.at[pl.ds(0, TM), :], gbuf.at[slot, k], gsem.at[slot]).wait()
    w1 = r_ref[:, R_W:R_W + 1]
    w2 = r_ref[:, R_W + 1:R_W + 2]
    o_ref[...] = x_ref[...] + mod_ref[5:6, :] * (gbuf[slot, 0] * w1 + gbuf[slot, 1] * w2)


def _combine(lay, n_tiles, dest, x, rec, mod_l, ys):
    d = x.shape[1]
    return pl.pallas_call(
        _combine_kernel,
        grid=(n_tiles,),
        in_specs=[pl.BlockSpec(memory_space=pl.ANY),
                  pl.BlockSpec((TM, d), lambda i: (i, 0)),
                  pl.BlockSpec((TM, LANES), lambda i: (i, 0)),
                  pl.BlockSpec((None, N_MOD, d), lambda i: (lay.mod_row(i), 0, 0)),
                  pl.BlockSpec(memory_space=pl.ANY)],
        out_specs=pl.BlockSpec((TM, d), lambda i: (i, 0)),
        out_shape=jax.ShapeDtypeStruct((n_tiles * TM, d), F32),
        scratch_shapes=[pltpu.SMEM((3, 2 * TM), I32),
                        pltpu.VMEM((2, 2, TM, d), F32),
                        pltpu.SemaphoreType.DMA((3,)),
                        pltpu.SemaphoreType.DMA((2,))],
        compiler_params=_cparams(("arbitrary",)),
        name="moe_combine",
    )(dest, x, rec, mod_l, ys)


def _moe(lay, layer, n_tiles, x, mod_l, g2, w_group, b_group, w_router, b_router, wg, wu, wd):
    d = x.shape[1]
    t_tok = n_tiles * TM
    pad = LANES - MOE_EXPERTS - MOE_GROUPS
    wr = jnp.concatenate([w_router, w_group, jnp.zeros((d, pad), F32)], axis=1)
    br = jnp.concatenate([b_router, b_group, jnp.zeros((pad,), F32)])[None, :]
    f, rec, cnt = _router(lay, n_tiles, x, mod_l, g2, wr, br)

    rb = MOE_ROW_BLOCK
    eid = rec[:, R_EID:R_EID + 2].astype(I32)
    rank = rec[:, R_RANK:R_RANK + 2].astype(I32)
    counts = cnt[0, :MOE_EXPERTS].astype(I32)
    padded = (counts + rb - 1) // rb * rb
    pad_ends = jnp.cumsum(padded)
    pad_starts = pad_ends - padded
    dest = (pad_starts[eid] + rank).reshape(n_tiles, 2 * TM)
    n_blocks = -(-(2 * t_tok) // rb) + MOE_EXPERTS
    used = pad_ends[-1] // rb
    idx = jnp.arange(n_blocks, dtype=I32)
    block_src = jnp.minimum(idx, used - 1)
    start = block_src * rb
    block_e = jnp.sum(pad_ends[None, :] <= start[:, None], axis=1).astype(I32)
    block_nv = jnp.where(idx < used, jnp.clip(counts[block_e] - (start - pad_starts[block_e]), 0, rb), 0).astype(I32)

    cap = n_blocks * rb
    fill = jnp.concatenate([pad_starts + counts, pad_ends[-1:]]).astype(I32)
    plen = jnp.concatenate([padded - counts, cap - pad_ends[-1:]]).astype(I32)
    xs = _dispatch(fill, plen, dest, f, cap)
    ys = _experts(layer, block_e, block_nv, block_src, xs, wg, wu, wd)
    return _combine(lay, n_tiles, dest, x, rec, mod_l, ys)


def _rope_tables(lay):
    half = ATT_HEAD_DIM // 2
    pos = jnp.arange(lay.seq)
    rowp = (pos // GRID_W).astype(F32)
    colp = (pos % GRID_W).astype(F32)
    freqs = ROPE_THETA ** (-jnp.arange(0, half, 2, dtype=F32) / half)
    ang = jnp.concatenate([rowp[:, None] * freqs, colp[:, None] * freqs], axis=-1)
    cos = jnp.concatenate([jnp.cos(ang), jnp.ones((lay.n_ctx, half), F32)], axis=0)
    sin = jnp.concatenate([jnp.sin(ang), jnp.zeros((lay.n_ctx, half), F32)], axis=0)
    cos_d = jnp.repeat(cos, 2, axis=1)
    sin_d = jnp.repeat(sin, 2, axis=1) * jnp.tile(jnp.asarray([-1.0, 1.0], F32), half)
    reps = LANES // ATT_HEAD_DIM
    return jnp.tile(cos_d, (1, reps)), jnp.tile(sin_d, (1, reps)), cos_d.T, sin_d.T


def kernel(x, c, ctx, c_ctx, mod_w, mod_b, norm1_g, norm2_g, ssd_w_in, ssd_conv_w, ssd_conv_b, ssd_dt_bias,
           ssd_a_log, ssd_d, ssd_norm_g, ssd_w_out, att_w_qkv, att_q_gain, att_k_gain, att_w_o, moe_w_group,
           moe_b_group, moe_w_router, moe_b_router, moe_w_gate, moe_w_up, moe_w_down):
    b, n_lat, d = x.shape
    n_ctx = ctx.shape[1]
    depth = mod_w.shape[0]
    lay = _Layout(b, n_lat, n_ctx)
    assert b + 1 <= 8

    xs = jnp.concatenate([x.reshape(lay.t_lat, d), ctx.reshape(lay.t_ctx, d)], axis=0)
    c8 = jnp.concatenate([c, c_ctx[None, :], jnp.zeros((8 - b - 1, d), F32)], axis=0)
    mods = _mod_table(c8, mod_w, mod_b).reshape(depth, 8, N_MOD, d)

    nh = ssd_dt_bias.shape[2]
    di = nh * SSD_HEAD_DIM
    gn = SSD_N_GROUPS * SSD_D_STATE
    dc = di + 2 * gn
    expand = jnp.repeat(jnp.eye(nh, dtype=F32), SSD_HEAD_DIM, axis=1)

    hq = att_w_qkv.shape[2] // ATT_HEAD_DIM - 2 * ATT_KV_HEADS
    dq = hq * ATT_HEAD_DIM
    dkv = ATT_KV_HEADS * ATT_HEAD_DIM
    gsum = np.zeros((dq, LANES), np.float32)
    gsum[np.arange(dq), np.arange(dq) // ATT_HEAD_DIM] = 1.0
    gexp = np.concatenate([gsum.T, gsum.T], axis=0)
    vone = np.zeros((1, ATT_KV_HEADS * LANES), np.float32)
    vone[0, np.arange(ATT_KV_HEADS) * LANES + ATT_HEAD_DIM] = 1.0
    cos128, sin128, cos_t, sin_t = _rope_tables(lay)

    for layer in range(depth):
        last = layer == depth - 1
        mod_l = mods[layer]
        j = layer // 2
        g1 = norm1_g[layer][None, :]
        if layer % 2 == 0:
            w_in = ssd_w_in[j].astype(BF16)
            wz, wx, wdt = w_in[:, :di], w_in[:, di:di + dc], w_in[:, di + dc:]
            dtb = ssd_dt_bias[j].reshape(1, 2 * nh)
            z, xbc, dt, dtt = _ssd_inproj(lay, xs, mod_l, g1, wz, wx, wdt, wdt.T, dtb, dtb.T)
            xc, bm, cm = _ssd_conv(lay, xbc, ssd_conv_w[j], ssd_conv_b[j][None, :], di, gn)
            y = _ssd_scan(lay, xc, bm, cm, dt, dtt, ssd_a_log[j], expand)
            dexp = jnp.repeat(ssd_d[j], SSD_HEAD_DIM)[None, :]
            xs = _ssd_outproj(lay, y, xc, z, dexp, ssd_norm_g[j][None, :], ssd_w_out[j].astype(BF16), xs, mod_l)
        else:
            w = att_w_qkv[j]
            wq = w[:, :dq].astype(BF16)
            wkt = w[:, dq:dq + dkv].T.astype(BF16)
            wv = w[:, dq + dkv:].reshape(d, ATT_KV_HEADS, ATT_HEAD_DIM)
            wv2 = jnp.pad(wv, ((0, 0), (0, 0), (0, LANES - ATT_HEAD_DIM))).reshape(d, ATT_KV_HEADS * LANES).astype(BF16)
            qg = (jnp.tile(att_q_gain[j], hq) * (ATT_HEAD_DIM ** -0.5 * LOG2E))[None, :]
            kgt = att_k_gain[j][:, None]
            qn, kt, v = _att_inproj(lay, xs, mod_l, g1, wq, wkt, wv2, jnp.asarray(vone), qg, kgt, jnp.asarray(gsum, BF16),
                                    jnp.asarray(gexp, BF16), cos128, sin128, cos_t, sin_t)
            o = _att_core(lay, qn, kt, v)
            xs = _att_outproj(lay, o, att_w_o[j].astype(BF16), xs, mod_l)
        n_tiles = lay.lat_tiles if last else lay.n_tiles
        xs = _moe(lay, layer, n_tiles, xs, mod_l, norm2_g[layer][None, :], moe_w_group[layer], moe_b_group[layer],
                  moe_w_router[layer], moe_b_router[layer], moe_w_gate, moe_w_up, moe_w_down)
    return xs[:lay.t_lat].reshape(b, n_lat, d)
```

```python
import functools

import numpy as np
import jax
import jax.numpy as jnp
from jax import lax
from jax.experimental import pallas as pl
from jax.experimental.pallas import tpu as pltpu

F32 = jnp.float32
BF16 = jnp.bfloat16
I32 = jnp.int32
HIGHEST = lax.Precision.HIGHEST

NORM_EPS = 1e-6
N_MOD = 6
GRID_W = 64
ROPE_THETA = 10000.0

SSD_HEAD_DIM = 64
SSD_N_GROUPS = 4
SSD_D_STATE = 128
SSD_CONV = 5
SSD_CHUNK = 128

ATT_HEAD_DIM = 64
ATT_KV_HEADS = 4
ATT_KEY_BLOCK = 4352
LOG2E = 1.4426950408889634

MOE_GROUPS = 4
MOE_EPG = 8
MOE_EXPERTS = MOE_GROUPS * MOE_EPG
MOE_ROW_BLOCK = 256
ZROWS = MOE_ROW_BLOCK // 2
DEST_ROW = 1024

TM = 256
LANES = 128
SUBLANES = 8
HALO = 16
VMEM_LIMIT = 56 * 1024 * 1024


def _cparams(sem):
    return pltpu.CompilerParams(dimension_semantics=sem, vmem_limit_bytes=VMEM_LIMIT)


def _silu(v):
    return v / (1.0 + jnp.exp(-v))


def _softplus(v):
    return jnp.maximum(v, 0.0) + jnp.log1p(jnp.exp(-jnp.abs(v)))


def _norm_mod(x, g, shift, scale):
    ms = jnp.mean(x * x, axis=-1, keepdims=True)
    y = x * lax.rsqrt(ms + NORM_EPS) * g
    return y * (1.0 + scale) + shift


class _Layout:
    def __init__(self, batch, seq, n_ctx):
        assert seq % TM == 0 and n_ctx % TM == 0
        self.batch, self.seq, self.n_ctx = batch, seq, n_ctx
        self.t_lat = batch * seq
        self.t_ctx = batch * n_ctx
        self.t = self.t_lat + self.t_ctx
        self.lat_tiles = self.t_lat // TM
        self.seq_tiles = seq // TM
        self.ctx_tiles = n_ctx // TM
        self.n_tiles = self.t // TM
        self.nk = seq + n_ctx

    def mod_row(self, i):
        return jnp.where(i < self.lat_tiles, i // self.seq_tiles, self.batch)

    def tile_batch(self, i):
        return jnp.where(i < self.lat_tiles, i // self.seq_tiles, (i - self.lat_tiles) // self.ctx_tiles)

    def tile_pos(self, i):
        return jnp.where(i < self.lat_tiles, i % self.seq_tiles,
                         self.seq_tiles + (i - self.lat_tiles) % self.ctx_tiles)

    def seg_first(self, i):
        return jnp.where(i < self.lat_tiles, i % self.seq_tiles == 0, (i - self.lat_tiles) % self.ctx_tiles == 0)

    def seg_last(self, i):
        return jnp.where(i < self.lat_tiles, i % self.seq_tiles == self.seq_tiles - 1,
                         (i - self.lat_tiles) % self.ctx_tiles == self.ctx_tiles - 1)


def _mod_kernel(c_ref, w_ref, b_ref, o_ref):
    s = _silu(c_ref[...])
    o_ref[...] = jnp.dot(s, w_ref[...], precision=HIGHEST, preferred_element_type=F32) + b_ref[...]


def _mod_table(c8, mod_w, mod_b):
    depth, d, n = mod_w.shape
    tn = 1536
    assert n % tn == 0
    return pl.pallas_call(
        _mod_kernel,
        grid=(depth, n // tn),
        in_specs=[pl.BlockSpec((8, d), lambda l, j: (0, 0)),
                  pl.BlockSpec((None, d, tn), lambda l, j: (l, 0, j)),
                  pl.BlockSpec((None, 1, tn), lambda l, j: (l, 0, j))],
        out_specs=pl.BlockSpec((None, 8, tn), lambda l, j: (l, 0, j)),
        out_shape=jax.ShapeDtypeStruct((depth, 8, n), F32),
        compiler_params=_cparams(("parallel", "parallel")),
        name="mod_table",
    )(c8, mod_w, mod_b.reshape(depth, 1, n))


def _ssd_inproj_kernel(x_ref, mod_ref, g_ref, wz_ref, wx_ref, wdt_ref, wdtt_ref, dtb_ref, dtbt_ref,
                       z_ref, xbc_ref, dt_ref, dtt_ref):
    h = _norm_mod(x_ref[...], g_ref[...], mod_ref[0:1, :], mod_ref[1:2, :]).astype(BF16)
    z_ref[...] = jnp.dot(h, wz_ref[...], preferred_element_type=F32).astype(BF16)
    xbc_ref[...] = jnp.dot(h, wx_ref[...], preferred_element_type=F32).astype(BF16)
    nh = dtb_ref.shape[1] // 2
    dt = _softplus(jnp.dot(h, wdt_ref[...], preferred_element_type=F32) + dtb_ref[...])
    dt_ref[0] = dt[:, :nh]
    dt_ref[1] = dt[:, nh:]
    dtt = lax.dot_general(wdtt_ref[...], h, (((1,), (1,)), ((), ())), preferred_element_type=F32)
    dtt = _softplus(dtt + dtbt_ref[...])
    dtt_ref[0] = dtt[:nh, :]
    dtt_ref[1] = dtt[nh:, :]


def _ssd_inproj(lay, x, mod_l, g, wz, wx, wdt, wdtt, dtb, dtbt):
    t, d = x.shape
    di, dc, nh2 = wz.shape[1], wx.shape[1], wdt.shape[1]
    nh = nh2 // 2
    full = lambda a: pl.BlockSpec(a.shape, lambda i: (0,) * a.ndim)
    return pl.pallas_call(
        _ssd_inproj_kernel,
        grid=(lay.n_tiles,),
        in_specs=[pl.BlockSpec((TM, d), lambda i: (i, 0)),
                  pl.BlockSpec((None, N_MOD, d), lambda i: (lay.mod_row(i), 0, 0)),
                  full(g), full(wz), full(wx), full(wdt), full(wdtt), full(dtb), full(dtbt)],
        out_specs=[pl.BlockSpec((TM, di), lambda i: (i, 0)),
                   pl.BlockSpec((TM, dc), lambda i: (i, 0)),
                   pl.BlockSpec((2, TM, nh), lambda i: (0, i, 0)),
                   pl.BlockSpec((2, nh, TM), lambda i: (0, 0, i))],
        out_shape=[jax.ShapeDtypeStruct((t, di), BF16),
                   jax.ShapeDtypeStruct((t, dc), BF16),
                   jax.ShapeDtypeStruct((2, t, nh), F32),
                   jax.ShapeDtypeStruct((2, nh, t), F32)],
        compiler_params=_cparams(("parallel",)),
        name="ssd_inproj",
    )(x, mod_l, g, wz, wx, wdt, wdtt, dtb, dtbt)


def _ssd_conv_kernel(lay, di, gn, prev_ref, main_ref, next_ref, w_ref, b_ref, xs_ref, bm_ref, cm_ref, ext_ref):
    i = pl.program_id(0)
    dc = main_ref.shape[1]
    prev = jnp.where(lay.seg_first(i), 0.0, prev_ref[...].astype(F32))
    nxt = jnp.where(lay.seg_last(i), 0.0, next_ref[...].astype(F32))
    ext_ref[0:HALO, :] = prev
    ext_ref[HALO:HALO + TM, :] = main_ref[...].astype(F32)
    ext_ref[HALO + TM:HALO + TM + HALO, :] = nxt
    half = SSD_CONV // 2
    cw = 512
    for c0 in range(0, dc, cw):
        acc = jnp.zeros((TM, cw), F32) + b_ref[:, c0:c0 + cw]
        for k in range(SSD_CONV):
            acc = acc + ext_ref[HALO + k - half:HALO + k - half + TM, c0:c0 + cw] * w_ref[k:k + 1, c0:c0 + cw]
        y = _silu(acc).astype(BF16)
        if c0 < di:
            xs_ref[:, c0:c0 + cw] = y
        elif c0 < di + gn:
            bm_ref[:, c0 - di:c0 - di + cw] = y
        else:
            cm_ref[:, c0 - di - gn:c0 - di - gn + cw] = y


def _ssd_conv(lay, xbc, conv_w, conv_b, di, gn):
    t, dc = xbc.shape
    hb = TM // HALO
    nhb = t // HALO
    assert gn == 512 and di % 512 == 0
    return pl.pallas_call(
        functools.partial(_ssd_conv_kernel, lay, di, gn),
        grid=(lay.n_tiles,),
        in_specs=[pl.BlockSpec((HALO, dc), lambda i: (jnp.maximum(i * hb - 1, 0), 0)),
                  pl.BlockSpec((TM, dc), lambda i: (i, 0)),
                  pl.BlockSpec((HALO, dc), lambda i: (jnp.minimum((i + 1) * hb, nhb - 1), 0)),
                  pl.BlockSpec(conv_w.shape, lambda i: (0, 0)),
                  pl.BlockSpec(conv_b.shape, lambda i: (0, 0))],
        out_specs=[pl.BlockSpec((TM, di), lambda i: (i, 0)),
                   pl.BlockSpec((TM, gn), lambda i: (i, 0)),
                   pl.BlockSpec((TM, gn), lambda i: (i, 0))],
        out_shape=[jax.ShapeDtypeStruct((t, di), BF16),
                   jax.ShapeDtypeStruct((t, gn), BF16),
                   jax.ShapeDtypeStruct((t, gn), BF16)],
        scratch_shapes=[pltpu.VMEM((TM + 2 * HALO, dc), F32)],
        compiler_params=_cparams(("parallel",)),
        name="ssd_conv",
    )(xbc, xbc, xbc, conv_w, conv_b)


def _ssd_scan_kernel(xs_ref, bm_ref, cm_ref, dt_ref, dtt_ref, alr_ref, alc_ref, e_ref, y_ref, state_ref):
    d = pl.program_id(0)
    step = pl.program_id(2)
    q = SSD_CHUNK
    ng = SSD_N_GROUPS
    ns = SSD_D_STATE
    gw = xs_ref.shape[1] // ng
    hpg = gw // SSD_HEAD_DIM

    @pl.when(step == 0)
    def _():
        state_ref[...] = jnp.zeros_like(state_ref)

    sgn = jnp.where(d == 0, 1, -1)
    row = lax.broadcasted_iota(I32, (q, q), 0)
    col = lax.broadcasted_iota(I32, (q, q), 1)
    lmask = (row - col) * sgn >= 0
    lmask_t = (col - row) * sgn >= 0

    dt = dt_ref[...]
    dtt = dtt_ref[...]
    da = dt * (-jnp.exp(alr_ref[...]))
    dat = dtt * (-jnp.exp(alc_ref[...]))
    a_cum = jnp.dot(lmask.astype(F32), da, precision=HIGHEST, preferred_element_type=F32)
    a_cum_t = jnp.dot(dat, lmask_t.astype(F32), precision=HIGHEST, preferred_element_type=F32)
    a_end = jnp.sum(da, axis=0, keepdims=True)
    w_end = jnp.exp(a_end - a_cum) * dt

    e3 = e_ref[...]

    def expand(v):
        hi = v.astype(BF16)
        r1 = v - hi.astype(F32)
        mid = r1.astype(BF16)
        lo = (r1 - mid.astype(F32)).astype(BF16)
        return jnp.dot(jnp.concatenate([hi, mid, lo], axis=1), e3, preferred_element_type=F32)

    a_exp = expand(a_cum)
    w_exp = expand(w_end)
    end_exp = expand(jnp.broadcast_to(a_end, (SUBLANES, a_end.shape[1])))[0:1, :]
    decay_in = jnp.exp(a_exp)
    chunk_decay = jnp.exp(end_exp)
    xw = (xs_ref[...].astype(F32) * w_exp).astype(BF16)

    lane = lax.broadcasted_iota(I32, (q, LANES), 1)
    for g in range(ng):
        bg = bm_ref[:, g * ns:(g + 1) * ns]
        cg = cm_ref[:, g * ns:(g + 1) * ns]
        cb = lax.dot_general(cg, bg, (((1,), (1,)), ((), ())), preferred_element_type=F32)
        s_in = state_ref[g]
        y_off = jnp.dot(cg, s_in.astype(BF16), preferred_element_type=F32)
        new_s = lax.dot_general(bg, xw[:, g * gw:(g + 1) * gw], (((0,), (0,)), ((), ())),
                                preferred_element_type=F32)
        state_ref[g] = s_in * chunk_decay[:, g * gw:(g + 1) * gw] + new_s
        for pr in range(hpg // 2):
            mixes = []
            for hh in range(2):
                h = g * hpg + 2 * pr + hh
                seg = a_cum[:, h:h + 1] - a_cum_t[h:h + 1, :]
                lh = jnp.where(lmask, jnp.exp(jnp.minimum(seg, 0.0)), 0.0)
                mixes.append((cb * lh * dtt[h:h + 1, :]).astype(BF16))
            lhs = jnp.concatenate(mixes, axis=1)
            l0 = g * gw + pr * LANES
            xp = xs_ref[:, l0:l0 + LANES]
            zero = jnp.zeros_like(xp)
            rhs = jnp.concatenate([jnp.where(lane < SSD_HEAD_DIM, xp, zero),
                                   jnp.where(lane >= SSD_HEAD_DIM, xp, zero)], axis=0)
            y_pair = jnp.dot(lhs, rhs, preferred_element_type=F32)
            y_pair = y_pair + y_off[:, pr * LANES:(pr + 1) * LANES] * decay_in[:, l0:l0 + LANES]
            y_ref[:, l0:l0 + LANES] = y_pair.astype(BF16)


def _ssd_scan(lay, xs, bm, cm, dt, dtt, a_log, expand):
    t, di = xs.shape
    gn = bm.shape[1]
    nh = dt.shape[2]
    q = SSD_CHUNK
    nct, nlt = lay.n_ctx // q, lay.seq // q
    nc = nct + nlt
    ctx_base = lay.t_lat // q

    def blk(d, b, s):
        j_ctx = jnp.where(d == 0, s, nct - 1 - s)
        j_lat = jnp.where(d == 0, s - nct, nlt - 1 - (s - nct))
        return jnp.where(s < nct, ctx_base + b * nct + j_ctx, b * nlt + j_lat)

    return pl.pallas_call(
        _ssd_scan_kernel,
        grid=(2, lay.batch, nc),
        in_specs=[pl.BlockSpec((q, di), lambda d, b, s: (blk(d, b, s), 0)),
                  pl.BlockSpec((q, gn), lambda d, b, s: (blk(d, b, s), 0)),
                  pl.BlockSpec((q, gn), lambda d, b, s: (blk(d, b, s), 0)),
                  pl.BlockSpec((None, q, nh), lambda d, b, s: (d, blk(d, b, s), 0)),
                  pl.BlockSpec((None, nh, q), lambda d, b, s: (d, 0, blk(d, b, s))),
                  pl.BlockSpec((None, 1, nh), lambda d, b, s: (d, 0, 0)),
                  pl.BlockSpec((None, nh, 1), lambda d, b, s: (d, 0, 0)),
                  pl.BlockSpec(expand.shape, lambda d, b, s: (0, 0))],
        out_specs=pl.BlockSpec((None, q, di), lambda d, b, s: (d, blk(d, b, s), 0)),
        out_shape=jax.ShapeDtypeStruct((2, t, di), BF16),
        scratch_shapes=[pltpu.VMEM((SSD_N_GROUPS, SSD_D_STATE, di // SSD_N_GROUPS), F32)],
        compiler_params=_cparams(("arbitrary", "arbitrary", "arbitrary")),
        name="ssd_scan",
    )(xs, bm, cm, dt, dtt, a_log.reshape(2, 1, nh), a_log.reshape(2, nh, 1), expand)


def _ssd_outproj_kernel(y_ref, xs_ref, z_ref, dexp_ref, ng_ref, w_ref, x_ref, mod_ref, o_ref):
    y = y_ref[0].astype(F32) + y_ref[1].astype(F32) + xs_ref[...].astype(F32) * dexp_ref[...]
    y = y * _silu(z_ref[...].astype(F32))
    gw = y.shape[1] // SSD_N_GROUPS
    parts = []
    for g in range(SSD_N_GROUPS):
        yg = y[:, g * gw:(g + 1) * gw]
        ms = jnp.mean(yg * yg, axis=-1, keepdims=True)
        parts.append((yg * lax.rsqrt(ms + NORM_EPS) * ng_ref[:, g * gw:(g + 1) * gw]).astype(BF16))
    yn = jnp.concatenate(parts, axis=1)
    m = jnp.dot(yn, w_ref[...], preferred_element_type=F32)
    o_ref[...] = x_ref[...] + mod_ref[2:3, :] * m


def _ssd_outproj(lay, y, xs, z, dexp, ng, w, x, mod_l):
    t, d = x.shape
    di = xs.shape[1]
    full = lambda a: pl.BlockSpec(a.shape, lambda i: (0,) * a.ndim)
    return pl.pallas_call(
        _ssd_outproj_kernel,
        grid=(lay.n_tiles,),
        in_specs=[pl.BlockSpec((2, TM, di), lambda i: (0, i, 0)),
                  pl.BlockSpec((TM, di), lambda i: (i, 0)),
                  pl.BlockSpec((TM, di), lambda i: (i, 0)),
                  full(dexp), full(ng), full(w),
                  pl.BlockSpec((TM, d), lambda i: (i, 0)),
                  pl.BlockSpec((None, N_MOD, d), lambda i: (lay.mod_row(i), 0, 0))],
        out_specs=pl.BlockSpec((TM, d), lambda i: (i, 0)),
        out_shape=jax.ShapeDtypeStruct((t, d), F32),
        compiler_params=_cparams(("parallel",)),
        name="ssd_outproj",
    )(y, xs, z, dexp, ng, w, x, mod_l)


def _att_inproj_kernel(x_ref, mod_ref, g_ref, wq_ref, wkt_ref, wv_ref, vone_ref, qg_ref, kgt_ref, gsum_ref, gexp_ref,
                       cos_ref, sin_ref, cost_ref, sint_ref, q_ref, kt_ref, v_ref):
    h = _norm_mod(x_ref[...], g_ref[...], mod_ref[0:1, :], mod_ref[1:2, :]).astype(BF16)
    hd = ATT_HEAD_DIM
    q = jnp.dot(h, wq_ref[...], preferred_element_type=F32)
    ssum = jnp.dot((q * q).astype(BF16), gsum_ref[...], preferred_element_type=F32)
    r = lax.rsqrt(ssum * (1.0 / hd) + NORM_EPS)
    r_hi = r.astype(BF16)
    r_lo = (r - r_hi.astype(F32)).astype(BF16)
    r_exp = jnp.dot(jnp.concatenate([r_hi, r_lo], axis=1), gexp_ref[...], preferred_element_type=F32)
    qn = q * r_exp * qg_ref[...]
    nl = qn.shape[1]
    lane = lax.broadcasted_iota(I32, qn.shape, 1)
    partner = jnp.where((lane & 1) == 0, pltpu.roll(qn, nl - 1, 1), pltpu.roll(qn, 1, 1))
    reps = nl // LANES
    cos = jnp.concatenate([cos_ref[...]] * reps, axis=1)
    sin = jnp.concatenate([sin_ref[...]] * reps, axis=1)
    q_ref[...] = (qn * cos + partner * sin).astype(BF16)
    kt = lax.dot_general(wkt_ref[...], h, (((1,), (1,)), ((), ())), preferred_element_type=F32)
    ct, st = cost_ref[...], sint_ref[...]
    sub = lax.broadcasted_iota(I32, ct.shape, 0)
    for kh in range(ATT_KV_HEADS):
        blk = kt[kh * hd:(kh + 1) * hd]
        rk = lax.rsqrt(jnp.sum(blk * blk, axis=0, keepdims=True) * (1.0 / hd) + NORM_EPS)
        kn = blk * rk * kgt_ref[...]
        kpart = jnp.where((sub & 1) == 0, pltpu.roll(kn, hd - 1, 0), pltpu.roll(kn, 1, 0))
        kt_ref[kh] = (kn * ct + kpart * st).astype(BF16)
    v = (jnp.dot(h, wv_ref[...], preferred_element_type=F32) + vone_ref[...]).astype(BF16)
    for kh in range(ATT_KV_HEADS):
        v_ref[kh] = v[:, kh * LANES:(kh + 1) * LANES]


def _att_inproj(lay, x, mod_l, g, wq, wkt, wv2, vone, qg, kgt, gsum, gexp, cos, sin, cost, sint):
    t, d = x.shape
    dq = wq.shape[1]
    hd = ATT_HEAD_DIM
    full = lambda a: pl.BlockSpec(a.shape, lambda i: (0,) * a.ndim)
    return pl.pallas_call(
        _att_inproj_kernel,
        grid=(lay.n_tiles,),
        in_specs=[pl.BlockSpec((TM, d), lambda i: (i, 0)),
                  pl.BlockSpec((None, N_MOD, d), lambda i: (lay.mod_row(i), 0, 0)),
                  full(g), full(wq), full(wkt), full(wv2), full(vone), full(qg), full(kgt), full(gsum), full(gexp),
                  pl.BlockSpec((TM, LANES), lambda i: (lay.tile_pos(i), 0)),
                  pl.BlockSpec((TM, LANES), lambda i: (lay.tile_pos(i), 0)),
                  pl.BlockSpec((hd, TM), lambda i: (0, lay.tile_pos(i))),
                  pl.BlockSpec((hd, TM), lambda i: (0, lay.tile_pos(i)))],
        out_specs=[pl.BlockSpec((TM, dq), lambda i: (i, 0)),
                   pl.BlockSpec((None, ATT_KV_HEADS, hd, TM), lambda i: (lay.tile_batch(i), 0, 0, lay.tile_pos(i))),
                   pl.BlockSpec((None, ATT_KV_HEADS, TM, LANES), lambda i: (lay.tile_batch(i), 0, lay.tile_pos(i), 0))],
        out_shape=[jax.ShapeDtypeStruct((t, dq), BF16),
                   jax.ShapeDtypeStruct((lay.batch, ATT_KV_HEADS, hd, lay.nk), BF16),
                   jax.ShapeDtypeStruct((lay.batch, ATT_KV_HEADS, lay.nk, LANES), BF16)],
        compiler_params=_cparams(("parallel",)),
        name="att_inproj",
    )(x, mod_l, g, wq, wkt, wv2, vone, qg, kgt, gsum, gexp, cos, sin, cost, sint)


def _attend(q_ref, kt_ref, v_ref, o_ref, k0, n_keys):
    hd = ATT_HEAD_DIM
    tm = q_ref.shape[0]
    heads = q_ref.shape[1] // hd
    qs = [q_ref[:, u * hd:(u + 1) * hd] for u in range(heads)]

    bk = min(ATT_KEY_BLOCK, n_keys)
    assert n_keys % bk == 0 and k0 % LANES == 0 and bk % LANES == 0

    def body(j, carry):
        start = pl.multiple_of(k0 + j * bk, LANES)
        kb = kt_ref[:, pl.ds(start, bk)]
        vb = v_ref[pl.ds(start, bk), :]
        out = []
        for u in range(heads):
            m_old, acc = carry[u]
            s = jnp.dot(qs[u], kb, preferred_element_type=F32)
            m_new = jnp.maximum(m_old, jnp.max(s, axis=-1, keepdims=True))
            p = jnp.exp2((s - m_new).astype(BF16))
            acc = jnp.exp2(m_old - m_new) * acc + jnp.dot(p, vb, preferred_element_type=F32)
            out.append((m_new, acc))
        return tuple(out)

    init = tuple((jnp.full((tm, 1), -jnp.inf, F32), jnp.zeros((tm, LANES), F32)) for _ in range(heads))
    final = lax.fori_loop(0, n_keys // bk, body, init) if n_keys > bk else body(0, init)
    outs = [acc[:, :hd] / acc[:, hd:hd + 1] for _, acc in final]
    o_ref[...] = jnp.concatenate(outs, axis=1).astype(BF16)


def _att_core_kernel(lay, q_ref, kt_ref, v_ref, o_ref):
    qi = pl.program_id(2)

    @pl.when(qi < lay.seq_tiles)
    def _():
        _attend(q_ref, kt_ref, v_ref, o_ref, 0, lay.nk)

    @pl.when(qi >= lay.seq_tiles)
    def _():
        _attend(q_ref, kt_ref, v_ref, o_ref, lay.seq, lay.n_ctx)


def _att_core(lay, qn, kt, v):
    t, dq = qn.shape
    hd = ATT_HEAD_DIM
    qw = dq // ATT_KV_HEADS
    per_b = lay.seq_tiles + lay.ctx_tiles

    def row_tile(b, qi):
        return jnp.where(qi < lay.seq_tiles, b * lay.seq_tiles + qi,
                         lay.lat_tiles + b * lay.ctx_tiles + (qi - lay.seq_tiles))

    return pl.pallas_call(
        functools.partial(_att_core_kernel, lay),
        grid=(lay.batch, ATT_KV_HEADS, per_b),
        in_specs=[pl.BlockSpec((TM, qw), lambda b, kh, qi: (row_tile(b, qi), kh)),
                  pl.BlockSpec((None, None, hd, lay.nk), lambda b, kh, qi: (b, kh, 0, 0)),
                  pl.BlockSpec((None, None, lay.nk, LANES), lambda b, kh, qi: (b, kh, 0, 0))],
        out_specs=pl.BlockSpec((TM, qw), lambda b, kh, qi: (row_tile(b, qi), kh)),
        out_shape=jax.ShapeDtypeStruct((t, dq), BF16),
        compiler_params=_cparams(("parallel", "parallel", "arbitrary")),
        name="att_core",
    )(qn, kt, v)


def _att_outproj_kernel(o_ref, w_ref, x_ref, mod_ref, out_ref):
    m = jnp.dot(o_ref[...], w_ref[...], preferred_element_type=F32)
    out_ref[...] = x_ref[...] + mod_ref[2:3, :] * m


def _att_outproj(lay, o, w, x, mod_l):
    t, d = x.shape
    return pl.pallas_call(
        _att_outproj_kernel,
        grid=(lay.n_tiles,),
        in_specs=[pl.BlockSpec((TM, o.shape[1]), lambda i: (i, 0)),
                  pl.BlockSpec(w.shape, lambda i: (0, 0)),
                  pl.BlockSpec((TM, d), lambda i: (i, 0)),
                  pl.BlockSpec((None, N_MOD, d), lambda i: (lay.mod_row(i), 0, 0))],
        out_specs=pl.BlockSpec((TM, d), lambda i: (i, 0)),
        out_shape=jax.ShapeDtypeStruct((t, d), F32),
        compiler_params=_cparams(("parallel",)),
        name="att_outproj",
    )(o, w, x, mod_l)


R_EID, R_RANK, R_W = 0, 2, 4
GROUP_LANE0 = MOE_EXPERTS


def _store_token_rows(ref, val):
    rows, d = val.shape
    for s in range(d // LANES):
        ref[pl.ds(s, rows, stride=d // LANES), :] = val[:, s * LANES:(s + 1) * LANES]


def _load_token_rows(ref, d):
    n = d // LANES
    rows = ref.shape[0] // n
    return jnp.concatenate([ref[pl.ds(s, rows, stride=n), :] for s in range(n)], axis=1)


def _token(ref, r, n):
    return ref.at[pl.ds(pl.multiple_of(r * n, n), n)]


def _router_kernel(x_ref, mod_ref, g_ref, wr_ref, br_ref, f_ref, r_ref, rt_ref, cnt_ref, base_ref):
    i = pl.program_id(0)

    @pl.when(i == 0)
    def _():
        base_ref[...] = jnp.zeros_like(base_ref)

    f = _norm_mod(x_ref[...], g_ref[...], mod_ref[3:4, :], mod_ref[4:5, :])
    _store_token_rows(f_ref, f)
    logits = jnp.dot(f, wr_ref[...], precision=HIGHEST, preferred_element_type=F32) + br_ref[...]
    lane = lax.broadcasted_iota(I32, logits.shape, 1)
    neg = jnp.float32(-jnp.inf)
    big = jnp.int32(LANES)

    def first_max(vals):
        top = jnp.max(vals, axis=-1, keepdims=True)
        idx = jnp.min(jnp.where(vals == top, lane, big), axis=-1, keepdims=True)
        return top, idx

    g_mask = (lane >= GROUP_LANE0) & (lane < GROUP_LANE0 + MOE_GROUPS)
    glog = jnp.where(g_mask, logits, neg)
    g_top, g_idx = first_max(glog)
    g_w = 1.0 / jnp.sum(jnp.exp(glog - g_top), axis=-1, keepdims=True)
    e0 = (g_idx - GROUP_LANE0) * MOE_EPG
    elog = jnp.where((lane >= e0) & (lane < e0 + MOE_EPG), logits, neg)
    v1, i1 = first_max(elog)
    v2, i2 = first_max(jnp.where(lane == i1, neg, elog))
    ex = jnp.exp(v2 - v1)
    w1 = g_w / (1.0 + ex)
    w2 = g_w * ex / (1.0 + ex)

    oh1 = (lane == i1).astype(F32)
    oh2 = (lane == i2).astype(F32)
    tm = logits.shape[0]
    rr = lax.broadcasted_iota(I32, (tm, tm), 0)
    cc = lax.broadcasted_iota(I32, (tm, tm), 1)
    before = (cc < rr).astype(BF16)
    cum1 = jnp.dot(before, oh1.astype(BF16), preferred_element_type=F32)
    cum2 = jnp.dot(before, oh2.astype(BF16), preferred_element_type=F32)
    tot1 = jnp.sum(oh1, axis=0, keepdims=True)
    tot2 = jnp.sum(oh2, axis=0, keepdims=True)
    base = base_ref[...]
    rank1 = jnp.sum(oh1 * (cum1 + base), axis=-1, keepdims=True)
    rank2 = jnp.sum(oh2 * (cum2 + base + tot1), axis=-1, keepdims=True)
    base = base + tot1 + tot2
    base_ref[...] = base
    cnt_ref[...] = base

    rec = jnp.zeros(logits.shape, F32)
    for k, val in ((R_EID, i1.astype(F32)), (R_EID + 1, i2.astype(F32)), (R_RANK, rank1), (R_RANK + 1, rank2),
                   (R_W, w1), (R_W + 1, w2)):
        rec = jnp.where(lane == k, val, rec)
    r_ref[...] = rec
    rt_ref[...] = rec.T[:SUBLANES, :]


def _router(lay, n_tiles, x, mod_l, g, wr, br):
    t, d = x.shape
    return pl.pallas_call(
        _router_kernel,
        grid=(n_tiles,),
        in_specs=[pl.BlockSpec((TM, d), lambda i: (i, 0)),
                  pl.BlockSpec((None, N_MOD, d), lambda i: (lay.mod_row(i), 0, 0)),
                  pl.BlockSpec(g.shape, lambda i: (0, 0)),
                  pl.BlockSpec(wr.shape, lambda i: (0, 0)),
                  pl.BlockSpec(br.shape, lambda i: (0, 0))],
        out_specs=[pl.BlockSpec((TM * (d // LANES), LANES), lambda i: (i, 0)),
                   pl.BlockSpec((TM, LANES), lambda i: (i, 0)),
                   pl.BlockSpec((SUBLANES, TM), lambda i: (0, i)),
                   pl.BlockSpec((1, LANES), lambda i: (0, 0))],
        out_shape=[jax.ShapeDtypeStruct((n_tiles * TM * (d // LANES), LANES), F32),
                   jax.ShapeDtypeStruct((n_tiles * TM, LANES), F32),
                   jax.ShapeDtypeStruct((SUBLANES, n_tiles * TM), F32),
                   jax.ShapeDtypeStruct((1, LANES), F32)],
        scratch_shapes=[pltpu.VMEM((1, LANES), F32)],
        compiler_params=_cparams(("arbitrary",)),
        name="moe_router",
    )(x, mod_l, g, wr, br)


def _dispatch_kernel(fill_ref, plen_ref, dest_hbm, f_ref, xs_hbm, dest_s, zbuf, dsem, fsem, ssem):
    i = pl.program_id(0)
    n = pl.num_programs(0)
    tn = f_ref.shape[0] // TM

    def dest_copy(j):
        return pltpu.make_async_copy(dest_hbm.at[j], dest_s.at[pl.ds((j % 2) * DEST_ROW, DEST_ROW)], dsem.at[j % 2])

    def pad_pieces(e, fn):
        plen = plen_ref[e]
        for bit in range(ZROWS.bit_length() - 1, -1, -1):
            size = 1 << bit

            @pl.when((plen & size) != 0)
            def _():
                start = fill_ref[e] + ((plen >> (bit + 1)) << (bit + 1))
                fn(pltpu.make_async_copy(zbuf.at[pl.ds(0, size * tn)],
                                         xs_hbm.at[pl.ds(pl.multiple_of(start * tn, tn), size * tn)], fsem.at[0]))

    @pl.when(i == 0)
    def _():
        zbuf[...] = jnp.zeros_like(zbuf)
        dest_copy(0).start()

        def start_e(e, c):
            pad_pieces(e, lambda cp: cp.start())
            return c

        def wait_e(e, c):
            pad_pieces(e, lambda cp: cp.wait())
            return c

        def tail_pieces(fn):
            def piece(t, c):
                start = fill_ref[MOE_EXPERTS] + t * ZROWS
                fn(pltpu.make_async_copy(zbuf, xs_hbm.at[pl.ds(pl.multiple_of(start * tn, tn), ZROWS * tn)],
                                         fsem.at[0]))
                return c
            lax.fori_loop(0, plen_ref[MOE_EXPERTS] // ZROWS, piece, 0)

        lax.fori_loop(0, MOE_EXPERTS, start_e, 0)
        tail_pieces(lambda cp: cp.start())
        lax.fori_loop(0, MOE_EXPERTS, wait_e, 0)
        tail_pieces(lambda cp: cp.wait())

    @pl.when(i + 1 < n)
    def _():
        dest_copy(i + 1).start()

    dest_copy(i).wait()
    slot = i % 2

    def body(r, c):
        for k in range(2):
            row = dest_s[slot * DEST_ROW + k * TM + r]
            pltpu.make_async_copy(_token(f_ref, r, tn), _token(xs_hbm, row, tn), ssem.at[0]).start(priority=k)
        return c

    lax.fori_loop(0, TM, body, 0, unroll=8)
    for k in range(2):
        pltpu.make_async_copy(f_ref, xs_hbm.at[pl.ds(0, TM * tn)], ssem.at[0]).wait()


def _dispatch(fill, plen, dest, f, cap):
    n_tiles = dest.shape[0]
    tn = f.shape[0] // (n_tiles * TM)
    grid_spec = pltpu.PrefetchScalarGridSpec(
        num_scalar_prefetch=2,
        grid=(n_tiles,),
        in_specs=[pl.BlockSpec(memory_space=pl.ANY),
                  pl.BlockSpec((TM * tn, LANES), lambda i, fill, plen: (i, 0))],
        out_specs=pl.BlockSpec(memory_space=pl.ANY),
        scratch_shapes=[pltpu.SMEM((2 * DEST_ROW,), I32),
                        pltpu.VMEM((ZROWS * tn, LANES), F32),
                        pltpu.SemaphoreType.DMA((2,)),
                        pltpu.SemaphoreType.DMA((1,)),
                        pltpu.SemaphoreType.DMA((1,))],
    )
    return pl.pallas_call(
        _dispatch_kernel,
        grid_spec=grid_spec,
        out_shape=jax.ShapeDtypeStruct((cap * tn, LANES), F32),
        compiler_params=_cparams(("arbitrary",)),
        name="moe_dispatch",
    )(fill, plen, dest, f)


def _expert_kernel(be_ref, nv_ref, blk_ref, x_ref, wg_ref, wu_ref, wd_ref, y_ref, wgb, wub, wdb):
    i = pl.program_id(0)
    live = nv_ref[i] > 0
    changed = jnp.logical_or(i == 0, be_ref[i] != be_ref[jnp.maximum(i - 1, 0)])

    @pl.when(jnp.logical_and(live, changed))
    def _():
        wgb[...] = wg_ref[...].astype(BF16)
        wub[...] = wu_ref[...].astype(BF16)
        wdb[...] = wd_ref[...].astype(BF16)

    @pl.when(live)
    def _():
        xb = _load_token_rows(x_ref, wgb.shape[0]).astype(BF16)
        hg = jnp.dot(xb, wgb[...], preferred_element_type=F32)
        hu = jnp.dot(xb, wub[...], preferred_element_type=F32)
        act = (_silu(hg) * hu).astype(BF16)
        _store_token_rows(y_ref, jnp.dot(act, wdb[...], preferred_element_type=F32))

    @pl.when(jnp.logical_not(live))
    def _():
        y_ref[...] = jnp.zeros_like(y_ref)


def _experts(layer, block_e, block_nv, block_src, xs, wg, wu, wd):
    n_blocks = block_e.shape[0]
    d, dff = wg.shape[2:]
    rb = MOE_ROW_BLOCK
    blk = (rb * (d // LANES), LANES)
    grid_spec = pltpu.PrefetchScalarGridSpec(
        num_scalar_prefetch=3,
        grid=(n_blocks,),
        in_specs=[pl.BlockSpec(blk, lambda i, be, nv, src: (src[i], 0)),
                  pl.BlockSpec((None, None, d, dff), lambda i, be, nv, src: (layer, be[i], 0, 0)),
                  pl.BlockSpec((None, None, d, dff), lambda i, be, nv, src: (layer, be[i], 0, 0)),
                  pl.BlockSpec((None, None, dff, d), lambda i, be, nv, src: (layer, be[i], 0, 0))],
        out_specs=pl.BlockSpec(blk, lambda i, be, nv, src: (i, 0)),
        scratch_shapes=[pltpu.VMEM((d, dff), BF16),
                        pltpu.VMEM((d, dff), BF16),
                        pltpu.VMEM((dff, d), BF16)],
    )
    return pl.pallas_call(
        _expert_kernel,
        grid_spec=grid_spec,
        out_shape=jax.ShapeDtypeStruct(xs.shape, F32),
        compiler_params=_cparams(("arbitrary",)),
        name="moe_experts",
    )(block_e, block_nv, block_src, xs, wg, wu, wd)


def _combine_kernel(dest_hbm, x_ref, r_ref, mod_ref, ys_hbm, o_ref, dest_s, gbuf, dsem, gsem):
    i = pl.program_id(0)
    n = pl.num_programs(0)
    tn = x_ref.shape[1] // LANES

    def dest_copy(j):
        return pltpu.make_async_copy(dest_hbm.at[j], dest_s.at[pl.ds((j % 3) * DEST_ROW, DEST_ROW)], dsem.at[j % 3])

    def gather_tile(j):
        def body(r, c):
            for k in range(2):
                row = dest_s[(j % 3) * DEST_ROW + k * TM + r]
                pltpu.make_async_copy(_token(ys_hbm, row, tn), _token(gbuf.at[j % 2, k], r, tn),
                                      gsem.at[j % 2]).start(priority=k)
            return c
        lax.fori_loop(0, TM, body, 0, unroll=8)

    @pl.when(i == 0)
    def _():
        dest_copy(0).start()
        dest_copy(0).wait()
        gather_tile(0)

        @pl.when(n > 1)
        def _():
            dest_copy(1).start()

    @pl.when(i + 1 < n)
    def _():
        dest_copy(i + 1).wait()

        @pl.when(i + 2 < n)
        def _():
            dest_copy(i + 2).start()

        gather_tile(i + 1)

    slot = i % 2
    for k in range(2):
        pltpu.make_async_copy(ys_hbm.at[pl.ds(0, TM * tn)], gbuf.at[slot, k], gsem.at[slot]).wait()
    w1 = r_ref[:, R_W:R_W + 1]
    w2 = r_ref[:, R_W + 1:R_W + 2]
    d = x_ref.shape[1]
    y = _load_token_rows(gbuf.at[slot, 0], d) * w1 + _load_token_rows(gbuf.at[slot, 1], d) * w2
    o_ref[...] = x_ref[...] + mod_ref[5:6, :] * y


def _combine(lay, n_tiles, dest, x, rec, mod_l, ys):
    d = x.shape[1]
    return pl.pallas_call(
        _combine_kernel,
        grid=(n_tiles,),
        in_specs=[pl.BlockSpec(memory_space=pl.ANY),
                  pl.BlockSpec((TM, d), lambda i: (i, 0)),
                  pl.BlockSpec((TM, LANES), lambda i: (i, 0)),
                  pl.BlockSpec((None, N_MOD, d), lambda i: (lay.mod_row(i), 0, 0)),
                  pl.BlockSpec(memory_space=pl.ANY)],
        out_specs=pl.BlockSpec((TM, d), lambda i: (i, 0)),
        out_shape=jax.ShapeDtypeStruct((n_tiles * TM, d), F32),
        scratch_shapes=[pltpu.SMEM((3 * DEST_ROW,), I32),
                        pltpu.VMEM((2, 2, TM * (d // LANES), LANES), F32),
                        pltpu.SemaphoreType.DMA((3,)),
                        pltpu.SemaphoreType.DMA((2,))],
        compiler_params=_cparams(("arbitrary",)),
        name="moe_combine",
    )(dest, x, rec, mod_l, ys)


def _moe(lay, layer, n_tiles, x, mod_l, g2, w_group, b_group, w_router, b_router, wg, wu, wd):
    d = x.shape[1]
    t_tok = n_tiles * TM
    pad = LANES - MOE_EXPERTS - MOE_GROUPS
    wr = jnp.concatenate([w_router, w_group, jnp.zeros((d, pad), F32)], axis=1)
    br = jnp.concatenate([b_router, b_group, jnp.zeros((pad,), F32)])[None, :]
    f, rec, rec_t, cnt = _router(lay, n_tiles, x, mod_l, g2, wr, br)

    rb = MOE_ROW_BLOCK
    eid = rec_t[R_EID:R_EID + 2].astype(I32)
    rank = rec_t[R_RANK:R_RANK + 2].astype(I32)
    counts = cnt[0, :MOE_EXPERTS].astype(I32)
    padded = (counts + rb - 1) // rb * rb
    pad_ends = jnp.cumsum(padded)
    pad_starts = pad_ends - padded
    dest = pad_starts[eid] + rank
    dest = dest.reshape(2, n_tiles, TM).transpose(1, 0, 2).reshape(n_tiles, 2 * TM)
    dest = jnp.pad(dest, ((0, 0), (0, DEST_ROW - 2 * TM)))
    n_blocks = -(-(2 * t_tok) // rb) + MOE_EXPERTS
    used = pad_ends[-1] // rb
    idx = jnp.arange(n_blocks, dtype=I32)
    block_src = jnp.clip(idx, 0, jnp.maximum(used - 1, 0))
    start = block_src * rb
    block_e = jnp.sum(pad_ends[None, :] <= start[:, None], axis=1).astype(I32)
    block_nv = jnp.where(idx < used, jnp.clip(counts[block_e] - (start - pad_starts[block_e]), 0, rb), 0).astype(I32)

    cap = n_blocks * rb
    fill = jnp.concatenate([pad_starts + counts, pad_ends[-1:]]).astype(I32)
    plen = jnp.concatenate([padded - counts, cap - pad_ends[-1:]]).astype(I32)
    xs = _dispatch(fill, plen, dest, f, cap)
    ys = _experts(layer, block_e, block_nv, block_src, xs, wg, wu, wd)
    return _combine(lay, n_tiles, dest, x, rec, mod_l, ys)


def _rope_tables(lay):
    half = ATT_HEAD_DIM // 2
    pos = jnp.arange(lay.seq)
    rowp = (pos // GRID_W).astype(F32)
    colp = (pos % GRID_W).astype(F32)
    freqs = ROPE_THETA ** (-jnp.arange(0, half, 2, dtype=F32) / half)
    ang = jnp.concatenate([rowp[:, None] * freqs, colp[:, None] * freqs], axis=-1)
    cos = jnp.concatenate([jnp.cos(ang), jnp.ones((lay.n_ctx, half), F32)], axis=0)
    sin = jnp.concatenate([jnp.sin(ang), jnp.zeros((lay.n_ctx, half), F32)], axis=0)
    cos_d = jnp.repeat(cos, 2, axis=1)
    sin_d = jnp.repeat(sin, 2, axis=1) * jnp.tile(jnp.asarray([-1.0, 1.0], F32), half)
    reps = LANES // ATT_HEAD_DIM
    return jnp.tile(cos_d, (1, reps)), jnp.tile(sin_d, (1, reps)), cos_d.T, sin_d.T


def kernel(x, c, ctx, c_ctx, mod_w, mod_b, norm1_g, norm2_g, ssd_w_in, ssd_conv_w, ssd_conv_b, ssd_dt_bias,
           ssd_a_log, ssd_d, ssd_norm_g, ssd_w_out, att_w_qkv, att_q_gain, att_k_gain, att_w_o, moe_w_group,
           moe_b_group, moe_w_router, moe_b_router, moe_w_gate, moe_w_up, moe_w_down):
    b, n_lat, d = x.shape
    n_ctx = ctx.shape[1]
    depth = mod_w.shape[0]
    lay = _Layout(b, n_lat, n_ctx)
    assert b + 1 <= 8

    xs = jnp.concatenate([x.reshape(lay.t_lat, d), ctx.reshape(lay.t_ctx, d)], axis=0)
    c8 = jnp.concatenate([c, c_ctx[None, :], jnp.zeros((8 - b - 1, d), F32)], axis=0)
    mods = _mod_table(c8, mod_w, mod_b).reshape(depth, 8, N_MOD, d)

    nh = ssd_dt_bias.shape[2]
    di = nh * SSD_HEAD_DIM
    gn = SSD_N_GROUPS * SSD_D_STATE
    dc = di + 2 * gn
    expand = jnp.tile(jnp.repeat(jnp.eye(nh, dtype=BF16), SSD_HEAD_DIM, axis=1), (3, 1))

    hq = att_w_qkv.shape[2] // ATT_HEAD_DIM - 2 * ATT_KV_HEADS
    dq = hq * ATT_HEAD_DIM
    dkv = ATT_KV_HEADS * ATT_HEAD_DIM
    gsum = np.zeros((dq, LANES), np.float32)
    gsum[np.arange(dq), np.arange(dq) // ATT_HEAD_DIM] = 1.0
    gexp = np.concatenate([gsum.T, gsum.T], axis=0)
    vone = np.zeros((1, ATT_KV_HEADS * LANES), np.float32)
    vone[0, np.arange(ATT_KV_HEADS) * LANES + ATT_HEAD_DIM] = 1.0
    cos128, sin128, cos_t, sin_t = _rope_tables(lay)

    for layer in range(depth):
        last = layer == depth - 1
        mod_l = mods[layer]
        j = layer // 2
        g1 = norm1_g[layer][None, :]
        if layer % 2 == 0:
            w_in = ssd_w_in[j].astype(BF16)
            wz, wx, wdt = w_in[:, :di], w_in[:, di:di + dc], w_in[:, di + dc:]
            dtb = ssd_dt_bias[j].reshape(1, 2 * nh)
            z, xbc, dt, dtt = _ssd_inproj(lay, xs, mod_l, g1, wz, wx, wdt, wdt.T, dtb, dtb.T)
            xc, bm, cm = _ssd_conv(lay, xbc, ssd_conv_w[j], ssd_conv_b[j][None, :], di, gn)
            y = _ssd_scan(lay, xc, bm, cm, dt, dtt, ssd_a_log[j], expand)
            dexp = jnp.repeat(ssd_d[j], SSD_HEAD_DIM)[None, :]
            xs = _ssd_outproj(lay, y, xc, z, dexp, ssd_norm_g[j][None, :], ssd_w_out[j].astype(BF16), xs, mod_l)
        else:
            w = att_w_qkv[j]
            wq = w[:, :dq].astype(BF16)
            wkt = w[:, dq:dq + dkv].T.astype(BF16)
            wv = w[:, dq + dkv:].reshape(d, ATT_KV_HEADS, ATT_HEAD_DIM)
            wv2 = jnp.pad(wv, ((0, 0), (0, 0), (0, LANES - ATT_HEAD_DIM))).reshape(d, ATT_KV_HEADS * LANES).astype(BF16)
            qg = (jnp.tile(att_q_gain[j], hq) * (ATT_HEAD_DIM ** -0.5 * LOG2E))[None, :]
            kgt = att_k_gain[j][:, None]
            qn, kt, v = _att_inproj(lay, xs, mod_l, g1, wq, wkt, wv2, jnp.asarray(vone), qg, kgt, jnp.asarray(gsum, BF16),
                                    jnp.asarray(gexp, BF16), cos128, sin128, cos_t, sin_t)
            o = _att_core(lay, qn, kt, v)
            xs = _att_outproj(lay, o, att_w_o[j].astype(BF16), xs, mod_l)
        n_tiles = lay.lat_tiles if last else lay.n_tiles
        xs = _moe(lay, layer, n_tiles, xs, mod_l, norm2_g[layer][None, :], moe_w_group[layer], moe_b_group[layer],
                  moe_w_router[layer], moe_b_router[layer], moe_w_gate, moe_w_up, moe_w_down)
    return xs[:lay.t_lat].reshape(b, n_lat, d)
```

```python
import functools

import numpy as np
import jax
import jax.numpy as jnp
from jax import lax
from jax.experimental import pallas as pl
from jax.experimental.pallas import tpu as pltpu

F32 = jnp.float32
BF16 = jnp.bfloat16
I32 = jnp.int32
HIGHEST = lax.Precision.HIGHEST

NORM_EPS = 1e-6
N_MOD = 6
GRID_W = 64
ROPE_THETA = 10000.0

SSD_HEAD_DIM = 64
SSD_N_GROUPS = 4
SSD_D_STATE = 128
SSD_CONV = 5
SSD_CHUNK = 128

ATT_HEAD_DIM = 64
ATT_KV_HEADS = 4
ATT_KEY_BLOCK = 4352
LOG2E = 1.4426950408889634

MOE_GROUPS = 4
MOE_EPG = 8
MOE_EXPERTS = MOE_GROUPS * MOE_EPG
MOE_ROW_BLOCK = 256
ZROWS = MOE_ROW_BLOCK // 2
DEST_ROW = 1024

TM = 256
LANES = 128
SUBLANES = 8
HALO = 16
VMEM_LIMIT = 56 * 1024 * 1024


def _cparams(sem):
    return pltpu.CompilerParams(dimension_semantics=sem, vmem_limit_bytes=VMEM_LIMIT)


def _silu(v):
    return v / (1.0 + jnp.exp(-v))


def _softplus(v):
    return jnp.maximum(v, 0.0) + jnp.log1p(jnp.exp(-jnp.abs(v)))


def _norm_mod(x, g, shift, scale):
    ms = jnp.mean(x * x, axis=-1, keepdims=True)
    y = x * lax.rsqrt(ms + NORM_EPS) * g
    return y * (1.0 + scale) + shift


class _Layout:
    def __init__(self, batch, seq, n_ctx):
        assert seq % TM == 0 and n_ctx % TM == 0
        self.batch, self.seq, self.n_ctx = batch, seq, n_ctx
        self.t_lat = batch * seq
        self.t_ctx = batch * n_ctx
        self.t = self.t_lat + self.t_ctx
        self.lat_tiles = self.t_lat // TM
        self.seq_tiles = seq // TM
        self.ctx_tiles = n_ctx // TM
        self.n_tiles = self.t // TM
        self.nk = seq + n_ctx

    def mod_row(self, i):
        return jnp.where(i < self.lat_tiles, i // self.seq_tiles, self.batch)

    def tile_batch(self, i):
        return jnp.where(i < self.lat_tiles, i // self.seq_tiles, (i - self.lat_tiles) // self.ctx_tiles)

    def tile_pos(self, i):
        return jnp.where(i < self.lat_tiles, i % self.seq_tiles,
                         self.seq_tiles + (i - self.lat_tiles) % self.ctx_tiles)

    def seg_first(self, i):
        return jnp.where(i < self.lat_tiles, i % self.seq_tiles == 0, (i - self.lat_tiles) % self.ctx_tiles == 0)

    def seg_last(self, i):
        return jnp.where(i < self.lat_tiles, i % self.seq_tiles == self.seq_tiles - 1,
                         (i - self.lat_tiles) % self.ctx_tiles == self.ctx_tiles - 1)


def _mod_kernel(c_ref, w_ref, b_ref, o_ref):
    s = _silu(c_ref[...])
    o_ref[...] = jnp.dot(s, w_ref[...], precision=HIGHEST, preferred_element_type=F32) + b_ref[...]


def _mod_table(c8, mod_w, mod_b):
    depth, d, n = mod_w.shape
    tn = 1536
    assert n % tn == 0
    return pl.pallas_call(
        _mod_kernel,
        grid=(depth, n // tn),
        in_specs=[pl.BlockSpec((8, d), lambda l, j: (0, 0)),
                  pl.BlockSpec((None, d, tn), lambda l, j: (l, 0, j)),
                  pl.BlockSpec((None, 1, tn), lambda l, j: (l, 0, j))],
        out_specs=pl.BlockSpec((None, 8, tn), lambda l, j: (l, 0, j)),
        out_shape=jax.ShapeDtypeStruct((depth, 8, n), F32),
        compiler_params=_cparams(("parallel", "parallel")),
        name="mod_table",
    )(c8, mod_w, mod_b.reshape(depth, 1, n))


def _ssd_inproj_kernel(x_ref, mod_ref, g_ref, wz_ref, wx_ref, wdt_ref, wdtt_ref, dtb_ref, dtbt_ref,
                       z_ref, xbc_ref, dt_ref, dtt_ref):
    h = _norm_mod(x_ref[...], g_ref[...], mod_ref[0:1, :], mod_ref[1:2, :]).astype(BF16)
    z_ref[...] = jnp.dot(h, wz_ref[...], preferred_element_type=F32).astype(BF16)
    xbc_ref[...] = jnp.dot(h, wx_ref[...], preferred_element_type=F32).astype(BF16)
    nh = dtb_ref.shape[1] // 2
    dt = _softplus(jnp.dot(h, wdt_ref[...], preferred_element_type=F32) + dtb_ref[...])
    dt_ref[0] = dt[:, :nh]
    dt_ref[1] = dt[:, nh:]
    dtt = lax.dot_general(wdtt_ref[...], h, (((1,), (1,)), ((), ())), preferred_element_type=F32)
    dtt = _softplus(dtt + dtbt_ref[...])
    dtt_ref[0] = dtt[:nh, :]
    dtt_ref[1] = dtt[nh:, :]


def _ssd_inproj(lay, x, mod_l, g, wz, wx, wdt, wdtt, dtb, dtbt):
    t, d = x.shape
    di, dc, nh2 = wz.shape[1], wx.shape[1], wdt.shape[1]
    nh = nh2 // 2
    full = lambda a: pl.BlockSpec(a.shape, lambda i: (0,) * a.ndim)
    return pl.pallas_call(
        _ssd_inproj_kernel,
        grid=(lay.n_tiles,),
        in_specs=[pl.BlockSpec((TM, d), lambda i: (i, 0)),
                  pl.BlockSpec((None, N_MOD, d), lambda i: (lay.mod_row(i), 0, 0)),
                  full(g), full(wz), full(wx), full(wdt), full(wdtt), full(dtb), full(dtbt)],
        out_specs=[pl.BlockSpec((TM, di), lambda i: (i, 0)),
                   pl.BlockSpec((TM, dc), lambda i: (i, 0)),
                   pl.BlockSpec((2, TM, nh), lambda i: (0, i, 0)),
                   pl.BlockSpec((2, nh, TM), lambda i: (0, 0, i))],
        out_shape=[jax.ShapeDtypeStruct((t, di), BF16),
                   jax.ShapeDtypeStruct((t, dc), BF16),
                   jax.ShapeDtypeStruct((2, t, nh), F32),
                   jax.ShapeDtypeStruct((2, nh, t), F32)],
        compiler_params=_cparams(("parallel",)),
        name="ssd_inproj",
    )(x, mod_l, g, wz, wx, wdt, wdtt, dtb, dtbt)


def _ssd_conv_kernel(lay, di, gn, prev_ref, main_ref, next_ref, w_ref, b_ref, xs_ref, bm_ref, cm_ref, ext_ref):
    i = pl.program_id(0)
    dc = main_ref.shape[1]
    prev = jnp.where(lay.seg_first(i), 0.0, prev_ref[...].astype(F32))
    nxt = jnp.where(lay.seg_last(i), 0.0, next_ref[...].astype(F32))
    ext_ref[0:HALO, :] = prev
    ext_ref[HALO:HALO + TM, :] = main_ref[...].astype(F32)
    ext_ref[HALO + TM:HALO + TM + HALO, :] = nxt
    half = SSD_CONV // 2
    cw = 512
    for c0 in range(0, dc, cw):
        acc = jnp.zeros((TM, cw), F32) + b_ref[:, c0:c0 + cw]
        for k in range(SSD_CONV):
            acc = acc + ext_ref[HALO + k - half:HALO + k - half + TM, c0:c0 + cw] * w_ref[k:k + 1, c0:c0 + cw]
        y = _silu(acc).astype(BF16)
        if c0 < di:
            xs_ref[:, c0:c0 + cw] = y
        elif c0 < di + gn:
            bm_ref[:, c0 - di:c0 - di + cw] = y
        else:
            cm_ref[:, c0 - di - gn:c0 - di - gn + cw] = y


def _ssd_conv(lay, xbc, conv_w, conv_b, di, gn):
    t, dc = xbc.shape
    hb = TM // HALO
    nhb = t // HALO
    assert gn == 512 and di % 512 == 0
    return pl.pallas_call(
        functools.partial(_ssd_conv_kernel, lay, di, gn),
        grid=(lay.n_tiles,),
        in_specs=[pl.BlockSpec((HALO, dc), lambda i: (jnp.maximum(i * hb - 1, 0), 0)),
                  pl.BlockSpec((TM, dc), lambda i: (i, 0)),
                  pl.BlockSpec((HALO, dc), lambda i: (jnp.minimum((i + 1) * hb, nhb - 1), 0)),
                  pl.BlockSpec(conv_w.shape, lambda i: (0, 0)),
                  pl.BlockSpec(conv_b.shape, lambda i: (0, 0))],
        out_specs=[pl.BlockSpec((TM, di), lambda i: (i, 0)),
                   pl.BlockSpec((TM, gn), lambda i: (i, 0)),
                   pl.BlockSpec((TM, gn), lambda i: (i, 0))],
        out_shape=[jax.ShapeDtypeStruct((t, di), BF16),
                   jax.ShapeDtypeStruct((t, gn), BF16),
                   jax.ShapeDtypeStruct((t, gn), BF16)],
        scratch_shapes=[pltpu.VMEM((TM + 2 * HALO, dc), F32)],
        compiler_params=_cparams(("parallel",)),
        name="ssd_conv",
    )(xbc, xbc, xbc, conv_w, conv_b)


def _ssd_scan_kernel(xs_ref, bm_ref, cm_ref, dt_ref, dtt_ref, alr_ref, alc_ref, e_ref, y_ref, state_ref):
    d = pl.program_id(0)
    step = pl.program_id(2)
    q = SSD_CHUNK
    ng = SSD_N_GROUPS
    ns = SSD_D_STATE
    gw = xs_ref.shape[1] // ng
    hpg = gw // SSD_HEAD_DIM

    @pl.when(step == 0)
    def _():
        state_ref[...] = jnp.zeros_like(state_ref)

    sgn = jnp.where(d == 0, 1, -1)
    row = lax.broadcasted_iota(I32, (q, q), 0)
    col = lax.broadcasted_iota(I32, (q, q), 1)
    lmask = (row - col) * sgn >= 0
    lmask_t = (col - row) * sgn >= 0

    dt = dt_ref[...]
    dtt = dtt_ref[...]
    da = dt * (-jnp.exp(alr_ref[...]))
    dat = dtt * (-jnp.exp(alc_ref[...]))
    a_cum = jnp.dot(lmask.astype(F32), da, precision=HIGHEST, preferred_element_type=F32)
    a_cum_t = jnp.dot(dat, lmask_t.astype(F32), precision=HIGHEST, preferred_element_type=F32)
    a_end = jnp.sum(da, axis=0, keepdims=True)
    w_end = jnp.exp(a_end - a_cum) * dt

    e3 = e_ref[...]

    def expand(v):
        hi = v.astype(BF16)
        r1 = v - hi.astype(F32)
        mid = r1.astype(BF16)
        lo = (r1 - mid.astype(F32)).astype(BF16)
        return jnp.dot(jnp.concatenate([hi, mid, lo], axis=1), e3, preferred_element_type=F32)

    a_exp = expand(a_cum)
    w_exp = expand(w_end)
    end_exp = expand(jnp.broadcast_to(a_end, (SUBLANES, a_end.shape[1])))[0:1, :]
    decay_in = jnp.exp(a_exp)
    chunk_decay = jnp.exp(end_exp)
    xw = (xs_ref[...].astype(F32) * w_exp).astype(BF16)

    lane = lax.broadcasted_iota(I32, (q, LANES), 1)
    for g in range(ng):
        bg = bm_ref[:, g * ns:(g + 1) * ns]
        cg = cm_ref[:, g * ns:(g + 1) * ns]
        cb = lax.dot_general(cg, bg, (((1,), (1,)), ((), ())), preferred_element_type=F32)
        cb = jnp.where(lmask, cb, 0.0)
        s_in = state_ref[g]
        y_off = jnp.dot(cg, s_in.astype(BF16), preferred_element_type=F32)
        new_s = lax.dot_general(bg, xw[:, g * gw:(g + 1) * gw], (((0,), (0,)), ((), ())),
                                preferred_element_type=F32)
        state_ref[g] = s_in * chunk_decay[:, g * gw:(g + 1) * gw] + new_s
        for pr in range(hpg // 2):
            mixes = []
            for hh in range(2):
                h = g * hpg + 2 * pr + hh
                seg = a_cum[:, h:h + 1] - a_cum_t[h:h + 1, :]
                lh = jnp.exp(jnp.minimum(seg, 0.0))
                mixes.append((cb * lh * dtt[h:h + 1, :]).astype(BF16))
            lhs = jnp.concatenate(mixes, axis=1)
            l0 = g * gw + pr * LANES
            xp = xs_ref[:, l0:l0 + LANES]
            zero = jnp.zeros_like(xp)
            rhs = jnp.concatenate([jnp.where(lane < SSD_HEAD_DIM, xp, zero),
                                   jnp.where(lane >= SSD_HEAD_DIM, xp, zero)], axis=0)
            y_pair = jnp.dot(lhs, rhs, preferred_element_type=F32)
            y_pair = y_pair + y_off[:, pr * LANES:(pr + 1) * LANES] * decay_in[:, l0:l0 + LANES]
            y_ref[:, l0:l0 + LANES] = y_pair.astype(BF16)


def _ssd_scan(lay, xs, bm, cm, dt, dtt, a_log, expand):
    t, di = xs.shape
    gn = bm.shape[1]
    nh = dt.shape[2]
    q = SSD_CHUNK
    nct, nlt = lay.n_ctx // q, lay.seq // q
    nc = nct + nlt
    ctx_base = lay.t_lat // q

    def blk(d, b, s):
        j_ctx = jnp.where(d == 0, s, nct - 1 - s)
        j_lat = jnp.where(d == 0, s - nct, nlt - 1 - (s - nct))
        return jnp.where(s < nct, ctx_base + b * nct + j_ctx, b * nlt + j_lat)

    return pl.pallas_call(
        _ssd_scan_kernel,
        grid=(2, lay.batch, nc),
        in_specs=[pl.BlockSpec((q, di), lambda d, b, s: (blk(d, b, s), 0)),
                  pl.BlockSpec((q, gn), lambda d, b, s: (blk(d, b, s), 0)),
                  pl.BlockSpec((q, gn), lambda d, b, s: (blk(d, b, s), 0)),
                  pl.BlockSpec((None, q, nh), lambda d, b, s: (d, blk(d, b, s), 0)),
                  pl.BlockSpec((None, nh, q), lambda d, b, s: (d, 0, blk(d, b, s))),
                  pl.BlockSpec((None, 1, nh), lambda d, b, s: (d, 0, 0)),
                  pl.BlockSpec((None, nh, 1), lambda d, b, s: (d, 0, 0)),
                  pl.BlockSpec(expand.shape, lambda d, b, s: (0, 0))],
        out_specs=pl.BlockSpec((None, q, di), lambda d, b, s: (d, blk(d, b, s), 0)),
        out_shape=jax.ShapeDtypeStruct((2, t, di), BF16),
        scratch_shapes=[pltpu.VMEM((SSD_N_GROUPS, SSD_D_STATE, di // SSD_N_GROUPS), F32)],
        compiler_params=_cparams(("arbitrary", "arbitrary", "arbitrary")),
        name="ssd_scan",
    )(xs, bm, cm, dt, dtt, a_log.reshape(2, 1, nh), a_log.reshape(2, nh, 1), expand)


def _ssd_outproj_kernel(y_ref, xs_ref, z_ref, dexp_ref, ng_ref, w_ref, x_ref, mod_ref, o_ref):
    y = y_ref[0].astype(F32) + y_ref[1].astype(F32) + xs_ref[...].astype(F32) * dexp_ref[...]
    y = y * _silu(z_ref[...].astype(F32))
    gw = y.shape[1] // SSD_N_GROUPS
    parts = []
    for g in range(SSD_N_GROUPS):
        yg = y[:, g * gw:(g + 1) * gw]
        ms = jnp.mean(yg * yg, axis=-1, keepdims=True)
        parts.append((yg * lax.rsqrt(ms + NORM_EPS) * ng_ref[:, g * gw:(g + 1) * gw]).astype(BF16))
    yn = jnp.concatenate(parts, axis=1)
    m = jnp.dot(yn, w_ref[...], preferred_element_type=F32)
    o_ref[...] = x_ref[...] + mod_ref[2:3, :] * m


def _ssd_outproj(lay, y, xs, z, dexp, ng, w, x, mod_l):
    t, d = x.shape
    di = xs.shape[1]
    full = lambda a: pl.BlockSpec(a.shape, lambda i: (0,) * a.ndim)
    return pl.pallas_call(
        _ssd_outproj_kernel,
        grid=(lay.n_tiles,),
        in_specs=[pl.BlockSpec((2, TM, di), lambda i: (0, i, 0)),
                  pl.BlockSpec((TM, di), lambda i: (i, 0)),
                  pl.BlockSpec((TM, di), lambda i: (i, 0)),
                  full(dexp), full(ng), full(w),
                  pl.BlockSpec((TM, d), lambda i: (i, 0)),
                  pl.BlockSpec((None, N_MOD, d), lambda i: (lay.mod_row(i), 0, 0))],
        out_specs=pl.BlockSpec((TM, d), lambda i: (i, 0)),
        out_shape=jax.ShapeDtypeStruct((t, d), F32),
        compiler_params=_cparams(("parallel",)),
        name="ssd_outproj",
    )(y, xs, z, dexp, ng, w, x, mod_l)


def _att_inproj_kernel(x_ref, mod_ref, g_ref, wq_ref, wkt_ref, wv_ref, vone_ref, qg_ref, kgt_ref, gsum_ref, gexp_ref,
                       cos_ref, sin_ref, cost_ref, sint_ref, q_ref, kt_ref, v_ref):
    h = _norm_mod(x_ref[...], g_ref[...], mod_ref[0:1, :], mod_ref[1:2, :]).astype(BF16)
    hd = ATT_HEAD_DIM
    q = jnp.dot(h, wq_ref[...], preferred_element_type=F32)
    ssum = jnp.dot((q * q).astype(BF16), gsum_ref[...], preferred_element_type=F32)
    r = lax.rsqrt(ssum * (1.0 / hd) + NORM_EPS)
    r_hi = r.astype(BF16)
    r_lo = (r - r_hi.astype(F32)).astype(BF16)
    r_exp = jnp.dot(jnp.concatenate([r_hi, r_lo], axis=1), gexp_ref[...], preferred_element_type=F32)
    qn = q * r_exp * qg_ref[...]
    nl = qn.shape[1]
    lane = lax.broadcasted_iota(I32, qn.shape, 1)
    partner = jnp.where((lane & 1) == 0, pltpu.roll(qn, nl - 1, 1), pltpu.roll(qn, 1, 1))
    reps = nl // LANES
    cos = jnp.concatenate([cos_ref[...]] * reps, axis=1)
    sin = jnp.concatenate([sin_ref[...]] * reps, axis=1)
    q_ref[...] = (qn * cos + partner * sin).astype(BF16)
    kt = lax.dot_general(wkt_ref[...], h, (((1,), (1,)), ((), ())), preferred_element_type=F32)
    ct, st = cost_ref[...], sint_ref[...]
    sub = lax.broadcasted_iota(I32, ct.shape, 0)
    for kh in range(ATT_KV_HEADS):
        blk = kt[kh * hd:(kh + 1) * hd]
        rk = lax.rsqrt(jnp.sum(blk * blk, axis=0, keepdims=True) * (1.0 / hd) + NORM_EPS)
        kn = blk * rk * kgt_ref[...]
        kpart = jnp.where((sub & 1) == 0, pltpu.roll(kn, hd - 1, 0), pltpu.roll(kn, 1, 0))
        kt_ref[kh] = (kn * ct + kpart * st).astype(BF16)
    v = (jnp.dot(h, wv_ref[...], preferred_element_type=F32) + vone_ref[...]).astype(BF16)
    for kh in range(ATT_KV_HEADS):
        v_ref[kh] = v[:, kh * LANES:(kh + 1) * LANES]


def _att_inproj(lay, x, mod_l, g, wq, wkt, wv2, vone, qg, kgt, gsum, gexp, cos, sin, cost, sint):
    t, d = x.shape
    dq = wq.shape[1]
    hd = ATT_HEAD_DIM
    full = lambda a: pl.BlockSpec(a.shape, lambda i: (0,) * a.ndim)
    return pl.pallas_call(
        _att_inproj_kernel,
        grid=(lay.n_tiles,),
        in_specs=[pl.BlockSpec((TM, d), lambda i: (i, 0)),
                  pl.BlockSpec((None, N_MOD, d), lambda i: (lay.mod_row(i), 0, 0)),
                  full(g), full(wq), full(wkt), full(wv2), full(vone), full(qg), full(kgt), full(gsum), full(gexp),
                  pl.BlockSpec((TM, LANES), lambda i: (lay.tile_pos(i), 0)),
                  pl.BlockSpec((TM, LANES), lambda i: (lay.tile_pos(i), 0)),
                  pl.BlockSpec((hd, TM), lambda i: (0, lay.tile_pos(i))),
                  pl.BlockSpec((hd, TM), lambda i: (0, lay.tile_pos(i)))],
        out_specs=[pl.BlockSpec((TM, dq), lambda i: (i, 0)),
                   pl.BlockSpec((None, ATT_KV_HEADS, hd, TM), lambda i: (lay.tile_batch(i), 0, 0, lay.tile_pos(i))),
                   pl.BlockSpec((None, ATT_KV_HEADS, TM, LANES), lambda i: (lay.tile_batch(i), 0, lay.tile_pos(i), 0))],
        out_shape=[jax.ShapeDtypeStruct((t, dq), BF16),
                   jax.ShapeDtypeStruct((lay.batch, ATT_KV_HEADS, hd, lay.nk), BF16),
                   jax.ShapeDtypeStruct((lay.batch, ATT_KV_HEADS, lay.nk, LANES), BF16)],
        compiler_params=_cparams(("parallel",)),
        name="att_inproj",
    )(x, mod_l, g, wq, wkt, wv2, vone, qg, kgt, gsum, gexp, cos, sin, cost, sint)


def _attend(q_ref, kt_ref, v_ref, o_ref, k0, n_keys):
    hd = ATT_HEAD_DIM
    tm = q_ref.shape[0]
    heads = q_ref.shape[1] // hd
    qs = [q_ref[:, u * hd:(u + 1) * hd] for u in range(heads)]

    bk = min(ATT_KEY_BLOCK, n_keys)
    assert n_keys % bk == 0 and k0 % LANES == 0 and bk % LANES == 0

    def body(j, carry):
        start = pl.multiple_of(k0 + j * bk, LANES)
        kb = kt_ref[:, pl.ds(start, bk)]
        vb = v_ref[pl.ds(start, bk), :]
        out = []
        for u in range(heads):
            m_old, acc = carry[u]
            s = jnp.dot(qs[u], kb, preferred_element_type=F32)
            m_new = jnp.maximum(m_old, jnp.max(s, axis=-1, keepdims=True))
            p = jnp.exp2((s - m_new).astype(BF16))
            acc = jnp.exp2(m_old - m_new) * acc + jnp.dot(p, vb, preferred_element_type=F32)
            out.append((m_new, acc))
        return tuple(out)

    init = tuple((jnp.full((tm, 1), -jnp.inf, F32), jnp.zeros((tm, LANES), F32)) for _ in range(heads))
    final = lax.fori_loop(0, n_keys // bk, body, init) if n_keys > bk else body(0, init)
    outs = [acc[:, :hd] / acc[:, hd:hd + 1] for _, acc in final]
    o_ref[...] = jnp.concatenate(outs, axis=1).astype(BF16)


def _att_core_kernel(lay, q_ref, kt_ref, v_ref, o_ref):
    qi = pl.program_id(2)

    @pl.when(qi < lay.seq_tiles)
    def _():
        _attend(q_ref, kt_ref, v_ref, o_ref, 0, lay.nk)

    @pl.when(qi >= lay.seq_tiles)
    def _():
        _attend(q_ref, kt_ref, v_ref, o_ref, lay.seq, lay.n_ctx)


def _att_core(lay, qn, kt, v):
    t, dq = qn.shape
    hd = ATT_HEAD_DIM
    qw = dq // ATT_KV_HEADS
    per_b = lay.seq_tiles + lay.ctx_tiles

    def row_tile(b, qi):
        return jnp.where(qi < lay.seq_tiles, b * lay.seq_tiles + qi,
                         lay.lat_tiles + b * lay.ctx_tiles + (qi - lay.seq_tiles))

    return pl.pallas_call(
        functools.partial(_att_core_kernel, lay),
        grid=(lay.batch, ATT_KV_HEADS, per_b),
        in_specs=[pl.BlockSpec((TM, qw), lambda b, kh, qi: (row_tile(b, qi), kh)),
                  pl.BlockSpec((None, None, hd, lay.nk), lambda b, kh, qi: (b, kh, 0, 0)),
                  pl.BlockSpec((None, None, lay.nk, LANES), lambda b, kh, qi: (b, kh, 0, 0))],
        out_specs=pl.BlockSpec((TM, qw), lambda b, kh, qi: (row_tile(b, qi), kh)),
        out_shape=jax.ShapeDtypeStruct((t, dq), BF16),
        compiler_params=_cparams(("parallel", "parallel", "arbitrary")),
        name="att_core",
    )(qn, kt, v)


def _att_outproj_kernel(o_ref, w_ref, x_ref, mod_ref, out_ref):
    m = jnp.dot(o_ref[...], w_ref[...], preferred_element_type=F32)
    out_ref[...] = x_ref[...] + mod_ref[2:3, :] * m


def _att_outproj(lay, o, w, x, mod_l):
    t, d = x.shape
    return pl.pallas_call(
        _att_outproj_kernel,
        grid=(lay.n_tiles,),
        in_specs=[pl.BlockSpec((TM, o.shape[1]), lambda i: (i, 0)),
                  pl.BlockSpec(w.shape, lambda i: (0, 0)),
                  pl.BlockSpec((TM, d), lambda i: (i, 0)),
                  pl.BlockSpec((None, N_MOD, d), lambda i: (lay.mod_row(i), 0, 0))],
        out_specs=pl.BlockSpec((TM, d), lambda i: (i, 0)),
        out_shape=jax.ShapeDtypeStruct((t, d), F32),
        compiler_params=_cparams(("parallel",)),
        name="att_outproj",
    )(o, w, x, mod_l)


R_EID, R_RANK, R_W = 0, 2, 4
GROUP_LANE0 = MOE_EXPERTS


def _store_token_rows(ref, val):
    rows, d = val.shape
    for s in range(d // LANES):
        ref[pl.ds(s, rows, stride=d // LANES), :] = val[:, s * LANES:(s + 1) * LANES]


def _load_token_rows(ref, d):
    n = d // LANES
    rows = ref.shape[0] // n
    return jnp.concatenate([ref[pl.ds(s, rows, stride=n), :] for s in range(n)], axis=1)


def _token(ref, r, n):
    return ref.at[pl.ds(pl.multiple_of(r * n, n), n)]


def _router_kernel(x_ref, mod_ref, g_ref, wr_ref, br_ref, f_ref, r_ref, rt_ref, cnt_ref, base_ref):
    i = pl.program_id(0)

    @pl.when(i == 0)
    def _():
        base_ref[...] = jnp.zeros_like(base_ref)

    f = _norm_mod(x_ref[...], g_ref[...], mod_ref[3:4, :], mod_ref[4:5, :])
    _store_token_rows(f_ref, f)
    f_hi = f.astype(BF16)
    f_lo = (f - f_hi.astype(F32)).astype(BF16)
    part = jnp.dot(f_hi, wr_ref[...], preferred_element_type=F32)
    part_lo = jnp.dot(f_lo, wr_ref[:, :LANES], preferred_element_type=F32)
    logits = part[:, :LANES] + part[:, LANES:] + part_lo + br_ref[...]
    lane = lax.broadcasted_iota(I32, logits.shape, 1)
    neg = jnp.float32(-jnp.inf)
    big = jnp.int32(LANES)

    def first_max(vals):
        top = jnp.max(vals, axis=-1, keepdims=True)
        idx = jnp.min(jnp.where(vals == top, lane, big), axis=-1, keepdims=True)
        return top, idx

    g_mask = (lane >= GROUP_LANE0) & (lane < GROUP_LANE0 + MOE_GROUPS)
    glog = jnp.where(g_mask, logits, neg)
    g_top, g_idx = first_max(glog)
    g_w = 1.0 / jnp.sum(jnp.exp(glog - g_top), axis=-1, keepdims=True)
    e0 = (g_idx - GROUP_LANE0) * MOE_EPG
    elog = jnp.where((lane >= e0) & (lane < e0 + MOE_EPG), logits, neg)
    v1, i1 = first_max(elog)
    v2, i2 = first_max(jnp.where(lane == i1, neg, elog))
    ex = jnp.exp(v2 - v1)
    w1 = g_w / (1.0 + ex)
    w2 = g_w * ex / (1.0 + ex)

    oh1 = (lane == i1).astype(F32)
    oh2 = (lane == i2).astype(F32)
    tm = logits.shape[0]
    rr = lax.broadcasted_iota(I32, (tm, tm), 0)
    cc = lax.broadcasted_iota(I32, (tm, tm), 1)
    before = (cc < rr).astype(BF16)
    cum1 = jnp.dot(before, oh1.astype(BF16), preferred_element_type=F32)
    cum2 = jnp.dot(before, oh2.astype(BF16), preferred_element_type=F32)
    tot1 = jnp.sum(oh1, axis=0, keepdims=True)
    tot2 = jnp.sum(oh2, axis=0, keepdims=True)
    base = base_ref[...]
    rank1 = jnp.sum(oh1 * (cum1 + base), axis=-1, keepdims=True)
    rank2 = jnp.sum(oh2 * (cum2 + base + tot1), axis=-1, keepdims=True)
    base = base + tot1 + tot2
    base_ref[...] = base
    cnt_ref[...] = base

    rec = jnp.zeros(logits.shape, F32)
    for k, val in ((R_EID, i1.astype(F32)), (R_EID + 1, i2.astype(F32)), (R_RANK, rank1), (R_RANK + 1, rank2),
                   (R_W, w1), (R_W + 1, w2)):
        rec = jnp.where(lane == k, val, rec)
    r_ref[...] = rec
    rt_ref[...] = rec.T[:SUBLANES, :]


def _router(lay, n_tiles, x, mod_l, g, wr, br):
    t, d = x.shape
    return pl.pallas_call(
        _router_kernel,
        grid=(n_tiles,),
        in_specs=[pl.BlockSpec((TM, d), lambda i: (i, 0)),
                  pl.BlockSpec((None, N_MOD, d), lambda i: (lay.mod_row(i), 0, 0)),
                  pl.BlockSpec(g.shape, lambda i: (0, 0)),
                  pl.BlockSpec(wr.shape, lambda i: (0, 0)),
                  pl.BlockSpec(br.shape, lambda i: (0, 0))],
        out_specs=[pl.BlockSpec((TM * (d // LANES), LANES), lambda i: (i, 0)),
                   pl.BlockSpec((TM, LANES), lambda i: (i, 0)),
                   pl.BlockSpec((SUBLANES, TM), lambda i: (0, i)),
                   pl.BlockSpec((1, LANES), lambda i: (0, 0))],
        out_shape=[jax.ShapeDtypeStruct((n_tiles * TM * (d // LANES), LANES), F32),
                   jax.ShapeDtypeStruct((n_tiles * TM, LANES), F32),
                   jax.ShapeDtypeStruct((SUBLANES, n_tiles * TM), F32),
                   jax.ShapeDtypeStruct((1, LANES), F32)],
        scratch_shapes=[pltpu.VMEM((1, LANES), F32)],
        compiler_params=_cparams(("arbitrary",)),
        name="moe_router",
    )(x, mod_l, g, wr, br)


def _dispatch_kernel(fill_ref, plen_ref, dest_hbm, f_ref, xs_hbm, dest_s, zbuf, dsem, fsem, ssem):
    i = pl.program_id(0)
    n = pl.num_programs(0)
    tn = f_ref.shape[0] // TM

    def dest_copy(j):
        return pltpu.make_async_copy(dest_hbm.at[j], dest_s.at[pl.ds((j % 2) * DEST_ROW, DEST_ROW)], dsem.at[j % 2])

    def pad_pieces(e, fn):
        plen = plen_ref[e]
        for bit in range(ZROWS.bit_length() - 1, -1, -1):
            size = 1 << bit

            @pl.when((plen & size) != 0)
            def _():
                start = fill_ref[e] + ((plen >> (bit + 1)) << (bit + 1))
                fn(pltpu.make_async_copy(zbuf.at[pl.ds(0, size * tn)],
                                         xs_hbm.at[pl.ds(pl.multiple_of(start * tn, tn), size * tn)], fsem.at[0]))

    @pl.when(i == 0)
    def _():
        zbuf[...] = jnp.zeros_like(zbuf)
        dest_copy(0).start()

        def start_e(e, c):
            pad_pieces(e, lambda cp: cp.start())
            return c

        def wait_e(e, c):
            pad_pieces(e, lambda cp: cp.wait())
            return c

        def tail_pieces(fn):
            def piece(t, c):
                start = fill_ref[MOE_EXPERTS] + t * ZROWS
                fn(pltpu.make_async_copy(zbuf, xs_hbm.at[pl.ds(pl.multiple_of(start * tn, tn), ZROWS * tn)],
                                         fsem.at[0]))
                return c
            lax.fori_loop(0, plen_ref[MOE_EXPERTS] // ZROWS, piece, 0)

        lax.fori_loop(0, MOE_EXPERTS, start_e, 0)
        tail_pieces(lambda cp: cp.start())
        lax.fori_loop(0, MOE_EXPERTS, wait_e, 0)
        tail_pieces(lambda cp: cp.wait())

    @pl.when(i + 1 < n)
    def _():
        dest_copy(i + 1).start()

    dest_copy(i).wait()
    slot = i % 2

    def body(r, c):
        for k in range(2):
            row = dest_s[slot * DEST_ROW + k * TM + r]
            pltpu.make_async_copy(_token(f_ref, r, tn), _token(xs_hbm, row, tn), ssem.at[0]).start(priority=k)
        return c

    lax.fori_loop(0, TM, body, 0, unroll=8)
    for k in range(2):
        pltpu.make_async_copy(f_ref, xs_hbm.at[pl.ds(0, TM * tn)], ssem.at[0]).wait()


def _dispatch(fill, plen, dest, f, cap):
    n_tiles = dest.shape[0]
    tn = f.shape[0] // (n_tiles * TM)
    grid_spec = pltpu.PrefetchScalarGridSpec(
        num_scalar_prefetch=2,
        grid=(n_tiles,),
        in_specs=[pl.BlockSpec(memory_space=pl.ANY),
                  pl.BlockSpec((TM * tn, LANES), lambda i, fill, plen: (i, 0))],
        out_specs=pl.BlockSpec(memory_space=pl.ANY),
        scratch_shapes=[pltpu.SMEM((2 * DEST_ROW,), I32),
                        pltpu.VMEM((ZROWS * tn, LANES), F32),
                        pltpu.SemaphoreType.DMA((2,)),
                        pltpu.SemaphoreType.DMA((1,)),
                        pltpu.SemaphoreType.DMA((1,))],
    )
    return pl.pallas_call(
        _dispatch_kernel,
        grid_spec=grid_spec,
        out_shape=jax.ShapeDtypeStruct((cap * tn, LANES), F32),
        compiler_params=_cparams(("arbitrary",)),
        name="moe_dispatch",
    )(fill, plen, dest, f)


def _expert_kernel(layer, be_ref, nv_ref, blk_ref, grp_ref, nxt_ref, x_ref, wg_hbm, wu_hbm, wd_hbm, y_ref,
                   wgf, wuf, wdf, wgb, wub, wdb, wsem):
    i = pl.program_id(0)
    live = nv_ref[i] > 0
    first = jnp.logical_or(i == 0, be_ref[i] != be_ref[jnp.maximum(i - 1, 0)])
    slot = grp_ref[i] % 2

    def fetch(e, s):
        return (pltpu.make_async_copy(wg_hbm.at[layer, e], wgf.at[s], wsem.at[s]),
                pltpu.make_async_copy(wu_hbm.at[layer, e], wuf.at[s], wsem.at[s]),
                pltpu.make_async_copy(wd_hbm.at[layer, e], wdf.at[s], wsem.at[s]))

    @pl.when(i == 0)
    def _():
        for cp in fetch(be_ref[0], 0):
            cp.start()

    @pl.when(jnp.logical_and(live, first))
    def _():
        for cp in fetch(be_ref[i], slot):
            cp.wait()

        @pl.when(nxt_ref[i] >= 0)
        def _():
            for cp in fetch(nxt_ref[i], 1 - slot):
                cp.start()

        wgb[...] = wgf[slot].astype(BF16)
        wub[...] = wuf[slot].astype(BF16)
        wdb[...] = wdf[slot].astype(BF16)

    @pl.when(live)
    def _():
        xb = _load_token_rows(x_ref, wgb.shape[0]).astype(BF16)
        hg = jnp.dot(xb, wgb[...], preferred_element_type=F32)
        hu = jnp.dot(xb, wub[...], preferred_element_type=F32)
        act = (_silu(hg) * hu).astype(BF16)
        _store_token_rows(y_ref, jnp.dot(act, wdb[...], preferred_element_type=F32))

    @pl.when(jnp.logical_not(live))
    def _():
        y_ref[...] = jnp.zeros_like(y_ref)


def _experts(layer, block_e, block_nv, block_src, block_grp, block_nxt, xs, wg, wu, wd):
    n_blocks = block_e.shape[0]
    d, dff = wg.shape[2:]
    rb = MOE_ROW_BLOCK
    blk = (rb * (d // LANES), LANES)
    grid_spec = pltpu.PrefetchScalarGridSpec(
        num_scalar_prefetch=5,
        grid=(n_blocks,),
        in_specs=[pl.BlockSpec(blk, lambda i, be, nv, src, grp, nxt: (src[i], 0)),
                  pl.BlockSpec(memory_space=pl.ANY),
                  pl.BlockSpec(memory_space=pl.ANY),
                  pl.BlockSpec(memory_space=pl.ANY)],
        out_specs=pl.BlockSpec(blk, lambda i, be, nv, src, grp, nxt: (i, 0)),
        scratch_shapes=[pltpu.VMEM((2, d, dff), F32),
                        pltpu.VMEM((2, d, dff), F32),
                        pltpu.VMEM((2, dff, d), F32),
                        pltpu.VMEM((d, dff), BF16),
                        pltpu.VMEM((d, dff), BF16),
                        pltpu.VMEM((dff, d), BF16),
                        pltpu.SemaphoreType.DMA((2,))],
    )
    return pl.pallas_call(
        functools.partial(_expert_kernel, layer),
        grid_spec=grid_spec,
        out_shape=jax.ShapeDtypeStruct(xs.shape, F32),
        compiler_params=_cparams(("arbitrary",)),
        name="moe_experts",
    )(block_e, block_nv, block_src, block_grp, block_nxt, xs, wg, wu, wd)


def _combine_kernel(dest_hbm, x_ref, r_ref, mod_ref, ys_hbm, o_ref, dest_s, gbuf, dsem, gsem):
    i = pl.program_id(0)
    n = pl.num_programs(0)
    tn = x_ref.shape[1] // LANES

    def dest_copy(j):
        return pltpu.make_async_copy(dest_hbm.at[j], dest_s.at[pl.ds((j % 3) * DEST_ROW, DEST_ROW)], dsem.at[j % 3])

    def gather_tile(j):
        def body(r, c):
            for k in range(2):
                row = dest_s[(j % 3) * DEST_ROW + k * TM + r]
                pltpu.make_async_copy(_token(ys_hbm, row, tn), _token(gbuf.at[j % 2, k], r, tn),
                                      gsem.at[j % 2]).start(priority=k)
            return c
        lax.fori_loop(0, TM, body, 0, unroll=8)

    @pl.when(i == 0)
    def _():
        dest_copy(0).start()
        dest_copy(0).wait()
        gather_tile(0)

        @pl.when(n > 1)
        def _():
            dest_copy(1).start()

    @pl.when(i + 1 < n)
    def _():
        dest_copy(i + 1).wait()

        @pl.when(i + 2 < n)
        def _():
            dest_copy(i + 2).start()

        gather_tile(i + 1)

    slot = i % 2
    for k in range(2):
        pltpu.make_async_copy(ys_hbm.at[pl.ds(0, TM * tn)], gbuf.at[slot, k], gsem.at[slot]).wait()
    w1 = r_ref[:, R_W:R_W + 1]
    w2 = r_ref[:, R_W + 1:R_W + 2]
    d = x_ref.shape[1]
    y = _load_token_rows(gbuf.at[slot, 0], d) * w1 + _load_token_rows(gbuf.at[slot, 1], d) * w2
    o_ref[...] = x_ref[...] + mod_ref[5:6, :] * y


def _combine(lay, n_tiles, dest, x, rec, mod_l, ys):
    d = x.shape[1]
    return pl.pallas_call(
        _combine_kernel,
        grid=(n_tiles,),
        in_specs=[pl.BlockSpec(memory_space=pl.ANY),
                  pl.BlockSpec((TM, d), lambda i: (i, 0)),
                  pl.BlockSpec((TM, LANES), lambda i: (i, 0)),
                  pl.BlockSpec((None, N_MOD, d), lambda i: (lay.mod_row(i), 0, 0)),
                  pl.BlockSpec(memory_space=pl.ANY)],
        out_specs=pl.BlockSpec((TM, d), lambda i: (i, 0)),
        out_shape=jax.ShapeDtypeStruct((n_tiles * TM, d), F32),
        scratch_shapes=[pltpu.SMEM((3 * DEST_ROW,), I32),
                        pltpu.VMEM((2, 2, TM * (d // LANES), LANES), F32),
                        pltpu.SemaphoreType.DMA((3,)),
                        pltpu.SemaphoreType.DMA((2,))],
        compiler_params=_cparams(("arbitrary",)),
        name="moe_combine",
    )(dest, x, rec, mod_l, ys)


def _moe(lay, layer, n_tiles, x, mod_l, g2, w_group, b_group, w_router, b_router, wg, wu, wd):
    d = x.shape[1]
    t_tok = n_tiles * TM
    pad = LANES - MOE_EXPERTS - MOE_GROUPS
    wr = jnp.concatenate([w_router, w_group, jnp.zeros((d, pad), F32)], axis=1)
    br = jnp.concatenate([b_router, b_group, jnp.zeros((pad,), F32)])[None, :]
    wr_hi = wr.astype(BF16)
    wr_lo = (wr - wr_hi.astype(F32)).astype(BF16)
    f, rec, rec_t, cnt = _router(lay, n_tiles, x, mod_l, g2, jnp.concatenate([wr_hi, wr_lo], axis=1), br)

    rb = MOE_ROW_BLOCK
    eid = rec_t[R_EID:R_EID + 2].astype(I32)
    rank = rec_t[R_RANK:R_RANK + 2].astype(I32)
    counts = cnt[0, :MOE_EXPERTS].astype(I32)
    padded = (counts + rb - 1) // rb * rb
    pad_ends = jnp.cumsum(padded)
    pad_starts = pad_ends - padded
    experts = jnp.arange(MOE_EXPERTS, dtype=I32)[:, None, None]
    dest = jnp.sum(jnp.where(eid[None] == experts, pad_starts[:, None, None], 0), axis=0) + rank
    dest = dest.reshape(2, n_tiles, TM).transpose(1, 0, 2).reshape(n_tiles, 2 * TM)
    dest = jnp.pad(dest, ((0, 0), (0, DEST_ROW - 2 * TM)))
    n_blocks = -(-(2 * t_tok) // rb) + MOE_EXPERTS
    used = pad_ends[-1] // rb
    idx = jnp.arange(n_blocks, dtype=I32)
    block_src = jnp.clip(idx, 0, jnp.maximum(used - 1, 0))
    start = block_src * rb
    block_e = jnp.sum(pad_ends[None, :] <= start[:, None], axis=1).astype(I32)
    block_nv = jnp.where(idx < used, jnp.clip(counts[block_e] - (start - pad_starts[block_e]), 0, rb), 0).astype(I32)

    cap = n_blocks * rb
    fill = jnp.concatenate([pad_starts + counts, pad_ends[-1:]]).astype(I32)
    plen = jnp.concatenate([padded - counts, cap - pad_ends[-1:]]).astype(I32)
    xs = _dispatch(fill, plen, dest, f, cap)
    nonempty = counts > 0
    e_ids = jnp.arange(MOE_EXPERTS, dtype=I32)
    later = jnp.where((e_ids[None, :] > e_ids[:, None]) & nonempty[None, :], e_ids[None, :], MOE_EXPERTS)
    nxt_of_e = jnp.min(later, axis=1)
    nxt_of_e = jnp.where(nxt_of_e == MOE_EXPERTS, -1, nxt_of_e).astype(I32)
    grp_of_e = (jnp.cumsum(nonempty) - nonempty).astype(I32)
    ys = _experts(layer, block_e, block_nv, block_src, grp_of_e[block_e], nxt_of_e[block_e], xs, wg, wu, wd)
    return _combine(lay, n_tiles, dest, x, rec, mod_l, ys)


def _rope_tables(lay):
    half = ATT_HEAD_DIM // 2
    pos = jnp.arange(lay.seq)
    rowp = (pos // GRID_W).astype(F32)
    colp = (pos % GRID_W).astype(F32)
    freqs = ROPE_THETA ** (-jnp.arange(0, half, 2, dtype=F32) / half)
    ang = jnp.concatenate([rowp[:, None] * freqs, colp[:, None] * freqs], axis=-1)
    cos = jnp.concatenate([jnp.cos(ang), jnp.ones((lay.n_ctx, half), F32)], axis=0)
    sin = jnp.concatenate([jnp.sin(ang), jnp.zeros((lay.n_ctx, half), F32)], axis=0)
    cos_d = jnp.repeat(cos, 2, axis=1)
    sin_d = jnp.repeat(sin, 2, axis=1) * jnp.tile(jnp.asarray([-1.0, 1.0], F32), half)
    reps = LANES // ATT_HEAD_DIM
    return jnp.tile(cos_d, (1, reps)), jnp.tile(sin_d, (1, reps)), cos_d.T, sin_d.T


def kernel(x, c, ctx, c_ctx, mod_w, mod_b, norm1_g, norm2_g, ssd_w_in, ssd_conv_w, ssd_conv_b, ssd_dt_bias,
           ssd_a_log, ssd_d, ssd_norm_g, ssd_w_out, att_w_qkv, att_q_gain, att_k_gain, att_w_o, moe_w_group,
           moe_b_group, moe_w_router, moe_b_router, moe_w_gate, moe_w_up, moe_w_down):
    b, n_lat, d = x.shape
    n_ctx = ctx.shape[1]
    depth = mod_w.shape[0]
    lay = _Layout(b, n_lat, n_ctx)
    assert b + 1 <= 8

    xs = jnp.concatenate([x.reshape(lay.t_lat, d), ctx.reshape(lay.t_ctx, d)], axis=0)
    c8 = jnp.concatenate([c, c_ctx[None, :], jnp.zeros((8 - b - 1, d), F32)], axis=0)
    mods = _mod_table(c8, mod_w, mod_b).reshape(depth, 8, N_MOD, d)

    nh = ssd_dt_bias.shape[2]
    di = nh * SSD_HEAD_DIM
    gn = SSD_N_GROUPS * SSD_D_STATE
    dc = di + 2 * gn
    expand = jnp.tile(jnp.repeat(jnp.eye(nh, dtype=BF16), SSD_HEAD_DIM, axis=1), (3, 1))

    hq = att_w_qkv.shape[2] // ATT_HEAD_DIM - 2 * ATT_KV_HEADS
    dq = hq * ATT_HEAD_DIM
    dkv = ATT_KV_HEADS * ATT_HEAD_DIM
    gsum = np.zeros((dq, LANES), np.float32)
    gsum[np.arange(dq), np.arange(dq) // ATT_HEAD_DIM] = 1.0
    gexp = np.concatenate([gsum.T, gsum.T], axis=0)
    vone = np.zeros((1, ATT_KV_HEADS * LANES), np.float32)
    vone[0, np.arange(ATT_KV_HEADS) * LANES + ATT_HEAD_DIM] = 1.0
    cos128, sin128, cos_t, sin_t = _rope_tables(lay)

    for layer in range(depth):
        last = layer == depth - 1
        mod_l = mods[layer]
        j = layer // 2
        g1 = norm1_g[layer][None, :]
        if layer % 2 == 0:
            w_in = ssd_w_in[j].astype(BF16)
            wz, wx, wdt = w_in[:, :di], w_in[:, di:di + dc], w_in[:, di + dc:]
            dtb = ssd_dt_bias[j].reshape(1, 2 * nh)
            z, xbc, dt, dtt = _ssd_inproj(lay, xs, mod_l, g1, wz, wx, wdt, wdt.T, dtb, dtb.T)
            xc, bm, cm = _ssd_conv(lay, xbc, ssd_conv_w[j], ssd_conv_b[j][None, :], di, gn)
            y = _ssd_scan(lay, xc, bm, cm, dt, dtt, ssd_a_log[j], expand)
            dexp = jnp.repeat(ssd_d[j], SSD_HEAD_DIM)[None, :]
            xs = _ssd_outproj(lay, y, xc, z, dexp, ssd_norm_g[j][None, :], ssd_w_out[j].astype(BF16), xs, mod_l)
        else:
            w = att_w_qkv[j]
            wq = w[:, :dq].astype(BF16)
            wkt = w[:, dq:dq + dkv].T.astype(BF16)
            wv = w[:, dq + dkv:].reshape(d, ATT_KV_HEADS, ATT_HEAD_DIM)
            wv2 = jnp.pad(wv, ((0, 0), (0, 0), (0, LANES - ATT_HEAD_DIM))).reshape(d, ATT_KV_HEADS * LANES).astype(BF16)
            qg = (jnp.tile(att_q_gain[j], hq) * (ATT_HEAD_DIM ** -0.5 * LOG2E))[None, :]
            kgt = att_k_gain[j][:, None]
            qn, kt, v = _att_inproj(lay, xs, mod_l, g1, wq, wkt, wv2, jnp.asarray(vone), qg, kgt, jnp.asarray(gsum, BF16),
                                    jnp.asarray(gexp, BF16), cos128, sin128, cos_t, sin_t)
            o = _att_core(lay, qn, kt, v)
            xs = _att_outproj(lay, o, att_w_o[j].astype(BF16), xs, mod_l)
        n_tiles = lay.lat_tiles if last else lay.n_tiles
        xs = _moe(lay, layer, n_tiles, xs, mod_l, norm2_g[layer][None, :], moe_w_group[layer], moe_b_group[layer],
                  moe_w_router[layer], moe_b_router[layer], moe_w_gate, moe_w_up, moe_w_down)
    return xs[:lay.t_lat].reshape(b, n_lat, d)
```

```python
import functools

import numpy as np
import jax
import jax.numpy as jnp
from jax import lax
from jax.experimental import pallas as pl
from jax.experimental.pallas import tpu as pltpu

F32 = jnp.float32
BF16 = jnp.bfloat16
I32 = jnp.int32
HIGHEST = lax.Precision.HIGHEST

NORM_EPS = 1e-6
N_MOD = 6
GRID_W = 64
ROPE_THETA = 10000.0

SSD_HEAD_DIM = 64
SSD_N_GROUPS = 4
SSD_D_STATE = 128
SSD_CONV = 5
SSD_CHUNK = 128

ATT_HEAD_DIM = 64
ATT_KV_HEADS = 4
ATT_KEY_BLOCK = 4352
LOG2E = 1.4426950408889634
ATT_MIN_DENOM = 2.0 ** -60

MOE_GROUPS = 4
MOE_EPG = 8
MOE_EXPERTS = MOE_GROUPS * MOE_EPG
MOE_ROW_BLOCK = 256
ZROWS = MOE_ROW_BLOCK // 2
DEST_ROW = 1024

TM = 256
LANES = 128
SUBLANES = 8
HALO = 16
VMEM_LIMIT = 56 * 1024 * 1024


def _cparams(sem):
    return pltpu.CompilerParams(dimension_semantics=sem, vmem_limit_bytes=VMEM_LIMIT)


def _silu(v):
    return v / (1.0 + jnp.exp(-v))


def _softplus(v):
    return jnp.maximum(v, 0.0) + jnp.log1p(jnp.exp(-jnp.abs(v)))


def _norm_mod(x, g, shift, scale):
    ms = jnp.mean(x * x, axis=-1, keepdims=True)
    y = x * lax.rsqrt(ms + NORM_EPS) * g
    return y * (1.0 + scale) + shift


class _Layout:
    def __init__(self, batch, seq, n_ctx):
        assert seq % TM == 0 and n_ctx % TM == 0
        self.batch, self.seq, self.n_ctx = batch, seq, n_ctx
        self.t_lat = batch * seq
        self.t_ctx = batch * n_ctx
        self.t = self.t_lat + self.t_ctx
        self.lat_tiles = self.t_lat // TM
        self.seq_tiles = seq // TM
        self.ctx_tiles = n_ctx // TM
        self.n_tiles = self.t // TM
        self.nk = seq + n_ctx

    def mod_row(self, i):
        return jnp.where(i < self.lat_tiles, i // self.seq_tiles, self.batch)

    def tile_batch(self, i):
        return jnp.where(i < self.lat_tiles, i // self.seq_tiles, (i - self.lat_tiles) // self.ctx_tiles)

    def tile_pos(self, i):
        return jnp.where(i < self.lat_tiles, i % self.seq_tiles,
                         self.seq_tiles + (i - self.lat_tiles) % self.ctx_tiles)

    def seg_first(self, i):
        return jnp.where(i < self.lat_tiles, i % self.seq_tiles == 0, (i - self.lat_tiles) % self.ctx_tiles == 0)

    def seg_last(self, i):
        return jnp.where(i < self.lat_tiles, i % self.seq_tiles == self.seq_tiles - 1,
                         (i - self.lat_tiles) % self.ctx_tiles == self.ctx_tiles - 1)


def _mod_kernel(c_ref, w_ref, b_ref, o_ref):
    s = _silu(c_ref[...])
    o_ref[...] = jnp.dot(s, w_ref[...], precision=HIGHEST, preferred_element_type=F32) + b_ref[...]


def _mod_table(c8, mod_w, mod_b):
    depth, d, n = mod_w.shape
    tn = 1536
    assert n % tn == 0
    return pl.pallas_call(
        _mod_kernel,
        grid=(depth, n // tn),
        in_specs=[pl.BlockSpec((8, d), lambda l, j: (0, 0)),
                  pl.BlockSpec((None, d, tn), lambda l, j: (l, 0, j)),
                  pl.BlockSpec((None, 1, tn), lambda l, j: (l, 0, j))],
        out_specs=pl.BlockSpec((None, 8, tn), lambda l, j: (l, 0, j)),
        out_shape=jax.ShapeDtypeStruct((depth, 8, n), F32),
        compiler_params=_cparams(("parallel", "parallel")),
        name="mod_table",
    )(c8, mod_w, mod_b.reshape(depth, 1, n))


def _ssd_inproj_kernel(x_ref, mod_ref, g_ref, wz_ref, wx_ref, wdt_ref, wdtt_ref, dtb_ref, dtbt_ref,
                       z_ref, xbc_ref, dt_ref, dtt_ref):
    h = _norm_mod(x_ref[...], g_ref[...], mod_ref[0:1, :], mod_ref[1:2, :]).astype(BF16)
    z_ref[...] = jnp.dot(h, wz_ref[...], preferred_element_type=F32).astype(BF16)
    xbc_ref[...] = jnp.dot(h, wx_ref[...], preferred_element_type=F32).astype(BF16)
    nh = dtb_ref.shape[1] // 2
    dt = _softplus(jnp.dot(h, wdt_ref[...], preferred_element_type=F32) + dtb_ref[...])
    dt_ref[0] = dt[:, :nh]
    dt_ref[1] = dt[:, nh:]
    dtt = lax.dot_general(wdtt_ref[...], h, (((1,), (1,)), ((), ())), preferred_element_type=F32)
    dtt = _softplus(dtt + dtbt_ref[...])
    dtt_ref[0] = dtt[:nh, :]
    dtt_ref[1] = dtt[nh:, :]


def _ssd_inproj(lay, x, mod_l, g, wz, wx, wdt, wdtt, dtb, dtbt):
    t, d = x.shape
    di, dc, nh2 = wz.shape[1], wx.shape[1], wdt.shape[1]
    nh = nh2 // 2
    full = lambda a: pl.BlockSpec(a.shape, lambda i: (0,) * a.ndim)
    return pl.pallas_call(
        _ssd_inproj_kernel,
        grid=(lay.n_tiles,),
        in_specs=[pl.BlockSpec((TM, d), lambda i: (i, 0)),
                  pl.BlockSpec((None, N_MOD, d), lambda i: (lay.mod_row(i), 0, 0)),
                  full(g), full(wz), full(wx), full(wdt), full(wdtt), full(dtb), full(dtbt)],
        out_specs=[pl.BlockSpec((TM, di), lambda i: (i, 0)),
                   pl.BlockSpec((TM, dc), lambda i: (i, 0)),
                   pl.BlockSpec((2, TM, nh), lambda i: (0, i, 0)),
                   pl.BlockSpec((2, nh, TM), lambda i: (0, 0, i))],
        out_shape=[jax.ShapeDtypeStruct((t, di), BF16),
                   jax.ShapeDtypeStruct((t, dc), BF16),
                   jax.ShapeDtypeStruct((2, t, nh), F32),
                   jax.ShapeDtypeStruct((2, nh, t), F32)],
        compiler_params=_cparams(("parallel",)),
        name="ssd_inproj",
    )(x, mod_l, g, wz, wx, wdt, wdtt, dtb, dtbt)


def _ssd_conv_kernel(lay, di, gn, prev_ref, main_ref, next_ref, w_ref, b_ref, xs_ref, bm_ref, cm_ref, ext_ref):
    i = pl.program_id(0)
    dc = main_ref.shape[1]
    prev = jnp.where(lay.seg_first(i), 0.0, prev_ref[...].astype(F32))
    nxt = jnp.where(lay.seg_last(i), 0.0, next_ref[...].astype(F32))
    ext_ref[0:HALO, :] = prev
    ext_ref[HALO:HALO + TM, :] = main_ref[...].astype(F32)
    ext_ref[HALO + TM:HALO + TM + HALO, :] = nxt
    half = SSD_CONV // 2
    cw = 512
    for c0 in range(0, dc, cw):
        acc = jnp.zeros((TM, cw), F32) + b_ref[:, c0:c0 + cw]
        for k in range(SSD_CONV):
            acc = acc + ext_ref[HALO + k - half:HALO + k - half + TM, c0:c0 + cw] * w_ref[k:k + 1, c0:c0 + cw]
        y = _silu(acc).astype(BF16)
        if c0 < di:
            xs_ref[:, c0:c0 + cw] = y
        elif c0 < di + gn:
            bm_ref[:, c0 - di:c0 - di + cw] = y
        else:
            cm_ref[:, c0 - di - gn:c0 - di - gn + cw] = y


def _ssd_conv(lay, xbc, conv_w, conv_b, di, gn):
    t, dc = xbc.shape
    hb = TM // HALO
    nhb = t // HALO
    assert gn == 512 and di % 512 == 0
    return pl.pallas_call(
        functools.partial(_ssd_conv_kernel, lay, di, gn),
        grid=(lay.n_tiles,),
        in_specs=[pl.BlockSpec((HALO, dc), lambda i: (jnp.maximum(i * hb - 1, 0), 0)),
                  pl.BlockSpec((TM, dc), lambda i: (i, 0)),
                  pl.BlockSpec((HALO, dc), lambda i: (jnp.minimum((i + 1) * hb, nhb - 1), 0)),
                  pl.BlockSpec(conv_w.shape, lambda i: (0, 0)),
                  pl.BlockSpec(conv_b.shape, lambda i: (0, 0))],
        out_specs=[pl.BlockSpec((TM, di), lambda i: (i, 0)),
                   pl.BlockSpec((TM, gn), lambda i: (i, 0)),
                   pl.BlockSpec((TM, gn), lambda i: (i, 0))],
        out_shape=[jax.ShapeDtypeStruct((t, di), BF16),
                   jax.ShapeDtypeStruct((t, gn), BF16),
                   jax.ShapeDtypeStruct((t, gn), BF16)],
        scratch_shapes=[pltpu.VMEM((TM + 2 * HALO, dc), F32)],
        compiler_params=_cparams(("parallel",)),
        name="ssd_conv",
    )(xbc, xbc, xbc, conv_w, conv_b)


def _ssd_scan_kernel(xs_ref, bm_ref, cm_ref, dt_ref, dtt_ref, alr_ref, alc_ref, e_ref, y_ref, state_ref):
    d = pl.program_id(0)
    step = pl.program_id(2)
    q = SSD_CHUNK
    ng = SSD_N_GROUPS
    ns = SSD_D_STATE
    gw = xs_ref.shape[1] // ng
    hpg = gw // SSD_HEAD_DIM

    @pl.when(step == 0)
    def _():
        state_ref[...] = jnp.zeros_like(state_ref)

    sgn = jnp.where(d == 0, 1, -1)
    row = lax.broadcasted_iota(I32, (q, q), 0)
    col = lax.broadcasted_iota(I32, (q, q), 1)
    lmask = (row - col) * sgn >= 0
    lmask_t = (col - row) * sgn >= 0

    dt = dt_ref[...]
    dtt = dtt_ref[...]
    da = dt * (-jnp.exp(alr_ref[...]))
    dat = dtt * (-jnp.exp(alc_ref[...]))
    a_cum = jnp.dot(lmask.astype(F32), da, precision=HIGHEST, preferred_element_type=F32)
    a_cum_t = jnp.dot(dat, lmask_t.astype(F32), precision=HIGHEST, preferred_element_type=F32)
    a_end = jnp.sum(da, axis=0, keepdims=True)
    w_end = jnp.exp(a_end - a_cum) * dt

    e3 = e_ref[...]

    def expand(v):
        hi = v.astype(BF16)
        r1 = v - hi.astype(F32)
        mid = r1.astype(BF16)
        lo = (r1 - mid.astype(F32)).astype(BF16)
        return jnp.dot(jnp.concatenate([hi, mid, lo], axis=1), e3, preferred_element_type=F32)

    a_exp = expand(a_cum)
    w_exp = expand(w_end)
    end_exp = expand(jnp.broadcast_to(a_end, (SUBLANES, a_end.shape[1])))[0:1, :]
    decay_in = jnp.exp(a_exp)
    chunk_decay = jnp.exp(end_exp)
    xw = (xs_ref[...].astype(F32) * w_exp).astype(BF16)

    lane = lax.broadcasted_iota(I32, (q, LANES), 1)
    for g in range(ng):
        bg = bm_ref[:, g * ns:(g + 1) * ns]
        cg = cm_ref[:, g * ns:(g + 1) * ns]
        cb = lax.dot_general(cg, bg, (((1,), (1,)), ((), ())), preferred_element_type=F32)
        cb = jnp.where(lmask, cb, 0.0)
        s_in = state_ref[g]
        y_off = jnp.dot(cg, s_in.astype(BF16), preferred_element_type=F32)
        new_s = lax.dot_general(bg, xw[:, g * gw:(g + 1) * gw], (((0,), (0,)), ((), ())),
                                preferred_element_type=F32)
        state_ref[g] = s_in * chunk_decay[:, g * gw:(g + 1) * gw] + new_s
        for pr in range(hpg // 2):
            mixes = []
            for hh in range(2):
                h = g * hpg + 2 * pr + hh
                seg = a_cum[:, h:h + 1] - a_cum_t[h:h + 1, :]
                lh = jnp.exp(jnp.minimum(seg, 0.0))
                mixes.append((cb * lh * dtt[h:h + 1, :]).astype(BF16))
            lhs = jnp.concatenate(mixes, axis=1)
            l0 = g * gw + pr * LANES
            xp = xs_ref[:, l0:l0 + LANES]
            zero = jnp.zeros_like(xp)
            rhs = jnp.concatenate([jnp.where(lane < SSD_HEAD_DIM, xp, zero),
                                   jnp.where(lane >= SSD_HEAD_DIM, xp, zero)], axis=0)
            y_pair = jnp.dot(lhs, rhs, preferred_element_type=F32)
            y_pair = y_pair + y_off[:, pr * LANES:(pr + 1) * LANES] * decay_in[:, l0:l0 + LANES]
            y_ref[:, l0:l0 + LANES] = y_pair.astype(BF16)


def _ssd_scan(lay, xs, bm, cm, dt, dtt, a_log, expand):
    t, di = xs.shape
    gn = bm.shape[1]
    nh = dt.shape[2]
    q = SSD_CHUNK
    nct, nlt = lay.n_ctx // q, lay.seq // q
    nc = nct + nlt
    ctx_base = lay.t_lat // q

    def blk(d, b, s):
        j_ctx = jnp.where(d == 0, s, nct - 1 - s)
        j_lat = jnp.where(d == 0, s - nct, nlt - 1 - (s - nct))
        return jnp.where(s < nct, ctx_base + b * nct + j_ctx, b * nlt + j_lat)

    return pl.pallas_call(
        _ssd_scan_kernel,
        grid=(2, lay.batch, nc),
        in_specs=[pl.BlockSpec((q, di), lambda d, b, s: (blk(d, b, s), 0)),
                  pl.BlockSpec((q, gn), lambda d, b, s: (blk(d, b, s), 0)),
                  pl.BlockSpec((q, gn), lambda d, b, s: (blk(d, b, s), 0)),
                  pl.BlockSpec((None, q, nh), lambda d, b, s: (d, blk(d, b, s), 0)),
                  pl.BlockSpec((None, nh, q), lambda d, b, s: (d, 0, blk(d, b, s))),
                  pl.BlockSpec((None, 1, nh), lambda d, b, s: (d, 0, 0)),
                  pl.BlockSpec((None, nh, 1), lambda d, b, s: (d, 0, 0)),
                  pl.BlockSpec(expand.shape, lambda d, b, s: (0, 0))],
        out_specs=pl.BlockSpec((None, q, di), lambda d, b, s: (d, blk(d, b, s), 0)),
        out_shape=jax.ShapeDtypeStruct((2, t, di), BF16),
        scratch_shapes=[pltpu.VMEM((SSD_N_GROUPS, SSD_D_STATE, di // SSD_N_GROUPS), F32)],
        compiler_params=_cparams(("arbitrary", "arbitrary", "arbitrary")),
        name="ssd_scan",
    )(xs, bm, cm, dt, dtt, a_log.reshape(2, 1, nh), a_log.reshape(2, nh, 1), expand)


def _ssd_outproj_kernel(y_ref, xs_ref, z_ref, dexp_ref, ng_ref, w_ref, x_ref, mod_ref, o_ref):
    y = y_ref[0].astype(F32) + y_ref[1].astype(F32) + xs_ref[...].astype(F32) * dexp_ref[...]
    y = y * _silu(z_ref[...].astype(F32))
    gw = y.shape[1] // SSD_N_GROUPS
    parts = []
    for g in range(SSD_N_GROUPS):
        yg = y[:, g * gw:(g + 1) * gw]
        ms = jnp.mean(yg * yg, axis=-1, keepdims=True)
        parts.append((yg * lax.rsqrt(ms + NORM_EPS) * ng_ref[:, g * gw:(g + 1) * gw]).astype(BF16))
    yn = jnp.concatenate(parts, axis=1)
    m = jnp.dot(yn, w_ref[...], preferred_element_type=F32)
    o_ref[...] = x_ref[...] + mod_ref[2:3, :] * m


def _ssd_outproj(lay, y, xs, z, dexp, ng, w, x, mod_l):
    t, d = x.shape
    di = xs.shape[1]
    full = lambda a: pl.BlockSpec(a.shape, lambda i: (0,) * a.ndim)
    return pl.pallas_call(
        _ssd_outproj_kernel,
        grid=(lay.n_tiles,),
        in_specs=[pl.BlockSpec((2, TM, di), lambda i: (0, i, 0)),
                  pl.BlockSpec((TM, di), lambda i: (i, 0)),
                  pl.BlockSpec((TM, di), lambda i: (i, 0)),
                  full(dexp), full(ng), full(w),
                  pl.BlockSpec((TM, d), lambda i: (i, 0)),
                  pl.BlockSpec((None, N_MOD, d), lambda i: (lay.mod_row(i), 0, 0))],
        out_specs=pl.BlockSpec((TM, d), lambda i: (i, 0)),
        out_shape=jax.ShapeDtypeStruct((t, d), F32),
        compiler_params=_cparams(("parallel",)),
        name="ssd_outproj",
    )(y, xs, z, dexp, ng, w, x, mod_l)


def _att_inproj_kernel(x_ref, mod_ref, g_ref, wq_ref, wkt_ref, wv_ref, vone_ref, qg_ref, kgt_ref, bound_ref, gsum_ref, gexp_ref,
                       cos_ref, sin_ref, cost_ref, sint_ref, q_ref, kt_ref, v_ref):
    h = _norm_mod(x_ref[...], g_ref[...], mod_ref[0:1, :], mod_ref[1:2, :]).astype(BF16)
    hd = ATT_HEAD_DIM
    q = jnp.dot(h, wq_ref[...], preferred_element_type=F32)
    ssum = jnp.dot((q * q).astype(BF16), gsum_ref[...], preferred_element_type=F32)
    r = lax.rsqrt(ssum * (1.0 / hd) + NORM_EPS)
    r_hi = r.astype(BF16)
    r_lo = (r - r_hi.astype(F32)).astype(BF16)
    r_exp = jnp.dot(jnp.concatenate([r_hi, r_lo], axis=1), gexp_ref[...], preferred_element_type=F32)
    qn = q * r_exp * qg_ref[...]
    nl = qn.shape[1]
    lane = lax.broadcasted_iota(I32, qn.shape, 1)
    partner = jnp.where((lane & 1) == 0, pltpu.roll(qn, nl - 1, 1), pltpu.roll(qn, 1, 1))
    reps = nl // LANES
    cos = jnp.concatenate([cos_ref[...]] * reps, axis=1)
    sin = jnp.concatenate([sin_ref[...]] * reps, axis=1)
    q_ref[...] = (qn * cos + partner * sin).astype(BF16)
    kt = lax.dot_general(wkt_ref[...], h, (((1,), (1,)), ((), ())), preferred_element_type=F32)
    ct, st = cost_ref[...], sint_ref[...]
    sub = lax.broadcasted_iota(I32, ct.shape, 0)
    for kh in range(ATT_KV_HEADS):
        blk = kt[kh * hd:(kh + 1) * hd]
        rk = lax.rsqrt(jnp.sum(blk * blk, axis=0, keepdims=True) * (1.0 / hd) + NORM_EPS)
        kn = blk * rk * kgt_ref[...]
        kpart = jnp.where((sub & 1) == 0, pltpu.roll(kn, hd - 1, 0), pltpu.roll(kn, 1, 0))
        kt_ref[kh, 0:hd, :] = (kn * ct + kpart * st).astype(BF16)
        kt_ref[kh, hd:2 * hd, :] = jnp.where(sub == 0, -bound_ref[...], 0.0).astype(BF16)
    v = (jnp.dot(h, wv_ref[...], preferred_element_type=F32) + vone_ref[...]).astype(BF16)
    for kh in range(ATT_KV_HEADS):
        v_ref[kh] = v[:, kh * LANES:(kh + 1) * LANES]


def _att_inproj(lay, x, mod_l, g, wq, wkt, wv2, vone, qg, kgt, bound, gsum, gexp, cos, sin, cost, sint):
    t, d = x.shape
    dq = wq.shape[1]
    hd = ATT_HEAD_DIM
    full = lambda a: pl.BlockSpec(a.shape, lambda i: (0,) * a.ndim)
    return pl.pallas_call(
        _att_inproj_kernel,
        grid=(lay.n_tiles,),
        in_specs=[pl.BlockSpec((TM, d), lambda i: (i, 0)),
                  pl.BlockSpec((None, N_MOD, d), lambda i: (lay.mod_row(i), 0, 0)),
                  full(g), full(wq), full(wkt), full(wv2), full(vone), full(qg), full(kgt), full(bound), full(gsum), full(gexp),
                  pl.BlockSpec((TM, LANES), lambda i: (lay.tile_pos(i), 0)),
                  pl.BlockSpec((TM, LANES), lambda i: (lay.tile_pos(i), 0)),
                  pl.BlockSpec((hd, TM), lambda i: (0, lay.tile_pos(i))),
                  pl.BlockSpec((hd, TM), lambda i: (0, lay.tile_pos(i)))],
        out_specs=[pl.BlockSpec((TM, dq), lambda i: (i, 0)),
                   pl.BlockSpec((None, ATT_KV_HEADS, 2 * hd, TM), lambda i: (lay.tile_batch(i), 0, 0, lay.tile_pos(i))),
                   pl.BlockSpec((None, ATT_KV_HEADS, TM, LANES), lambda i: (lay.tile_batch(i), 0, lay.tile_pos(i), 0))],
        out_shape=[jax.ShapeDtypeStruct((t, dq), BF16),
                   jax.ShapeDtypeStruct((lay.batch, ATT_KV_HEADS, 2 * hd, lay.nk), BF16),
                   jax.ShapeDtypeStruct((lay.batch, ATT_KV_HEADS, lay.nk, LANES), BF16)],
        compiler_params=_cparams(("parallel",)),
        name="att_inproj",
    )(x, mod_l, g, wq, wkt, wv2, vone, qg, kgt, bound, gsum, gexp, cos, sin, cost, sint)


def _attend(q_ref, kt_ref, v_ref, o_ref, k0, n_keys):
    hd = ATT_HEAD_DIM
    tm = q_ref.shape[0]
    heads = q_ref.shape[1] // hd
    qs = [q_ref[:, u * hd:(u + 1) * hd] for u in range(heads)]

    bk = min(ATT_KEY_BLOCK, n_keys)
    assert n_keys % bk == 0 and k0 % LANES == 0 and bk % LANES == 0

    def body(j, carry):
        start = pl.multiple_of(k0 + j * bk, LANES)
        kb = kt_ref[0:hd, pl.ds(start, bk)]
        vb = v_ref[pl.ds(start, bk), :]
        out = []
        for u in range(heads):
            m_old, acc = carry[u]
            s = jnp.dot(qs[u], kb, preferred_element_type=F32)
            m_new = jnp.maximum(m_old, jnp.max(s, axis=-1, keepdims=True))
            p = jnp.exp2((s - m_new).astype(BF16))
            acc = jnp.exp2(m_old - m_new) * acc + jnp.dot(p, vb, preferred_element_type=F32)
            out.append((m_new, acc))
        return tuple(out)

    init = tuple((jnp.full((tm, 1), -jnp.inf, F32), jnp.zeros((tm, LANES), F32)) for _ in range(heads))
    final = lax.fori_loop(0, n_keys // bk, body, init) if n_keys > bk else body(0, init)
    outs = [acc[:, :hd] / acc[:, hd:hd + 1] for _, acc in final]
    o_ref[...] = jnp.concatenate(outs, axis=1).astype(BF16)


def _attend_bounded(q_ref, kt_ref, v_ref, o_ref):
    hd = ATT_HEAD_DIM
    tm = q_ref.shape[0]
    k_lo = kt_ref[...]
    k_hi = jnp.concatenate([k_lo[hd:], k_lo[:hd]], axis=0)
    vb = v_ref[...]
    lane = lax.broadcasted_iota(I32, (tm, LANES), 1)
    outs, dens = [], []
    for pr in range(q_ref.shape[1] // LANES):
        qp = q_ref[:, pr * LANES:(pr + 1) * LANES]
        q_lo = jnp.where(lane < hd, qp, (lane == hd).astype(BF16))
        q_hi = jnp.where(lane >= hd, qp, (lane == 0).astype(BF16))
        for qa, ka in ((q_lo, k_lo), (q_hi, k_hi)):
            s = jnp.dot(qa, ka, preferred_element_type=F32)
            acc = jnp.dot(jnp.exp2(s).astype(BF16), vb, preferred_element_type=F32)
            den = acc[:, hd:hd + 1]
            outs.append(acc[:, :hd] / den)
            dens.append(den)
    o_ref[...] = jnp.concatenate(outs, axis=1).astype(BF16)
    return jnp.min(jnp.concatenate(dens, axis=1))


def _att_core_kernel(lay, q_ref, kt_ref, v_ref, o_ref):
    qi = pl.program_id(2)

    @pl.when(qi < lay.seq_tiles)
    def _():
        smallest = _attend_bounded(q_ref, kt_ref, v_ref, o_ref)

        @pl.when(jnp.logical_not(smallest > ATT_MIN_DENOM))
        def _():
            _attend(q_ref, kt_ref, v_ref, o_ref, 0, lay.nk)

    @pl.when(qi >= lay.seq_tiles)
    def _():
        _attend(q_ref, kt_ref, v_ref, o_ref, lay.seq, lay.n_ctx)


def _att_core(lay, qn, kt, v):
    t, dq = qn.shape
    hd = ATT_HEAD_DIM
    qw = dq // ATT_KV_HEADS
    per_b = lay.seq_tiles + lay.ctx_tiles

    def row_tile(b, qi):
        return jnp.where(qi < lay.seq_tiles, b * lay.seq_tiles + qi,
                         lay.lat_tiles + b * lay.ctx_tiles + (qi - lay.seq_tiles))

    return pl.pallas_call(
        functools.partial(_att_core_kernel, lay),
        grid=(lay.batch, ATT_KV_HEADS, per_b),
        in_specs=[pl.BlockSpec((TM, qw), lambda b, kh, qi: (row_tile(b, qi), kh)),
                  pl.BlockSpec((None, None, 2 * hd, lay.nk), lambda b, kh, qi: (b, kh, 0, 0)),
                  pl.BlockSpec((None, None, lay.nk, LANES), lambda b, kh, qi: (b, kh, 0, 0))],
        out_specs=pl.BlockSpec((TM, qw), lambda b, kh, qi: (row_tile(b, qi), kh)),
        out_shape=jax.ShapeDtypeStruct((t, dq), BF16),
        compiler_params=_cparams(("parallel", "parallel", "arbitrary")),
        name="att_core",
    )(qn, kt, v)


def _att_outproj_kernel(o_ref, w_ref, x_ref, mod_ref, out_ref):
    m = jnp.dot(o_ref[...], w_ref[...], preferred_element_type=F32)
    out_ref[...] = x_ref[...] + mod_ref[2:3, :] * m


def _att_outproj(lay, o, w, x, mod_l):
    t, d = x.shape
    return pl.pallas_call(
        _att_outproj_kernel,
        grid=(lay.n_tiles,),
        in_specs=[pl.BlockSpec((TM, o.shape[1]), lambda i: (i, 0)),
                  pl.BlockSpec(w.shape, lambda i: (0, 0)),
                  pl.BlockSpec((TM, d), lambda i: (i, 0)),
                  pl.BlockSpec((None, N_MOD, d), lambda i: (lay.mod_row(i), 0, 0))],
        out_specs=pl.BlockSpec((TM, d), lambda i: (i, 0)),
        out_shape=jax.ShapeDtypeStruct((t, d), F32),
        compiler_params=_cparams(("parallel",)),
        name="att_outproj",
    )(o, w, x, mod_l)


R_EID, R_RANK, R_W = 0, 2, 4
GROUP_LANE0 = MOE_EXPERTS


def _store_token_rows(ref, val):
    rows, d = val.shape
    for s in range(d // LANES):
        ref[pl.ds(s, rows, stride=d // LANES), :] = val[:, s * LANES:(s + 1) * LANES]


def _load_token_rows(ref, d):
    n = d // LANES
    rows = ref.shape[0] // n
    return jnp.concatenate([ref[pl.ds(s, rows, stride=n), :] for s in range(n)], axis=1)


def _token(ref, r, n):
    return ref.at[pl.ds(pl.multiple_of(r * n, n), n)]


def _router_kernel(x_ref, mod_ref, g_ref, wr_ref, br_ref, r_ref, rt_ref, cnt_ref, base_ref):
    i = pl.program_id(0)

    @pl.when(i == 0)
    def _():
        base_ref[...] = jnp.zeros_like(base_ref)

    f = _norm_mod(x_ref[...], g_ref[...], mod_ref[3:4, :], mod_ref[4:5, :])
    f_hi = f.astype(BF16)
    f_lo = (f - f_hi.astype(F32)).astype(BF16)
    part = jnp.dot(f_hi, wr_ref[...], preferred_element_type=F32)
    part_lo = jnp.dot(f_lo, wr_ref[:, :LANES], preferred_element_type=F32)
    logits = part[:, :LANES] + part[:, LANES:] + part_lo + br_ref[...]
    lane = lax.broadcasted_iota(I32, logits.shape, 1)
    neg = jnp.float32(-jnp.inf)
    big = jnp.int32(LANES)

    def first_max(vals):
        top = jnp.max(vals, axis=-1, keepdims=True)
        idx = jnp.min(jnp.where(vals == top, lane, big), axis=-1, keepdims=True)
        return top, idx

    g_mask = (lane >= GROUP_LANE0) & (lane < GROUP_LANE0 + MOE_GROUPS)
    glog = jnp.where(g_mask, logits, neg)
    g_top, g_idx = first_max(glog)
    g_w = 1.0 / jnp.sum(jnp.exp(glog - g_top), axis=-1, keepdims=True)
    e0 = (g_idx - GROUP_LANE0) * MOE_EPG
    elog = jnp.where((lane >= e0) & (lane < e0 + MOE_EPG), logits, neg)
    v1, i1 = first_max(elog)
    v2, i2 = first_max(jnp.where(lane == i1, neg, elog))
    ex = jnp.exp(v2 - v1)
    w1 = g_w / (1.0 + ex)
    w2 = g_w * ex / (1.0 + ex)

    oh1 = (lane == i1).astype(F32)
    oh2 = (lane == i2).astype(F32)
    tm = logits.shape[0]
    rr = lax.broadcasted_iota(I32, (tm, tm), 0)
    cc = lax.broadcasted_iota(I32, (tm, tm), 1)
    before = (cc < rr).astype(BF16)
    cum1 = jnp.dot(before, oh1.astype(BF16), preferred_element_type=F32)
    cum2 = jnp.dot(before, oh2.astype(BF16), preferred_element_type=F32)
    tot1 = jnp.sum(oh1, axis=0, keepdims=True)
    tot2 = jnp.sum(oh2, axis=0, keepdims=True)
    base = base_ref[...]
    rank1 = jnp.sum(oh1 * (cum1 + base), axis=-1, keepdims=True)
    rank2 = jnp.sum(oh2 * (cum2 + base + tot1), axis=-1, keepdims=True)
    base = base + tot1 + tot2
    base_ref[...] = base
    cnt_ref[...] = base

    rec = jnp.zeros(logits.shape, F32)
    for k, val in ((R_EID, i1.astype(F32)), (R_EID + 1, i2.astype(F32)), (R_RANK, rank1), (R_RANK + 1, rank2),
                   (R_W, w1), (R_W + 1, w2)):
        rec = jnp.where(lane == k, val, rec)
    r_ref[...] = rec
    rt_ref[...] = rec.T[:SUBLANES, :]


def _router(lay, n_tiles, x, mod_l, g, wr, br):
    t, d = x.shape
    return pl.pallas_call(
        _router_kernel,
        grid=(n_tiles,),
        in_specs=[pl.BlockSpec((TM, d), lambda i: (i, 0)),
                  pl.BlockSpec((None, N_MOD, d), lambda i: (lay.mod_row(i), 0, 0)),
                  pl.BlockSpec(g.shape, lambda i: (0, 0)),
                  pl.BlockSpec(wr.shape, lambda i: (0, 0)),
                  pl.BlockSpec(br.shape, lambda i: (0, 0))],
        out_specs=[pl.BlockSpec((TM, LANES), lambda i: (i, 0)),
                   pl.BlockSpec((SUBLANES, TM), lambda i: (0, i)),
                   pl.BlockSpec((1, LANES), lambda i: (0, 0))],
        out_shape=[jax.ShapeDtypeStruct((n_tiles * TM, LANES), F32),
                   jax.ShapeDtypeStruct((SUBLANES, n_tiles * TM), F32),
                   jax.ShapeDtypeStruct((1, LANES), F32)],
        scratch_shapes=[pltpu.VMEM((1, LANES), F32)],
        compiler_params=_cparams(("arbitrary",)),
        name="moe_router",
    )(x, mod_l, g, wr, br)


def _dispatch_kernel(fill_ref, plen_ref, dest_hbm, x_ref, mod_ref, g_ref, xs_hbm, dest_s, fbuf, zbuf, dsem, fsem, ssem):
    i = pl.program_id(0)
    n = pl.num_programs(0)
    tn = x_ref.shape[1] // LANES

    def drain(s):
        for k in range(2):
            pltpu.make_async_copy(fbuf.at[s], xs_hbm.at[pl.ds(0, TM * tn)], ssem.at[s]).wait()

    def dest_copy(j):
        return pltpu.make_async_copy(dest_hbm.at[j], dest_s.at[pl.ds((j % 2) * DEST_ROW, DEST_ROW)], dsem.at[j % 2])

    def pad_pieces(e, fn):
        plen = plen_ref[e]
        for bit in range(ZROWS.bit_length() - 1, -1, -1):
            size = 1 << bit

            @pl.when((plen & size) != 0)
            def _():
                start = fill_ref[e] + ((plen >> (bit + 1)) << (bit + 1))
                fn(pltpu.make_async_copy(zbuf.at[pl.ds(0, size * tn)],
                                         xs_hbm.at[pl.ds(pl.multiple_of(start * tn, tn), size * tn)], fsem.at[0]))

    @pl.when(i == 0)
    def _():
        zbuf[...] = jnp.zeros_like(zbuf)
        dest_copy(0).start()

        def start_e(e, c):
            pad_pieces(e, lambda cp: cp.start())
            return c

        def wait_e(e, c):
            pad_pieces(e, lambda cp: cp.wait())
            return c

        def tail_pieces(fn):
            def piece(t, c):
                start = fill_ref[MOE_EXPERTS] + t * ZROWS
                fn(pltpu.make_async_copy(zbuf, xs_hbm.at[pl.ds(pl.multiple_of(start * tn, tn), ZROWS * tn)],
                                         fsem.at[0]))
                return c
            lax.fori_loop(0, plen_ref[MOE_EXPERTS] // ZROWS, piece, 0)

        lax.fori_loop(0, MOE_EXPERTS, start_e, 0)
        tail_pieces(lambda cp: cp.start())
        lax.fori_loop(0, MOE_EXPERTS, wait_e, 0)
        tail_pieces(lambda cp: cp.wait())

    @pl.when(i + 1 < n)
    def _():
        dest_copy(i + 1).start()

    dest_copy(i).wait()
    slot = i % 2

    @pl.when(i >= 2)
    def _():
        drain(slot)

    f = _norm_mod(x_ref[...], g_ref[...], mod_ref[3:4, :], mod_ref[4:5, :])
    _store_token_rows(fbuf.at[slot], f)

    def body(r, c):
        for k in range(2):
            row = dest_s[slot * DEST_ROW + k * TM + r]
            pltpu.make_async_copy(_token(fbuf.at[slot], r, tn), _token(xs_hbm, row, tn),
                                  ssem.at[slot]).start(priority=k)
        return c

    lax.fori_loop(0, TM, body, 0, unroll=8)

    @pl.when(i == n - 1)
    def _():
        drain(slot)

        @pl.when(n > 1)
        def _():
            drain(1 - slot)


def _dispatch(lay, fill, plen, dest, x, mod_l, g, cap):
    n_tiles = dest.shape[0]
    d = x.shape[1]
    tn = d // LANES
    grid_spec = pltpu.PrefetchScalarGridSpec(
        num_scalar_prefetch=2,
        grid=(n_tiles,),
        in_specs=[pl.BlockSpec(memory_space=pl.ANY),
                  pl.BlockSpec((TM, d), lambda i, fill, plen: (i, 0)),
                  pl.BlockSpec((None, N_MOD, d), lambda i, fill, plen: (lay.mod_row(i), 0, 0)),
                  pl.BlockSpec(g.shape, lambda i, fill, plen: (0, 0))],
        out_specs=pl.BlockSpec(memory_space=pl.ANY),
        scratch_shapes=[pltpu.SMEM((2 * DEST_ROW,), I32),
                        pltpu.VMEM((2, TM * tn, LANES), F32),
                        pltpu.VMEM((ZROWS * tn, LANES), F32),
                        pltpu.SemaphoreType.DMA((2,)),
                        pltpu.SemaphoreType.DMA((1,)),
                        pltpu.SemaphoreType.DMA((2,))],
    )
    return pl.pallas_call(
        _dispatch_kernel,
        grid_spec=grid_spec,
        out_shape=jax.ShapeDtypeStruct((cap * tn, LANES), F32),
        compiler_params=_cparams(("arbitrary",)),
        name="moe_dispatch",
    )(fill, plen, dest, x, mod_l, g)


def _expert_kernel(layer, be_ref, nv_ref, blk_ref, grp_ref, nxt_ref, x_ref, wg_hbm, wu_hbm, wd_hbm, y_ref,
                   wgf, wuf, wdf, wgb, wub, wdb, wsem):
    i = pl.program_id(0)
    live = nv_ref[i] > 0
    first = jnp.logical_or(i == 0, be_ref[i] != be_ref[jnp.maximum(i - 1, 0)])
    slot = grp_ref[i] % 2

    def fetch(e, s):
        return (pltpu.make_async_copy(wg_hbm.at[layer, e], wgf.at[s], wsem.at[s]),
                pltpu.make_async_copy(wu_hbm.at[layer, e], wuf.at[s], wsem.at[s]),
                pltpu.make_async_copy(wd_hbm.at[layer, e], wdf.at[s], wsem.at[s]))

    @pl.when(i == 0)
    def _():
        for cp in fetch(be_ref[0], 0):
            cp.start()

    @pl.when(jnp.logical_and(live, first))
    def _():
        for cp in fetch(be_ref[i], slot):
            cp.wait()

        @pl.when(nxt_ref[i] >= 0)
        def _():
            for cp in fetch(nxt_ref[i], 1 - slot):
                cp.start()

        wgb[...] = wgf[slot].astype(BF16)
        wub[...] = wuf[slot].astype(BF16)
        wdb[...] = wdf[slot].astype(BF16)

    @pl.when(live)
    def _():
        xb = _load_token_rows(x_ref, wgb.shape[0]).astype(BF16)
        hg = jnp.dot(xb, wgb[...], preferred_element_type=F32)
        hu = jnp.dot(xb, wub[...], preferred_element_type=F32)
        act = (_silu(hg) * hu).astype(BF16)
        _store_token_rows(y_ref, jnp.dot(act, wdb[...], preferred_element_type=F32))

    @pl.when(jnp.logical_not(live))
    def _():
        y_ref[...] = jnp.zeros_like(y_ref)


def _experts(layer, block_e, block_nv, block_src, block_grp, block_nxt, xs, wg, wu, wd):
    n_blocks = block_e.shape[0]
    d, dff = wg.shape[2:]
    rb = MOE_ROW_BLOCK
    blk = (rb * (d // LANES), LANES)
    grid_spec = pltpu.PrefetchScalarGridSpec(
        num_scalar_prefetch=5,
        grid=(n_blocks,),
        in_specs=[pl.BlockSpec(blk, lambda i, be, nv, src, grp, nxt: (src[i], 0)),
                  pl.BlockSpec(memory_space=pl.ANY),
                  pl.BlockSpec(memory_space=pl.ANY),
                  pl.BlockSpec(memory_space=pl.ANY)],
        out_specs=pl.BlockSpec(blk, lambda i, be, nv, src, grp, nxt: (i, 0)),
        scratch_shapes=[pltpu.VMEM((2, d, dff), F32),
                        pltpu.VMEM((2, d, dff), F32),
                        pltpu.VMEM((2, dff, d), F32),
                        pltpu.VMEM((d, dff), BF16),
                        pltpu.VMEM((d, dff), BF16),
                        pltpu.VMEM((dff, d), BF16),
                        pltpu.SemaphoreType.DMA((2,))],
    )
    return pl.pallas_call(
        functools.partial(_expert_kernel, layer),
        grid_spec=grid_spec,
        out_shape=jax.ShapeDtypeStruct(xs.shape, F32),
        compiler_params=_cparams(("arbitrary",)),
        name="moe_experts",
    )(block_e, block_nv, block_src, block_grp, block_nxt, xs, wg, wu, wd)


def _combine_kernel(dest_hbm, x_ref, r_ref, mod_ref, ys_hbm, o_ref, dest_s, gbuf, dsem, gsem):
    i = pl.program_id(0)
    n = pl.num_programs(0)
    tn = x_ref.shape[1] // LANES

    def dest_copy(j):
        return pltpu.make_async_copy(dest_hbm.at[j], dest_s.at[pl.ds((j % 3) * DEST_ROW, DEST_ROW)], dsem.at[j % 3])

    def gather_tile(j):
        def body(r, c):
            for k in range(2):
                row = dest_s[(j % 3) * DEST_ROW + k * TM + r]
                pltpu.make_async_copy(_token(ys_hbm, row, tn), _token(gbuf.at[j % 2, k], r, tn),
                                      gsem.at[j % 2]).start(priority=k)
            return c
        lax.fori_loop(0, TM, body, 0, unroll=8)

    @pl.when(i == 0)
    def _():
        dest_copy(0).start()
        dest_copy(0).wait()
        gather_tile(0)

        @pl.when(n > 1)
        def _():
            dest_copy(1).start()

    @pl.when(i + 1 < n)
    def _():
        dest_copy(i + 1).wait()

        @pl.when(i + 2 < n)
        def _():
            dest_copy(i + 2).start()

        gather_tile(i + 1)

    slot = i % 2
    for k in range(2):
        pltpu.make_async_copy(ys_hbm.at[pl.ds(0, TM * tn)], gbuf.at[slot, k], gsem.at[slot]).wait()
    w1 = r_ref[:, R_W:R_W + 1]
    w2 = r_ref[:, R_W + 1:R_W + 2]
    d = x_ref.shape[1]
    y = _load_token_rows(gbuf.at[slot, 0], d) * w1 + _load_token_rows(gbuf.at[slot, 1], d) * w2
    o_ref[...] = x_ref[...] + mod_ref[5:6, :] * y


def _combine(lay, n_tiles, dest, x, rec, mod_l, ys):
    d = x.shape[1]
    return pl.pallas_call(
        _combine_kernel,
        grid=(n_tiles,),
        in_specs=[pl.BlockSpec(memory_space=pl.ANY),
                  pl.BlockSpec((TM, d), lambda i: (i, 0)),
                  pl.BlockSpec((TM, LANES), lambda i: (i, 0)),
                  pl.BlockSpec((None, N_MOD, d), lambda i: (lay.mod_row(i), 0, 0)),
                  pl.BlockSpec(memory_space=pl.ANY)],
        out_specs=pl.BlockSpec((TM, d), lambda i: (i, 0)),
        out_shape=jax.ShapeDtypeStruct((n_tiles * TM, d), F32),
        scratch_shapes=[pltpu.SMEM((3 * DEST_ROW,), I32),
                        pltpu.VMEM((2, 2, TM * (d // LANES), LANES), F32),
                        pltpu.SemaphoreType.DMA((3,)),
                        pltpu.SemaphoreType.DMA((2,))],
        compiler_params=_cparams(("arbitrary",)),
        name="moe_combine",
    )(dest, x, rec, mod_l, ys)


def _moe(lay, layer, n_tiles, x, mod_l, g2, w_group, b_group, w_router, b_router, wg, wu, wd):
    d = x.shape[1]
    t_tok = n_tiles * TM
    pad = LANES - MOE_EXPERTS - MOE_GROUPS
    wr = jnp.concatenate([w_router, w_group, jnp.zeros((d, pad), F32)], axis=1)
    br = jnp.concatenate([b_router, b_group, jnp.zeros((pad,), F32)])[None, :]
    wr_hi = wr.astype(BF16)
    wr_lo = (wr - wr_hi.astype(F32)).astype(BF16)
    rec, rec_t, cnt = _router(lay, n_tiles, x, mod_l, g2, jnp.concatenate([wr_hi, wr_lo], axis=1), br)

    rb = MOE_ROW_BLOCK
    eid = rec_t[R_EID:R_EID + 2].astype(I32)
    rank = rec_t[R_RANK:R_RANK + 2].astype(I32)
    counts = cnt[0, :MOE_EXPERTS].astype(I32)
    padded = (counts + rb - 1) // rb * rb
    pad_ends = jnp.cumsum(padded)
    pad_starts = pad_ends - padded
    experts = jnp.arange(MOE_EXPERTS, dtype=I32)[:, None, None]
    dest = jnp.sum(jnp.where(eid[None] == experts, pad_starts[:, None, None], 0), axis=0) + rank
    dest = dest.reshape(2, n_tiles, TM).transpose(1, 0, 2).reshape(n_tiles, 2 * TM)
    dest = jnp.pad(dest, ((0, 0), (0, DEST_ROW - 2 * TM)))
    n_blocks = -(-(2 * t_tok) // rb) + MOE_EXPERTS
    used = pad_ends[-1] // rb
    idx = jnp.arange(n_blocks, dtype=I32)
    block_src = jnp.clip(idx, 0, jnp.maximum(used - 1, 0))
    start = block_src * rb
    block_e = jnp.sum(pad_ends[None, :] <= start[:, None], axis=1).astype(I32)
    block_nv = jnp.where(idx < used, jnp.clip(counts[block_e] - (start - pad_starts[block_e]), 0, rb), 0).astype(I32)

    cap = n_blocks * rb
    fill = jnp.concatenate([pad_starts + counts, pad_ends[-1:]]).astype(I32)
    plen = jnp.concatenate([padded - counts, cap - pad_ends[-1:]]).astype(I32)
    xs = _dispatch(lay, fill, plen, dest, x, mod_l, g2, cap)
    nonempty = counts > 0
    e_ids = jnp.arange(MOE_EXPERTS, dtype=I32)
    later = jnp.where((e_ids[None, :] > e_ids[:, None]) & nonempty[None, :], e_ids[None, :], MOE_EXPERTS)
    nxt_of_e = jnp.min(later, axis=1)
    nxt_of_e = jnp.where(nxt_of_e == MOE_EXPERTS, -1, nxt_of_e).astype(I32)
    grp_of_e = (jnp.cumsum(nonempty) - nonempty).astype(I32)
    ys = _experts(layer, block_e, block_nv, block_src, grp_of_e[block_e], nxt_of_e[block_e], xs, wg, wu, wd)
    return _combine(lay, n_tiles, dest, x, rec, mod_l, ys)


def _rope_tables(lay):
    half = ATT_HEAD_DIM // 2
    pos = jnp.arange(lay.seq)
    rowp = (pos // GRID_W).astype(F32)
    colp = (pos % GRID_W).astype(F32)
    freqs = ROPE_THETA ** (-jnp.arange(0, half, 2, dtype=F32) / half)
    ang = jnp.concatenate([rowp[:, None] * freqs, colp[:, None] * freqs], axis=-1)
    cos = jnp.concatenate([jnp.cos(ang), jnp.ones((lay.n_ctx, half), F32)], axis=0)
    sin = jnp.concatenate([jnp.sin(ang), jnp.zeros((lay.n_ctx, half), F32)], axis=0)
    cos_d = jnp.repeat(cos, 2, axis=1)
    sin_d = jnp.repeat(sin, 2, axis=1) * jnp.tile(jnp.asarray([-1.0, 1.0], F32), half)
    reps = LANES // ATT_HEAD_DIM
    return jnp.tile(cos_d, (1, reps)), jnp.tile(sin_d, (1, reps)), cos_d.T, sin_d.T


def kernel(x, c, ctx, c_ctx, mod_w, mod_b, norm1_g, norm2_g, ssd_w_in, ssd_conv_w, ssd_conv_b, ssd_dt_bias,
           ssd_a_log, ssd_d, ssd_norm_g, ssd_w_out, att_w_qkv, att_q_gain, att_k_gain, att_w_o, moe_w_group,
           moe_b_group, moe_w_router, moe_b_router, moe_w_gate, moe_w_up, moe_w_down):
    b, n_lat, d = x.shape
    n_ctx = ctx.shape[1]
    depth = mod_w.shape[0]
    lay = _Layout(b, n_lat, n_ctx)
    assert b + 1 <= 8

    xs = jnp.concatenate([x.reshape(lay.t_lat, d), ctx.reshape(lay.t_ctx, d)], axis=0)
    c8 = jnp.concatenate([c, c_ctx[None, :], jnp.zeros((8 - b - 1, d), F32)], axis=0)
    mods = _mod_table(c8, mod_w, mod_b).reshape(depth, 8, N_MOD, d)

    nh = ssd_dt_bias.shape[2]
    di = nh * SSD_HEAD_DIM
    gn = SSD_N_GROUPS * SSD_D_STATE
    dc = di + 2 * gn
    expand = jnp.tile(jnp.repeat(jnp.eye(nh, dtype=BF16), SSD_HEAD_DIM, axis=1), (3, 1))

    hq = att_w_qkv.shape[2] // ATT_HEAD_DIM - 2 * ATT_KV_HEADS
    dq = hq * ATT_HEAD_DIM
    dkv = ATT_KV_HEADS * ATT_HEAD_DIM
    gsum = np.zeros((dq, LANES), np.float32)
    gsum[np.arange(dq), np.arange(dq) // ATT_HEAD_DIM] = 1.0
    gexp = np.concatenate([gsum.T, gsum.T], axis=0)
    vone = np.zeros((1, ATT_KV_HEADS * LANES), np.float32)
    vone[0, np.arange(ATT_KV_HEADS) * LANES + ATT_HEAD_DIM] = 1.0
    cos128, sin128, cos_t, sin_t = _rope_tables(lay)

    for layer in range(depth):
        last = layer == depth - 1
        mod_l = mods[layer]
        j = layer // 2
        g1 = norm1_g[layer][None, :]
        if layer % 2 == 0:
            w_in = ssd_w_in[j].astype(BF16)
            wz, wx, wdt = w_in[:, :di], w_in[:, di:di + dc], w_in[:, di + dc:]
            dtb = ssd_dt_bias[j].reshape(1, 2 * nh)
            z, xbc, dt, dtt = _ssd_inproj(lay, xs, mod_l, g1, wz, wx, wdt, wdt.T, dtb, dtb.T)
            xc, bm, cm = _ssd_conv(lay, xbc, ssd_conv_w[j], ssd_conv_b[j][None, :], di, gn)
            y = _ssd_scan(lay, xc, bm, cm, dt, dtt, ssd_a_log[j], expand)
            dexp = jnp.repeat(ssd_d[j], SSD_HEAD_DIM)[None, :]
            xs = _ssd_outproj(lay, y, xc, z, dexp, ssd_norm_g[j][None, :], ssd_w_out[j].astype(BF16), xs, mod_l)
        else:
            w = att_w_qkv[j]
            wq = w[:, :dq].astype(BF16)
            wkt = w[:, dq:dq + dkv].T.astype(BF16)
            wv = w[:, dq + dkv:].reshape(d, ATT_KV_HEADS, ATT_HEAD_DIM)
            wv2 = jnp.pad(wv, ((0, 0), (0, 0), (0, LANES - ATT_HEAD_DIM))).reshape(d, ATT_KV_HEADS * LANES).astype(BF16)
            qg = (jnp.tile(att_q_gain[j], hq) * (ATT_HEAD_DIM ** -0.5 * LOG2E))[None, :]
            kgt = att_k_gain[j][:, None]
            bound = (1.01 * ATT_HEAD_DIM * jnp.max(jnp.abs(qg)) * jnp.max(jnp.abs(kgt))).reshape(1, 1)
            qn, kt, v = _att_inproj(lay, xs, mod_l, g1, wq, wkt, wv2, jnp.asarray(vone), qg, kgt, bound, jnp.asarray(gsum, BF16),
                                    jnp.asarray(gexp, BF16), cos128, sin128, cos_t, sin_t)
            o = _att_core(lay, qn, kt, v)
            xs = _att_outproj(lay, o, att_w_o[j].astype(BF16), xs, mod_l)
        n_tiles = lay.lat_tiles if last else lay.n_tiles
        xs = _moe(lay, layer, n_tiles, xs, mod_l, norm2_g[layer][None, :], moe_w_group[layer], moe_b_group[layer],
                  moe_w_router[layer], moe_b_router[layer], moe_w_gate, moe_w_up, moe_w_down)
    return xs[:lay.t_lat].reshape(b, n_lat, d)
```

```python
import functools

import numpy as np
import jax
import jax.numpy as jnp
from jax import lax
from jax.experimental import pallas as pl
from jax.experimental.pallas import tpu as pltpu

F32 = jnp.float32
BF16 = jnp.bfloat16
I32 = jnp.int32
HIGHEST = lax.Precision.HIGHEST

NORM_EPS = 1e-6
N_MOD = 6
GRID_W = 64
ROPE_THETA = 10000.0

SSD_HEAD_DIM = 64
SSD_N_GROUPS = 4
SSD_D_STATE = 128
SSD_CONV = 5
SSD_CHUNK = 128

ATT_HEAD_DIM = 64
ATT_KV_HEADS = 4
ATT_KEY_BLOCK = 4352
LOG2E = 1.4426950408889634
ATT_MIN_DENOM = 2.0 ** -60

MOE_GROUPS = 4
MOE_EPG = 8
MOE_EXPERTS = MOE_GROUPS * MOE_EPG
MOE_ROW_BLOCK = 256
ZROWS = MOE_ROW_BLOCK // 2
DEST_ROW = 1024

TM = 256
LANES = 128
SUBLANES = 8
HALO = 16
VMEM_LIMIT = 56 * 1024 * 1024


def _cparams(sem):
    return pltpu.CompilerParams(dimension_semantics=sem, vmem_limit_bytes=VMEM_LIMIT)


def _silu(v):
    return v / (1.0 + jnp.exp(-v))


def _softplus(v):
    return jnp.maximum(v, 0.0) + jnp.log1p(jnp.exp(-jnp.abs(v)))


def _norm_mod(x, g, shift, scale):
    ms = jnp.mean(x * x, axis=-1, keepdims=True)
    y = x * lax.rsqrt(ms + NORM_EPS) * g
    return y * (1.0 + scale) + shift


class _Layout:
    def __init__(self, batch, seq, n_ctx):
        assert seq % TM == 0 and n_ctx % TM == 0
        self.batch, self.seq, self.n_ctx = batch, seq, n_ctx
        self.t_lat = batch * seq
        self.t_ctx = batch * n_ctx
        self.t = self.t_lat + self.t_ctx
        self.lat_tiles = self.t_lat // TM
        self.seq_tiles = seq // TM
        self.ctx_tiles = n_ctx // TM
        self.n_tiles = self.t // TM
        self.nk = seq + n_ctx

    def mod_row(self, i):
        return jnp.where(i < self.lat_tiles, i // self.seq_tiles, self.batch)

    def tile_batch(self, i):
        return jnp.where(i < self.lat_tiles, i // self.seq_tiles, (i - self.lat_tiles) // self.ctx_tiles)

    def tile_pos(self, i):
        return jnp.where(i < self.lat_tiles, i % self.seq_tiles,
                         self.seq_tiles + (i - self.lat_tiles) % self.ctx_tiles)

    def seg_first(self, i):
        return jnp.where(i < self.lat_tiles, i % self.seq_tiles == 0, (i - self.lat_tiles) % self.ctx_tiles == 0)

    def seg_last(self, i):
        return jnp.where(i < self.lat_tiles, i % self.seq_tiles == self.seq_tiles - 1,
                         (i - self.lat_tiles) % self.ctx_tiles == self.ctx_tiles - 1)


def _mod_kernel(c_ref, w_ref, b_ref, o_ref):
    s = _silu(c_ref[...])
    o_ref[...] = jnp.dot(s, w_ref[...], precision=HIGHEST, preferred_element_type=F32) + b_ref[...]


def _mod_table(c8, mod_w, mod_b):
    depth, d, n = mod_w.shape
    tn = 1536
    assert n % tn == 0
    return pl.pallas_call(
        _mod_kernel,
        grid=(depth, n // tn),
        in_specs=[pl.BlockSpec((8, d), lambda l, j: (0, 0)),
                  pl.BlockSpec((None, d, tn), lambda l, j: (l, 0, j)),
                  pl.BlockSpec((None, 1, tn), lambda l, j: (l, 0, j))],
        out_specs=pl.BlockSpec((None, 8, tn), lambda l, j: (l, 0, j)),
        out_shape=jax.ShapeDtypeStruct((depth, 8, n), F32),
        compiler_params=_cparams(("parallel", "parallel")),
        name="mod_table",
    )(c8, mod_w, mod_b.reshape(depth, 1, n))


def _ssd_inproj_kernel(x_ref, mod_ref, g_ref, wz_ref, wx_ref, wdt_ref, wdtt_ref, dtb_ref, dtbt_ref,
                       z_ref, xbc_ref, dt_ref, dtt_ref):
    h = _norm_mod(x_ref[...], g_ref[...], mod_ref[0:1, :], mod_ref[1:2, :]).astype(BF16)
    z_ref[...] = jnp.dot(h, wz_ref[...], preferred_element_type=F32).astype(BF16)
    xbc_ref[...] = jnp.dot(h, wx_ref[...], preferred_element_type=F32).astype(BF16)
    nh = dtb_ref.shape[1] // 2
    dt = _softplus(jnp.dot(h, wdt_ref[...], preferred_element_type=F32) + dtb_ref[...])
    dt_ref[0] = dt[:, :nh]
    dt_ref[1] = dt[:, nh:]
    dtt = lax.dot_general(wdtt_ref[...], h, (((1,), (1,)), ((), ())), preferred_element_type=F32)
    dtt = _softplus(dtt + dtbt_ref[...])
    dtt_ref[0] = dtt[:nh, :]
    dtt_ref[1] = dtt[nh:, :]


def _ssd_inproj(lay, x, mod_l, g, wz, wx, wdt, wdtt, dtb, dtbt):
    t, d = x.shape
    di, dc, nh2 = wz.shape[1], wx.shape[1], wdt.shape[1]
    nh = nh2 // 2
    full = lambda a: pl.BlockSpec(a.shape, lambda i: (0,) * a.ndim)
    return pl.pallas_call(
        _ssd_inproj_kernel,
        grid=(lay.n_tiles,),
        in_specs=[pl.BlockSpec((TM, d), lambda i: (i, 0)),
                  pl.BlockSpec((None, N_MOD, d), lambda i: (lay.mod_row(i), 0, 0)),
                  full(g), full(wz), full(wx), full(wdt), full(wdtt), full(dtb), full(dtbt)],
        out_specs=[pl.BlockSpec((TM, di), lambda i: (i, 0)),
                   pl.BlockSpec((TM, dc), lambda i: (i, 0)),
                   pl.BlockSpec((2, TM, nh), lambda i: (0, i, 0)),
                   pl.BlockSpec((2, nh, TM), lambda i: (0, 0, i))],
        out_shape=[jax.ShapeDtypeStruct((t, di), BF16),
                   jax.ShapeDtypeStruct((t, dc), BF16),
                   jax.ShapeDtypeStruct((2, t, nh), F32),
                   jax.ShapeDtypeStruct((2, nh, t), F32)],
        compiler_params=_cparams(("parallel",)),
        name="ssd_inproj",
    )(x, mod_l, g, wz, wx, wdt, wdtt, dtb, dtbt)


def _ssd_conv_kernel(lay, di, gn, prev_ref, main_ref, next_ref, w_ref, b_ref, sh_ref, sht_ref, shb_ref,
                     xs_ref, bm_ref, cm_ref):
    i = pl.program_id(0)
    dc = main_ref.shape[1]
    half = SSD_CONV // 2
    zero_halo = jnp.zeros((HALO, dc), BF16)
    prev = jnp.where(lay.seg_first(i), zero_halo, prev_ref[...])
    nxt = jnp.where(lay.seg_last(i), zero_halo, next_ref[...])
    cw = 256
    for c0 in range(0, dc, cw):
        u = main_ref[:, c0:c0 + cw]
        acc = b_ref[:, c0:c0 + cw] + u.astype(F32) * w_ref[half:half + 1, c0:c0 + cw]
        top = jnp.zeros((SUBLANES, cw), F32)
        bot = jnp.zeros((SUBLANES, cw), F32)
        for k in range(SSD_CONV):
            if k == half:
                continue
            wk = w_ref[k:k + 1, c0:c0 + cw]
            acc = acc + jnp.dot(sh_ref[k], u, preferred_element_type=F32) * wk
            if k < half:
                top = top + jnp.dot(sht_ref[k], prev[:, c0:c0 + cw], preferred_element_type=F32)[:SUBLANES] * wk
            else:
                bot = bot + jnp.dot(shb_ref[k], nxt[:, c0:c0 + cw], preferred_element_type=F32)[:SUBLANES] * wk
        acc = jnp.concatenate([acc[:SUBLANES] + top, acc[SUBLANES:TM - SUBLANES], acc[TM - SUBLANES:] + bot], axis=0)
        y = _silu(acc).astype(BF16)
        if c0 < di:
            xs_ref[:, c0:c0 + cw] = y
        elif c0 < di + gn:
            bm_ref[:, c0 - di:c0 - di + cw] = y
        else:
            cm_ref[:, c0 - di - gn:c0 - di - gn + cw] = y


def _conv_shift_matrices():
    half = SSD_CONV // 2
    sh = np.zeros((SSD_CONV, TM, TM), np.float32)
    sht = np.zeros((SSD_CONV, HALO, HALO), np.float32)
    shb = np.zeros((SSD_CONV, HALO, HALO), np.float32)
    for k in range(SSD_CONV):
        s = k - half
        sh[k] = np.eye(TM, k=s)
        for r in range(SUBLANES):
            if r + s < 0:
                sht[k, r, HALO + r + s] = 1.0
            if r - SUBLANES + s >= 0:
                shb[k, r, r - SUBLANES + s] = 1.0
    return jnp.asarray(sh, BF16), jnp.asarray(sht, BF16), jnp.asarray(shb, BF16)


def _ssd_conv(lay, xbc, conv_w, conv_b, di, gn):
    t, dc = xbc.shape
    sh, sht, shb = _conv_shift_matrices()
    hb = TM // HALO
    nhb = t // HALO
    assert gn == 512 and di % 512 == 0
    return pl.pallas_call(
        functools.partial(_ssd_conv_kernel, lay, di, gn),
        grid=(lay.n_tiles,),
        in_specs=[pl.BlockSpec((HALO, dc), lambda i: (jnp.maximum(i * hb - 1, 0), 0)),
                  pl.BlockSpec((TM, dc), lambda i: (i, 0)),
                  pl.BlockSpec((HALO, dc), lambda i: (jnp.minimum((i + 1) * hb, nhb - 1), 0)),
                  pl.BlockSpec(conv_w.shape, lambda i: (0, 0)),
                  pl.BlockSpec(conv_b.shape, lambda i: (0, 0)),
                  pl.BlockSpec(sh.shape, lambda i: (0, 0, 0)),
                  pl.BlockSpec(sht.shape, lambda i: (0, 0, 0)),
                  pl.BlockSpec(shb.shape, lambda i: (0, 0, 0))],
        out_specs=[pl.BlockSpec((TM, di), lambda i: (i, 0)),
                   pl.BlockSpec((TM, gn), lambda i: (i, 0)),
                   pl.BlockSpec((TM, gn), lambda i: (i, 0))],
        out_shape=[jax.ShapeDtypeStruct((t, di), BF16),
                   jax.ShapeDtypeStruct((t, gn), BF16),
                   jax.ShapeDtypeStruct((t, gn), BF16)],
        compiler_params=_cparams(("parallel",)),
        name="ssd_conv",
    )(xbc, xbc, xbc, conv_w, conv_b, sh, sht, shb)


def _ssd_scan_kernel(xf_ref, bf_ref, cf_ref, dtf_ref, dttf_ref, xb_ref, bb_ref, cb_ref, dtb_ref, dttb_ref,
                     alr_ref, alc_ref, e_ref, yf_ref, yb_ref, state_ref):
    step = pl.program_id(1)

    @pl.when(step == 0)
    def _():
        state_ref[...] = jnp.zeros_like(state_ref)

    _scan_chunk(0, step, xf_ref, bf_ref, cf_ref, dtf_ref, dttf_ref, alr_ref.at[0], alc_ref.at[0], e_ref,
                yf_ref, state_ref.at[0])
    _scan_chunk(1, step, xb_ref, bb_ref, cb_ref, dtb_ref, dttb_ref, alr_ref.at[1], alc_ref.at[1], e_ref,
                yb_ref, state_ref.at[1])


def _scan_chunk(d, step, xs_ref, bm_ref, cm_ref, dt_ref, dtt_ref, alr_ref, alc_ref, e_ref, y_ref, state_ref):
    q = SSD_CHUNK
    ng = SSD_N_GROUPS
    ns = SSD_D_STATE
    gw = xs_ref.shape[1] // ng
    hpg = gw // SSD_HEAD_DIM

    row = lax.broadcasted_iota(I32, (q, q), 0)
    col = lax.broadcasted_iota(I32, (q, q), 1)
    lmask = row >= col if d == 0 else row <= col
    lmask_t = col >= row if d == 0 else col <= row

    dt = dt_ref[...]
    dtt = dtt_ref[...]
    da = dt * (-jnp.exp(alr_ref[...]))
    dat = dtt * (-jnp.exp(alc_ref[...]))
    a_cum = jnp.dot(lmask.astype(F32), da, precision=HIGHEST, preferred_element_type=F32)
    a_cum_t = jnp.dot(dat, lmask_t.astype(F32), precision=HIGHEST, preferred_element_type=F32)
    a_end = jnp.sum(da, axis=0, keepdims=True)
    w_end = jnp.exp(a_end - a_cum) * dt

    e3 = e_ref[...]

    def expand(v):
        hi = v.astype(BF16)
        r1 = v - hi.astype(F32)
        mid = r1.astype(BF16)
        lo = (r1 - mid.astype(F32)).astype(BF16)
        return jnp.dot(jnp.concatenate([hi, mid, lo], axis=1), e3, preferred_element_type=F32)

    a_exp = expand(a_cum)
    w_exp = expand(w_end)
    end_exp = expand(jnp.broadcast_to(a_end, (SUBLANES, a_end.shape[1])))[0:1, :]
    decay_in = jnp.exp(a_exp)
    chunk_decay = jnp.exp(end_exp)
    xw = (xs_ref[...].astype(F32) * w_exp).astype(BF16)

    lane = lax.broadcasted_iota(I32, (q, LANES), 1)
    for g in range(ng):
        bg = bm_ref[:, g * ns:(g + 1) * ns]
        cg = cm_ref[:, g * ns:(g + 1) * ns]
        cb = lax.dot_general(cg, bg, (((1,), (1,)), ((), ())), preferred_element_type=F32)
        cb = jnp.where(lmask, cb, 0.0)
        s_in = state_ref[g]
        y_off = jnp.dot(cg, s_in.astype(BF16), preferred_element_type=F32)
        new_s = lax.dot_general(bg, xw[:, g * gw:(g + 1) * gw], (((0,), (0,)), ((), ())),
                                preferred_element_type=F32)
        state_ref[g] = s_in * chunk_decay[:, g * gw:(g + 1) * gw] + new_s
        for pr in range(hpg // 2):
            mixes = []
            for hh in range(2):
                h = g * hpg + 2 * pr + hh
                seg = a_cum[:, h:h + 1] - a_cum_t[h:h + 1, :]
                lh = jnp.exp(jnp.minimum(seg, 0.0))
                mixes.append((cb * lh * dtt[h:h + 1, :]).astype(BF16))
            lhs = jnp.concatenate(mixes, axis=1)
            l0 = g * gw + pr * LANES
            xp = xs_ref[:, l0:l0 + LANES]
            zero = jnp.zeros_like(xp)
            rhs = jnp.concatenate([jnp.where(lane < SSD_HEAD_DIM, xp, zero),
                                   jnp.where(lane >= SSD_HEAD_DIM, xp, zero)], axis=0)
            y_pair = jnp.dot(lhs, rhs, preferred_element_type=F32)
            y_pair = y_pair + y_off[:, pr * LANES:(pr + 1) * LANES] * decay_in[:, l0:l0 + LANES]
            y_ref[:, l0:l0 + LANES] = y_pair.astype(BF16)


def _ssd_scan(lay, xs, bm, cm, dt, dtt, a_log, expand):
    t, di = xs.shape
    gn = bm.shape[1]
    nh = dt.shape[2]
    q = SSD_CHUNK
    nct, nlt = lay.n_ctx // q, lay.seq // q
    nc = nct + nlt
    ctx_base = lay.t_lat // q

    def blk(d, b, s):
        j_ctx = jnp.where(d == 0, s, nct - 1 - s)
        j_lat = jnp.where(d == 0, s - nct, nlt - 1 - (s - nct))
        return jnp.where(s < nct, ctx_base + b * nct + j_ctx, b * nlt + j_lat)

    def chunk_specs(d):
        return [pl.BlockSpec((q, di), lambda b, s: (blk(d, b, s), 0)),
                pl.BlockSpec((q, gn), lambda b, s: (blk(d, b, s), 0)),
                pl.BlockSpec((q, gn), lambda b, s: (blk(d, b, s), 0)),
                pl.BlockSpec((None, q, nh), lambda b, s: (d, blk(d, b, s), 0)),
                pl.BlockSpec((None, nh, q), lambda b, s: (d, 0, blk(d, b, s)))]

    return pl.pallas_call(
        _ssd_scan_kernel,
        grid=(lay.batch, nc),
        in_specs=chunk_specs(0) + chunk_specs(1) + [
            pl.BlockSpec((2, 1, nh), lambda b, s: (0, 0, 0)),
            pl.BlockSpec((2, nh, 1), lambda b, s: (0, 0, 0)),
            pl.BlockSpec(expand.shape, lambda b, s: (0, 0))],
        out_specs=[pl.BlockSpec((q, di), lambda b, s: (blk(0, b, s), 0)),
                   pl.BlockSpec((q, di), lambda b, s: (blk(1, b, s), 0))],
        out_shape=[jax.ShapeDtypeStruct((t, di), BF16), jax.ShapeDtypeStruct((t, di), BF16)],
        scratch_shapes=[pltpu.VMEM((2, SSD_N_GROUPS, SSD_D_STATE, di // SSD_N_GROUPS), F32)],
        compiler_params=_cparams(("arbitrary", "arbitrary")),
        name="ssd_scan",
    )(xs, bm, cm, dt, dtt, xs, bm, cm, dt, dtt, a_log.reshape(2, 1, nh), a_log.reshape(2, nh, 1), expand)


def _ssd_outproj_kernel(yf_ref, yb_ref, xs_ref, z_ref, dexp_ref, ng_ref, w_ref, x_ref, mod_ref, o_ref):
    y = yf_ref[...].astype(F32) + yb_ref[...].astype(F32) + xs_ref[...].astype(F32) * dexp_ref[...]
    y = y * _silu(z_ref[...].astype(F32))
    gw = y.shape[1] // SSD_N_GROUPS
    parts = []
    for g in range(SSD_N_GROUPS):
        yg = y[:, g * gw:(g + 1) * gw]
        ms = jnp.mean(yg * yg, axis=-1, keepdims=True)
        parts.append((yg * lax.rsqrt(ms + NORM_EPS) * ng_ref[:, g * gw:(g + 1) * gw]).astype(BF16))
    yn = jnp.concatenate(parts, axis=1)
    m = jnp.dot(yn, w_ref[...], preferred_element_type=F32)
    o_ref[...] = x_ref[...] + mod_ref[2:3, :] * m


def _ssd_outproj(lay, yf, yb, xs, z, dexp, ng, w, x, mod_l):
    t, d = x.shape
    di = xs.shape[1]
    full = lambda a: pl.BlockSpec(a.shape, lambda i: (0,) * a.ndim)
    return pl.pallas_call(
        _ssd_outproj_kernel,
        grid=(lay.n_tiles,),
        in_specs=[pl.BlockSpec((TM, di), lambda i: (i, 0)),
                  pl.BlockSpec((TM, di), lambda i: (i, 0)),
                  pl.BlockSpec((TM, di), lambda i: (i, 0)),
                  pl.BlockSpec((TM, di), lambda i: (i, 0)),
                  full(dexp), full(ng), full(w),
                  pl.BlockSpec((TM, d), lambda i: (i, 0)),
                  pl.BlockSpec((None, N_MOD, d), lambda i: (lay.mod_row(i), 0, 0))],
        out_specs=pl.BlockSpec((TM, d), lambda i: (i, 0)),
        out_shape=jax.ShapeDtypeStruct((t, d), F32),
        compiler_params=_cparams(("parallel",)),
        name="ssd_outproj",
    )(yf, yb, xs, z, dexp, ng, w, x, mod_l)


def _att_inproj_kernel(x_ref, mod_ref, g_ref, wq_ref, wkt_ref, wv_ref, vone_ref, qg_ref, kgt_ref, bound_ref, gsum_ref, gexp_ref,
                       cos_ref, sin_ref, cost_ref, sint_ref, q_ref, kt_ref, v_ref):
    h = _norm_mod(x_ref[...], g_ref[...], mod_ref[0:1, :], mod_ref[1:2, :]).astype(BF16)
    hd = ATT_HEAD_DIM
    q = jnp.dot(h, wq_ref[...], preferred_element_type=F32)
    ssum = jnp.dot((q * q).astype(BF16), gsum_ref[...], preferred_element_type=F32)
    r = lax.rsqrt(ssum * (1.0 / hd) + NORM_EPS)
    r_hi = r.astype(BF16)
    r_lo = (r - r_hi.astype(F32)).astype(BF16)
    r_exp = jnp.dot(jnp.concatenate([r_hi, r_lo], axis=1), gexp_ref[...], preferred_element_type=F32)
    qn = q * r_exp * qg_ref[...]
    nl = qn.shape[1]
    lane = lax.broadcasted_iota(I32, qn.shape, 1)
    partner = jnp.where((lane & 1) == 0, pltpu.roll(qn, nl - 1, 1), pltpu.roll(qn, 1, 1))
    reps = nl // LANES
    cos = jnp.concatenate([cos_ref[...]] * reps, axis=1)
    sin = jnp.concatenate([sin_ref[...]] * reps, axis=1)
    q_ref[...] = (qn * cos + partner * sin).astype(BF16)
    kt = lax.dot_general(wkt_ref[...], h, (((1,), (1,)), ((), ())), preferred_element_type=F32)
    ct, st = cost_ref[...], sint_ref[...]
    sub = lax.broadcasted_iota(I32, ct.shape, 0)
    for kh in range(ATT_KV_HEADS):
        blk = kt[kh * hd:(kh + 1) * hd]
        rk = lax.rsqrt(jnp.sum(blk * blk, axis=0, keepdims=True) * (1.0 / hd) + NORM_EPS)
        kn = blk * rk * kgt_ref[...]
        kpart = jnp.where((sub & 1) == 0, pltpu.roll(kn, hd - 1, 0), pltpu.roll(kn, 1, 0))
        kt_ref[kh, 0:hd, :] = (kn * ct + kpart * st).astype(BF16)
        kt_ref[kh, hd:2 * hd, :] = jnp.where(sub == 0, -bound_ref[...], 0.0).astype(BF16)
    v = (jnp.dot(h, wv_ref[...], preferred_element_type=F32) + vone_ref[...]).astype(BF16)
    for kh in range(ATT_KV_HEADS):
        v_ref[kh] = v[:, kh * LANES:(kh + 1) * LANES]


def _att_inproj(lay, x, mod_l, g, wq, wkt, wv2, vone, qg, kgt, bound, gsum, gexp, cos, sin, cost, sint):
    t, d = x.shape
    dq = wq.shape[1]
    hd = ATT_HEAD_DIM
    full = lambda a: pl.BlockSpec(a.shape, lambda i: (0,) * a.ndim)
    return pl.pallas_call(
        _att_inproj_kernel,
        grid=(lay.n_tiles,),
        in_specs=[pl.BlockSpec((TM, d), lambda i: (i, 0)),
                  pl.BlockSpec((None, N_MOD, d), lambda i: (lay.mod_row(i), 0, 0)),
                  full(g), full(wq), full(wkt), full(wv2), full(vone), full(qg), full(kgt), full(bound), full(gsum), full(gexp),
                  pl.BlockSpec((TM, LANES), lambda i: (lay.tile_pos(i), 0)),
                  pl.BlockSpec((TM, LANES), lambda i: (lay.tile_pos(i), 0)),
                  pl.BlockSpec((hd, TM), lambda i: (0, lay.tile_pos(i))),
                  pl.BlockSpec((hd, TM), lambda i: (0, lay.tile_pos(i)))],
        out_specs=[pl.BlockSpec((TM, dq), lambda i: (i, 0)),
                   pl.BlockSpec((None, ATT_KV_HEADS, 2 * hd, TM), lambda i: (lay.tile_batch(i), 0, 0, lay.tile_pos(i))),
                   pl.BlockSpec((None, ATT_KV_HEADS, TM, LANES), lambda i: (lay.tile_batch(i), 0, lay.tile_pos(i), 0))],
        out_shape=[jax.ShapeDtypeStruct((t, dq), BF16),
                   jax.ShapeDtypeStruct((lay.batch, ATT_KV_HEADS, 2 * hd, lay.nk), BF16),
                   jax.ShapeDtypeStruct((lay.batch, ATT_KV_HEADS, lay.nk, LANES), BF16)],
        compiler_params=_cparams(("parallel",)),
        name="att_inproj",
    )(x, mod_l, g, wq, wkt, wv2, vone, qg, kgt, bound, gsum, gexp, cos, sin, cost, sint)


def _attend(q_ref, kt_ref, v_ref, o_ref, k0, n_keys):
    hd = ATT_HEAD_DIM
    tm = q_ref.shape[0]
    heads = q_ref.shape[1] // hd
    qs = [q_ref[:, u * hd:(u + 1) * hd] for u in range(heads)]

    bk = min(ATT_KEY_BLOCK, n_keys)
    assert n_keys % bk == 0 and k0 % LANES == 0 and bk % LANES == 0

    def body(j, carry):
        start = pl.multiple_of(k0 + j * bk, LANES)
        kb = kt_ref[0:hd, pl.ds(start, bk)]
        vb = v_ref[pl.ds(start, bk), :]
        out = []
        for u in range(heads):
            m_old, acc = carry[u]
            s = jnp.dot(qs[u], kb, preferred_element_type=F32)
            m_new = jnp.maximum(m_old, jnp.max(s, axis=-1, keepdims=True))
            p = jnp.exp2((s - m_new).astype(BF16))
            acc = jnp.exp2(m_old - m_new) * acc + jnp.dot(p, vb, preferred_element_type=F32)
            out.append((m_new, acc))
        return tuple(out)

    init = tuple((jnp.full((tm, 1), -jnp.inf, F32), jnp.zeros((tm, LANES), F32)) for _ in range(heads))
    final = lax.fori_loop(0, n_keys // bk, body, init) if n_keys > bk else body(0, init)
    outs = [acc[:, :hd] / acc[:, hd:hd + 1] for _, acc in final]
    o_ref[...] = jnp.concatenate(outs, axis=1).astype(BF16)


def _attend_bounded(q_ref, kt_ref, v_ref, o_ref):
    hd = ATT_HEAD_DIM
    tm = q_ref.shape[0]
    k_lo = kt_ref[...]
    k_hi = jnp.concatenate([k_lo[hd:], k_lo[:hd]], axis=0)
    vb = v_ref[...]
    lane = lax.broadcasted_iota(I32, (tm, LANES), 1)
    outs, dens = [], []
    for pr in range(q_ref.shape[1] // LANES):
        qp = q_ref[:, pr * LANES:(pr + 1) * LANES]
        q_lo = jnp.where(lane < hd, qp, (lane == hd).astype(BF16))
        q_hi = jnp.where(lane >= hd, qp, (lane == 0).astype(BF16))
        for qa, ka in ((q_lo, k_lo), (q_hi, k_hi)):
            s = jnp.dot(qa, ka, preferred_element_type=F32)
            acc = jnp.dot(jnp.exp2(s).astype(BF16), vb, preferred_element_type=F32)
            den = acc[:, hd:hd + 1]
            outs.append(acc[:, :hd] / den)
            dens.append(den)
    o_ref[...] = jnp.concatenate(outs, axis=1).astype(BF16)
    return jnp.min(jnp.concatenate(dens, axis=1))


def _att_core_kernel(lay, q_ref, kt_ref, v_ref, o_ref):
    qi = pl.program_id(2)

    @pl.when(qi < lay.seq_tiles)
    def _():
        smallest = _attend_bounded(q_ref, kt_ref, v_ref, o_ref)

        @pl.when(jnp.logical_not(smallest > ATT_MIN_DENOM))
        def _():
            _attend(q_ref, kt_ref, v_ref, o_ref, 0, lay.nk)

    @pl.when(qi >= lay.seq_tiles)
    def _():
        _attend(q_ref, kt_ref, v_ref, o_ref, lay.seq, lay.n_ctx)


def _att_core(lay, qn, kt, v):
    t, dq = qn.shape
    hd = ATT_HEAD_DIM
    qw = dq // ATT_KV_HEADS
    per_b = lay.seq_tiles + lay.ctx_tiles

    def row_tile(b, qi):
        return jnp.where(qi < lay.seq_tiles, b * lay.seq_tiles + qi,
                         lay.lat_tiles + b * lay.ctx_tiles + (qi - lay.seq_tiles))

    return pl.pallas_call(
        functools.partial(_att_core_kernel, lay),
        grid=(lay.batch, ATT_KV_HEADS, per_b),
        in_specs=[pl.BlockSpec((TM, qw), lambda b, kh, qi: (row_tile(b, qi), kh)),
                  pl.BlockSpec((None, None, 2 * hd, lay.nk), lambda b, kh, qi: (b, kh, 0, 0)),
                  pl.BlockSpec((None, None, lay.nk, LANES), lambda b, kh, qi: (b, kh, 0, 0))],
        out_specs=pl.BlockSpec((TM, qw), lambda b, kh, qi: (row_tile(b, qi), kh)),
        out_shape=jax.ShapeDtypeStruct((t, dq), BF16),
        compiler_params=_cparams(("parallel", "parallel", "arbitrary")),
        name="att_core",
    )(qn, kt, v)


def _att_outproj_kernel(o_ref, w_ref, x_ref, mod_ref, out_ref):
    m = jnp.dot(o_ref[...], w_ref[...], preferred_element_type=F32)
    out_ref[...] = x_ref[...] + mod_ref[2:3, :] * m


def _att_outproj(lay, o, w, x, mod_l):
    t, d = x.shape
    return pl.pallas_call(
        _att_outproj_kernel,
        grid=(lay.n_tiles,),
        in_specs=[pl.BlockSpec((TM, o.shape[1]), lambda i: (i, 0)),
                  pl.BlockSpec(w.shape, lambda i: (0, 0)),
                  pl.BlockSpec((TM, d), lambda i: (i, 0)),
                  pl.BlockSpec((None, N_MOD, d), lambda i: (lay.mod_row(i), 0, 0))],
        out_specs=pl.BlockSpec((TM, d), lambda i: (i, 0)),
        out_shape=jax.ShapeDtypeStruct((t, d), F32),
        compiler_params=_cparams(("parallel",)),
        name="att_outproj",
    )(o, w, x, mod_l)


R_EID, R_RANK, R_W = 0, 2, 4
GROUP_LANE0 = MOE_EXPERTS


def _store_token_rows(ref, val):
    rows, d = val.shape
    for s in range(d // LANES):
        ref[pl.ds(s, rows, stride=d // LANES), :] = val[:, s * LANES:(s + 1) * LANES]


def _load_token_rows(ref, d):
    n = d // LANES
    rows = ref.shape[0] // n
    return jnp.concatenate([ref[pl.ds(s, rows, stride=n), :] for s in range(n)], axis=1)


def _token(ref, r, n):
    return ref.at[pl.ds(pl.multiple_of(r * n, n), n)]


def _router_kernel(x_ref, mod_ref, g_ref, wr_ref, br_ref, r_ref, rt_ref, cnt_ref, base_ref):
    i = pl.program_id(0)

    @pl.when(i == 0)
    def _():
        base_ref[...] = jnp.zeros_like(base_ref)

    f = _norm_mod(x_ref[...], g_ref[...], mod_ref[3:4, :], mod_ref[4:5, :])
    f_hi = f.astype(BF16)
    f_lo = (f - f_hi.astype(F32)).astype(BF16)
    part = jnp.dot(f_hi, wr_ref[...], preferred_element_type=F32)
    part_lo = jnp.dot(f_lo, wr_ref[:, :LANES], preferred_element_type=F32)
    logits = part[:, :LANES] + part[:, LANES:] + part_lo + br_ref[...]
    lane = lax.broadcasted_iota(I32, logits.shape, 1)
    neg = jnp.float32(-jnp.inf)
    big = jnp.int32(LANES)

    def first_max(vals):
        top = jnp.max(vals, axis=-1, keepdims=True)
        idx = jnp.min(jnp.where(vals == top, lane, big), axis=-1, keepdims=True)
        return top, idx

    g_mask = (lane >= GROUP_LANE0) & (lane < GROUP_LANE0 + MOE_GROUPS)
    glog = jnp.where(g_mask, logits, neg)
    g_top, g_idx = first_max(glog)
    g_w = 1.0 / jnp.sum(jnp.exp(glog - g_top), axis=-1, keepdims=True)
    e0 = (g_idx - GROUP_LANE0) * MOE_EPG
    elog = jnp.where((lane >= e0) & (lane < e0 + MOE_EPG), logits, neg)
    v1, i1 = first_max(elog)
    v2, i2 = first_max(jnp.where(lane == i1, neg, elog))
    ex = jnp.exp(v2 - v1)
    w1 = g_w / (1.0 + ex)
    w2 = g_w * ex / (1.0 + ex)

    oh1 = (lane == i1).astype(F32)
    oh2 = (lane == i2).astype(F32)
    tm = logits.shape[0]
    rr = lax.broadcasted_iota(I32, (tm, tm), 0)
    cc = lax.broadcasted_iota(I32, (tm, tm), 1)
    before = (cc < rr).astype(BF16)
    cum1 = jnp.dot(before, oh1.astype(BF16), preferred_element_type=F32)
    cum2 = jnp.dot(before, oh2.astype(BF16), preferred_element_type=F32)
    tot1 = jnp.sum(oh1, axis=0, keepdims=True)
    tot2 = jnp.sum(oh2, axis=0, keepdims=True)
    base = base_ref[...]
    rank1 = jnp.sum(oh1 * (cum1 + base), axis=-1, keepdims=True)
    rank2 = jnp.sum(oh2 * (cum2 + base + tot1), axis=-1, keepdims=True)
    base = base + tot1 + tot2
    base_ref[...] = base
    cnt_ref[...] = base

    rec = jnp.zeros(logits.shape, F32)
    for k, val in ((R_EID, i1.astype(F32)), (R_EID + 1, i2.astype(F32)), (R_RANK, rank1), (R_RANK + 1, rank2),
                   (R_W, w1), (R_W + 1, w2)):
        rec = jnp.where(lane == k, val, rec)
    r_ref[...] = rec
    rt_ref[...] = rec.T[:SUBLANES, :]


def _router(lay, n_tiles, x, mod_l, g, wr, br):
    t, d = x.shape
    return pl.pallas_call(
        _router_kernel,
        grid=(n_tiles,),
        in_specs=[pl.BlockSpec((TM, d), lambda i: (i, 0)),
                  pl.BlockSpec((None, N_MOD, d), lambda i: (lay.mod_row(i), 0, 0)),
                  pl.BlockSpec(g.shape, lambda i: (0, 0)),
                  pl.BlockSpec(wr.shape, lambda i: (0, 0)),
                  pl.BlockSpec(br.shape, lambda i: (0, 0))],
        out_specs=[pl.BlockSpec((TM, LANES), lambda i: (i, 0)),
                   pl.BlockSpec((SUBLANES, TM), lambda i: (0, i)),
                   pl.BlockSpec((1, LANES), lambda i: (0, 0))],
        out_shape=[jax.ShapeDtypeStruct((n_tiles * TM, LANES), F32),
                   jax.ShapeDtypeStruct((SUBLANES, n_tiles * TM), F32),
                   jax.ShapeDtypeStruct((1, LANES), F32)],
        scratch_shapes=[pltpu.VMEM((1, LANES), F32)],
        compiler_params=_cparams(("arbitrary",)),
        name="moe_router",
    )(x, mod_l, g, wr, br)


def _dispatch_kernel(fill_ref, plen_ref, dest_hbm, x_ref, mod_ref, g_ref, xs_hbm, dest_s, fbuf, zbuf, dsem, fsem, ssem):
    i = pl.program_id(0)
    n = pl.num_programs(0)
    tn = x_ref.shape[1] // LANES

    def drain(s):
        for k in range(2):
            pltpu.make_async_copy(fbuf.at[s], xs_hbm.at[pl.ds(0, TM * tn)], ssem.at[s]).wait()

    def dest_copy(j):
        return pltpu.make_async_copy(dest_hbm.at[j], dest_s.at[pl.ds((j % 2) * DEST_ROW, DEST_ROW)], dsem.at[j % 2])

    def pad_pieces(e, fn):
        plen = plen_ref[e]
        for bit in range(ZROWS.bit_length() - 1, -1, -1):
            size = 1 << bit

            @pl.when((plen & size) != 0)
            def _():
                start = fill_ref[e] + ((plen >> (bit + 1)) << (bit + 1))
                fn(pltpu.make_async_copy(zbuf.at[pl.ds(0, size * tn)],
                                         xs_hbm.at[pl.ds(pl.multiple_of(start * tn, tn), size * tn)], fsem.at[0]))

    @pl.when(i == 0)
    def _():
        zbuf[...] = jnp.zeros_like(zbuf)
        dest_copy(0).start()

        def start_e(e, c):
            pad_pieces(e, lambda cp: cp.start())
            return c

        def wait_e(e, c):
            pad_pieces(e, lambda cp: cp.wait())
            return c

        def tail_pieces(fn):
            def piece(t, c):
                start = fill_ref[MOE_EXPERTS] + t * ZROWS
                fn(pltpu.make_async_copy(zbuf, xs_hbm.at[pl.ds(pl.multiple_of(start * tn, tn), ZROWS * tn)],
                                         fsem.at[0]))
                return c
            lax.fori_loop(0, plen_ref[MOE_EXPERTS] // ZROWS, piece, 0)

        lax.fori_loop(0, MOE_EXPERTS, start_e, 0)
        tail_pieces(lambda cp: cp.start())
        lax.fori_loop(0, MOE_EXPERTS, wait_e, 0)
        tail_pieces(lambda cp: cp.wait())

    @pl.when(i + 1 < n)
    def _():
        dest_copy(i + 1).start()

    dest_copy(i).wait()
    slot = i % 2

    @pl.when(i >= 2)
    def _():
        drain(slot)

    f = _norm_mod(x_ref[...], g_ref[...], mod_ref[3:4, :], mod_ref[4:5, :])
    _store_token_rows(fbuf.at[slot], f)

    def body(r, c):
        for k in range(2):
            row = dest_s[slot * DEST_ROW + k * TM + r]
            pltpu.make_async_copy(_token(fbuf.at[slot], r, tn), _token(xs_hbm, row, tn),
                                  ssem.at[slot]).start(priority=k)
        return c

    lax.fori_loop(0, TM, body, 0, unroll=8)

    @pl.when(i == n - 1)
    def _():
        drain(slot)

        @pl.when(n > 1)
        def _():
            drain(1 - slot)


def _dispatch(lay, fill, plen, dest, x, mod_l, g, cap):
    n_tiles = dest.shape[0]
    d = x.shape[1]
    tn = d // LANES
    grid_spec = pltpu.PrefetchScalarGridSpec(
        num_scalar_prefetch=2,
        grid=(n_tiles,),
        in_specs=[pl.BlockSpec(memory_space=pl.ANY),
                  pl.BlockSpec((TM, d), lambda i, fill, plen: (i, 0)),
                  pl.BlockSpec((None, N_MOD, d), lambda i, fill, plen: (lay.mod_row(i), 0, 0)),
                  pl.BlockSpec(g.shape, lambda i, fill, plen: (0, 0))],
        out_specs=pl.BlockSpec(memory_space=pl.ANY),
        scratch_shapes=[pltpu.SMEM((2 * DEST_ROW,), I32),
                        pltpu.VMEM((2, TM * tn, LANES), F32),
                        pltpu.VMEM((ZROWS * tn, LANES), F32),
                        pltpu.SemaphoreType.DMA((2,)),
                        pltpu.SemaphoreType.DMA((1,)),
                        pltpu.SemaphoreType.DMA((2,))],
    )
    return pl.pallas_call(
        _dispatch_kernel,
        grid_spec=grid_spec,
        out_shape=jax.ShapeDtypeStruct((cap * tn, LANES), F32),
        compiler_params=_cparams(("arbitrary",)),
        name="moe_dispatch",
    )(fill, plen, dest, x, mod_l, g)


def _expert_kernel(layer, be_ref, nv_ref, blk_ref, grp_ref, nxt_ref, x_ref, wg_hbm, wu_hbm, wd_hbm, y_ref,
                   wgf, wuf, wdf, wgb, wub, wdb, wsem):
    i = pl.program_id(0)
    live = nv_ref[i] > 0
    first = jnp.logical_or(i == 0, be_ref[i] != be_ref[jnp.maximum(i - 1, 0)])
    slot = grp_ref[i] % 2

    def fetch(e, s):
        return (pltpu.make_async_copy(wg_hbm.at[layer, e], wgf.at[s], wsem.at[s]),
                pltpu.make_async_copy(wu_hbm.at[layer, e], wuf.at[s], wsem.at[s]),
                pltpu.make_async_copy(wd_hbm.at[layer, e], wdf.at[s], wsem.at[s]))

    @pl.when(i == 0)
    def _():
        for cp in fetch(be_ref[0], 0):
            cp.start()

    @pl.when(jnp.logical_and(live, first))
    def _():
        for cp in fetch(be_ref[i], slot):
            cp.wait()

        @pl.when(nxt_ref[i] >= 0)
        def _():
            for cp in fetch(nxt_ref[i], 1 - slot):
                cp.start()

        wgb[...] = wgf[slot].astype(BF16)
        wub[...] = wuf[slot].astype(BF16)
        wdb[...] = wdf[slot].astype(BF16)

    @pl.when(live)
    def _():
        xb = _load_token_rows(x_ref, wgb.shape[0]).astype(BF16)
        hg = jnp.dot(xb, wgb[...], preferred_element_type=F32)
        hu = jnp.dot(xb, wub[...], preferred_element_type=F32)
        act = (_silu(hg) * hu).astype(BF16)
        _store_token_rows(y_ref, jnp.dot(act, wdb[...], preferred_element_type=F32))

    @pl.when(jnp.logical_not(live))
    def _():
        y_ref[...] = jnp.zeros_like(y_ref)


def _experts(layer, block_e, block_nv, block_src, block_grp, block_nxt, xs, wg, wu, wd):
    n_blocks = block_e.shape[0]
    d, dff = wg.shape[2:]
    rb = MOE_ROW_BLOCK
    blk = (rb * (d // LANES), LANES)
    grid_spec = pltpu.PrefetchScalarGridSpec(
        num_scalar_prefetch=5,
        grid=(n_blocks,),
        in_specs=[pl.BlockSpec(blk, lambda i, be, nv, src, grp, nxt: (src[i], 0)),
                  pl.BlockSpec(memory_space=pl.ANY),
                  pl.BlockSpec(memory_space=pl.ANY),
                  pl.BlockSpec(memory_space=pl.ANY)],
        out_specs=pl.BlockSpec(blk, lambda i, be, nv, src, grp, nxt: (i, 0)),
        scratch_shapes=[pltpu.VMEM((2, d, dff), F32),
                        pltpu.VMEM((2, d, dff), F32),
                        pltpu.VMEM((2, dff, d), F32),
                        pltpu.VMEM((d, dff), BF16),
                        pltpu.VMEM((d, dff), BF16),
                        pltpu.VMEM((dff, d), BF16),
                        pltpu.SemaphoreType.DMA((2,))],
    )
    return pl.pallas_call(
        functools.partial(_expert_kernel, layer),
        grid_spec=grid_spec,
        out_shape=jax.ShapeDtypeStruct(xs.shape, F32),
        compiler_params=_cparams(("arbitrary",)),
        name="moe_experts",
    )(block_e, block_nv, block_src, block_grp, block_nxt, xs, wg, wu, wd)


def _combine_kernel(dest_hbm, x_ref, r_ref, mod_ref, ys_hbm, o_ref, dest_s, gbuf, dsem, gsem):
    i = pl.program_id(0)
    n = pl.num_programs(0)
    tn = x_ref.shape[1] // LANES

    def dest_copy(j):
        return pltpu.make_async_copy(dest_hbm.at[j], dest_s.at[pl.ds((j % 3) * DEST_ROW, DEST_ROW)], dsem.at[j % 3])

    def gather_tile(j):
        def body(r, c):
            for k in range(2):
                row = dest_s[(j % 3) * DEST_ROW + k * TM + r]
                pltpu.make_async_copy(_token(ys_hbm, row, tn), _token(gbuf.at[j % 2, k], r, tn),
                                      gsem.at[j % 2]).start(priority=k)
            return c
        lax.fori_loop(0, TM, body, 0, unroll=8)

    @pl.when(i == 0)
    def _():
        dest_copy(0).start()
        dest_copy(0).wait()
        gather_tile(0)

        @pl.when(n > 1)
        def _():
            dest_copy(1).start()

    @pl.when(i + 1 < n)
    def _():
        dest_copy(i + 1).wait()

        @pl.when(i + 2 < n)
        def _():
            dest_copy(i + 2).start()

        gather_tile(i + 1)

    slot = i % 2
    for k in range(2):
        pltpu.make_async_copy(ys_hbm.at[pl.ds(0, TM * tn)], gbuf.at[slot, k], gsem.at[slot]).wait()
    w1 = r_ref[:, R_W:R_W + 1]
    w2 = r_ref[:, R_W + 1:R_W + 2]
    d = x_ref.shape[1]
    y = _load_token_rows(gbuf.at[slot, 0], d) * w1 + _load_token_rows(gbuf.at[slot, 1], d) * w2
    o_ref[...] = x_ref[...] + mod_ref[5:6, :] * y


def _combine(lay, n_tiles, dest, x, rec, mod_l, ys):
    d = x.shape[1]
    return pl.pallas_call(
        _combine_kernel,
        grid=(n_tiles,),
        in_specs=[pl.BlockSpec(memory_space=pl.ANY),
                  pl.BlockSpec((TM, d), lambda i: (i, 0)),
                  pl.BlockSpec((TM, LANES), lambda i: (i, 0)),
                  pl.BlockSpec((None, N_MOD, d), lambda i: (lay.mod_row(i), 0, 0)),
                  pl.BlockSpec(memory_space=pl.ANY)],
        out_specs=pl.BlockSpec((TM, d), lambda i: (i, 0)),
        out_shape=jax.ShapeDtypeStruct((n_tiles * TM, d), F32),
        scratch_shapes=[pltpu.SMEM((3 * DEST_ROW,), I32),
                        pltpu.VMEM((2, 2, TM * (d // LANES), LANES), F32),
                        pltpu.SemaphoreType.DMA((3,)),
                        pltpu.SemaphoreType.DMA((2,))],
        compiler_params=_cparams(("arbitrary",)),
        name="moe_combine",
    )(dest, x, rec, mod_l, ys)


def _moe(lay, layer, n_tiles, x, mod_l, g2, w_group, b_group, w_router, b_router, wg, wu, wd):
    d = x.shape[1]
    t_tok = n_tiles * TM
    pad = LANES - MOE_EXPERTS - MOE_GROUPS
    wr = jnp.concatenate([w_router, w_group, jnp.zeros((d, pad), F32)], axis=1)
    br = jnp.concatenate([b_router, b_group, jnp.zeros((pad,), F32)])[None, :]
    wr_hi = wr.astype(BF16)
    wr_lo = (wr - wr_hi.astype(F32)).astype(BF16)
    rec, rec_t, cnt = _router(lay, n_tiles, x, mod_l, g2, jnp.concatenate([wr_hi, wr_lo], axis=1), br)

    rb = MOE_ROW_BLOCK
    eid = rec_t[R_EID:R_EID + 2].astype(I32)
    rank = rec_t[R_RANK:R_RANK + 2].astype(I32)
    counts = cnt[0, :MOE_EXPERTS].astype(I32)
    padded = (counts + rb - 1) // rb * rb
    pad_ends = jnp.cumsum(padded)
    pad_starts = pad_ends - padded
    experts = jnp.arange(MOE_EXPERTS, dtype=I32)[:, None, None]
    dest = jnp.sum(jnp.where(eid[None] == experts, pad_starts[:, None, None], 0), axis=0) + rank
    dest = dest.reshape(2, n_tiles, TM).transpose(1, 0, 2).reshape(n_tiles, 2 * TM)
    dest = jnp.pad(dest, ((0, 0), (0, DEST_ROW - 2 * TM)))
    n_blocks = -(-(2 * t_tok) // rb) + MOE_EXPERTS
    used = pad_ends[-1] // rb
    idx = jnp.arange(n_blocks, dtype=I32)
    block_src = jnp.clip(idx, 0, jnp.maximum(used - 1, 0))
    start = block_src * rb
    block_e = jnp.sum(pad_ends[None, :] <= start[:, None], axis=1).astype(I32)
    block_nv = jnp.where(idx < used, jnp.clip(counts[block_e] - (start - pad_starts[block_e]), 0, rb), 0).astype(I32)

    cap = n_blocks * rb
    fill = jnp.concatenate([pad_starts + counts, pad_ends[-1:]]).astype(I32)
    plen = jnp.concatenate([padded - counts, cap - pad_ends[-1:]]).astype(I32)
    xs = _dispatch(lay, fill, plen, dest, x, mod_l, g2, cap)
    nonempty = counts > 0
    e_ids = jnp.arange(MOE_EXPERTS, dtype=I32)
    later = jnp.where((e_ids[None, :] > e_ids[:, None]) & nonempty[None, :], e_ids[None, :], MOE_EXPERTS)
    nxt_of_e = jnp.min(later, axis=1)
    nxt_of_e = jnp.where(nxt_of_e == MOE_EXPERTS, -1, nxt_of_e).astype(I32)
    grp_of_e = (jnp.cumsum(nonempty) - nonempty).astype(I32)
    ys = _experts(layer, block_e, block_nv, block_src, grp_of_e[block_e], nxt_of_e[block_e], xs, wg, wu, wd)
    return _combine(lay, n_tiles, dest, x, rec, mod_l, ys)


def _rope_tables(lay):
    half = ATT_HEAD_DIM // 2
    pos = jnp.arange(lay.seq)
    rowp = (pos // GRID_W).astype(F32)
    colp = (pos % GRID_W).astype(F32)
    freqs = ROPE_THETA ** (-jnp.arange(0, half, 2, dtype=F32) / half)
    ang = jnp.concatenate([rowp[:, None] * freqs, colp[:, None] * freqs], axis=-1)
    cos = jnp.concatenate([jnp.cos(ang), jnp.ones((lay.n_ctx, half), F32)], axis=0)
    sin = jnp.concatenate([jnp.sin(ang), jnp.zeros((lay.n_ctx, half), F32)], axis=0)
    cos_d = jnp.repeat(cos, 2, axis=1)
    sin_d = jnp.repeat(sin, 2, axis=1) * jnp.tile(jnp.asarray([-1.0, 1.0], F32), half)
    reps = LANES // ATT_HEAD_DIM
    return jnp.tile(cos_d, (1, reps)), jnp.tile(sin_d, (1, reps)), cos_d.T, sin_d.T


def kernel(x, c, ctx, c_ctx, mod_w, mod_b, norm1_g, norm2_g, ssd_w_in, ssd_conv_w, ssd_conv_b, ssd_dt_bias,
           ssd_a_log, ssd_d, ssd_norm_g, ssd_w_out, att_w_qkv, att_q_gain, att_k_gain, att_w_o, moe_w_group,
           moe_b_group, moe_w_router, moe_b_router, moe_w_gate, moe_w_up, moe_w_down):
    b, n_lat, d = x.shape
    n_ctx = ctx.shape[1]
    depth = mod_w.shape[0]
    lay = _Layout(b, n_lat, n_ctx)
    assert b + 1 <= 8

    xs = jnp.concatenate([x.reshape(lay.t_lat, d), ctx.reshape(lay.t_ctx, d)], axis=0)
    c8 = jnp.concatenate([c, c_ctx[None, :], jnp.zeros((8 - b - 1, d), F32)], axis=0)
    mods = _mod_table(c8, mod_w, mod_b).reshape(depth, 8, N_MOD, d)

    nh = ssd_dt_bias.shape[2]
    di = nh * SSD_HEAD_DIM
    gn = SSD_N_GROUPS * SSD_D_STATE
    dc = di + 2 * gn
    expand = jnp.tile(jnp.repeat(jnp.eye(nh, dtype=BF16), SSD_HEAD_DIM, axis=1), (3, 1))

    hq = att_w_qkv.shape[2] // ATT_HEAD_DIM - 2 * ATT_KV_HEADS
    dq = hq * ATT_HEAD_DIM
    dkv = ATT_KV_HEADS * ATT_HEAD_DIM
    gsum = np.zeros((dq, LANES), np.float32)
    gsum[np.arange(dq), np.arange(dq) // ATT_HEAD_DIM] = 1.0
    gexp = np.concatenate([gsum.T, gsum.T], axis=0)
    vone = np.zeros((1, ATT_KV_HEADS * LANES), np.float32)
    vone[0, np.arange(ATT_KV_HEADS) * LANES + ATT_HEAD_DIM] = 1.0
    cos128, sin128, cos_t, sin_t = _rope_tables(lay)

    for layer in range(depth):
        last = layer == depth - 1
        mod_l = mods[layer]
        j = layer // 2
        g1 = norm1_g[layer][None, :]
        if layer % 2 == 0:
            w_in = ssd_w_in[j].astype(BF16)
            wz, wx, wdt = w_in[:, :di], w_in[:, di:di + dc], w_in[:, di + dc:]
            dtb = ssd_dt_bias[j].reshape(1, 2 * nh)
            z, xbc, dt, dtt = _ssd_inproj(lay, xs, mod_l, g1, wz, wx, wdt, wdt.T, dtb, dtb.T)
            xc, bm, cm = _ssd_conv(lay, xbc, ssd_conv_w[j], ssd_conv_b[j][None, :], di, gn)
            yf, yb = _ssd_scan(lay, xc, bm, cm, dt, dtt, ssd_a_log[j], expand)
            dexp = jnp.repeat(ssd_d[j], SSD_HEAD_DIM)[None, :]
            xs = _ssd_outproj(lay, yf, yb, xc, z, dexp, ssd_norm_g[j][None, :], ssd_w_out[j].astype(BF16), xs, mod_l)
        else:
            w = att_w_qkv[j]
            wq = w[:, :dq].astype(BF16)
            wkt = w[:, dq:dq + dkv].T.astype(BF16)
            wv = w[:, dq + dkv:].reshape(d, ATT_KV_HEADS, ATT_HEAD_DIM)
            wv2 = jnp.pad(wv, ((0, 0), (0, 0), (0, LANES - ATT_HEAD_DIM))).reshape(d, ATT_KV_HEADS * LANES).astype(BF16)
            qg = (jnp.tile(att_q_gain[j], hq) * (ATT_HEAD_DIM ** -0.5 * LOG2E))[None, :]
            kgt = att_k_gain[j][:, None]
            bound = (1.01 * ATT_HEAD_DIM * jnp.max(jnp.abs(qg)) * jnp.max(jnp.abs(kgt))).reshape(1, 1)
            qn, kt, v = _att_inproj(lay, xs, mod_l, g1, wq, wkt, wv2, jnp.asarray(vone), qg, kgt, bound, jnp.asarray(gsum, BF16),
                                    jnp.asarray(gexp, BF16), cos128, sin128, cos_t, sin_t)
            o = _att_core(lay, qn, kt, v)
            xs = _att_outproj(lay, o, att_w_o[j].astype(BF16), xs, mod_l)
        n_tiles = lay.lat_tiles if last else lay.n_tiles
        xs = _moe(lay, layer, n_tiles, xs, mod_l, norm2_g[layer][None, :], moe_w_group[layer], moe_b_group[layer],
                  moe_w_router[layer], moe_b_router[layer], moe_w_gate, moe_w_up, moe_w_down)
    return xs[:lay.t_lat].reshape(b, n_lat, d)
```

```python
import functools

import numpy as np
import jax
import jax.numpy as jnp
from jax import lax
from jax.experimental import pallas as pl
from jax.experimental.pallas import tpu as pltpu

F32 = jnp.float32
BF16 = jnp.bfloat16
I32 = jnp.int32
HIGHEST = lax.Precision.HIGHEST

NORM_EPS = 1e-6
N_MOD = 6
GRID_W = 64
ROPE_THETA = 10000.0

SSD_HEAD_DIM = 64
SSD_N_GROUPS = 4
SSD_D_STATE = 128
SSD_CONV = 5
SSD_CHUNK = 128

ATT_HEAD_DIM = 64
ATT_KV_HEADS = 4
ATT_KEY_BLOCK = 4352
LOG2E = 1.4426950408889634
ATT_MIN_DENOM = 2.0 ** -60

MOE_GROUPS = 4
MOE_EPG = 8
MOE_EXPERTS = MOE_GROUPS * MOE_EPG
MOE_ROW_BLOCK = 256
ZROWS = MOE_ROW_BLOCK // 2
DEST_ROW = 1024

TM = 256
LANES = 128
SUBLANES = 8
HALO = 16
VMEM_LIMIT = 56 * 1024 * 1024


def _cparams(sem):
    return pltpu.CompilerParams(dimension_semantics=sem, vmem_limit_bytes=VMEM_LIMIT)


def _silu(v):
    return v / (1.0 + jnp.exp(-v))


def _softplus(v):
    return jnp.maximum(v, 0.0) + jnp.log1p(jnp.exp(-jnp.abs(v)))


def _norm_mod(x, g, shift, scale):
    ms = jnp.mean(x * x, axis=-1, keepdims=True)
    y = x * lax.rsqrt(ms + NORM_EPS) * g
    return y * (1.0 + scale) + shift


class _Layout:
    def __init__(self, batch, seq, n_ctx):
        assert seq % TM == 0 and n_ctx % TM == 0
        self.batch, self.seq, self.n_ctx = batch, seq, n_ctx
        self.t_lat = batch * seq
        self.t_ctx = batch * n_ctx
        self.t = self.t_lat + self.t_ctx
        self.lat_tiles = self.t_lat // TM
        self.seq_tiles = seq // TM
        self.ctx_tiles = n_ctx // TM
        self.n_tiles = self.t // TM
        self.nk = seq + n_ctx

    def mod_row(self, i):
        return jnp.where(i < self.lat_tiles, i // self.seq_tiles, self.batch)

    def tile_batch(self, i):
        return jnp.where(i < self.lat_tiles, i // self.seq_tiles, (i - self.lat_tiles) // self.ctx_tiles)

    def tile_pos(self, i):
        return jnp.where(i < self.lat_tiles, i % self.seq_tiles,
                         self.seq_tiles + (i - self.lat_tiles) % self.ctx_tiles)

    def seg_first(self, i):
        return jnp.where(i < self.lat_tiles, i % self.seq_tiles == 0, (i - self.lat_tiles) % self.ctx_tiles == 0)

    def seg_last(self, i):
        return jnp.where(i < self.lat_tiles, i % self.seq_tiles == self.seq_tiles - 1,
                         (i - self.lat_tiles) % self.ctx_tiles == self.ctx_tiles - 1)


def _mod_kernel(c_ref, w_ref, b_ref, o_ref):
    s = _silu(c_ref[...])
    o_ref[...] = jnp.dot(s, w_ref[...], precision=HIGHEST, preferred_element_type=F32) + b_ref[...]


def _mod_table(c8, mod_w, mod_b):
    depth, d, n = mod_w.shape
    tn = 1536
    assert n % tn == 0
    return pl.pallas_call(
        _mod_kernel,
        grid=(depth, n // tn),
        in_specs=[pl.BlockSpec((8, d), lambda l, j: (0, 0)),
                  pl.BlockSpec((None, d, tn), lambda l, j: (l, 0, j)),
                  pl.BlockSpec((None, 1, tn), lambda l, j: (l, 0, j))],
        out_specs=pl.BlockSpec((None, 8, tn), lambda l, j: (l, 0, j)),
        out_shape=jax.ShapeDtypeStruct((depth, 8, n), F32),
        compiler_params=_cparams(("parallel", "parallel")),
        name="mod_table",
    )(c8, mod_w, mod_b.reshape(depth, 1, n))


def _ssd_inproj_kernel(x_ref, mod_ref, g_ref, wz_ref, wx_ref, wdt_ref, wdtt_ref, dtb_ref, dtbt_ref,
                       z_ref, xbc_ref, dt_ref, dtt_ref):
    h = _norm_mod(x_ref[...], g_ref[...], mod_ref[0:1, :], mod_ref[1:2, :]).astype(BF16)
    z_ref[...] = jnp.dot(h, wz_ref[...], preferred_element_type=F32).astype(BF16)
    xbc_ref[...] = jnp.dot(h, wx_ref[...], preferred_element_type=F32).astype(BF16)
    nh = dtb_ref.shape[1] // 2
    dt = _softplus(jnp.dot(h, wdt_ref[...], preferred_element_type=F32) + dtb_ref[...])
    dt_ref[0] = dt[:, :nh]
    dt_ref[1] = dt[:, nh:]
    dtt = lax.dot_general(wdtt_ref[...], h, (((1,), (1,)), ((), ())), preferred_element_type=F32)
    dtt = _softplus(dtt + dtbt_ref[...])
    dtt_ref[0] = dtt[:nh, :]
    dtt_ref[1] = dtt[nh:, :]


def _ssd_inproj(lay, x, mod_l, g, wz, wx, wdt, wdtt, dtb, dtbt):
    t, d = x.shape
    di, dc, nh2 = wz.shape[1], wx.shape[1], wdt.shape[1]
    nh = nh2 // 2
    full = lambda a: pl.BlockSpec(a.shape, lambda i: (0,) * a.ndim)
    return pl.pallas_call(
        _ssd_inproj_kernel,
        grid=(lay.n_tiles,),
        in_specs=[pl.BlockSpec((TM, d), lambda i: (i, 0)),
                  pl.BlockSpec((None, N_MOD, d), lambda i: (lay.mod_row(i), 0, 0)),
                  full(g), full(wz), full(wx), full(wdt), full(wdtt), full(dtb), full(dtbt)],
        out_specs=[pl.BlockSpec((TM, di), lambda i: (i, 0)),
                   pl.BlockSpec((TM, dc), lambda i: (i, 0)),
                   pl.BlockSpec((2, TM, nh), lambda i: (0, i, 0)),
                   pl.BlockSpec((2, nh, TM), lambda i: (0, 0, i))],
        out_shape=[jax.ShapeDtypeStruct((t, di), BF16),
                   jax.ShapeDtypeStruct((t, dc), BF16),
                   jax.ShapeDtypeStruct((2, t, nh), F32),
                   jax.ShapeDtypeStruct((2, nh, t), F32)],
        compiler_params=_cparams(("parallel",)),
        name="ssd_inproj",
    )(x, mod_l, g, wz, wx, wdt, wdtt, dtb, dtbt)


def _ssd_conv_kernel(lay, di, gn, prev_ref, main_ref, next_ref, w_ref, b_ref, sh_ref, sht_ref, shb_ref,
                     xs_ref, bm_ref, cm_ref):
    i = pl.program_id(0)
    dc = main_ref.shape[1]
    half = SSD_CONV // 2
    zero_halo = jnp.zeros((HALO, dc), BF16)
    prev = jnp.where(lay.seg_first(i), zero_halo, prev_ref[...])
    nxt = jnp.where(lay.seg_last(i), zero_halo, next_ref[...])
    cw = 256
    for c0 in range(0, dc, cw):
        u = main_ref[:, c0:c0 + cw]
        acc = b_ref[:, c0:c0 + cw] + u.astype(F32) * w_ref[half:half + 1, c0:c0 + cw]
        top = jnp.zeros((SUBLANES, cw), F32)
        bot = jnp.zeros((SUBLANES, cw), F32)
        for k in range(SSD_CONV):
            if k == half:
                continue
            wk = w_ref[k:k + 1, c0:c0 + cw]
            acc = acc + jnp.dot(sh_ref[k], u, preferred_element_type=F32) * wk
            if k < half:
                top = top + jnp.dot(sht_ref[k], prev[:, c0:c0 + cw], preferred_element_type=F32)[:SUBLANES] * wk
            else:
                bot = bot + jnp.dot(shb_ref[k], nxt[:, c0:c0 + cw], preferred_element_type=F32)[:SUBLANES] * wk
        acc = jnp.concatenate([acc[:SUBLANES] + top, acc[SUBLANES:TM - SUBLANES], acc[TM - SUBLANES:] + bot], axis=0)
        y = _silu(acc).astype(BF16)
        if c0 < di:
            xs_ref[:, c0:c0 + cw] = y
        elif c0 < di + gn:
            bm_ref[:, c0 - di:c0 - di + cw] = y
        else:
            cm_ref[:, c0 - di - gn:c0 - di - gn + cw] = y


def _conv_shift_matrices():
    half = SSD_CONV // 2
    sh = np.zeros((SSD_CONV, TM, TM), np.float32)
    sht = np.zeros((SSD_CONV, HALO, HALO), np.float32)
    shb = np.zeros((SSD_CONV, HALO, HALO), np.float32)
    for k in range(SSD_CONV):
        s = k - half
        sh[k] = np.eye(TM, k=s)
        for r in range(SUBLANES):
            if r + s < 0:
                sht[k, r, HALO + r + s] = 1.0
            if r - SUBLANES + s >= 0:
                shb[k, r, r - SUBLANES + s] = 1.0
    return jnp.asarray(sh, BF16), jnp.asarray(sht, BF16), jnp.asarray(shb, BF16)


def _ssd_conv(lay, xbc, conv_w, conv_b, di, gn):
    t, dc = xbc.shape
    sh, sht, shb = _conv_shift_matrices()
    hb = TM // HALO
    nhb = t // HALO
    assert gn == 512 and di % 512 == 0
    return pl.pallas_call(
        functools.partial(_ssd_conv_kernel, lay, di, gn),
        grid=(lay.n_tiles,),
        in_specs=[pl.BlockSpec((HALO, dc), lambda i: (jnp.maximum(i * hb - 1, 0), 0)),
                  pl.BlockSpec((TM, dc), lambda i: (i, 0)),
                  pl.BlockSpec((HALO, dc), lambda i: (jnp.minimum((i + 1) * hb, nhb - 1), 0)),
                  pl.BlockSpec(conv_w.shape, lambda i: (0, 0)),
                  pl.BlockSpec(conv_b.shape, lambda i: (0, 0)),
                  pl.BlockSpec(sh.shape, lambda i: (0, 0, 0)),
                  pl.BlockSpec(sht.shape, lambda i: (0, 0, 0)),
                  pl.BlockSpec(shb.shape, lambda i: (0, 0, 0))],
        out_specs=[pl.BlockSpec((TM, di), lambda i: (i, 0)),
                   pl.BlockSpec((TM, gn), lambda i: (i, 0)),
                   pl.BlockSpec((TM, gn), lambda i: (i, 0))],
        out_shape=[jax.ShapeDtypeStruct((t, di), BF16),
                   jax.ShapeDtypeStruct((t, gn), BF16),
                   jax.ShapeDtypeStruct((t, gn), BF16)],
        compiler_params=_cparams(("parallel",)),
        name="ssd_conv",
    )(xbc, xbc, xbc, conv_w, conv_b, sh, sht, shb)


def _ssd_scan_kernel(xf_ref, bf_ref, cf_ref, dtf_ref, dttf_ref, xb_ref, bb_ref, cb_ref, dtb_ref, dttb_ref,
                     alr_ref, alc_ref, e_ref, yf_ref, yb_ref, state_ref):
    step = pl.program_id(1)

    @pl.when(step == 0)
    def _():
        state_ref[...] = jnp.zeros_like(state_ref)

    _scan_chunk(0, step, xf_ref, bf_ref, cf_ref, dtf_ref, dttf_ref, alr_ref.at[0], alc_ref.at[0], e_ref,
                yf_ref, state_ref.at[0])
    _scan_chunk(1, step, xb_ref, bb_ref, cb_ref, dtb_ref, dttb_ref, alr_ref.at[1], alc_ref.at[1], e_ref,
                yb_ref, state_ref.at[1])


def _scan_chunk(d, step, xs_ref, bm_ref, cm_ref, dt_ref, dtt_ref, alr_ref, alc_ref, e_ref, y_ref, state_ref):
    q = SSD_CHUNK
    ng = SSD_N_GROUPS
    ns = SSD_D_STATE
    gw = xs_ref.shape[1] // ng
    hpg = gw // SSD_HEAD_DIM

    row = lax.broadcasted_iota(I32, (q, q), 0)
    col = lax.broadcasted_iota(I32, (q, q), 1)
    lmask = row >= col if d == 0 else row <= col
    lmask_t = col >= row if d == 0 else col <= row

    dt = dt_ref[...]
    dtt = dtt_ref[...]
    da = dt * (-jnp.exp(alr_ref[...]))
    dat = dtt * (-jnp.exp(alc_ref[...]))
    a_cum = jnp.dot(lmask.astype(F32), da, precision=HIGHEST, preferred_element_type=F32)
    a_cum_t = jnp.dot(dat, lmask_t.astype(F32), precision=HIGHEST, preferred_element_type=F32)
    a_end = jnp.sum(da, axis=0, keepdims=True)
    w_end = jnp.exp(a_end - a_cum) * dt

    e3 = e_ref[...]

    def expand(v):
        hi = v.astype(BF16)
        r1 = v - hi.astype(F32)
        mid = r1.astype(BF16)
        lo = (r1 - mid.astype(F32)).astype(BF16)
        return jnp.dot(jnp.concatenate([hi, mid, lo], axis=1), e3, preferred_element_type=F32)

    a_exp = expand(a_cum)
    w_exp = expand(w_end)
    end_exp = expand(jnp.broadcast_to(a_end, (SUBLANES, a_end.shape[1])))[0:1, :]
    decay_in = jnp.exp(a_exp)
    chunk_decay = jnp.exp(end_exp)
    xw = (xs_ref[...].astype(F32) * w_exp).astype(BF16)

    lane = lax.broadcasted_iota(I32, (q, LANES), 1)
    for g in range(ng):
        bg = bm_ref[:, g * ns:(g + 1) * ns]
        cg = cm_ref[:, g * ns:(g + 1) * ns]
        cb = lax.dot_general(cg, bg, (((1,), (1,)), ((), ())), preferred_element_type=F32)
        cb = jnp.where(lmask, cb, 0.0)
        s_in = state_ref[g]
        y_off = jnp.dot(cg, s_in.astype(BF16), preferred_element_type=F32)
        new_s = lax.dot_general(bg, xw[:, g * gw:(g + 1) * gw], (((0,), (0,)), ((), ())),
                                preferred_element_type=F32)
        state_ref[g] = s_in * chunk_decay[:, g * gw:(g + 1) * gw] + new_s
        for pr in range(hpg // 2):
            mixes = []
            for hh in range(2):
                h = g * hpg + 2 * pr + hh
                seg = a_cum[:, h:h + 1] - a_cum_t[h:h + 1, :]
                lh = jnp.exp(jnp.minimum(seg, 0.0))
                mixes.append((cb * lh * dtt[h:h + 1, :]).astype(BF16))
            lhs = jnp.concatenate(mixes, axis=1)
            l0 = g * gw + pr * LANES
            xp = xs_ref[:, l0:l0 + LANES]
            zero = jnp.zeros_like(xp)
            rhs = jnp.concatenate([jnp.where(lane < SSD_HEAD_DIM, xp, zero),
                                   jnp.where(lane >= SSD_HEAD_DIM, xp, zero)], axis=0)
            y_pair = jnp.dot(lhs, rhs, preferred_element_type=F32)
            y_pair = y_pair + y_off[:, pr * LANES:(pr + 1) * LANES] * decay_in[:, l0:l0 + LANES]
            y_ref[:, l0:l0 + LANES] = y_pair.astype(BF16)


def _ssd_scan(lay, xs, bm, cm, dt, dtt, a_log, expand):
    t, di = xs.shape
    gn = bm.shape[1]
    nh = dt.shape[2]
    q = SSD_CHUNK
    nct, nlt = lay.n_ctx // q, lay.seq // q
    nc = nct + nlt
    ctx_base = lay.t_lat // q

    def blk(d, b, s):
        j_ctx = jnp.where(d == 0, s, nct - 1 - s)
        j_lat = jnp.where(d == 0, s - nct, nlt - 1 - (s - nct))
        return jnp.where(s < nct, ctx_base + b * nct + j_ctx, b * nlt + j_lat)

    def chunk_specs(d):
        return [pl.BlockSpec((q, di), lambda b, s: (blk(d, b, s), 0)),
                pl.BlockSpec((q, gn), lambda b, s: (blk(d, b, s), 0)),
                pl.BlockSpec((q, gn), lambda b, s: (blk(d, b, s), 0)),
                pl.BlockSpec((None, q, nh), lambda b, s: (d, blk(d, b, s), 0)),
                pl.BlockSpec((None, nh, q), lambda b, s: (d, 0, blk(d, b, s)))]

    return pl.pallas_call(
        _ssd_scan_kernel,
        grid=(lay.batch, nc),
        in_specs=chunk_specs(0) + chunk_specs(1) + [
            pl.BlockSpec((2, 1, nh), lambda b, s: (0, 0, 0)),
            pl.BlockSpec((2, nh, 1), lambda b, s: (0, 0, 0)),
            pl.BlockSpec(expand.shape, lambda b, s: (0, 0))],
        out_specs=[pl.BlockSpec((q, di), lambda b, s: (blk(0, b, s), 0)),
                   pl.BlockSpec((q, di), lambda b, s: (blk(1, b, s), 0))],
        out_shape=[jax.ShapeDtypeStruct((t, di), BF16), jax.ShapeDtypeStruct((t, di), BF16)],
        scratch_shapes=[pltpu.VMEM((2, SSD_N_GROUPS, SSD_D_STATE, di // SSD_N_GROUPS), F32)],
        compiler_params=_cparams(("arbitrary", "arbitrary")),
        name="ssd_scan",
    )(xs, bm, cm, dt, dtt, xs, bm, cm, dt, dtt, a_log.reshape(2, 1, nh), a_log.reshape(2, nh, 1), expand)


def _ssd_outproj_kernel(n_route, yf_ref, yb_ref, xs_ref, z_ref, dexp_ref, ng_ref, w_ref, x_ref, mod_ref,
                        g2_ref, wr_ref, br_ref, o_ref, r_ref, rt_ref, cnt_ref, base_ref):
    y = yf_ref[...].astype(F32) + yb_ref[...].astype(F32) + xs_ref[...].astype(F32) * dexp_ref[...]
    y = y * _silu(z_ref[...].astype(F32))
    gw = y.shape[1] // SSD_N_GROUPS
    parts = []
    for g in range(SSD_N_GROUPS):
        yg = y[:, g * gw:(g + 1) * gw]
        ms = jnp.mean(yg * yg, axis=-1, keepdims=True)
        parts.append((yg * lax.rsqrt(ms + NORM_EPS) * ng_ref[:, g * gw:(g + 1) * gw]).astype(BF16))
    yn = jnp.concatenate(parts, axis=1)
    m = jnp.dot(yn, w_ref[...], preferred_element_type=F32)
    x_new = x_ref[...] + mod_ref[2:3, :] * m
    o_ref[...] = x_new
    _route(pl.program_id(0), n_route, x_new, mod_ref, g2_ref, wr_ref, br_ref, r_ref, rt_ref, cnt_ref, base_ref)


def _ssd_outproj(lay, n_route, yf, yb, xs, z, dexp, ng, w, x, mod_l, g2, wr, br):
    t, d = x.shape
    di = xs.shape[1]
    full = lambda a: pl.BlockSpec(a.shape, lambda i: (0,) * a.ndim)
    r_in, r_out, r_shape, r_scratch = _route_specs(lay, g2, wr, br)
    return pl.pallas_call(
        functools.partial(_ssd_outproj_kernel, n_route),
        grid=(lay.n_tiles,),
        in_specs=[pl.BlockSpec((TM, di), lambda i: (i, 0)),
                  pl.BlockSpec((TM, di), lambda i: (i, 0)),
                  pl.BlockSpec((TM, di), lambda i: (i, 0)),
                  pl.BlockSpec((TM, di), lambda i: (i, 0)),
                  full(dexp), full(ng), full(w),
                  pl.BlockSpec((TM, d), lambda i: (i, 0)),
                  pl.BlockSpec((None, N_MOD, d), lambda i: (lay.mod_row(i), 0, 0))] + r_in,
        out_specs=[pl.BlockSpec((TM, d), lambda i: (i, 0))] + r_out,
        out_shape=[jax.ShapeDtypeStruct((t, d), F32)] + r_shape,
        scratch_shapes=r_scratch,
        compiler_params=_cparams(("arbitrary",)),
        name="ssd_outproj",
    )(yf, yb, xs, z, dexp, ng, w, x, mod_l, g2, wr, br)


def _att_inproj_kernel(x_ref, mod_ref, g_ref, wq_ref, wkt_ref, wv_ref, vone_ref, qg_ref, kgt_ref, bound_ref, gsum_ref, gexp_ref,
                       cos_ref, sin_ref, cost_ref, sint_ref, q_ref, kt_ref, v_ref):
    h = _norm_mod(x_ref[...], g_ref[...], mod_ref[0:1, :], mod_ref[1:2, :]).astype(BF16)
    hd = ATT_HEAD_DIM
    q = jnp.dot(h, wq_ref[...], preferred_element_type=F32)
    ssum = jnp.dot((q * q).astype(BF16), gsum_ref[...], preferred_element_type=F32)
    r = lax.rsqrt(ssum * (1.0 / hd) + NORM_EPS)
    r_hi = r.astype(BF16)
    r_lo = (r - r_hi.astype(F32)).astype(BF16)
    r_exp = jnp.dot(jnp.concatenate([r_hi, r_lo], axis=1), gexp_ref[...], preferred_element_type=F32)
    qn = q * r_exp * qg_ref[...]
    nl = qn.shape[1]
    lane = lax.broadcasted_iota(I32, qn.shape, 1)
    partner = jnp.where((lane & 1) == 0, pltpu.roll(qn, nl - 1, 1), pltpu.roll(qn, 1, 1))
    reps = nl // LANES
    cos = jnp.concatenate([cos_ref[...]] * reps, axis=1)
    sin = jnp.concatenate([sin_ref[...]] * reps, axis=1)
    q_ref[...] = (qn * cos + partner * sin).astype(BF16)
    kt = lax.dot_general(wkt_ref[...], h, (((1,), (1,)), ((), ())), preferred_element_type=F32)
    ct, st = cost_ref[...], sint_ref[...]
    sub = lax.broadcasted_iota(I32, ct.shape, 0)
    for kh in range(ATT_KV_HEADS):
        blk = kt[kh * hd:(kh + 1) * hd]
        rk = lax.rsqrt(jnp.sum(blk * blk, axis=0, keepdims=True) * (1.0 / hd) + NORM_EPS)
        kn = blk * rk * kgt_ref[...]
        kpart = jnp.where((sub & 1) == 0, pltpu.roll(kn, hd - 1, 0), pltpu.roll(kn, 1, 0))
        kt_ref[kh, 0:hd, :] = (kn * ct + kpart * st).astype(BF16)
        kt_ref[kh, hd:2 * hd, :] = jnp.where(sub == 0, -bound_ref[...], 0.0).astype(BF16)
    v = (jnp.dot(h, wv_ref[...], preferred_element_type=F32) + vone_ref[...]).astype(BF16)
    for kh in range(ATT_KV_HEADS):
        v_ref[kh] = v[:, kh * LANES:(kh + 1) * LANES]


def _att_inproj(lay, x, mod_l, g, wq, wkt, wv2, vone, qg, kgt, bound, gsum, gexp, cos, sin, cost, sint):
    t, d = x.shape
    dq = wq.shape[1]
    hd = ATT_HEAD_DIM
    full = lambda a: pl.BlockSpec(a.shape, lambda i: (0,) * a.ndim)
    return pl.pallas_call(
        _att_inproj_kernel,
        grid=(lay.n_tiles,),
        in_specs=[pl.BlockSpec((TM, d), lambda i: (i, 0)),
                  pl.BlockSpec((None, N_MOD, d), lambda i: (lay.mod_row(i), 0, 0)),
                  full(g), full(wq), full(wkt), full(wv2), full(vone), full(qg), full(kgt), full(bound), full(gsum), full(gexp),
                  pl.BlockSpec((TM, LANES), lambda i: (lay.tile_pos(i), 0)),
                  pl.BlockSpec((TM, LANES), lambda i: (lay.tile_pos(i), 0)),
                  pl.BlockSpec((hd, TM), lambda i: (0, lay.tile_pos(i))),
                  pl.BlockSpec((hd, TM), lambda i: (0, lay.tile_pos(i)))],
        out_specs=[pl.BlockSpec((TM, dq), lambda i: (i, 0)),
                   pl.BlockSpec((None, ATT_KV_HEADS, 2 * hd, TM), lambda i: (lay.tile_batch(i), 0, 0, lay.tile_pos(i))),
                   pl.BlockSpec((None, ATT_KV_HEADS, TM, LANES), lambda i: (lay.tile_batch(i), 0, lay.tile_pos(i), 0))],
        out_shape=[jax.ShapeDtypeStruct((t, dq), BF16),
                   jax.ShapeDtypeStruct((lay.batch, ATT_KV_HEADS, 2 * hd, lay.nk), BF16),
                   jax.ShapeDtypeStruct((lay.batch, ATT_KV_HEADS, lay.nk, LANES), BF16)],
        compiler_params=_cparams(("parallel",)),
        name="att_inproj",
    )(x, mod_l, g, wq, wkt, wv2, vone, qg, kgt, bound, gsum, gexp, cos, sin, cost, sint)


def _attend(q_ref, kt_ref, v_ref, o_ref, k0, n_keys):
    hd = ATT_HEAD_DIM
    tm = q_ref.shape[0]
    heads = q_ref.shape[1] // hd
    qs = [q_ref[:, u * hd:(u + 1) * hd] for u in range(heads)]

    bk = min(ATT_KEY_BLOCK, n_keys)
    assert n_keys % bk == 0 and k0 % LANES == 0 and bk % LANES == 0

    def body(j, carry):
        start = pl.multiple_of(k0 + j * bk, LANES)
        kb = kt_ref[0:hd, pl.ds(start, bk)]
        vb = v_ref[pl.ds(start, bk), :]
        out = []
        for u in range(heads):
            m_old, acc = carry[u]
            s = jnp.dot(qs[u], kb, preferred_element_type=F32)
            m_new = jnp.maximum(m_old, jnp.max(s, axis=-1, keepdims=True))
            p = jnp.exp2((s - m_new).astype(BF16))
            acc = jnp.exp2(m_old - m_new) * acc + jnp.dot(p, vb, preferred_element_type=F32)
            out.append((m_new, acc))
        return tuple(out)

    init = tuple((jnp.full((tm, 1), -jnp.inf, F32), jnp.zeros((tm, LANES), F32)) for _ in range(heads))
    final = lax.fori_loop(0, n_keys // bk, body, init) if n_keys > bk else body(0, init)
    outs = [acc[:, :hd] / acc[:, hd:hd + 1] for _, acc in final]
    o_ref[...] = jnp.concatenate(outs, axis=1).astype(BF16)


def _attend_bounded(q_ref, kt_ref, v_ref, o_ref):
    hd = ATT_HEAD_DIM
    tm = q_ref.shape[0]
    k_lo = kt_ref[...]
    k_hi = jnp.concatenate([k_lo[hd:], k_lo[:hd]], axis=0)
    vb = v_ref[...]
    lane = lax.broadcasted_iota(I32, (tm, LANES), 1)
    outs, dens = [], []
    for pr in range(q_ref.shape[1] // LANES):
        qp = q_ref[:, pr * LANES:(pr + 1) * LANES]
        q_lo = jnp.where(lane < hd, qp, (lane == hd).astype(BF16))
        q_hi = jnp.where(lane >= hd, qp, (lane == 0).astype(BF16))
        for qa, ka in ((q_lo, k_lo), (q_hi, k_hi)):
            s = jnp.dot(qa, ka, preferred_element_type=F32)
            acc = jnp.dot(jnp.exp2(s).astype(BF16), vb, preferred_element_type=F32)
            den = acc[:, hd:hd + 1]
            outs.append(acc[:, :hd] / den)
            dens.append(den)
    o_ref[...] = jnp.concatenate(outs, axis=1).astype(BF16)
    return jnp.min(jnp.concatenate(dens, axis=1))


def _att_core_kernel(lay, q_ref, kt_ref, v_ref, o_ref):
    qi = pl.program_id(2)

    @pl.when(qi < lay.seq_tiles)
    def _():
        smallest = _attend_bounded(q_ref, kt_ref, v_ref, o_ref)

        @pl.when(jnp.logical_not(smallest > ATT_MIN_DENOM))
        def _():
            _attend(q_ref, kt_ref, v_ref, o_ref, 0, lay.nk)

    @pl.when(qi >= lay.seq_tiles)
    def _():
        _attend(q_ref, kt_ref, v_ref, o_ref, lay.seq, lay.n_ctx)


def _att_core(lay, qn, kt, v):
    t, dq = qn.shape
    hd = ATT_HEAD_DIM
    qw = dq // ATT_KV_HEADS
    per_b = lay.seq_tiles + lay.ctx_tiles

    def row_tile(b, qi):
        return jnp.where(qi < lay.seq_tiles, b * lay.seq_tiles + qi,
                         lay.lat_tiles + b * lay.ctx_tiles + (qi - lay.seq_tiles))

    return pl.pallas_call(
        functools.partial(_att_core_kernel, lay),
        grid=(lay.batch, ATT_KV_HEADS, per_b),
        in_specs=[pl.BlockSpec((TM, qw), lambda b, kh, qi: (row_tile(b, qi), kh)),
                  pl.BlockSpec((None, None, 2 * hd, lay.nk), lambda b, kh, qi: (b, kh, 0, 0)),
                  pl.BlockSpec((None, None, lay.nk, LANES), lambda b, kh, qi: (b, kh, 0, 0))],
        out_specs=pl.BlockSpec((TM, qw), lambda b, kh, qi: (row_tile(b, qi), kh)),
        out_shape=jax.ShapeDtypeStruct((t, dq), BF16),
        compiler_params=_cparams(("parallel", "parallel", "arbitrary")),
        name="att_core",
    )(qn, kt, v)


def _att_outproj_kernel(n_route, o_ref, w_ref, x_ref, mod_ref, g2_ref, wr_ref, br_ref,
                        out_ref, r_ref, rt_ref, cnt_ref, base_ref):
    m = jnp.dot(o_ref[...], w_ref[...], preferred_element_type=F32)
    x_new = x_ref[...] + mod_ref[2:3, :] * m
    out_ref[...] = x_new
    _route(pl.program_id(0), n_route, x_new, mod_ref, g2_ref, wr_ref, br_ref, r_ref, rt_ref, cnt_ref, base_ref)


def _att_outproj(lay, n_route, o, w, x, mod_l, g2, wr, br):
    t, d = x.shape
    r_in, r_out, r_shape, r_scratch = _route_specs(lay, g2, wr, br)
    return pl.pallas_call(
        functools.partial(_att_outproj_kernel, n_route),
        grid=(lay.n_tiles,),
        in_specs=[pl.BlockSpec((TM, o.shape[1]), lambda i: (i, 0)),
                  pl.BlockSpec(w.shape, lambda i: (0, 0)),
                  pl.BlockSpec((TM, d), lambda i: (i, 0)),
                  pl.BlockSpec((None, N_MOD, d), lambda i: (lay.mod_row(i), 0, 0))] + r_in,
        out_specs=[pl.BlockSpec((TM, d), lambda i: (i, 0))] + r_out,
        out_shape=[jax.ShapeDtypeStruct((t, d), F32)] + r_shape,
        scratch_shapes=r_scratch,
        compiler_params=_cparams(("arbitrary",)),
        name="att_outproj",
    )(o, w, x, mod_l, g2, wr, br)


R_EID, R_RANK, R_W = 0, 2, 4
GROUP_LANE0 = MOE_EXPERTS


def _store_token_rows(ref, val):
    rows, d = val.shape
    for s in range(d // LANES):
        ref[pl.ds(s, rows, stride=d // LANES), :] = val[:, s * LANES:(s + 1) * LANES]


def _load_token_rows(ref, d):
    n = d // LANES
    rows = ref.shape[0] // n
    return jnp.concatenate([ref[pl.ds(s, rows, stride=n), :] for s in range(n)], axis=1)


def _token(ref, r, n):
    return ref.at[pl.ds(pl.multiple_of(r * n, n), n)]


def _route(i, n_route, x, mod_ref, g_ref, wr_ref, br_ref, r_ref, rt_ref, cnt_ref, base_ref):
    @pl.when(i == 0)
    def _():
        base_ref[...] = jnp.zeros_like(base_ref)

    f = _norm_mod(x, g_ref[...], mod_ref[3:4, :], mod_ref[4:5, :])
    f_hi = f.astype(BF16)
    f_lo = (f - f_hi.astype(F32)).astype(BF16)
    part = jnp.dot(f_hi, wr_ref[...], preferred_element_type=F32)
    part_lo = jnp.dot(f_lo, wr_ref[:, :LANES], preferred_element_type=F32)
    logits = part[:, :LANES] + part[:, LANES:] + part_lo + br_ref[...]
    lane = lax.broadcasted_iota(I32, logits.shape, 1)
    neg = jnp.float32(-jnp.inf)
    big = jnp.int32(LANES)

    def first_max(vals):
        top = jnp.max(vals, axis=-1, keepdims=True)
        idx = jnp.min(jnp.where(vals == top, lane, big), axis=-1, keepdims=True)
        return top, idx

    g_mask = (lane >= GROUP_LANE0) & (lane < GROUP_LANE0 + MOE_GROUPS)
    glog = jnp.where(g_mask, logits, neg)
    g_top, g_idx = first_max(glog)
    g_w = 1.0 / jnp.sum(jnp.exp(glog - g_top), axis=-1, keepdims=True)
    e0 = (g_idx - GROUP_LANE0) * MOE_EPG
    elog = jnp.where((lane >= e0) & (lane < e0 + MOE_EPG), logits, neg)
    v1, i1 = first_max(elog)
    v2, i2 = first_max(jnp.where(lane == i1, neg, elog))
    ex = jnp.exp(v2 - v1)
    w1 = g_w / (1.0 + ex)
    w2 = g_w * ex / (1.0 + ex)

    oh1 = (lane == i1).astype(F32)
    oh2 = (lane == i2).astype(F32)
    tm = logits.shape[0]
    rr = lax.broadcasted_iota(I32, (tm, tm), 0)
    cc = lax.broadcasted_iota(I32, (tm, tm), 1)
    before = (cc < rr).astype(BF16)
    cum1 = jnp.dot(before, oh1.astype(BF16), preferred_element_type=F32)
    cum2 = jnp.dot(before, oh2.astype(BF16), preferred_element_type=F32)
    tot1 = jnp.sum(oh1, axis=0, keepdims=True)
    tot2 = jnp.sum(oh2, axis=0, keepdims=True)
    base = base_ref[...]
    rank1 = jnp.sum(oh1 * (cum1 + base), axis=-1, keepdims=True)
    rank2 = jnp.sum(oh2 * (cum2 + base + tot1), axis=-1, keepdims=True)
    base = base + jnp.where(i < n_route, tot1 + tot2, 0.0)
    base_ref[...] = base
    cnt_ref[...] = base

    rec = jnp.zeros(logits.shape, F32)
    for k, val in ((R_EID, i1.astype(F32)), (R_EID + 1, i2.astype(F32)), (R_RANK, rank1), (R_RANK + 1, rank2),
                   (R_W, w1), (R_W + 1, w2)):
        rec = jnp.where(lane == k, val, rec)
    r_ref[...] = rec
    rt_ref[...] = rec.T[:SUBLANES, :]


def _route_specs(lay, g2, wr, br):
    t = lay.n_tiles * TM
    ins = [pl.BlockSpec(g2.shape, lambda i: (0, 0)),
           pl.BlockSpec(wr.shape, lambda i: (0, 0)),
           pl.BlockSpec(br.shape, lambda i: (0, 0))]
    outs = [pl.BlockSpec((TM, LANES), lambda i: (i, 0)),
            pl.BlockSpec((SUBLANES, TM), lambda i: (0, i)),
            pl.BlockSpec((1, LANES), lambda i: (0, 0))]
    shapes = [jax.ShapeDtypeStruct((t, LANES), F32),
              jax.ShapeDtypeStruct((SUBLANES, t), F32),
              jax.ShapeDtypeStruct((1, LANES), F32)]
    return ins, outs, shapes, [pltpu.VMEM((1, LANES), F32)]


def _dispatch_kernel(fill_ref, plen_ref, dest_hbm, x_ref, mod_ref, g_ref, xs_hbm, dest_s, fbuf, zbuf, dsem, fsem, ssem):
    i = pl.program_id(0)
    n = pl.num_programs(0)
    tn = x_ref.shape[1] // LANES

    def drain(s):
        for k in range(2):
            pltpu.make_async_copy(fbuf.at[s], xs_hbm.at[pl.ds(0, TM * tn)], ssem.at[s]).wait()

    def dest_copy(j):
        return pltpu.make_async_copy(dest_hbm.at[j], dest_s.at[pl.ds((j % 2) * DEST_ROW, DEST_ROW)], dsem.at[j % 2])

    def pad_pieces(e, fn):
        plen = plen_ref[e]
        for bit in range(ZROWS.bit_length() - 1, -1, -1):
            size = 1 << bit

            @pl.when((plen & size) != 0)
            def _():
                start = fill_ref[e] + ((plen >> (bit + 1)) << (bit + 1))
                fn(pltpu.make_async_copy(zbuf.at[pl.ds(0, size * tn)],
                                         xs_hbm.at[pl.ds(pl.multiple_of(start * tn, tn), size * tn)], fsem.at[0]))

    @pl.when(i == 0)
    def _():
        zbuf[...] = jnp.zeros_like(zbuf)
        dest_copy(0).start()

        def start_e(e, c):
            pad_pieces(e, lambda cp: cp.start())
            return c

        def wait_e(e, c):
            pad_pieces(e, lambda cp: cp.wait())
            return c

        def tail_pieces(fn):
            def piece(t, c):
                start = fill_ref[MOE_EXPERTS] + t * ZROWS
                fn(pltpu.make_async_copy(zbuf, xs_hbm.at[pl.ds(pl.multiple_of(start * tn, tn), ZROWS * tn)],
                                         fsem.at[0]))
                return c
            lax.fori_loop(0, plen_ref[MOE_EXPERTS] // ZROWS, piece, 0)

        lax.fori_loop(0, MOE_EXPERTS, start_e, 0)
        tail_pieces(lambda cp: cp.start())
        lax.fori_loop(0, MOE_EXPERTS, wait_e, 0)
        tail_pieces(lambda cp: cp.wait())

    @pl.when(i + 1 < n)
    def _():
        dest_copy(i + 1).start()

    dest_copy(i).wait()
    slot = i % 2

    @pl.when(i >= 2)
    def _():
        drain(slot)

    f = _norm_mod(x_ref[...], g_ref[...], mod_ref[3:4, :], mod_ref[4:5, :])
    _store_token_rows(fbuf.at[slot], f)

    def body(r, c):
        for k in range(2):
            row = dest_s[slot * DEST_ROW + k * TM + r]
            pltpu.make_async_copy(_token(fbuf.at[slot], r, tn), _token(xs_hbm, row, tn),
                                  ssem.at[slot]).start(priority=k)
        return c

    lax.fori_loop(0, TM, body, 0, unroll=8)

    @pl.when(i == n - 1)
    def _():
        drain(slot)

        @pl.when(n > 1)
        def _():
            drain(1 - slot)


def _dispatch(lay, fill, plen, dest, x, mod_l, g, cap):
    n_tiles = dest.shape[0]
    d = x.shape[1]
    tn = d // LANES
    grid_spec = pltpu.PrefetchScalarGridSpec(
        num_scalar_prefetch=2,
        grid=(n_tiles,),
        in_specs=[pl.BlockSpec(memory_space=pl.ANY),
                  pl.BlockSpec((TM, d), lambda i, fill, plen: (i, 0)),
                  pl.BlockSpec((None, N_MOD, d), lambda i, fill, plen: (lay.mod_row(i), 0, 0)),
                  pl.BlockSpec(g.shape, lambda i, fill, plen: (0, 0))],
        out_specs=pl.BlockSpec(memory_space=pl.ANY),
        scratch_shapes=[pltpu.SMEM((2 * DEST_ROW,), I32),
                        pltpu.VMEM((2, TM * tn, LANES), F32),
                        pltpu.VMEM((ZROWS * tn, LANES), F32),
                        pltpu.SemaphoreType.DMA((2,)),
                        pltpu.SemaphoreType.DMA((1,)),
                        pltpu.SemaphoreType.DMA((2,))],
    )
    return pl.pallas_call(
        _dispatch_kernel,
        grid_spec=grid_spec,
        out_shape=jax.ShapeDtypeStruct((cap * tn, LANES), F32),
        compiler_params=_cparams(("arbitrary",)),
        name="moe_dispatch",
    )(fill, plen, dest, x, mod_l, g)


def _expert_kernel(layer, be_ref, nv_ref, blk_ref, grp_ref, nxt_ref, x_ref, wg_hbm, wu_hbm, wd_hbm, y_ref,
                   wgf, wuf, wdf, wgb, wub, wdb, wsem):
    i = pl.program_id(0)
    live = nv_ref[i] > 0
    first = jnp.logical_or(i == 0, be_ref[i] != be_ref[jnp.maximum(i - 1, 0)])
    slot = grp_ref[i] % 2

    def fetch(e, s):
        return (pltpu.make_async_copy(wg_hbm.at[layer, e], wgf.at[s], wsem.at[s]),
                pltpu.make_async_copy(wu_hbm.at[layer, e], wuf.at[s], wsem.at[s]),
                pltpu.make_async_copy(wd_hbm.at[layer, e], wdf.at[s], wsem.at[s]))

    @pl.when(i == 0)
    def _():
        for cp in fetch(be_ref[0], 0):
            cp.start()

    @pl.when(jnp.logical_and(live, first))
    def _():
        for cp in fetch(be_ref[i], slot):
            cp.wait()

        @pl.when(nxt_ref[i] >= 0)
        def _():
            for cp in fetch(nxt_ref[i], 1 - slot):
                cp.start()

        wgb[...] = wgf[slot].astype(BF16)
        wub[...] = wuf[slot].astype(BF16)
        wdb[...] = wdf[slot].astype(BF16)

    @pl.when(live)
    def _():
        xb = _load_token_rows(x_ref, wgb.shape[0]).astype(BF16)
        hg = jnp.dot(xb, wgb[...], preferred_element_type=F32)
        hu = jnp.dot(xb, wub[...], preferred_element_type=F32)
        act = (_silu(hg) * hu).astype(BF16)
        _store_token_rows(y_ref, jnp.dot(act, wdb[...], preferred_element_type=F32))

    @pl.when(jnp.logical_not(live))
    def _():
        y_ref[...] = jnp.zeros_like(y_ref)


def _experts(layer, block_e, block_nv, block_src, block_grp, block_nxt, xs, wg, wu, wd):
    n_blocks = block_e.shape[0]
    d, dff = wg.shape[2:]
    rb = MOE_ROW_BLOCK
    blk = (rb * (d // LANES), LANES)
    grid_spec = pltpu.PrefetchScalarGridSpec(
        num_scalar_prefetch=5,
        grid=(n_blocks,),
        in_specs=[pl.BlockSpec(blk, lambda i, be, nv, src, grp, nxt: (src[i], 0)),
                  pl.BlockSpec(memory_space=pl.ANY),
                  pl.BlockSpec(memory_space=pl.ANY),
                  pl.BlockSpec(memory_space=pl.ANY)],
        out_specs=pl.BlockSpec(blk, lambda i, be, nv, src, grp, nxt: (i, 0)),
        scratch_shapes=[pltpu.VMEM((2, d, dff), F32),
                        pltpu.VMEM((2, d, dff), F32),
                        pltpu.VMEM((2, dff, d), F32),
                        pltpu.VMEM((d, dff), BF16),
                        pltpu.VMEM((d, dff), BF16),
                        pltpu.VMEM((dff, d), BF16),
                        pltpu.SemaphoreType.DMA((2,))],
    )
    return pl.pallas_call(
        functools.partial(_expert_kernel, layer),
        grid_spec=grid_spec,
        out_shape=jax.ShapeDtypeStruct(xs.shape, F32),
        compiler_params=_cparams(("arbitrary",)),
        name="moe_experts",
    )(block_e, block_nv, block_src, block_grp, block_nxt, xs, wg, wu, wd)


def _combine_kernel(dest_hbm, x_ref, r_ref, mod_ref, ys_hbm, o_ref, dest_s, gbuf, dsem, gsem):
    i = pl.program_id(0)
    n = pl.num_programs(0)
    tn = x_ref.shape[1] // LANES

    def dest_copy(j):
        return pltpu.make_async_copy(dest_hbm.at[j], dest_s.at[pl.ds((j % 3) * DEST_ROW, DEST_ROW)], dsem.at[j % 3])

    def gather_tile(j):
        def body(r, c):
            for k in range(2):
                row = dest_s[(j % 3) * DEST_ROW + k * TM + r]
                pltpu.make_async_copy(_token(ys_hbm, row, tn), _token(gbuf.at[j % 2, k], r, tn),
                                      gsem.at[j % 2]).start(priority=k)
            return c
        lax.fori_loop(0, TM, body, 0, unroll=8)

    @pl.when(i == 0)
    def _():
        dest_copy(0).start()
        dest_copy(0).wait()
        gather_tile(0)

        @pl.when(n > 1)
        def _():
            dest_copy(1).start()

    @pl.when(i + 1 < n)
    def _():
        dest_copy(i + 1).wait()

        @pl.when(i + 2 < n)
        def _():
            dest_copy(i + 2).start()

        gather_tile(i + 1)

    slot = i % 2
    for k in range(2):
        pltpu.make_async_copy(ys_hbm.at[pl.ds(0, TM * tn)], gbuf.at[slot, k], gsem.at[slot]).wait()
    w1 = r_ref[:, R_W:R_W + 1]
    w2 = r_ref[:, R_W + 1:R_W + 2]
    d = x_ref.shape[1]
    y = _load_token_rows(gbuf.at[slot, 0], d) * w1 + _load_token_rows(gbuf.at[slot, 1], d) * w2
    o_ref[...] = x_ref[...] + mod_ref[5:6, :] * y


def _combine(lay, n_tiles, dest, x, rec, mod_l, ys):
    d = x.shape[1]
    return pl.pallas_call(
        _combine_kernel,
        grid=(n_tiles,),
        in_specs=[pl.BlockSpec(memory_space=pl.ANY),
                  pl.BlockSpec((TM, d), lambda i: (i, 0)),
                  pl.BlockSpec((TM, LANES), lambda i: (i, 0)),
                  pl.BlockSpec((None, N_MOD, d), lambda i: (lay.mod_row(i), 0, 0)),
                  pl.BlockSpec(memory_space=pl.ANY)],
        out_specs=pl.BlockSpec((TM, d), lambda i: (i, 0)),
        out_shape=jax.ShapeDtypeStruct((n_tiles * TM, d), F32),
        scratch_shapes=[pltpu.SMEM((3 * DEST_ROW,), I32),
                        pltpu.VMEM((2, 2, TM * (d // LANES), LANES), F32),
                        pltpu.SemaphoreType.DMA((3,)),
                        pltpu.SemaphoreType.DMA((2,))],
        compiler_params=_cparams(("arbitrary",)),
        name="moe_combine",
    )(dest, x, rec, mod_l, ys)


def _router_weights(d, w_group, b_group, w_router, b_router):
    pad = LANES - MOE_EXPERTS - MOE_GROUPS
    wr = jnp.concatenate([w_router, w_group, jnp.zeros((d, pad), F32)], axis=1)
    br = jnp.concatenate([b_router, b_group, jnp.zeros((pad,), F32)])[None, :]
    wr_hi = wr.astype(BF16)
    wr_lo = (wr - wr_hi.astype(F32)).astype(BF16)
    return jnp.concatenate([wr_hi, wr_lo], axis=1), br


def _moe(lay, layer, n_tiles, x, mod_l, g2, rec, rec_t, cnt, wg, wu, wd):
    d = x.shape[1]
    t_tok = n_tiles * TM
    rec_t = rec_t[:, :t_tok]

    rb = MOE_ROW_BLOCK
    eid = rec_t[R_EID:R_EID + 2].astype(I32)
    rank = rec_t[R_RANK:R_RANK + 2].astype(I32)
    counts = cnt[0, :MOE_EXPERTS].astype(I32)
    padded = (counts + rb - 1) // rb * rb
    pad_ends = jnp.cumsum(padded)
    pad_starts = pad_ends - padded
    experts = jnp.arange(MOE_EXPERTS, dtype=I32)[:, None, None]
    dest = jnp.sum(jnp.where(eid[None] == experts, pad_starts[:, None, None], 0), axis=0) + rank
    dest = dest.reshape(2, n_tiles, TM).transpose(1, 0, 2).reshape(n_tiles, 2 * TM)
    dest = jnp.pad(dest, ((0, 0), (0, DEST_ROW - 2 * TM)))
    n_blocks = -(-(2 * t_tok) // rb) + MOE_EXPERTS
    used = pad_ends[-1] // rb
    idx = jnp.arange(n_blocks, dtype=I32)
    block_src = jnp.clip(idx, 0, jnp.maximum(used - 1, 0))
    start = block_src * rb
    block_e = jnp.sum(pad_ends[None, :] <= start[:, None], axis=1).astype(I32)
    block_nv = jnp.where(idx < used, jnp.clip(counts[block_e] - (start - pad_starts[block_e]), 0, rb), 0).astype(I32)

    cap = n_blocks * rb
    fill = jnp.concatenate([pad_starts + counts, pad_ends[-1:]]).astype(I32)
    plen = jnp.concatenate([padded - counts, cap - pad_ends[-1:]]).astype(I32)
    xs = _dispatch(lay, fill, plen, dest, x, mod_l, g2, cap)
    nonempty = counts > 0
    e_ids = jnp.arange(MOE_EXPERTS, dtype=I32)
    later = jnp.where((e_ids[None, :] > e_ids[:, None]) & nonempty[None, :], e_ids[None, :], MOE_EXPERTS)
    nxt_of_e = jnp.min(later, axis=1)
    nxt_of_e = jnp.where(nxt_of_e == MOE_EXPERTS, -1, nxt_of_e).astype(I32)
    grp_of_e = (jnp.cumsum(nonempty) - nonempty).astype(I32)
    ys = _experts(layer, block_e, block_nv, block_src, grp_of_e[block_e], nxt_of_e[block_e], xs, wg, wu, wd)
    return _combine(lay, n_tiles, dest, x, rec, mod_l, ys)


def _rope_tables(lay):
    half = ATT_HEAD_DIM // 2
    pos = jnp.arange(lay.seq)
    rowp = (pos // GRID_W).astype(F32)
    colp = (pos % GRID_W).astype(F32)
    freqs = ROPE_THETA ** (-jnp.arange(0, half, 2, dtype=F32) / half)
    ang = jnp.concatenate([rowp[:, None] * freqs, colp[:, None] * freqs], axis=-1)
    cos = jnp.concatenate([jnp.cos(ang), jnp.ones((lay.n_ctx, half), F32)], axis=0)
    sin = jnp.concatenate([jnp.sin(ang), jnp.zeros((lay.n_ctx, half), F32)], axis=0)
    cos_d = jnp.repeat(cos, 2, axis=1)
    sin_d = jnp.repeat(sin, 2, axis=1) * jnp.tile(jnp.asarray([-1.0, 1.0], F32), half)
    reps = LANES // ATT_HEAD_DIM
    return jnp.tile(cos_d, (1, reps)), jnp.tile(sin_d, (1, reps)), cos_d.T, sin_d.T


def kernel(x, c, ctx, c_ctx, mod_w, mod_b, norm1_g, norm2_g, ssd_w_in, ssd_conv_w, ssd_conv_b, ssd_dt_bias,
           ssd_a_log, ssd_d, ssd_norm_g, ssd_w_out, att_w_qkv, att_q_gain, att_k_gain, att_w_o, moe_w_group,
           moe_b_group, moe_w_router, moe_b_router, moe_w_gate, moe_w_up, moe_w_down):
    b, n_lat, d = x.shape
    n_ctx = ctx.shape[1]
    depth = mod_w.shape[0]
    lay = _Layout(b, n_lat, n_ctx)
    assert b + 1 <= 8

    xs = jnp.concatenate([x.reshape(lay.t_lat, d), ctx.reshape(lay.t_ctx, d)], axis=0)
    c8 = jnp.concatenate([c, c_ctx[None, :], jnp.zeros((8 - b - 1, d), F32)], axis=0)
    mods = _mod_table(c8, mod_w, mod_b).reshape(depth, 8, N_MOD, d)

    nh = ssd_dt_bias.shape[2]
    di = nh * SSD_HEAD_DIM
    gn = SSD_N_GROUPS * SSD_D_STATE
    dc = di + 2 * gn
    expand = jnp.tile(jnp.repeat(jnp.eye(nh, dtype=BF16), SSD_HEAD_DIM, axis=1), (3, 1))

    hq = att_w_qkv.shape[2] // ATT_HEAD_DIM - 2 * ATT_KV_HEADS
    dq = hq * ATT_HEAD_DIM
    dkv = ATT_KV_HEADS * ATT_HEAD_DIM
    gsum = np.zeros((dq, LANES), np.float32)
    gsum[np.arange(dq), np.arange(dq) // ATT_HEAD_DIM] = 1.0
    gexp = np.concatenate([gsum.T, gsum.T], axis=0)
    vone = np.zeros((1, ATT_KV_HEADS * LANES), np.float32)
    vone[0, np.arange(ATT_KV_HEADS) * LANES + ATT_HEAD_DIM] = 1.0
    cos128, sin128, cos_t, sin_t = _rope_tables(lay)

    for layer in range(depth):
        last = layer == depth - 1
        mod_l = mods[layer]
        j = layer // 2
        g1 = norm1_g[layer][None, :]
        g2 = norm2_g[layer][None, :]
        n_tiles = lay.lat_tiles if last else lay.n_tiles
        wr, br = _router_weights(d, moe_w_group[layer], moe_b_group[layer], moe_w_router[layer], moe_b_router[layer])
        if layer % 2 == 0:
            w_in = ssd_w_in[j].astype(BF16)
            wz, wx, wdt = w_in[:, :di], w_in[:, di:di + dc], w_in[:, di + dc:]
            dtb = ssd_dt_bias[j].reshape(1, 2 * nh)
            z, xbc, dt, dtt = _ssd_inproj(lay, xs, mod_l, g1, wz, wx, wdt, wdt.T, dtb, dtb.T)
            xc, bm, cm = _ssd_conv(lay, xbc, ssd_conv_w[j], ssd_conv_b[j][None, :], di, gn)
            yf, yb = _ssd_scan(lay, xc, bm, cm, dt, dtt, ssd_a_log[j], expand)
            dexp = jnp.repeat(ssd_d[j], SSD_HEAD_DIM)[None, :]
            xs, rec, rec_t, cnt = _ssd_outproj(lay, n_tiles, yf, yb, xc, z, dexp, ssd_norm_g[j][None, :],
                                               ssd_w_out[j].astype(BF16), xs, mod_l, g2, wr, br)
        else:
            w = att_w_qkv[j]
            wq = w[:, :dq].astype(BF16)
            wkt = w[:, dq:dq + dkv].T.astype(BF16)
            wv = w[:, dq + dkv:].reshape(d, ATT_KV_HEADS, ATT_HEAD_DIM)
            wv2 = jnp.pad(wv, ((0, 0), (0, 0), (0, LANES - ATT_HEAD_DIM))).reshape(d, ATT_KV_HEADS * LANES).astype(BF16)
            qg = (jnp.tile(att_q_gain[j], hq) * (ATT_HEAD_DIM ** -0.5 * LOG2E))[None, :]
            kgt = att_k_gain[j][:, None]
            bound = (1.01 * ATT_HEAD_DIM * jnp.max(jnp.abs(qg)) * jnp.max(jnp.abs(kgt))).reshape(1, 1)
            qn, kt, v = _att_inproj(lay, xs, mod_l, g1, wq, wkt, wv2, jnp.asarray(vone), qg, kgt, bound, jnp.asarray(gsum, BF16),
                                    jnp.asarray(gexp, BF16), cos128, sin128, cos_t, sin_t)
            o = _att_core(lay, qn, kt, v)
            xs, rec, rec_t, cnt = _att_outproj(lay, n_tiles, o, att_w_o[j].astype(BF16), xs, mod_l, g2, wr, br)
        xs = _moe(lay, layer, n_tiles, xs, mod_l, g2, rec, rec_t, cnt, moe_w_gate, moe_w_up, moe_w_down)
    return xs[:lay.t_lat].reshape(b, n_lat, d)
```

```python
import functools

import numpy as np
import jax
import jax.numpy as jnp
from jax import lax
from jax.experimental import pallas as pl
from jax.experimental.pallas import tpu as pltpu

F32 = jnp.float32
BF16 = jnp.bfloat16
I32 = jnp.int32
HIGHEST = lax.Precision.HIGHEST

NORM_EPS = 1e-6
N_MOD = 6
GRID_W = 64
ROPE_THETA = 10000.0

SSD_HEAD_DIM = 64
SSD_N_GROUPS = 4
SSD_D_STATE = 128
SSD_CONV = 5
SSD_CHUNK = 128

ATT_HEAD_DIM = 64
ATT_KV_HEADS = 4
ATT_KEY_BLOCK = 4352
LOG2E = 1.4426950408889634
ATT_MIN_DENOM = 2.0 ** -60

MOE_GROUPS = 4
MOE_EPG = 8
MOE_EXPERTS = MOE_GROUPS * MOE_EPG
MOE_ROW_BLOCK = 256
ZROWS = MOE_ROW_BLOCK // 2
DEST_ROW = 1024

TM = 256
LANES = 128
SUBLANES = 8
HALO = 16
VMEM_LIMIT = 56 * 1024 * 1024


def _cparams(sem):
    return pltpu.CompilerParams(dimension_semantics=sem, vmem_limit_bytes=VMEM_LIMIT)


def _silu(v):
    return v / (1.0 + jnp.exp(-v))


def _softplus(v):
    return jnp.maximum(v, 0.0) + jnp.log1p(jnp.exp(-jnp.abs(v)))


def _norm_mod(x, g, shift, scale):
    ms = jnp.mean(x * x, axis=-1, keepdims=True)
    y = x * lax.rsqrt(ms + NORM_EPS) * g
    return y * (1.0 + scale) + shift


class _Layout:
    def __init__(self, batch, seq, n_ctx):
        assert seq % TM == 0 and n_ctx % TM == 0
        self.batch, self.seq, self.n_ctx = batch, seq, n_ctx
        self.t_lat = batch * seq
        self.t_ctx = batch * n_ctx
        self.t = self.t_lat + self.t_ctx
        self.lat_tiles = self.t_lat // TM
        self.seq_tiles = seq // TM
        self.ctx_tiles = n_ctx // TM
        self.n_tiles = self.t // TM
        self.nk = seq + n_ctx

    def mod_row(self, i):
        return jnp.where(i < self.lat_tiles, i // self.seq_tiles, self.batch)

    def tile_batch(self, i):
        return jnp.where(i < self.lat_tiles, i // self.seq_tiles, (i - self.lat_tiles) // self.ctx_tiles)

    def tile_pos(self, i):
        return jnp.where(i < self.lat_tiles, i % self.seq_tiles,
                         self.seq_tiles + (i - self.lat_tiles) % self.ctx_tiles)

    def seg_first(self, i):
        return jnp.where(i < self.lat_tiles, i % self.seq_tiles == 0, (i - self.lat_tiles) % self.ctx_tiles == 0)

    def seg_last(self, i):
        return jnp.where(i < self.lat_tiles, i % self.seq_tiles == self.seq_tiles - 1,
                         (i - self.lat_tiles) % self.ctx_tiles == self.ctx_tiles - 1)


def _mod_kernel(c_ref, w_ref, b_ref, o_ref):
    s = _silu(c_ref[...])
    o_ref[...] = jnp.dot(s, w_ref[...], precision=HIGHEST, preferred_element_type=F32) + b_ref[...]


def _mod_table(c8, mod_w, mod_b):
    depth, d, n = mod_w.shape
    tn = 1536
    assert n % tn == 0
    return pl.pallas_call(
        _mod_kernel,
        grid=(depth, n // tn),
        in_specs=[pl.BlockSpec((8, d), lambda l, j: (0, 0)),
                  pl.BlockSpec((None, d, tn), lambda l, j: (l, 0, j)),
                  pl.BlockSpec((None, 1, tn), lambda l, j: (l, 0, j))],
        out_specs=pl.BlockSpec((None, 8, tn), lambda l, j: (l, 0, j)),
        out_shape=jax.ShapeDtypeStruct((depth, 8, n), F32),
        compiler_params=_cparams(("parallel", "parallel")),
        name="mod_table",
    )(c8, mod_w, mod_b.reshape(depth, 1, n))


def _ssd_inproj_kernel(x_ref, mod_ref, g_ref, wz_ref, wx_ref, wdt_ref, wdtt_ref, dtb_ref, dtbt_ref,
                       z_ref, xbc_ref, dt_ref, dtt_ref):
    h = _norm_mod(x_ref[...], g_ref[...], mod_ref[0:1, :], mod_ref[1:2, :]).astype(BF16)
    z_ref[...] = jnp.dot(h, wz_ref[...], preferred_element_type=F32).astype(BF16)
    xbc_ref[...] = jnp.dot(h, wx_ref[...], preferred_element_type=F32).astype(BF16)
    nh = dtb_ref.shape[1] // 2
    dt = _softplus(jnp.dot(h, wdt_ref[...], preferred_element_type=F32) + dtb_ref[...])
    dt_ref[0] = dt[:, :nh]
    dt_ref[1] = dt[:, nh:]
    dtt = lax.dot_general(wdtt_ref[...], h, (((1,), (1,)), ((), ())), preferred_element_type=F32)
    dtt = _softplus(dtt + dtbt_ref[...])
    dtt_ref[0] = dtt[:nh, :]
    dtt_ref[1] = dtt[nh:, :]


def _ssd_inproj(lay, x, mod_l, g, wz, wx, wdt, wdtt, dtb, dtbt):
    t, d = x.shape
    di, dc, nh2 = wz.shape[1], wx.shape[1], wdt.shape[1]
    nh = nh2 // 2
    full = lambda a: pl.BlockSpec(a.shape, lambda i: (0,) * a.ndim)
    return pl.pallas_call(
        _ssd_inproj_kernel,
        grid=(lay.n_tiles,),
        in_specs=[pl.BlockSpec((TM, d), lambda i: (i, 0)),
                  pl.BlockSpec((None, N_MOD, d), lambda i: (lay.mod_row(i), 0, 0)),
                  full(g), full(wz), full(wx), full(wdt), full(wdtt), full(dtb), full(dtbt)],
        out_specs=[pl.BlockSpec((TM, di), lambda i: (i, 0)),
                   pl.BlockSpec((TM, dc), lambda i: (i, 0)),
                   pl.BlockSpec((2, TM, nh), lambda i: (0, i, 0)),
                   pl.BlockSpec((2, nh, TM), lambda i: (0, 0, i))],
        out_shape=[jax.ShapeDtypeStruct((t, di), BF16),
                   jax.ShapeDtypeStruct((t, dc), BF16),
                   jax.ShapeDtypeStruct((2, t, nh), F32),
                   jax.ShapeDtypeStruct((2, nh, t), F32)],
        compiler_params=_cparams(("parallel",)),
        name="ssd_inproj",
    )(x, mod_l, g, wz, wx, wdt, wdtt, dtb, dtbt)


def _ssd_conv_kernel(lay, di, gn, prev_ref, main_ref, next_ref, w_ref, b_ref, sh_ref, sht_ref, shb_ref,
                     xs_ref, bm_ref, cm_ref):
    i = pl.program_id(0)
    dc = main_ref.shape[1]
    half = SSD_CONV // 2
    zero_halo = jnp.zeros((HALO, dc), BF16)
    prev = jnp.where(lay.seg_first(i), zero_halo, prev_ref[...])
    nxt = jnp.where(lay.seg_last(i), zero_halo, next_ref[...])
    cw = 256
    for c0 in range(0, dc, cw):
        u = main_ref[:, c0:c0 + cw]
        acc = b_ref[:, c0:c0 + cw] + u.astype(F32) * w_ref[half:half + 1, c0:c0 + cw]
        top = jnp.zeros((SUBLANES, cw), F32)
        bot = jnp.zeros((SUBLANES, cw), F32)
        for k in range(SSD_CONV):
            if k == half:
                continue
            wk = w_ref[k:k + 1, c0:c0 + cw]
            acc = acc + jnp.dot(sh_ref[k], u, preferred_element_type=F32) * wk
            if k < half:
                top = top + jnp.dot(sht_ref[k], prev[:, c0:c0 + cw], preferred_element_type=F32)[:SUBLANES] * wk
            else:
                bot = bot + jnp.dot(shb_ref[k], nxt[:, c0:c0 + cw], preferred_element_type=F32)[:SUBLANES] * wk
        acc = jnp.concatenate([acc[:SUBLANES] + top, acc[SUBLANES:TM - SUBLANES], acc[TM - SUBLANES:] + bot], axis=0)
        y = _silu(acc).astype(BF16)
        if c0 < di:
            xs_ref[:, c0:c0 + cw] = y
        elif c0 < di + gn:
            bm_ref[:, c0 - di:c0 - di + cw] = y
        else:
            cm_ref[:, c0 - di - gn:c0 - di - gn + cw] = y


def _conv_shift_matrices():
    half = SSD_CONV // 2
    sh = np.zeros((SSD_CONV, TM, TM), np.float32)
    sht = np.zeros((SSD_CONV, HALO, HALO), np.float32)
    shb = np.zeros((SSD_CONV, HALO, HALO), np.float32)
    for k in range(SSD_CONV):
        s = k - half
        sh[k] = np.eye(TM, k=s)
        for r in range(SUBLANES):
            if r + s < 0:
                sht[k, r, HALO + r + s] = 1.0
            if r - SUBLANES + s >= 0:
                shb[k, r, r - SUBLANES + s] = 1.0
    return jnp.asarray(sh, BF16), jnp.asarray(sht, BF16), jnp.asarray(shb, BF16)


def _ssd_conv(lay, xbc, conv_w, conv_b, di, gn):
    t, dc = xbc.shape
    sh, sht, shb = _conv_shift_matrices()
    hb = TM // HALO
    nhb = t // HALO
    assert gn == 512 and di % 512 == 0
    return pl.pallas_call(
        functools.partial(_ssd_conv_kernel, lay, di, gn),
        grid=(lay.n_tiles,),
        in_specs=[pl.BlockSpec((HALO, dc), lambda i: (jnp.maximum(i * hb - 1, 0), 0)),
                  pl.BlockSpec((TM, dc), lambda i: (i, 0)),
                  pl.BlockSpec((HALO, dc), lambda i: (jnp.minimum((i + 1) * hb, nhb - 1), 0)),
                  pl.BlockSpec(conv_w.shape, lambda i: (0, 0)),
                  pl.BlockSpec(conv_b.shape, lambda i: (0, 0)),
                  pl.BlockSpec(sh.shape, lambda i: (0, 0, 0)),
                  pl.BlockSpec(sht.shape, lambda i: (0, 0, 0)),
                  pl.BlockSpec(shb.shape, lambda i: (0, 0, 0))],
        out_specs=[pl.BlockSpec((TM, di), lambda i: (i, 0)),
                   pl.BlockSpec((TM, gn), lambda i: (i, 0)),
                   pl.BlockSpec((TM, gn), lambda i: (i, 0))],
        out_shape=[jax.ShapeDtypeStruct((t, di), BF16),
                   jax.ShapeDtypeStruct((t, gn), BF16),
                   jax.ShapeDtypeStruct((t, gn), BF16)],
        compiler_params=_cparams(("parallel",)),
        name="ssd_conv",
    )(xbc, xbc, xbc, conv_w, conv_b, sh, sht, shb)


def _ssd_scan_kernel(xf_ref, bf_ref, cf_ref, dtf_ref, dttf_ref, xb_ref, bb_ref, cb_ref, dtb_ref, dttb_ref,
                     alr_ref, alc_ref, e_ref, yf_ref, yb_ref, state_ref):
    step = pl.program_id(1)

    @pl.when(step == 0)
    def _():
        state_ref[...] = jnp.zeros_like(state_ref)

    _scan_chunk(0, step, xf_ref, bf_ref, cf_ref, dtf_ref, dttf_ref, alr_ref.at[0], alc_ref.at[0], e_ref,
                yf_ref, state_ref.at[0])
    _scan_chunk(1, step, xb_ref, bb_ref, cb_ref, dtb_ref, dttb_ref, alr_ref.at[1], alc_ref.at[1], e_ref,
                yb_ref, state_ref.at[1])


def _scan_chunk(d, step, xs_ref, bm_ref, cm_ref, dt_ref, dtt_ref, alr_ref, alc_ref, e_ref, y_ref, state_ref):
    q = SSD_CHUNK
    ng = SSD_N_GROUPS
    ns = SSD_D_STATE
    gw = xs_ref.shape[1] // ng
    hpg = gw // SSD_HEAD_DIM

    row = lax.broadcasted_iota(I32, (q, q), 0)
    col = lax.broadcasted_iota(I32, (q, q), 1)
    lmask = row >= col if d == 0 else row <= col
    lmask_t = col >= row if d == 0 else col <= row

    dt = dt_ref[...]
    dtt = dtt_ref[...]
    da = dt * (-jnp.exp(alr_ref[...]))
    dat = dtt * (-jnp.exp(alc_ref[...]))
    a_cum = jnp.dot(lmask.astype(F32), da, precision=HIGHEST, preferred_element_type=F32)
    a_cum_t = jnp.dot(dat, lmask_t.astype(F32), precision=HIGHEST, preferred_element_type=F32)
    a_end = jnp.sum(da, axis=0, keepdims=True)
    w_end = jnp.exp(a_end - a_cum) * dt

    e3 = e_ref[...]

    def expand(v):
        hi = v.astype(BF16)
        r1 = v - hi.astype(F32)
        mid = r1.astype(BF16)
        lo = (r1 - mid.astype(F32)).astype(BF16)
        return jnp.dot(jnp.concatenate([hi, mid, lo], axis=1), e3, preferred_element_type=F32)

    a_exp = expand(a_cum)
    w_exp = expand(w_end)
    end_exp = expand(jnp.broadcast_to(a_end, (SUBLANES, a_end.shape[1])))[0:1, :]
    decay_in = jnp.exp(a_exp)
    chunk_decay = jnp.exp(end_exp)
    xw = (xs_ref[...].astype(F32) * w_exp).astype(BF16)

    lane = lax.broadcasted_iota(I32, (q, LANES), 1)
    for g in range(ng):
        bg = bm_ref[:, g * ns:(g + 1) * ns]
        cg = cm_ref[:, g * ns:(g + 1) * ns]
        cb = lax.dot_general(cg, bg, (((1,), (1,)), ((), ())), preferred_element_type=F32)
        cb = jnp.where(lmask, cb, 0.0)
        s_in = state_ref[g]
        y_off = jnp.dot(cg, s_in.astype(BF16), preferred_element_type=F32)
        new_s = lax.dot_general(bg, xw[:, g * gw:(g + 1) * gw], (((0,), (0,)), ((), ())),
                                preferred_element_type=F32)
        state_ref[g] = s_in * chunk_decay[:, g * gw:(g + 1) * gw] + new_s
        for pr in range(hpg // 2):
            mixes = []
            for hh in range(2):
                h = g * hpg + 2 * pr + hh
                seg = a_cum[:, h:h + 1] - a_cum_t[h:h + 1, :]
                lh = jnp.exp(jnp.minimum(seg, 0.0))
                mixes.append((cb * lh * dtt[h:h + 1, :]).astype(BF16))
            lhs = jnp.concatenate(mixes, axis=1)
            l0 = g * gw + pr * LANES
            xp = xs_ref[:, l0:l0 + LANES]
            zero = jnp.zeros_like(xp)
            rhs = jnp.concatenate([jnp.where(lane < SSD_HEAD_DIM, xp, zero),
                                   jnp.where(lane >= SSD_HEAD_DIM, xp, zero)], axis=0)
            y_pair = jnp.dot(lhs, rhs, preferred_element_type=F32)
            y_pair = y_pair + y_off[:, pr * LANES:(pr + 1) * LANES] * decay_in[:, l0:l0 + LANES]
            y_ref[:, l0:l0 + LANES] = y_pair.astype(BF16)


def _ssd_scan(lay, xs, bm, cm, dt, dtt, a_log, expand):
    t, di = xs.shape
    gn = bm.shape[1]
    nh = dt.shape[2]
    q = SSD_CHUNK
    nct, nlt = lay.n_ctx // q, lay.seq // q
    nc = nct + nlt
    ctx_base = lay.t_lat // q

    def blk(d, b, s):
        j_ctx = jnp.where(d == 0, s, nct - 1 - s)
        j_lat = jnp.where(d == 0, s - nct, nlt - 1 - (s - nct))
        return jnp.where(s < nct, ctx_base + b * nct + j_ctx, b * nlt + j_lat)

    def chunk_specs(d):
        return [pl.BlockSpec((q, di), lambda b, s: (blk(d, b, s), 0)),
                pl.BlockSpec((q, gn), lambda b, s: (blk(d, b, s), 0)),
                pl.BlockSpec((q, gn), lambda b, s: (blk(d, b, s), 0)),
                pl.BlockSpec((None, q, nh), lambda b, s: (d, blk(d, b, s), 0)),
                pl.BlockSpec((None, nh, q), lambda b, s: (d, 0, blk(d, b, s)))]

    return pl.pallas_call(
        _ssd_scan_kernel,
        grid=(lay.batch, nc),
        in_specs=chunk_specs(0) + chunk_specs(1) + [
            pl.BlockSpec((2, 1, nh), lambda b, s: (0, 0, 0)),
            pl.BlockSpec((2, nh, 1), lambda b, s: (0, 0, 0)),
            pl.BlockSpec(expand.shape, lambda b, s: (0, 0))],
        out_specs=[pl.BlockSpec((q, di), lambda b, s: (blk(0, b, s), 0)),
                   pl.BlockSpec((q, di), lambda b, s: (blk(1, b, s), 0))],
        out_shape=[jax.ShapeDtypeStruct((t, di), BF16), jax.ShapeDtypeStruct((t, di), BF16)],
        scratch_shapes=[pltpu.VMEM((2, SSD_N_GROUPS, SSD_D_STATE, di // SSD_N_GROUPS), F32)],
        compiler_params=_cparams(("arbitrary", "arbitrary")),
        name="ssd_scan",
    )(xs, bm, cm, dt, dtt, xs, bm, cm, dt, dtt, a_log.reshape(2, 1, nh), a_log.reshape(2, nh, 1), expand)


def _ssd_outproj_kernel(n_route, yf_ref, yb_ref, xs_ref, z_ref, dexp_ref, ng_ref, w_ref, x_ref, mod_ref,
                        g2_ref, wr_ref, br_ref, o_ref, r_ref, rt_ref, cnt_ref, base_ref):
    y = yf_ref[...].astype(F32) + yb_ref[...].astype(F32) + xs_ref[...].astype(F32) * dexp_ref[...]
    y = y * _silu(z_ref[...].astype(F32))
    gw = y.shape[1] // SSD_N_GROUPS
    parts = []
    for g in range(SSD_N_GROUPS):
        yg = y[:, g * gw:(g + 1) * gw]
        ms = jnp.mean(yg * yg, axis=-1, keepdims=True)
        parts.append((yg * lax.rsqrt(ms + NORM_EPS) * ng_ref[:, g * gw:(g + 1) * gw]).astype(BF16))
    yn = jnp.concatenate(parts, axis=1)
    m = jnp.dot(yn, w_ref[...], preferred_element_type=F32)
    x_new = x_ref[...] + mod_ref[2:3, :] * m
    o_ref[...] = x_new
    _route(pl.program_id(0), n_route, x_new, mod_ref, g2_ref, wr_ref, br_ref, r_ref, rt_ref, cnt_ref, base_ref)


def _ssd_outproj(lay, n_route, yf, yb, xs, z, dexp, ng, w, x, mod_l, g2, wr, br):
    t, d = x.shape
    di = xs.shape[1]
    full = lambda a: pl.BlockSpec(a.shape, lambda i: (0,) * a.ndim)
    r_in, r_out, r_shape, r_scratch = _route_specs(lay, g2, wr, br)
    return pl.pallas_call(
        functools.partial(_ssd_outproj_kernel, n_route),
        grid=(lay.n_tiles,),
        in_specs=[pl.BlockSpec((TM, di), lambda i: (i, 0)),
                  pl.BlockSpec((TM, di), lambda i: (i, 0)),
                  pl.BlockSpec((TM, di), lambda i: (i, 0)),
                  pl.BlockSpec((TM, di), lambda i: (i, 0)),
                  full(dexp), full(ng), full(w),
                  pl.BlockSpec((TM, d), lambda i: (i, 0)),
                  pl.BlockSpec((None, N_MOD, d), lambda i: (lay.mod_row(i), 0, 0))] + r_in,
        out_specs=[pl.BlockSpec((TM, d), lambda i: (i, 0))] + r_out,
        out_shape=[jax.ShapeDtypeStruct((t, d), F32)] + r_shape,
        scratch_shapes=r_scratch,
        compiler_params=_cparams(("arbitrary",)),
        name="ssd_outproj",
    )(yf, yb, xs, z, dexp, ng, w, x, mod_l, g2, wr, br)


def _att_inproj_kernel(x_ref, mod_ref, g_ref, wq_ref, wkt_ref, wv_ref, vone_ref, qg_ref, kgt_ref, bound_ref, gsum_ref, gexp_ref,
                       cos_ref, sin_ref, cost_ref, sint_ref, q_ref, kt_ref, v_ref):
    h = _norm_mod(x_ref[...], g_ref[...], mod_ref[0:1, :], mod_ref[1:2, :]).astype(BF16)
    hd = ATT_HEAD_DIM
    q = jnp.dot(h, wq_ref[...], preferred_element_type=F32)
    ssum = jnp.dot((q * q).astype(BF16), gsum_ref[...], preferred_element_type=F32)
    r = lax.rsqrt(ssum * (1.0 / hd) + NORM_EPS)
    r_hi = r.astype(BF16)
    r_lo = (r - r_hi.astype(F32)).astype(BF16)
    r_exp = jnp.dot(jnp.concatenate([r_hi, r_lo], axis=1), gexp_ref[...], preferred_element_type=F32)
    qn = q * r_exp * qg_ref[...]
    nl = qn.shape[1]
    lane = lax.broadcasted_iota(I32, qn.shape, 1)
    partner = jnp.where((lane & 1) == 0, pltpu.roll(qn, nl - 1, 1), pltpu.roll(qn, 1, 1))
    reps = nl // LANES
    cos = jnp.concatenate([cos_ref[...]] * reps, axis=1)
    sin = jnp.concatenate([sin_ref[...]] * reps, axis=1)
    q_ref[...] = (qn * cos + partner * sin).astype(BF16)
    kt = lax.dot_general(wkt_ref[...], h, (((1,), (1,)), ((), ())), preferred_element_type=F32)
    ct, st = cost_ref[...], sint_ref[...]
    sub = lax.broadcasted_iota(I32, ct.shape, 0)
    for kh in range(ATT_KV_HEADS):
        blk = kt[kh * hd:(kh + 1) * hd]
        rk = lax.rsqrt(jnp.sum(blk * blk, axis=0, keepdims=True) * (1.0 / hd) + NORM_EPS)
        kn = blk * rk * kgt_ref[...]
        kpart = jnp.where((sub & 1) == 0, pltpu.roll(kn, hd - 1, 0), pltpu.roll(kn, 1, 0))
        kt_ref[kh, 0:hd, :] = (kn * ct + kpart * st).astype(BF16)
        kt_ref[kh, hd:2 * hd, :] = jnp.where(sub == 0, -bound_ref[...], 0.0).astype(BF16)
    v = (jnp.dot(h, wv_ref[...], preferred_element_type=F32) + vone_ref[...]).astype(BF16)
    for kh in range(ATT_KV_HEADS):
        v_ref[kh] = v[:, kh * LANES:(kh + 1) * LANES]


def _att_inproj(lay, x, mod_l, g, wq, wkt, wv2, vone, qg, kgt, bound, gsum, gexp, cos, sin, cost, sint):
    t, d = x.shape
    dq = wq.shape[1]
    hd = ATT_HEAD_DIM
    full = lambda a: pl.BlockSpec(a.shape, lambda i: (0,) * a.ndim)
    return pl.pallas_call(
        _att_inproj_kernel,
        grid=(lay.n_tiles,),
        in_specs=[pl.BlockSpec((TM, d), lambda i: (i, 0)),
                  pl.BlockSpec((None, N_MOD, d), lambda i: (lay.mod_row(i), 0, 0)),
                  full(g), full(wq), full(wkt), full(wv2), full(vone), full(qg), full(kgt), full(bound), full(gsum), full(gexp),
                  pl.BlockSpec((TM, LANES), lambda i: (lay.tile_pos(i), 0)),
                  pl.BlockSpec((TM, LANES), lambda i: (lay.tile_pos(i), 0)),
                  pl.BlockSpec((hd, TM), lambda i: (0, lay.tile_pos(i))),
                  pl.BlockSpec((hd, TM), lambda i: (0, lay.tile_pos(i)))],
        out_specs=[pl.BlockSpec((TM, dq), lambda i: (i, 0)),
                   pl.BlockSpec((None, ATT_KV_HEADS, 2 * hd, TM), lambda i: (lay.tile_batch(i), 0, 0, lay.tile_pos(i))),
                   pl.BlockSpec((None, ATT_KV_HEADS, TM, LANES), lambda i: (lay.tile_batch(i), 0, lay.tile_pos(i), 0))],
        out_shape=[jax.ShapeDtypeStruct((t, dq), BF16),
                   jax.ShapeDtypeStruct((lay.batch, ATT_KV_HEADS, 2 * hd, lay.nk), BF16),
                   jax.ShapeDtypeStruct((lay.batch, ATT_KV_HEADS, lay.nk, LANES), BF16)],
        compiler_params=_cparams(("parallel",)),
        name="att_inproj",
    )(x, mod_l, g, wq, wkt, wv2, vone, qg, kgt, bound, gsum, gexp, cos, sin, cost, sint)


def _attend(q_ref, kt_ref, v_ref, o_ref, k0, n_keys):
    hd = ATT_HEAD_DIM
    tm = q_ref.shape[0]
    heads = q_ref.shape[1] // hd
    qs = [q_ref[:, u * hd:(u + 1) * hd] for u in range(heads)]

    bk = min(ATT_KEY_BLOCK, n_keys)
    assert n_keys % bk == 0 and k0 % LANES == 0 and bk % LANES == 0

    def body(j, carry):
        start = pl.multiple_of(k0 + j * bk, LANES)
        kb = kt_ref[0:hd, pl.ds(start, bk)]
        vb = v_ref[pl.ds(start, bk), :]
        out = []
        for u in range(heads):
            m_old, acc = carry[u]
            s = jnp.dot(qs[u], kb, preferred_element_type=F32)
            m_new = jnp.maximum(m_old, jnp.max(s, axis=-1, keepdims=True))
            p = jnp.exp2((s - m_new).astype(BF16))
            acc = jnp.exp2(m_old - m_new) * acc + jnp.dot(p, vb, preferred_element_type=F32)
            out.append((m_new, acc))
        return tuple(out)

    init = tuple((jnp.full((tm, 1), -jnp.inf, F32), jnp.zeros((tm, LANES), F32)) for _ in range(heads))
    final = lax.fori_loop(0, n_keys // bk, body, init) if n_keys > bk else body(0, init)
    outs = [acc[:, :hd] / acc[:, hd:hd + 1] for _, acc in final]
    o_ref[...] = jnp.concatenate(outs, axis=1).astype(BF16)


def _attend_bounded(q_ref, kt_ref, v_ref, o_ref):
    hd = ATT_HEAD_DIM
    tm = q_ref.shape[0]
    k_lo = kt_ref[...]
    k_hi = jnp.concatenate([k_lo[hd:], k_lo[:hd]], axis=0)
    vb = v_ref[...]
    lane = lax.broadcasted_iota(I32, (tm, LANES), 1)
    outs, dens = [], []
    for pr in range(q_ref.shape[1] // LANES):
        qp = q_ref[:, pr * LANES:(pr + 1) * LANES]
        q_lo = jnp.where(lane < hd, qp, (lane == hd).astype(BF16))
        q_hi = jnp.where(lane >= hd, qp, (lane == 0).astype(BF16))
        for qa, ka in ((q_lo, k_lo), (q_hi, k_hi)):
            s = jnp.dot(qa, ka, preferred_element_type=F32)
            acc = jnp.dot(jnp.exp2(s).astype(BF16), vb, preferred_element_type=F32)
            den = acc[:, hd:hd + 1]
            outs.append(acc[:, :hd] / den)
            dens.append(den)
    o_ref[...] = jnp.concatenate(outs, axis=1).astype(BF16)
    return jnp.min(jnp.concatenate(dens, axis=1))


def _att_core_kernel(lay, q_ref, kt_ref, v_ref, o_ref):
    qi = pl.program_id(2)

    @pl.when(qi < lay.seq_tiles)
    def _():
        smallest = _attend_bounded(q_ref, kt_ref, v_ref, o_ref)

        @pl.when(jnp.logical_not(smallest > ATT_MIN_DENOM))
        def _():
            _attend(q_ref, kt_ref, v_ref, o_ref, 0, lay.nk)

    @pl.when(qi >= lay.seq_tiles)
    def _():
        _attend(q_ref, kt_ref, v_ref, o_ref, lay.seq, lay.n_ctx)


def _att_core(lay, qn, kt, v):
    t, dq = qn.shape
    hd = ATT_HEAD_DIM
    qw = dq // ATT_KV_HEADS
    per_b = lay.seq_tiles + lay.ctx_tiles

    def row_tile(b, qi):
        return jnp.where(qi < lay.seq_tiles, b * lay.seq_tiles + qi,
                         lay.lat_tiles + b * lay.ctx_tiles + (qi - lay.seq_tiles))

    return pl.pallas_call(
        functools.partial(_att_core_kernel, lay),
        grid=(lay.batch, ATT_KV_HEADS, per_b),
        in_specs=[pl.BlockSpec((TM, qw), lambda b, kh, qi: (row_tile(b, qi), kh)),
                  pl.BlockSpec((None, None, 2 * hd, lay.nk), lambda b, kh, qi: (b, kh, 0, 0)),
                  pl.BlockSpec((None, None, lay.nk, LANES), lambda b, kh, qi: (b, kh, 0, 0))],
        out_specs=pl.BlockSpec((TM, qw), lambda b, kh, qi: (row_tile(b, qi), kh)),
        out_shape=jax.ShapeDtypeStruct((t, dq), BF16),
        compiler_params=_cparams(("parallel", "parallel", "arbitrary")),
        name="att_core",
    )(qn, kt, v)


def _att_outproj_kernel(n_route, o_ref, w_ref, x_ref, mod_ref, g2_ref, wr_ref, br_ref,
                        out_ref, r_ref, rt_ref, cnt_ref, base_ref):
    m = jnp.dot(o_ref[...], w_ref[...], preferred_element_type=F32)
    x_new = x_ref[...] + mod_ref[2:3, :] * m
    out_ref[...] = x_new
    _route(pl.program_id(0), n_route, x_new, mod_ref, g2_ref, wr_ref, br_ref, r_ref, rt_ref, cnt_ref, base_ref)


def _att_outproj(lay, n_route, o, w, x, mod_l, g2, wr, br):
    t, d = x.shape
    r_in, r_out, r_shape, r_scratch = _route_specs(lay, g2, wr, br)
    return pl.pallas_call(
        functools.partial(_att_outproj_kernel, n_route),
        grid=(lay.n_tiles,),
        in_specs=[pl.BlockSpec((TM, o.shape[1]), lambda i: (i, 0)),
                  pl.BlockSpec(w.shape, lambda i: (0, 0)),
                  pl.BlockSpec((TM, d), lambda i: (i, 0)),
                  pl.BlockSpec((None, N_MOD, d), lambda i: (lay.mod_row(i), 0, 0))] + r_in,
        out_specs=[pl.BlockSpec((TM, d), lambda i: (i, 0))] + r_out,
        out_shape=[jax.ShapeDtypeStruct((t, d), F32)] + r_shape,
        scratch_shapes=r_scratch,
        compiler_params=_cparams(("arbitrary",)),
        name="att_outproj",
    )(o, w, x, mod_l, g2, wr, br)


R_EID, R_RANK, R_W = 0, 2, 4
GROUP_LANE0 = MOE_EXPERTS


def _store_token_rows(ref, val):
    rows, d = val.shape
    for s in range(d // LANES):
        ref[pl.ds(s, rows, stride=d // LANES), :] = val[:, s * LANES:(s + 1) * LANES]


def _load_token_rows(ref, d):
    n = d // LANES
    rows = ref.shape[0] // n
    return jnp.concatenate([ref[pl.ds(s, rows, stride=n), :] for s in range(n)], axis=1)


def _token(ref, r, n):
    return ref.at[pl.ds(pl.multiple_of(r * n, n), n)]


def _route(i, n_route, x, mod_ref, g_ref, wr_ref, br_ref, r_ref, rt_ref, cnt_ref, base_ref):
    @pl.when(i == 0)
    def _():
        base_ref[...] = jnp.zeros_like(base_ref)

    f = _norm_mod(x, g_ref[...], mod_ref[3:4, :], mod_ref[4:5, :])
    f_hi = f.astype(BF16)
    f_lo = (f - f_hi.astype(F32)).astype(BF16)
    part = jnp.dot(f_hi, wr_ref[...], preferred_element_type=F32)
    part_lo = jnp.dot(f_lo, wr_ref[:, :LANES], preferred_element_type=F32)
    logits = part[:, :LANES] + part[:, LANES:] + part_lo + br_ref[...]
    lane = lax.broadcasted_iota(I32, logits.shape, 1)
    neg = jnp.float32(-jnp.inf)
    big = jnp.int32(LANES)

    def first_max(vals):
        top = jnp.max(vals, axis=-1, keepdims=True)
        idx = jnp.min(jnp.where(vals == top, lane, big), axis=-1, keepdims=True)
        return top, idx

    g_mask = (lane >= GROUP_LANE0) & (lane < GROUP_LANE0 + MOE_GROUPS)
    glog = jnp.where(g_mask, logits, neg)
    g_top, g_idx = first_max(glog)
    g_w = 1.0 / jnp.sum(jnp.exp(glog - g_top), axis=-1, keepdims=True)
    e0 = (g_idx - GROUP_LANE0) * MOE_EPG
    elog = jnp.where((lane >= e0) & (lane < e0 + MOE_EPG), logits, neg)
    v1, i1 = first_max(elog)
    v2, i2 = first_max(jnp.where(lane == i1, neg, elog))
    ex = jnp.exp(v2 - v1)
    w1 = g_w / (1.0 + ex)
    w2 = g_w * ex / (1.0 + ex)

    oh1 = (lane == i1).astype(F32)
    oh2 = (lane == i2).astype(F32)
    tm = logits.shape[0]
    rr = lax.broadcasted_iota(I32, (tm, tm), 0)
    cc = lax.broadcasted_iota(I32, (tm, tm), 1)
    before = (cc < rr).astype(BF16)
    cum1 = jnp.dot(before, oh1.astype(BF16), preferred_element_type=F32)
    cum2 = jnp.dot(before, oh2.astype(BF16), preferred_element_type=F32)
    tot1 = jnp.sum(oh1, axis=0, keepdims=True)
    tot2 = jnp.sum(oh2, axis=0, keepdims=True)
    base = base_ref[...]
    rank1 = jnp.sum(oh1 * (cum1 + base), axis=-1, keepdims=True)
    rank2 = jnp.sum(oh2 * (cum2 + base + tot1), axis=-1, keepdims=True)
    base = base + jnp.where(i < n_route, tot1 + tot2, 0.0)
    base_ref[...] = base
    cnt_ref[...] = base

    rec = jnp.zeros(logits.shape, F32)
    for k, val in ((R_EID, i1.astype(F32)), (R_EID + 1, i2.astype(F32)), (R_RANK, rank1), (R_RANK + 1, rank2),
                   (R_W, w1), (R_W + 1, w2)):
        rec = jnp.where(lane == k, val, rec)
    r_ref[...] = rec
    rt_ref[...] = rec.T[:SUBLANES, :]


def _route_specs(lay, g2, wr, br):
    t = lay.n_tiles * TM
    ins = [pl.BlockSpec(g2.shape, lambda i: (0, 0)),
           pl.BlockSpec(wr.shape, lambda i: (0, 0)),
           pl.BlockSpec(br.shape, lambda i: (0, 0))]
    outs = [pl.BlockSpec((TM, LANES), lambda i: (i, 0)),
            pl.BlockSpec((SUBLANES, TM), lambda i: (0, i)),
            pl.BlockSpec((1, LANES), lambda i: (0, 0))]
    shapes = [jax.ShapeDtypeStruct((t, LANES), F32),
              jax.ShapeDtypeStruct((SUBLANES, t), F32),
              jax.ShapeDtypeStruct((1, LANES), F32)]
    return ins, outs, shapes, [pltpu.VMEM((1, LANES), F32)]


def _dispatch_kernel(fill_ref, plen_ref, dest_hbm, x_ref, mod_ref, g_ref, xs_hbm, dest_s, fbuf, zbuf, dsem, fsem, ssem):
    i = pl.program_id(0)
    n = pl.num_programs(0)
    tn = x_ref.shape[1] // LANES

    def drain(s):
        for k in range(2):
            pltpu.make_async_copy(fbuf.at[s], xs_hbm.at[pl.ds(0, TM * tn)], ssem.at[s]).wait()

    def dest_copy(j):
        return pltpu.make_async_copy(dest_hbm.at[j], dest_s.at[pl.ds((j % 2) * DEST_ROW, DEST_ROW)], dsem.at[j % 2])

    def pad_pieces(e, fn):
        plen = plen_ref[e]
        for bit in range(ZROWS.bit_length() - 1, -1, -1):
            size = 1 << bit

            @pl.when((plen & size) != 0)
            def _():
                start = fill_ref[e] + ((plen >> (bit + 1)) << (bit + 1))
                fn(pltpu.make_async_copy(zbuf.at[pl.ds(0, size * tn)],
                                         xs_hbm.at[pl.ds(pl.multiple_of(start * tn, tn), size * tn)], fsem.at[0]))

    @pl.when(i == 0)
    def _():
        zbuf[...] = jnp.zeros_like(zbuf)
        dest_copy(0).start()

        def start_e(e, c):
            pad_pieces(e, lambda cp: cp.start())
            return c

        def wait_e(e, c):
            pad_pieces(e, lambda cp: cp.wait())
            return c

        def tail_pieces(fn):
            def piece(t, c):
                start = fill_ref[MOE_EXPERTS] + t * ZROWS
                fn(pltpu.make_async_copy(zbuf, xs_hbm.at[pl.ds(pl.multiple_of(start * tn, tn), ZROWS * tn)],
                                         fsem.at[0]))
                return c
            lax.fori_loop(0, plen_ref[MOE_EXPERTS] // ZROWS, piece, 0)

        lax.fori_loop(0, MOE_EXPERTS, start_e, 0)
        tail_pieces(lambda cp: cp.start())
        lax.fori_loop(0, MOE_EXPERTS, wait_e, 0)
        tail_pieces(lambda cp: cp.wait())

    @pl.when(i + 1 < n)
    def _():
        dest_copy(i + 1).start()

    dest_copy(i).wait()
    slot = i % 2

    @pl.when(i >= 2)
    def _():
        drain(slot)

    f = _norm_mod(x_ref[...], g_ref[...], mod_ref[3:4, :], mod_ref[4:5, :])
    _store_token_rows(fbuf.at[slot], f)

    def body(r, c):
        for k in range(2):
            row = dest_s[slot * DEST_ROW + k * TM + r]
            pltpu.make_async_copy(_token(fbuf.at[slot], r, tn), _token(xs_hbm, row, tn),
                                  ssem.at[slot]).start(priority=k)
        return c

    lax.fori_loop(0, TM, body, 0, unroll=8)

    @pl.when(i == n - 1)
    def _():
        drain(slot)

        @pl.when(n > 1)
        def _():
            drain(1 - slot)


def _dispatch(lay, fill, plen, dest, x, mod_l, g, cap):
    n_tiles = dest.shape[0]
    d = x.shape[1]
    tn = d // LANES
    grid_spec = pltpu.PrefetchScalarGridSpec(
        num_scalar_prefetch=2,
        grid=(n_tiles,),
        in_specs=[pl.BlockSpec(memory_space=pl.ANY),
                  pl.BlockSpec((TM, d), lambda i, fill, plen: (i, 0)),
                  pl.BlockSpec((None, N_MOD, d), lambda i, fill, plen: (lay.mod_row(i), 0, 0)),
                  pl.BlockSpec(g.shape, lambda i, fill, plen: (0, 0))],
        out_specs=pl.BlockSpec(memory_space=pl.ANY),
        scratch_shapes=[pltpu.SMEM((2 * DEST_ROW,), I32),
                        pltpu.VMEM((2, TM * tn, LANES), F32),
                        pltpu.VMEM((ZROWS * tn, LANES), F32),
                        pltpu.SemaphoreType.DMA((2,)),
                        pltpu.SemaphoreType.DMA((1,)),
                        pltpu.SemaphoreType.DMA((2,))],
    )
    return pl.pallas_call(
        _dispatch_kernel,
        grid_spec=grid_spec,
        out_shape=jax.ShapeDtypeStruct((cap * tn, LANES), F32),
        compiler_params=_cparams(("arbitrary",)),
        name="moe_dispatch",
    )(fill, plen, dest, x, mod_l, g)


def _expert_kernel(layer, be_ref, nv_ref, blk_ref, grp_ref, nxt_ref, x_ref, wg_hbm, wu_hbm, wd_hbm, y_ref,
                   wgf, wuf, wdf, wgb, wub, wdb, wsem):
    i = pl.program_id(0)
    live = nv_ref[i] > 0
    first = jnp.logical_or(i == 0, be_ref[i] != be_ref[jnp.maximum(i - 1, 0)])
    slot = grp_ref[i] % 2

    def fetch(e, s):
        return (pltpu.make_async_copy(wg_hbm.at[layer, e], wgf.at[s], wsem.at[s]),
                pltpu.make_async_copy(wu_hbm.at[layer, e], wuf.at[s], wsem.at[s]),
                pltpu.make_async_copy(wd_hbm.at[layer, e], wdf.at[s], wsem.at[s]))

    @pl.when(i == 0)
    def _():
        for cp in fetch(be_ref[0], 0):
            cp.start()

    @pl.when(jnp.logical_and(live, first))
    def _():
        for cp in fetch(be_ref[i], slot):
            cp.wait()

        @pl.when(nxt_ref[i] >= 0)
        def _():
            for cp in fetch(nxt_ref[i], 1 - slot):
                cp.start()

        wgb[...] = wgf[slot].astype(BF16)
        wub[...] = wuf[slot].astype(BF16)
        wdb[...] = wdf[slot].astype(BF16)

    @pl.when(live)
    def _():
        xb = _load_token_rows(x_ref, wgb.shape[0]).astype(BF16)
        hg = jnp.dot(xb, wgb[...], preferred_element_type=F32)
        hu = jnp.dot(xb, wub[...], preferred_element_type=F32)
        act = (_silu(hg) * hu).astype(BF16)
        _store_token_rows(y_ref, jnp.dot(act, wdb[...], preferred_element_type=F32))

    @pl.when(jnp.logical_not(live))
    def _():
        y_ref[...] = jnp.zeros_like(y_ref)


def _experts(layer, block_e, block_nv, block_src, block_grp, block_nxt, xs, wg, wu, wd):
    n_blocks = block_e.shape[0]
    d, dff = wg.shape[2:]
    rb = MOE_ROW_BLOCK
    blk = (rb * (d // LANES), LANES)
    grid_spec = pltpu.PrefetchScalarGridSpec(
        num_scalar_prefetch=5,
        grid=(n_blocks,),
        in_specs=[pl.BlockSpec(blk, lambda i, be, nv, src, grp, nxt: (src[i], 0)),
                  pl.BlockSpec(memory_space=pl.ANY),
                  pl.BlockSpec(memory_space=pl.ANY),
                  pl.BlockSpec(memory_space=pl.ANY)],
        out_specs=pl.BlockSpec(blk, lambda i, be, nv, src, grp, nxt: (i, 0)),
        scratch_shapes=[pltpu.VMEM((2, d, dff), F32),
                        pltpu.VMEM((2, d, dff), F32),
                        pltpu.VMEM((2, dff, d), F32),
                        pltpu.VMEM((d, dff), BF16),
                        pltpu.VMEM((d, dff), BF16),
                        pltpu.VMEM((dff, d), BF16),
                        pltpu.SemaphoreType.DMA((2,))],
    )
    return pl.pallas_call(
        functools.partial(_expert_kernel, layer),
        grid_spec=grid_spec,
        out_shape=jax.ShapeDtypeStruct(xs.shape, F32),
        compiler_params=_cparams(("arbitrary",)),
        name="moe_experts",
    )(block_e, block_nv, block_src, block_grp, block_nxt, xs, wg, wu, wd)


def _combine_kernel(dest_hbm, x_ref, r_ref, mod_ref, ys_hbm, o_ref, dest_s, gbuf, dsem, gsem):
    i = pl.program_id(0)
    n = pl.num_programs(0)
    tn = x_ref.shape[1] // LANES

    def dest_copy(j):
        return pltpu.make_async_copy(dest_hbm.at[j], dest_s.at[pl.ds((j % 3) * DEST_ROW, DEST_ROW)], dsem.at[j % 3])

    def gather_tile(j):
        def body(r, c):
            for k in range(2):
                row = dest_s[(j % 3) * DEST_ROW + k * TM + r]
                pltpu.make_async_copy(_token(ys_hbm, row, tn), _token(gbuf.at[j % 2, k], r, tn),
                                      gsem.at[j % 2]).start(priority=k)
            return c
        lax.fori_loop(0, TM, body, 0, unroll=8)

    @pl.when(i == 0)
    def _():
        dest_copy(0).start()
        dest_copy(0).wait()
        gather_tile(0)

        @pl.when(n > 1)
        def _():
            dest_copy(1).start()

    @pl.when(i + 1 < n)
    def _():
        dest_copy(i + 1).wait()

        @pl.when(i + 2 < n)
        def _():
            dest_copy(i + 2).start()

        gather_tile(i + 1)

    slot = i % 2
    for k in range(2):
        pltpu.make_async_copy(ys_hbm.at[pl.ds(0, TM * tn)], gbuf.at[slot, k], gsem.at[slot]).wait()
    w1 = r_ref[:, R_W:R_W + 1]
    w2 = r_ref[:, R_W + 1:R_W + 2]
    d = x_ref.shape[1]
    y = _load_token_rows(gbuf.at[slot, 0], d) * w1 + _load_token_rows(gbuf.at[slot, 1], d) * w2
    o_ref[...] = x_ref[...] + mod_ref[5:6, :] * y


def _combine(lay, n_tiles, dest, x, rec, mod_l, ys):
    d = x.shape[1]
    return pl.pallas_call(
        _combine_kernel,
        grid=(n_tiles,),
        in_specs=[pl.BlockSpec(memory_space=pl.ANY),
                  pl.BlockSpec((TM, d), lambda i: (i, 0)),
                  pl.BlockSpec((TM, LANES), lambda i: (i, 0)),
                  pl.BlockSpec((None, N_MOD, d), lambda i: (lay.mod_row(i), 0, 0)),
                  pl.BlockSpec(memory_space=pl.ANY)],
        out_specs=pl.BlockSpec((TM, d), lambda i: (i, 0)),
        out_shape=jax.ShapeDtypeStruct((n_tiles * TM, d), F32),
        scratch_shapes=[pltpu.SMEM((3 * DEST_ROW,), I32),
                        pltpu.VMEM((2, 2, TM * (d // LANES), LANES), F32),
                        pltpu.SemaphoreType.DMA((3,)),
                        pltpu.SemaphoreType.DMA((2,))],
        compiler_params=_cparams(("arbitrary",)),
        name="moe_combine",
    )(dest, x, rec, mod_l, ys)


def _router_weights(d, w_group, b_group, w_router, b_router):
    pad = LANES - MOE_EXPERTS - MOE_GROUPS
    wr = jnp.concatenate([w_router, w_group, jnp.zeros((d, pad), F32)], axis=1)
    br = jnp.concatenate([b_router, b_group, jnp.zeros((pad,), F32)])[None, :]
    wr_hi = wr.astype(BF16)
    wr_lo = (wr - wr_hi.astype(F32)).astype(BF16)
    return jnp.concatenate([wr_hi, wr_lo], axis=1), br


def _moe(lay, layer, n_tiles, x, mod_l, g2, rec, rec_t, cnt, wg, wu, wd):
    d = x.shape[1]
    t_tok = n_tiles * TM
    rec_t = rec_t[:, :t_tok]

    rb = MOE_ROW_BLOCK
    eid = rec_t[R_EID:R_EID + 2].astype(I32)
    rank = rec_t[R_RANK:R_RANK + 2].astype(I32)
    counts = cnt[0, :MOE_EXPERTS].astype(I32)
    padded = (counts + rb - 1) // rb * rb
    pad_ends = jnp.cumsum(padded)
    pad_starts = pad_ends - padded
    experts = jnp.arange(MOE_EXPERTS, dtype=I32)[:, None, None]
    dest = jnp.sum(jnp.where(eid[None] == experts, pad_starts[:, None, None], 0), axis=0) + rank
    dest = dest.reshape(2, n_tiles, TM).transpose(1, 0, 2).reshape(n_tiles, 2 * TM)
    dest = jnp.pad(dest, ((0, 0), (0, DEST_ROW - 2 * TM)))
    n_blocks = -(-(2 * t_tok) // rb) + MOE_EXPERTS
    cap = n_blocks * rb
    used = pad_ends[-1] // rb
    idx = jnp.arange(n_blocks, dtype=I32)
    block_src = jnp.clip(idx, 0, jnp.maximum(used - 1, 0))
    nonempty = counts > 0
    e_ids = jnp.arange(MOE_EXPERTS, dtype=I32)
    later = jnp.where((e_ids[None, :] > e_ids[:, None]) & nonempty[None, :], e_ids[None, :], MOE_EXPERTS)
    nxt_of_e = jnp.min(later, axis=1)
    nxt_of_e = jnp.where(nxt_of_e == MOE_EXPERTS, -1, nxt_of_e)
    grp_of_e = jnp.cumsum(nonempty) - nonempty
    start = idx * rb
    owner = ((pad_starts[None, :] <= start[:, None]) & (start[:, None] < pad_ends[None, :])).astype(F32)
    per_e = jnp.stack([e_ids, pad_starts + counts, grp_of_e, nxt_of_e], axis=1).astype(F32)
    per_b = jnp.dot(owner, per_e, precision=HIGHEST).astype(I32)
    block_e, block_grp, block_nxt = per_b[:, 0], per_b[:, 2], per_b[:, 3]
    block_nv = jnp.where(idx < used, jnp.clip(per_b[:, 1] - start, 0, rb), 0)

    fill = jnp.concatenate([pad_starts + counts, pad_ends[-1:]]).astype(I32)
    plen = jnp.concatenate([padded - counts, cap - pad_ends[-1:]]).astype(I32)
    xs = _dispatch(lay, fill, plen, dest, x, mod_l, g2, cap)
    ys = _experts(layer, block_e, block_nv, block_src, block_grp, block_nxt, xs, wg, wu, wd)
    return _combine(lay, n_tiles, dest, x, rec, mod_l, ys)


def _rope_tables(lay):
    half = ATT_HEAD_DIM // 2
    pos = jnp.arange(lay.seq)
    rowp = (pos // GRID_W).astype(F32)
    colp = (pos % GRID_W).astype(F32)
    freqs = ROPE_THETA ** (-jnp.arange(0, half, 2, dtype=F32) / half)
    ang = jnp.concatenate([rowp[:, None] * freqs, colp[:, None] * freqs], axis=-1)
    cos = jnp.concatenate([jnp.cos(ang), jnp.ones((lay.n_ctx, half), F32)], axis=0)
    sin = jnp.concatenate([jnp.sin(ang), jnp.zeros((lay.n_ctx, half), F32)], axis=0)
    cos_d = jnp.repeat(cos, 2, axis=1)
    sin_d = jnp.repeat(sin, 2, axis=1) * jnp.tile(jnp.asarray([-1.0, 1.0], F32), half)
    reps = LANES // ATT_HEAD_DIM
    return jnp.tile(cos_d, (1, reps)), jnp.tile(sin_d, (1, reps)), cos_d.T, sin_d.T


def kernel(x, c, ctx, c_ctx, mod_w, mod_b, norm1_g, norm2_g, ssd_w_in, ssd_conv_w, ssd_conv_b, ssd_dt_bias,
           ssd_a_log, ssd_d, ssd_norm_g, ssd_w_out, att_w_qkv, att_q_gain, att_k_gain, att_w_o, moe_w_group,
           moe_b_group, moe_w_router, moe_b_router, moe_w_gate, moe_w_up, moe_w_down):
    b, n_lat, d = x.shape
    n_ctx = ctx.shape[1]
    depth = mod_w.shape[0]
    lay = _Layout(b, n_lat, n_ctx)
    assert b + 1 <= 8

    xs = jnp.concatenate([x.reshape(lay.t_lat, d), ctx.reshape(lay.t_ctx, d)], axis=0)
    c8 = jnp.concatenate([c, c_ctx[None, :], jnp.zeros((8 - b - 1, d), F32)], axis=0)
    mods = _mod_table(c8, mod_w, mod_b).reshape(depth, 8, N_MOD, d)

    nh = ssd_dt_bias.shape[2]
    di = nh * SSD_HEAD_DIM
    gn = SSD_N_GROUPS * SSD_D_STATE
    dc = di + 2 * gn
    expand = jnp.tile(jnp.repeat(jnp.eye(nh, dtype=BF16), SSD_HEAD_DIM, axis=1), (3, 1))

    hq = att_w_qkv.shape[2] // ATT_HEAD_DIM - 2 * ATT_KV_HEADS
    dq = hq * ATT_HEAD_DIM
    dkv = ATT_KV_HEADS * ATT_HEAD_DIM
    gsum = np.zeros((dq, LANES), np.float32)
    gsum[np.arange(dq), np.arange(dq) // ATT_HEAD_DIM] = 1.0
    gexp = np.concatenate([gsum.T, gsum.T], axis=0)
    vone = np.zeros((1, ATT_KV_HEADS * LANES), np.float32)
    vone[0, np.arange(ATT_KV_HEADS) * LANES + ATT_HEAD_DIM] = 1.0
    cos128, sin128, cos_t, sin_t = _rope_tables(lay)

    for layer in range(depth):
        last = layer == depth - 1
        mod_l = mods[layer]
        j = layer // 2
        g1 = norm1_g[layer][None, :]
        g2 = norm2_g[layer][None, :]
        n_tiles = lay.lat_tiles if last else lay.n_tiles
        wr, br = _router_weights(d, moe_w_group[layer], moe_b_group[layer], moe_w_router[layer], moe_b_router[layer])
        if layer % 2 == 0:
            w_in = ssd_w_in[j].astype(BF16)
            wz, wx, wdt = w_in[:, :di], w_in[:, di:di + dc], w_in[:, di + dc:]
            dtb = ssd_dt_bias[j].reshape(1, 2 * nh)
            z, xbc, dt, dtt = _ssd_inproj(lay, xs, mod_l, g1, wz, wx, wdt, wdt.T, dtb, dtb.T)
            xc, bm, cm = _ssd_conv(lay, xbc, ssd_conv_w[j], ssd_conv_b[j][None, :], di, gn)
            yf, yb = _ssd_scan(lay, xc, bm, cm, dt, dtt, ssd_a_log[j], expand)
            dexp = jnp.repeat(ssd_d[j], SSD_HEAD_DIM)[None, :]
            xs, rec, rec_t, cnt = _ssd_outproj(lay, n_tiles, yf, yb, xc, z, dexp, ssd_norm_g[j][None, :],
                                               ssd_w_out[j].astype(BF16), xs, mod_l, g2, wr, br)
        else:
            w = att_w_qkv[j]
            wq = w[:, :dq].astype(BF16)
            wkt = w[:, dq:dq + dkv].T.astype(BF16)
            wv = w[:, dq + dkv:].reshape(d, ATT_KV_HEADS, ATT_HEAD_DIM)
            wv2 = jnp.pad(wv, ((0, 0), (0, 0), (0, LANES - ATT_HEAD_DIM))).reshape(d, ATT_KV_HEADS * LANES).astype(BF16)
            qg = (jnp.tile(att_q_gain[j], hq) * (ATT_HEAD_DIM ** -0.5 * LOG2E))[None, :]
            kgt = att_k_gain[j][:, None]
            bound = (1.01 * ATT_HEAD_DIM * jnp.max(jnp.abs(qg)) * jnp.max(jnp.abs(kgt))).reshape(1, 1)
            qn, kt, v = _att_inproj(lay, xs, mod_l, g1, wq, wkt, wv2, jnp.asarray(vone), qg, kgt, bound, jnp.asarray(gsum, BF16),
                                    jnp.asarray(gexp, BF16), cos128, sin128, cos_t, sin_t)
            o = _att_core(lay, qn, kt, v)
            xs, rec, rec_t, cnt = _att_outproj(lay, n_tiles, o, att_w_o[j].astype(BF16), xs, mod_l, g2, wr, br)
        xs = _moe(lay, layer, n_tiles, xs, mod_l, g2, rec, rec_t, cnt, moe_w_gate, moe_w_up, moe_w_down)
    return xs[:lay.t_lat].reshape(b, n_lat, d)
```

```python
import functools

import numpy as np
import jax
import jax.numpy as jnp
from jax import lax
from jax.experimental import pallas as pl
from jax.experimental.pallas import tpu as pltpu

F32 = jnp.float32
BF16 = jnp.bfloat16
I32 = jnp.int32
HIGHEST = lax.Precision.HIGHEST

NORM_EPS = 1e-6
N_MOD = 6
GRID_W = 64
ROPE_THETA = 10000.0

SSD_HEAD_DIM = 64
SSD_N_GROUPS = 4
SSD_D_STATE = 128
SSD_CONV = 5
SSD_CHUNK = 128

ATT_HEAD_DIM = 64
ATT_KV_HEADS = 4
ATT_KEY_BLOCK = 4352
LOG2E = 1.4426950408889634
ATT_MIN_DENOM = 2.0 ** -60

MOE_GROUPS = 4
MOE_EPG = 8
MOE_EXPERTS = MOE_GROUPS * MOE_EPG
MOE_ROW_BLOCK = 256
ZROWS = MOE_ROW_BLOCK // 2
DEST_ROW = 1024

TM = 256
LANES = 128
SUBLANES = 8
HALO = 16
VMEM_LIMIT = 56 * 1024 * 1024


def _cparams(sem):
    return pltpu.CompilerParams(dimension_semantics=sem, vmem_limit_bytes=VMEM_LIMIT)


def _silu(v):
    return v / (1.0 + jnp.exp(-v))


def _softplus(v):
    return jnp.maximum(v, 0.0) + jnp.log1p(jnp.exp(-jnp.abs(v)))


def _norm_mod(x, g, shift, scale):
    ms = jnp.mean(x * x, axis=-1, keepdims=True)
    y = x * lax.rsqrt(ms + NORM_EPS) * g
    return y * (1.0 + scale) + shift


class _Layout:
    def __init__(self, batch, seq, n_ctx):
        assert seq % TM == 0 and n_ctx % TM == 0
        self.batch, self.seq, self.n_ctx = batch, seq, n_ctx
        self.t_lat = batch * seq
        self.t_ctx = batch * n_ctx
        self.t = self.t_lat + self.t_ctx
        self.lat_tiles = self.t_lat // TM
        self.seq_tiles = seq // TM
        self.ctx_tiles = n_ctx // TM
        self.n_tiles = self.t // TM
        self.nk = seq + n_ctx

    def mod_row(self, i):
        return jnp.where(i < self.lat_tiles, i // self.seq_tiles, self.batch)

    def tile_batch(self, i):
        return jnp.where(i < self.lat_tiles, i // self.seq_tiles, (i - self.lat_tiles) // self.ctx_tiles)

    def tile_pos(self, i):
        return jnp.where(i < self.lat_tiles, i % self.seq_tiles,
                         self.seq_tiles + (i - self.lat_tiles) % self.ctx_tiles)

    def seg_first(self, i):
        return jnp.where(i < self.lat_tiles, i % self.seq_tiles == 0, (i - self.lat_tiles) % self.ctx_tiles == 0)

    def seg_last(self, i):
        return jnp.where(i < self.lat_tiles, i % self.seq_tiles == self.seq_tiles - 1,
                         (i - self.lat_tiles) % self.ctx_tiles == self.ctx_tiles - 1)


def _mod_kernel(c_ref, w_ref, b_ref, o_ref):
    s = _silu(c_ref[...])
    o_ref[...] = jnp.dot(s, w_ref[...], precision=HIGHEST, preferred_element_type=F32) + b_ref[...]


def _mod_table(c8, mod_w, mod_b):
    depth, d, n = mod_w.shape
    tn = 1536
    assert n % tn == 0
    return pl.pallas_call(
        _mod_kernel,
        grid=(depth, n // tn),
        in_specs=[pl.BlockSpec((8, d), lambda l, j: (0, 0)),
                  pl.BlockSpec((None, d, tn), lambda l, j: (l, 0, j)),
                  pl.BlockSpec((None, 1, tn), lambda l, j: (l, 0, j))],
        out_specs=pl.BlockSpec((None, 8, tn), lambda l, j: (l, 0, j)),
        out_shape=jax.ShapeDtypeStruct((depth, 8, n), F32),
        compiler_params=_cparams(("parallel", "parallel")),
        name="mod_table",
    )(c8, mod_w, mod_b.reshape(depth, 1, n))


def _ssd_inproj_kernel(x_ref, mod_ref, g_ref, wz_ref, wx_ref, wdt_ref, wdtt_ref, dtb_ref, dtbt_ref,
                       z_ref, xbc_ref, dt_ref, dtt_ref):
    h = _norm_mod(x_ref[...], g_ref[...], mod_ref[0:1, :], mod_ref[1:2, :]).astype(BF16)
    z_ref[...] = jnp.dot(h, wz_ref[...], preferred_element_type=F32).astype(BF16)
    xbc_ref[...] = jnp.dot(h, wx_ref[...], preferred_element_type=F32).astype(BF16)
    nh = dtb_ref.shape[1] // 2
    dt = _softplus(jnp.dot(h, wdt_ref[...], preferred_element_type=F32) + dtb_ref[...])
    dt_ref[0] = dt[:, :nh]
    dt_ref[1] = dt[:, nh:]
    dtt = lax.dot_general(wdtt_ref[...], h, (((1,), (1,)), ((), ())), preferred_element_type=F32)
    dtt = _softplus(dtt + dtbt_ref[...])
    dtt_ref[0] = dtt[:nh, :]
    dtt_ref[1] = dtt[nh:, :]


def _ssd_inproj(lay, x, mod_l, g, wz, wx, wdt, wdtt, dtb, dtbt):
    t, d = x.shape
    di, dc, nh2 = wz.shape[1], wx.shape[1], wdt.shape[1]
    nh = nh2 // 2
    full = lambda a: pl.BlockSpec(a.shape, lambda i: (0,) * a.ndim)
    return pl.pallas_call(
        _ssd_inproj_kernel,
        grid=(lay.n_tiles,),
        in_specs=[pl.BlockSpec((TM, d), lambda i: (i, 0)),
                  pl.BlockSpec((None, N_MOD, d), lambda i: (lay.mod_row(i), 0, 0)),
                  full(g), full(wz), full(wx), full(wdt), full(wdtt), full(dtb), full(dtbt)],
        out_specs=[pl.BlockSpec((TM, di), lambda i: (i, 0)),
                   pl.BlockSpec((TM, dc), lambda i: (i, 0)),
                   pl.BlockSpec((2, TM, nh), lambda i: (0, i, 0)),
                   pl.BlockSpec((2, nh, TM), lambda i: (0, 0, i))],
        out_shape=[jax.ShapeDtypeStruct((t, di), BF16),
                   jax.ShapeDtypeStruct((t, dc), BF16),
                   jax.ShapeDtypeStruct((2, t, nh), F32),
                   jax.ShapeDtypeStruct((2, nh, t), F32)],
        compiler_params=_cparams(("parallel",)),
        name="ssd_inproj",
    )(x, mod_l, g, wz, wx, wdt, wdtt, dtb, dtbt)


def _ssd_conv_kernel(lay, di, gn, prev_ref, main_ref, next_ref, w_ref, b_ref, sh_ref, sht_ref, shb_ref,
                     xs_ref, bm_ref, cm_ref):
    i = pl.program_id(0)
    dc = main_ref.shape[1]
    half = SSD_CONV // 2
    zero_halo = jnp.zeros((HALO, dc), BF16)
    prev = jnp.where(lay.seg_first(i), zero_halo, prev_ref[...])
    nxt = jnp.where(lay.seg_last(i), zero_halo, next_ref[...])
    cw = 256
    for c0 in range(0, dc, cw):
        u = main_ref[:, c0:c0 + cw]
        acc = b_ref[:, c0:c0 + cw] + u.astype(F32) * w_ref[half:half + 1, c0:c0 + cw]
        top = jnp.zeros((SUBLANES, cw), F32)
        bot = jnp.zeros((SUBLANES, cw), F32)
        for k in range(SSD_CONV):
            if k == half:
                continue
            wk = w_ref[k:k + 1, c0:c0 + cw]
            acc = acc + jnp.dot(sh_ref[k], u, preferred_element_type=F32) * wk
            if k < half:
                top = top + jnp.dot(sht_ref[k], prev[:, c0:c0 + cw], preferred_element_type=F32)[:SUBLANES] * wk
            else:
                bot = bot + jnp.dot(shb_ref[k], nxt[:, c0:c0 + cw], preferred_element_type=F32)[:SUBLANES] * wk
        acc = jnp.concatenate([acc[:SUBLANES] + top, acc[SUBLANES:TM - SUBLANES], acc[TM - SUBLANES:] + bot], axis=0)
        y = _silu(acc).astype(BF16)
        if c0 < di:
            xs_ref[:, c0:c0 + cw] = y
        elif c0 < di + gn:
            bm_ref[:, c0 - di:c0 - di + cw] = y
        else:
            cm_ref[:, c0 - di - gn:c0 - di - gn + cw] = y


def _conv_shift_matrices():
    half = SSD_CONV // 2
    sh = np.zeros((SSD_CONV, TM, TM), np.float32)
    sht = np.zeros((SSD_CONV, HALO, HALO), np.float32)
    shb = np.zeros((SSD_CONV, HALO, HALO), np.float32)
    for k in range(SSD_CONV):
        s = k - half
        sh[k] = np.eye(TM, k=s)
        for r in range(SUBLANES):
            if r + s < 0:
                sht[k, r, HALO + r + s] = 1.0
            if r - SUBLANES + s >= 0:
                shb[k, r, r - SUBLANES + s] = 1.0
    return jnp.asarray(sh, BF16), jnp.asarray(sht, BF16), jnp.asarray(shb, BF16)


def _ssd_conv(lay, xbc, conv_w, conv_b, di, gn):
    t, dc = xbc.shape
    sh, sht, shb = _conv_shift_matrices()
    hb = TM // HALO
    nhb = t // HALO
    assert gn == 512 and di % 512 == 0
    return pl.pallas_call(
        functools.partial(_ssd_conv_kernel, lay, di, gn),
        grid=(lay.n_tiles,),
        in_specs=[pl.BlockSpec((HALO, dc), lambda i: (jnp.maximum(i * hb - 1, 0), 0)),
                  pl.BlockSpec((TM, dc), lambda i: (i, 0)),
                  pl.BlockSpec((HALO, dc), lambda i: (jnp.minimum((i + 1) * hb, nhb - 1), 0)),
                  pl.BlockSpec(conv_w.shape, lambda i: (0, 0)),
                  pl.BlockSpec(conv_b.shape, lambda i: (0, 0)),
                  pl.BlockSpec(sh.shape, lambda i: (0, 0, 0)),
                  pl.BlockSpec(sht.shape, lambda i: (0, 0, 0)),
                  pl.BlockSpec(shb.shape, lambda i: (0, 0, 0))],
        out_specs=[pl.BlockSpec((TM, di), lambda i: (i, 0)),
                   pl.BlockSpec((TM, gn), lambda i: (i, 0)),
                   pl.BlockSpec((TM, gn), lambda i: (i, 0))],
        out_shape=[jax.ShapeDtypeStruct((t, di), BF16),
                   jax.ShapeDtypeStruct((t, gn), BF16),
                   jax.ShapeDtypeStruct((t, gn), BF16)],
        compiler_params=_cparams(("parallel",)),
        name="ssd_conv",
    )(xbc, xbc, xbc, conv_w, conv_b, sh, sht, shb)


def _ssd_scan_kernel(xf_ref, bf_ref, cf_ref, dtf_ref, dttf_ref, xb_ref, bb_ref, cb_ref, dtb_ref, dttb_ref,
                     alr_ref, alc_ref, e_ref, yf_ref, yb_ref, state_ref):
    step = pl.program_id(1)

    @pl.when(step == 0)
    def _():
        state_ref[...] = jnp.zeros_like(state_ref)

    _scan_chunk(0, step, xf_ref, bf_ref, cf_ref, dtf_ref, dttf_ref, alr_ref.at[0], alc_ref.at[0], e_ref,
                yf_ref, state_ref.at[0])
    _scan_chunk(1, step, xb_ref, bb_ref, cb_ref, dtb_ref, dttb_ref, alr_ref.at[1], alc_ref.at[1], e_ref,
                yb_ref, state_ref.at[1])


def _scan_chunk(d, step, xs_ref, bm_ref, cm_ref, dt_ref, dtt_ref, alr_ref, alc_ref, e_ref, y_ref, state_ref):
    q = SSD_CHUNK
    ng = SSD_N_GROUPS
    ns = SSD_D_STATE
    gw = xs_ref.shape[1] // ng
    hpg = gw // SSD_HEAD_DIM

    row = lax.broadcasted_iota(I32, (q, q), 0)
    col = lax.broadcasted_iota(I32, (q, q), 1)
    lmask = row >= col if d == 0 else row <= col
    lmask_t = col >= row if d == 0 else col <= row

    dt = dt_ref[...]
    dtt = dtt_ref[...]
    da = dt * (-jnp.exp(alr_ref[...]))
    dat = dtt * (-jnp.exp(alc_ref[...]))
    a_cum = jnp.dot(lmask.astype(F32), da, precision=HIGHEST, preferred_element_type=F32)
    a_cum_t = jnp.dot(dat, lmask_t.astype(F32), precision=HIGHEST, preferred_element_type=F32)
    a_end = jnp.sum(da, axis=0, keepdims=True)
    w_end = jnp.exp(a_end - a_cum) * dt

    e3 = e_ref[...]

    def expand(v):
        hi = v.astype(BF16)
        r1 = v - hi.astype(F32)
        mid = r1.astype(BF16)
        lo = (r1 - mid.astype(F32)).astype(BF16)
        return jnp.dot(jnp.concatenate([hi, mid, lo], axis=1), e3, preferred_element_type=F32)

    a_end8 = jnp.broadcast_to(a_end, (SUBLANES, a_end.shape[1]))
    lane = lax.broadcasted_iota(I32, (q, LANES), 1)
    for g in range(ng):
        hs = slice(g * hpg, (g + 1) * hpg)
        decay_in = jnp.exp(expand(a_cum[:, hs]))
        chunk_decay = jnp.exp(expand(a_end8[:, hs])[0:1, :])
        xw = (xs_ref[:, g * gw:(g + 1) * gw].astype(F32) * expand(w_end[:, hs])).astype(BF16)
        bg = bm_ref[:, g * ns:(g + 1) * ns]
        cg = cm_ref[:, g * ns:(g + 1) * ns]
        cb = lax.dot_general(cg, bg, (((1,), (1,)), ((), ())), preferred_element_type=F32)
        cb = jnp.where(lmask, cb, 0.0)
        s_in = state_ref[g]
        y_off = jnp.dot(cg, s_in.astype(BF16), preferred_element_type=F32)
        new_s = lax.dot_general(bg, xw, (((0,), (0,)), ((), ())), preferred_element_type=F32)
        state_ref[g] = s_in * chunk_decay + new_s
        for pr in range(hpg // 2):
            mixes = []
            for hh in range(2):
                h = g * hpg + 2 * pr + hh
                seg = a_cum[:, h:h + 1] - a_cum_t[h:h + 1, :]
                lh = jnp.exp(jnp.minimum(seg, 0.0))
                mixes.append((cb * lh * dtt[h:h + 1, :]).astype(BF16))
            lhs = jnp.concatenate(mixes, axis=1)
            l0 = g * gw + pr * LANES
            xp = xs_ref[:, l0:l0 + LANES]
            zero = jnp.zeros_like(xp)
            rhs = jnp.concatenate([jnp.where(lane < SSD_HEAD_DIM, xp, zero),
                                   jnp.where(lane >= SSD_HEAD_DIM, xp, zero)], axis=0)
            y_pair = jnp.dot(lhs, rhs, preferred_element_type=F32)
            y_pair = y_pair + y_off[:, pr * LANES:(pr + 1) * LANES] * decay_in[:, pr * LANES:(pr + 1) * LANES]
            y_ref[:, l0:l0 + LANES] = y_pair.astype(BF16)


def _ssd_scan(lay, xs, bm, cm, dt, dtt, a_log, expand):
    t, di = xs.shape
    gn = bm.shape[1]
    nh = dt.shape[2]
    q = SSD_CHUNK
    nct, nlt = lay.n_ctx // q, lay.seq // q
    nc = nct + nlt
    ctx_base = lay.t_lat // q

    def blk(d, b, s):
        j_ctx = jnp.where(d == 0, s, nct - 1 - s)
        j_lat = jnp.where(d == 0, s - nct, nlt - 1 - (s - nct))
        return jnp.where(s < nct, ctx_base + b * nct + j_ctx, b * nlt + j_lat)

    def chunk_specs(d):
        return [pl.BlockSpec((q, di), lambda b, s: (blk(d, b, s), 0)),
                pl.BlockSpec((q, gn), lambda b, s: (blk(d, b, s), 0)),
                pl.BlockSpec((q, gn), lambda b, s: (blk(d, b, s), 0)),
                pl.BlockSpec((None, q, nh), lambda b, s: (d, blk(d, b, s), 0)),
                pl.BlockSpec((None, nh, q), lambda b, s: (d, 0, blk(d, b, s)))]

    return pl.pallas_call(
        _ssd_scan_kernel,
        grid=(lay.batch, nc),
        in_specs=chunk_specs(0) + chunk_specs(1) + [
            pl.BlockSpec((2, 1, nh), lambda b, s: (0, 0, 0)),
            pl.BlockSpec((2, nh, 1), lambda b, s: (0, 0, 0)),
            pl.BlockSpec(expand.shape, lambda b, s: (0, 0))],
        out_specs=[pl.BlockSpec((q, di), lambda b, s: (blk(0, b, s), 0)),
                   pl.BlockSpec((q, di), lambda b, s: (blk(1, b, s), 0))],
        out_shape=[jax.ShapeDtypeStruct((t, di), BF16), jax.ShapeDtypeStruct((t, di), BF16)],
        scratch_shapes=[pltpu.VMEM((2, SSD_N_GROUPS, SSD_D_STATE, di // SSD_N_GROUPS), F32)],
        compiler_params=_cparams(("arbitrary", "arbitrary")),
        name="ssd_scan",
    )(xs, bm, cm, dt, dtt, xs, bm, cm, dt, dtt, a_log.reshape(2, 1, nh), a_log.reshape(2, nh, 1), expand)


def _ssd_outproj_kernel(n_route, yf_ref, yb_ref, xs_ref, z_ref, dexp_ref, ng_ref, w_ref, x_ref, mod_ref,
                        g2_ref, wr_ref, br_ref, o_ref, r_ref, rt_ref, cnt_ref, base_ref):
    gw = xs_ref.shape[1] // SSD_N_GROUPS
    parts = []
    for g in range(SSD_N_GROUPS):
        gs = slice(g * gw, (g + 1) * gw)
        yg = yf_ref[:, gs].astype(F32) + yb_ref[:, gs].astype(F32) + xs_ref[:, gs].astype(F32) * dexp_ref[:, gs]
        yg = yg * _silu(z_ref[:, gs].astype(F32))
        ms = jnp.mean(yg * yg, axis=-1, keepdims=True)
        parts.append((yg * lax.rsqrt(ms + NORM_EPS) * ng_ref[:, gs]).astype(BF16))
    yn = jnp.concatenate(parts, axis=1)
    m = jnp.dot(yn, w_ref[...], preferred_element_type=F32)
    x_new = x_ref[...] + mod_ref[2:3, :] * m
    o_ref[...] = x_new
    _route(pl.program_id(0), n_route, x_new, mod_ref, g2_ref, wr_ref, br_ref, r_ref, rt_ref, cnt_ref, base_ref)


def _ssd_outproj(lay, n_route, yf, yb, xs, z, dexp, ng, w, x, mod_l, g2, wr, br):
    t, d = x.shape
    di = xs.shape[1]
    full = lambda a: pl.BlockSpec(a.shape, lambda i: (0,) * a.ndim)
    r_in, r_out, r_shape, r_scratch = _route_specs(lay, g2, wr, br)
    return pl.pallas_call(
        functools.partial(_ssd_outproj_kernel, n_route),
        grid=(lay.n_tiles,),
        in_specs=[pl.BlockSpec((TM, di), lambda i: (i, 0)),
                  pl.BlockSpec((TM, di), lambda i: (i, 0)),
                  pl.BlockSpec((TM, di), lambda i: (i, 0)),
                  pl.BlockSpec((TM, di), lambda i: (i, 0)),
                  full(dexp), full(ng), full(w),
                  pl.BlockSpec((TM, d), lambda i: (i, 0)),
                  pl.BlockSpec((None, N_MOD, d), lambda i: (lay.mod_row(i), 0, 0))] + r_in,
        out_specs=[pl.BlockSpec((TM, d), lambda i: (i, 0))] + r_out,
        out_shape=[jax.ShapeDtypeStruct((t, d), F32)] + r_shape,
        scratch_shapes=r_scratch,
        compiler_params=_cparams(("arbitrary",)),
        name="ssd_outproj",
    )(yf, yb, xs, z, dexp, ng, w, x, mod_l, g2, wr, br)


def _att_inproj_kernel(x_ref, mod_ref, g_ref, wq_ref, wkt_ref, wv_ref, vone_ref, qg_ref, kgt_ref, bound_ref, gsum_ref, gexp_ref,
                       cos_ref, sin_ref, cost_ref, sint_ref, q_ref, kt_ref, v_ref):
    h = _norm_mod(x_ref[...], g_ref[...], mod_ref[0:1, :], mod_ref[1:2, :]).astype(BF16)
    hd = ATT_HEAD_DIM
    q = jnp.dot(h, wq_ref[...], preferred_element_type=F32)
    ssum = jnp.dot((q * q).astype(BF16), gsum_ref[...], preferred_element_type=F32)
    r = lax.rsqrt(ssum * (1.0 / hd) + NORM_EPS)
    r_hi = r.astype(BF16)
    r_lo = (r - r_hi.astype(F32)).astype(BF16)
    r_exp = jnp.dot(jnp.concatenate([r_hi, r_lo], axis=1), gexp_ref[...], preferred_element_type=F32)
    qn = q * r_exp * qg_ref[...]
    nl = qn.shape[1]
    lane = lax.broadcasted_iota(I32, qn.shape, 1)
    partner = jnp.where((lane & 1) == 0, pltpu.roll(qn, nl - 1, 1), pltpu.roll(qn, 1, 1))
    reps = nl // LANES
    cos = jnp.concatenate([cos_ref[...]] * reps, axis=1)
    sin = jnp.concatenate([sin_ref[...]] * reps, axis=1)
    q_ref[...] = (qn * cos + partner * sin).astype(BF16)
    kt = lax.dot_general(wkt_ref[...], h, (((1,), (1,)), ((), ())), preferred_element_type=F32)
    ct, st = cost_ref[...], sint_ref[...]
    sub = lax.broadcasted_iota(I32, ct.shape, 0)
    for kh in range(ATT_KV_HEADS):
        blk = kt[kh * hd:(kh + 1) * hd]
        rk = lax.rsqrt(jnp.sum(blk * blk, axis=0, keepdims=True) * (1.0 / hd) + NORM_EPS)
        kn = blk * rk * kgt_ref[...]
        kpart = jnp.where((sub & 1) == 0, pltpu.roll(kn, hd - 1, 0), pltpu.roll(kn, 1, 0))
        kt_ref[kh, 0:hd, :] = (kn * ct + kpart * st).astype(BF16)
        kt_ref[kh, hd:2 * hd, :] = jnp.where(sub == 0, -bound_ref[...], 0.0).astype(BF16)
    v = (jnp.dot(h, wv_ref[...], preferred_element_type=F32) + vone_ref[...]).astype(BF16)
    for kh in range(ATT_KV_HEADS):
        v_ref[kh] = v[:, kh * LANES:(kh + 1) * LANES]


def _att_inproj(lay, x, mod_l, g, wq, wkt, wv2, vone, qg, kgt, bound, gsum, gexp, cos, sin, cost, sint):
    t, d = x.shape
    dq = wq.shape[1]
    hd = ATT_HEAD_DIM
    full = lambda a: pl.BlockSpec(a.shape, lambda i: (0,) * a.ndim)
    return pl.pallas_call(
        _att_inproj_kernel,
        grid=(lay.n_tiles,),
        in_specs=[pl.BlockSpec((TM, d), lambda i: (i, 0)),
                  pl.BlockSpec((None, N_MOD, d), lambda i: (lay.mod_row(i), 0, 0)),
                  full(g), full(wq), full(wkt), full(wv2), full(vone), full(qg), full(kgt), full(bound), full(gsum), full(gexp),
                  pl.BlockSpec((TM, LANES), lambda i: (lay.tile_pos(i), 0)),
                  pl.BlockSpec((TM, LANES), lambda i: (lay.tile_pos(i), 0)),
                  pl.BlockSpec((hd, TM), lambda i: (0, lay.tile_pos(i))),
                  pl.BlockSpec((hd, TM), lambda i: (0, lay.tile_pos(i)))],
        out_specs=[pl.BlockSpec((TM, dq), lambda i: (i, 0)),
                   pl.BlockSpec((None, ATT_KV_HEADS, 2 * hd, TM), lambda i: (lay.tile_batch(i), 0, 0, lay.tile_pos(i))),
                   pl.BlockSpec((None, ATT_KV_HEADS, TM, LANES), lambda i: (lay.tile_batch(i), 0, lay.tile_pos(i), 0))],
        out_shape=[jax.ShapeDtypeStruct((t, dq), BF16),
                   jax.ShapeDtypeStruct((lay.batch, ATT_KV_HEADS, 2 * hd, lay.nk), BF16),
                   jax.ShapeDtypeStruct((lay.batch, ATT_KV_HEADS, lay.nk, LANES), BF16)],
        compiler_params=_cparams(("parallel",)),
        name="att_inproj",
    )(x, mod_l, g, wq, wkt, wv2, vone, qg, kgt, bound, gsum, gexp, cos, sin, cost, sint)


def _attend(q_ref, kt_ref, v_ref, o_ref, k0, n_keys):
    hd = ATT_HEAD_DIM
    tm = q_ref.shape[0]
    heads = q_ref.shape[1] // hd
    qs = [q_ref[:, u * hd:(u + 1) * hd] for u in range(heads)]

    bk = min(ATT_KEY_BLOCK, n_keys)
    assert n_keys % bk == 0 and k0 % LANES == 0 and bk % LANES == 0

    def body(j, carry):
        start = pl.multiple_of(k0 + j * bk, LANES)
        kb = kt_ref[0:hd, pl.ds(start, bk)]
        vb = v_ref[pl.ds(start, bk), :]
        out = []
        for u in range(heads):
            m_old, acc = carry[u]
            s = jnp.dot(qs[u], kb, preferred_element_type=F32)
            m_new = jnp.maximum(m_old, jnp.max(s, axis=-1, keepdims=True))
            p = jnp.exp2((s - m_new).astype(BF16))
            acc = jnp.exp2(m_old - m_new) * acc + jnp.dot(p, vb, preferred_element_type=F32)
            out.append((m_new, acc))
        return tuple(out)

    init = tuple((jnp.full((tm, 1), -jnp.inf, F32), jnp.zeros((tm, LANES), F32)) for _ in range(heads))
    final = lax.fori_loop(0, n_keys // bk, body, init) if n_keys > bk else body(0, init)
    outs = [acc[:, :hd] / acc[:, hd:hd + 1] for _, acc in final]
    o_ref[...] = jnp.concatenate(outs, axis=1).astype(BF16)


def _attend_bounded(q_ref, kt_ref, v_ref, o_ref):
    hd = ATT_HEAD_DIM
    tm = q_ref.shape[0]
    k_lo = kt_ref[...]
    k_hi = jnp.concatenate([k_lo[hd:], k_lo[:hd]], axis=0)
    vb = v_ref[...]
    lane = lax.broadcasted_iota(I32, (tm, LANES), 1)
    outs, dens = [], []
    for pr in range(q_ref.shape[1] // LANES):
        qp = q_ref[:, pr * LANES:(pr + 1) * LANES]
        q_lo = jnp.where(lane < hd, qp, (lane == hd).astype(BF16))
        q_hi = jnp.where(lane >= hd, qp, (lane == 0).astype(BF16))
        for qa, ka in ((q_lo, k_lo), (q_hi, k_hi)):
            s = jnp.dot(qa, ka, preferred_element_type=F32)
            acc = jnp.dot(jnp.exp2(s).astype(BF16), vb, preferred_element_type=F32)
            den = acc[:, hd:hd + 1]
            outs.append(acc[:, :hd] / den)
            dens.append(den)
    o_ref[...] = jnp.concatenate(outs, axis=1).astype(BF16)
    return jnp.min(jnp.concatenate(dens, axis=1))


def _att_core_kernel(lay, q_ref, kt_ref, v_ref, o_ref):
    qi = pl.program_id(2)

    @pl.when(qi < lay.seq_tiles)
    def _():
        smallest = _attend_bounded(q_ref, kt_ref, v_ref, o_ref)

        @pl.when(jnp.logical_not(smallest > ATT_MIN_DENOM))
        def _():
            _attend(q_ref, kt_ref, v_ref, o_ref, 0, lay.nk)

    @pl.when(qi >= lay.seq_tiles)
    def _():
        _attend(q_ref, kt_ref, v_ref, o_ref, lay.seq, lay.n_ctx)


def _att_core(lay, qn, kt, v):
    t, dq = qn.shape
    hd = ATT_HEAD_DIM
    qw = dq // ATT_KV_HEADS
    per_b = lay.seq_tiles + lay.ctx_tiles

    def row_tile(b, qi):
        return jnp.where(qi < lay.seq_tiles, b * lay.seq_tiles + qi,
                         lay.lat_tiles + b * lay.ctx_tiles + (qi - lay.seq_tiles))

    return pl.pallas_call(
        functools.partial(_att_core_kernel, lay),
        grid=(lay.batch, ATT_KV_HEADS, per_b),
        in_specs=[pl.BlockSpec((TM, qw), lambda b, kh, qi: (row_tile(b, qi), kh)),
                  pl.BlockSpec((None, None, 2 * hd, lay.nk), lambda b, kh, qi: (b, kh, 0, 0)),
                  pl.BlockSpec((None, None, lay.nk, LANES), lambda b, kh, qi: (b, kh, 0, 0))],
        out_specs=pl.BlockSpec((TM, qw), lambda b, kh, qi: (row_tile(b, qi), kh)),
        out_shape=jax.ShapeDtypeStruct((t, dq), BF16),
        compiler_params=_cparams(("parallel", "parallel", "arbitrary")),
        name="att_core",
    )(qn, kt, v)


def _att_outproj_kernel(n_route, o_ref, w_ref, x_ref, mod_ref, g2_ref, wr_ref, br_ref,
                        out_ref, r_ref, rt_ref, cnt_ref, base_ref):
    m = jnp.dot(o_ref[...], w_ref[...], preferred_element_type=F32)
    x_new = x_ref[...] + mod_ref[2:3, :] * m
    out_ref[...] = x_new
    _route(pl.program_id(0), n_route, x_new, mod_ref, g2_ref, wr_ref, br_ref, r_ref, rt_ref, cnt_ref, base_ref)


def _att_outproj(lay, n_route, o, w, x, mod_l, g2, wr, br):
    t, d = x.shape
    r_in, r_out, r_shape, r_scratch = _route_specs(lay, g2, wr, br)
    return pl.pallas_call(
        functools.partial(_att_outproj_kernel, n_route),
        grid=(lay.n_tiles,),
        in_specs=[pl.BlockSpec((TM, o.shape[1]), lambda i: (i, 0)),
                  pl.BlockSpec(w.shape, lambda i: (0, 0)),
                  pl.BlockSpec((TM, d), lambda i: (i, 0)),
                  pl.BlockSpec((None, N_MOD, d), lambda i: (lay.mod_row(i), 0, 0))] + r_in,
        out_specs=[pl.BlockSpec((TM, d), lambda i: (i, 0))] + r_out,
        out_shape=[jax.ShapeDtypeStruct((t, d), F32)] + r_shape,
        scratch_shapes=r_scratch,
        compiler_params=_cparams(("arbitrary",)),
        name="att_outproj",
    )(o, w, x, mod_l, g2, wr, br)


R_EID, R_RANK, R_W = 0, 2, 4
GROUP_LANE0 = MOE_EXPERTS


def _store_token_rows(ref, val):
    rows, d = val.shape
    for s in range(d // LANES):
        ref[pl.ds(s, rows, stride=d // LANES), :] = val[:, s * LANES:(s + 1) * LANES]


def _load_token_rows(ref, d):
    n = d // LANES
    rows = ref.shape[0] // n
    return jnp.concatenate([ref[pl.ds(s, rows, stride=n), :] for s in range(n)], axis=1)


def _token(ref, r, n):
    return ref.at[pl.ds(pl.multiple_of(r * n, n), n)]


def _route(i, n_route, x, mod_ref, g_ref, wr_ref, br_ref, r_ref, rt_ref, cnt_ref, base_ref):
    @pl.when(i == 0)
    def _():
        base_ref[...] = jnp.zeros_like(base_ref)

    f = _norm_mod(x, g_ref[...], mod_ref[3:4, :], mod_ref[4:5, :])
    f_hi = f.astype(BF16)
    f_lo = (f - f_hi.astype(F32)).astype(BF16)
    part = jnp.dot(f_hi, wr_ref[...], preferred_element_type=F32)
    part_lo = jnp.dot(f_lo, wr_ref[:, :LANES], preferred_element_type=F32)
    logits = part[:, :LANES] + part[:, LANES:] + part_lo + br_ref[...]
    lane = lax.broadcasted_iota(I32, logits.shape, 1)
    neg = jnp.float32(-jnp.inf)
    big = jnp.int32(LANES)

    def first_max(vals):
        top = jnp.max(vals, axis=-1, keepdims=True)
        idx = jnp.min(jnp.where(vals == top, lane, big), axis=-1, keepdims=True)
        return top, idx

    g_mask = (lane >= GROUP_LANE0) & (lane < GROUP_LANE0 + MOE_GROUPS)
    glog = jnp.where(g_mask, logits, neg)
    g_top, g_idx = first_max(glog)
    g_w = 1.0 / jnp.sum(jnp.exp(glog - g_top), axis=-1, keepdims=True)
    e0 = (g_idx - GROUP_LANE0) * MOE_EPG
    elog = jnp.where((lane >= e0) & (lane < e0 + MOE_EPG), logits, neg)
    v1, i1 = first_max(elog)
    v2, i2 = first_max(jnp.where(lane == i1, neg, elog))
    ex = jnp.exp(v2 - v1)
    w1 = g_w / (1.0 + ex)
    w2 = g_w * ex / (1.0 + ex)

    oh1 = (lane == i1).astype(F32)
    oh2 = (lane == i2).astype(F32)
    tm = logits.shape[0]
    rr = lax.broadcasted_iota(I32, (tm, tm), 0)
    cc = lax.broadcasted_iota(I32, (tm, tm), 1)
    before = (cc < rr).astype(BF16)
    cum1 = jnp.dot(before, oh1.astype(BF16), preferred_element_type=F32)
    cum2 = jnp.dot(before, oh2.astype(BF16), preferred_element_type=F32)
    tot1 = jnp.sum(oh1, axis=0, keepdims=True)
    tot2 = jnp.sum(oh2, axis=0, keepdims=True)
    base = base_ref[...]
    rank1 = jnp.sum(oh1 * (cum1 + base), axis=-1, keepdims=True)
    rank2 = jnp.sum(oh2 * (cum2 + base + tot1), axis=-1, keepdims=True)
    base = base + jnp.where(i < n_route, tot1 + tot2, 0.0)
    base_ref[...] = base
    cnt_ref[...] = base

    rec = jnp.zeros(logits.shape, F32)
    for k, val in ((R_EID, i1.astype(F32)), (R_EID + 1, i2.astype(F32)), (R_RANK, rank1), (R_RANK + 1, rank2),
                   (R_W, w1), (R_W + 1, w2)):
        rec = jnp.where(lane == k, val, rec)
    r_ref[...] = rec
    rt_ref[...] = rec.T[:SUBLANES, :]


def _route_specs(lay, g2, wr, br):
    t = lay.n_tiles * TM
    ins = [pl.BlockSpec(g2.shape, lambda i: (0, 0)),
           pl.BlockSpec(wr.shape, lambda i: (0, 0)),
           pl.BlockSpec(br.shape, lambda i: (0, 0))]
    outs = [pl.BlockSpec((TM, LANES), lambda i: (i, 0)),
            pl.BlockSpec((SUBLANES, TM), lambda i: (0, i)),
            pl.BlockSpec((1, LANES), lambda i: (0, 0))]
    shapes = [jax.ShapeDtypeStruct((t, LANES), F32),
              jax.ShapeDtypeStruct((SUBLANES, t), F32),
              jax.ShapeDtypeStruct((1, LANES), F32)]
    return ins, outs, shapes, [pltpu.VMEM((1, LANES), F32)]


def _dispatch_kernel(fill_ref, plen_ref, dest_hbm, x_ref, mod_ref, g_ref, xs_hbm, dest_s, fbuf, zbuf, dsem, fsem, ssem):
    i = pl.program_id(0)
    n = pl.num_programs(0)
    tn = x_ref.shape[1] // LANES

    def drain(s):
        for k in range(2):
            pltpu.make_async_copy(fbuf.at[s], xs_hbm.at[pl.ds(0, TM * tn)], ssem.at[s]).wait()

    def dest_copy(j):
        return pltpu.make_async_copy(dest_hbm.at[j], dest_s.at[pl.ds((j % 2) * DEST_ROW, DEST_ROW)], dsem.at[j % 2])

    def pad_pieces(e, fn):
        plen = plen_ref[e]
        for bit in range(ZROWS.bit_length() - 1, -1, -1):
            size = 1 << bit

            @pl.when((plen & size) != 0)
            def _():
                start = fill_ref[e] + ((plen >> (bit + 1)) << (bit + 1))
                fn(pltpu.make_async_copy(zbuf.at[pl.ds(0, size * tn)],
                                         xs_hbm.at[pl.ds(pl.multiple_of(start * tn, tn), size * tn)], fsem.at[0]))

    @pl.when(i == 0)
    def _():
        zbuf[...] = jnp.zeros_like(zbuf)
        dest_copy(0).start()

        def start_e(e, c):
            pad_pieces(e, lambda cp: cp.start())
            return c

        def wait_e(e, c):
            pad_pieces(e, lambda cp: cp.wait())
            return c

        def tail_pieces(fn):
            def piece(t, c):
                start = fill_ref[MOE_EXPERTS] + t * ZROWS
                fn(pltpu.make_async_copy(zbuf, xs_hbm.at[pl.ds(pl.multiple_of(start * tn, tn), ZROWS * tn)],
                                         fsem.at[0]))
                return c
            lax.fori_loop(0, plen_ref[MOE_EXPERTS] // ZROWS, piece, 0)

        lax.fori_loop(0, MOE_EXPERTS, start_e, 0)
        tail_pieces(lambda cp: cp.start())
        lax.fori_loop(0, MOE_EXPERTS, wait_e, 0)
        tail_pieces(lambda cp: cp.wait())

    @pl.when(i + 1 < n)
    def _():
        dest_copy(i + 1).start()

    dest_copy(i).wait()
    slot = i % 2

    @pl.when(i >= 2)
    def _():
        drain(slot)

    f = _norm_mod(x_ref[...], g_ref[...], mod_ref[3:4, :], mod_ref[4:5, :])
    _store_token_rows(fbuf.at[slot], f)

    def body(r, c):
        for k in range(2):
            row = dest_s[slot * DEST_ROW + k * TM + r]
            pltpu.make_async_copy(_token(fbuf.at[slot], r, tn), _token(xs_hbm, row, tn),
                                  ssem.at[slot]).start(priority=k)
        return c

    lax.fori_loop(0, TM, body, 0, unroll=8)

    @pl.when(i == n - 1)
    def _():
        drain(slot)

        @pl.when(n > 1)
        def _():
            drain(1 - slot)


def _dispatch(lay, fill, plen, dest, x, mod_l, g, cap):
    n_tiles = dest.shape[0]
    d = x.shape[1]
    tn = d // LANES
    grid_spec = pltpu.PrefetchScalarGridSpec(
        num_scalar_prefetch=2,
        grid=(n_tiles,),
        in_specs=[pl.BlockSpec(memory_space=pl.ANY),
                  pl.BlockSpec((TM, d), lambda i, fill, plen: (i, 0)),
                  pl.BlockSpec((None, N_MOD, d), lambda i, fill, plen: (lay.mod_row(i), 0, 0)),
                  pl.BlockSpec(g.shape, lambda i, fill, plen: (0, 0))],
        out_specs=pl.BlockSpec(memory_space=pl.ANY),
        scratch_shapes=[pltpu.SMEM((2 * DEST_ROW,), I32),
                        pltpu.VMEM((2, TM * tn, LANES), F32),
                        pltpu.VMEM((ZROWS * tn, LANES), F32),
                        pltpu.SemaphoreType.DMA((2,)),
                        pltpu.SemaphoreType.DMA((1,)),
                        pltpu.SemaphoreType.DMA((2,))],
    )
    return pl.pallas_call(
        _dispatch_kernel,
        grid_spec=grid_spec,
        out_shape=jax.ShapeDtypeStruct((cap * tn, LANES), F32),
        compiler_params=_cparams(("arbitrary",)),
        name="moe_dispatch",
    )(fill, plen, dest, x, mod_l, g)


def _expert_kernel(layer, be_ref, nv_ref, blk_ref, grp_ref, nxt_ref, x_ref, wg_hbm, wu_hbm, wd_hbm, y_ref,
                   wgf, wuf, wdf, wgb, wub, wdb, wsem):
    i = pl.program_id(0)
    live = nv_ref[i] > 0
    first = jnp.logical_or(i == 0, be_ref[i] != be_ref[jnp.maximum(i - 1, 0)])
    slot = grp_ref[i] % 2

    def fetch(e, s):
        return (pltpu.make_async_copy(wg_hbm.at[layer, e], wgf.at[s], wsem.at[s]),
                pltpu.make_async_copy(wu_hbm.at[layer, e], wuf.at[s], wsem.at[s]),
                pltpu.make_async_copy(wd_hbm.at[layer, e], wdf.at[s], wsem.at[s]))

    @pl.when(i == 0)
    def _():
        for cp in fetch(be_ref[0], 0):
            cp.start()

    @pl.when(jnp.logical_and(live, first))
    def _():
        for cp in fetch(be_ref[i], slot):
            cp.wait()

        @pl.when(nxt_ref[i] >= 0)
        def _():
            for cp in fetch(nxt_ref[i], 1 - slot):
                cp.start()

        wgb[...] = wgf[slot].astype(BF16)
        wub[...] = wuf[slot].astype(BF16)
        wdb[...] = wdf[slot].astype(BF16)

    @pl.when(live)
    def _():
        xb = _load_token_rows(x_ref, wgb.shape[0]).astype(BF16)
        hg = jnp.dot(xb, wgb[...], preferred_element_type=F32)
        hu = jnp.dot(xb, wub[...], preferred_element_type=F32)
        act = (_silu(hg) * hu).astype(BF16)
        _store_token_rows(y_ref, jnp.dot(act, wdb[...], preferred_element_type=F32))

    @pl.when(jnp.logical_not(live))
    def _():
        y_ref[...] = jnp.zeros_like(y_ref)


def _experts(layer, block_e, block_nv, block_src, block_grp, block_nxt, xs, wg, wu, wd):
    n_blocks = block_e.shape[0]
    d, dff = wg.shape[2:]
    rb = MOE_ROW_BLOCK
    blk = (rb * (d // LANES), LANES)
    grid_spec = pltpu.PrefetchScalarGridSpec(
        num_scalar_prefetch=5,
        grid=(n_blocks,),
        in_specs=[pl.BlockSpec(blk, lambda i, be, nv, src, grp, nxt: (src[i], 0)),
                  pl.BlockSpec(memory_space=pl.ANY),
                  pl.BlockSpec(memory_space=pl.ANY),
                  pl.BlockSpec(memory_space=pl.ANY)],
        out_specs=pl.BlockSpec(blk, lambda i, be, nv, src, grp, nxt: (i, 0)),
        scratch_shapes=[pltpu.VMEM((2, d, dff), F32),
                        pltpu.VMEM((2, d, dff), F32),
                        pltpu.VMEM((2, dff, d), F32),
                        pltpu.VMEM((d, dff), BF16),
                        pltpu.VMEM((d, dff), BF16),
                        pltpu.VMEM((dff, d), BF16),
                        pltpu.SemaphoreType.DMA((2,))],
    )
    return pl.pallas_call(
        functools.partial(_expert_kernel, layer),
        grid_spec=grid_spec,
        out_shape=jax.ShapeDtypeStruct(xs.shape, F32),
        compiler_params=_cparams(("arbitrary",)),
        name="moe_experts",
    )(block_e, block_nv, block_src, block_grp, block_nxt, xs, wg, wu, wd)


def _combine_kernel(dest_hbm, x_ref, r_ref, mod_ref, ys_hbm, o_ref, dest_s, gbuf, dsem, gsem):
    i = pl.program_id(0)
    n = pl.num_programs(0)
    tn = x_ref.shape[1] // LANES

    def dest_copy(j):
        return pltpu.make_async_copy(dest_hbm.at[j], dest_s.at[pl.ds((j % 3) * DEST_ROW, DEST_ROW)], dsem.at[j % 3])

    def gather_tile(j):
        def body(r, c):
            for k in range(2):
                row = dest_s[(j % 3) * DEST_ROW + k * TM + r]
                pltpu.make_async_copy(_token(ys_hbm, row, tn), _token(gbuf.at[j % 2, k], r, tn),
                                      gsem.at[j % 2]).start(priority=k)
            return c
        lax.fori_loop(0, TM, body, 0, unroll=8)

    @pl.when(i == 0)
    def _():
        dest_copy(0).start()
        dest_copy(0).wait()
        gather_tile(0)

        @pl.when(n > 1)
        def _():
            dest_copy(1).start()

    @pl.when(i + 1 < n)
    def _():
        dest_copy(i + 1).wait()

        @pl.when(i + 2 < n)
        def _():
            dest_copy(i + 2).start()

        gather_tile(i + 1)

    slot = i % 2
    for k in range(2):
        pltpu.make_async_copy(ys_hbm.at[pl.ds(0, TM * tn)], gbuf.at[slot, k], gsem.at[slot]).wait()
    w1 = r_ref[:, R_W:R_W + 1]
    w2 = r_ref[:, R_W + 1:R_W + 2]
    d = x_ref.shape[1]
    y = _load_token_rows(gbuf.at[slot, 0], d) * w1 + _load_token_rows(gbuf.at[slot, 1], d) * w2
    o_ref[...] = x_ref[...] + mod_ref[5:6, :] * y


def _combine(lay, n_tiles, dest, x, rec, mod_l, ys):
    d = x.shape[1]
    return pl.pallas_call(
        _combine_kernel,
        grid=(n_tiles,),
        in_specs=[pl.BlockSpec(memory_space=pl.ANY),
                  pl.BlockSpec((TM, d), lambda i: (i, 0)),
                  pl.BlockSpec((TM, LANES), lambda i: (i, 0)),
                  pl.BlockSpec((None, N_MOD, d), lambda i: (lay.mod_row(i), 0, 0)),
                  pl.BlockSpec(memory_space=pl.ANY)],
        out_specs=pl.BlockSpec((TM, d), lambda i: (i, 0)),
        out_shape=jax.ShapeDtypeStruct((n_tiles * TM, d), F32),
        scratch_shapes=[pltpu.SMEM((3 * DEST_ROW,), I32),
                        pltpu.VMEM((2, 2, TM * (d // LANES), LANES), F32),
                        pltpu.SemaphoreType.DMA((3,)),
                        pltpu.SemaphoreType.DMA((2,))],
        compiler_params=_cparams(("arbitrary",)),
        name="moe_combine",
    )(dest, x, rec, mod_l, ys)


def _router_weights(d, w_group, b_group, w_router, b_router):
    pad = LANES - MOE_EXPERTS - MOE_GROUPS
    wr = jnp.concatenate([w_router, w_group, jnp.zeros((d, pad), F32)], axis=1)
    br = jnp.concatenate([b_router, b_group, jnp.zeros((pad,), F32)])[None, :]
    wr_hi = wr.astype(BF16)
    wr_lo = (wr - wr_hi.astype(F32)).astype(BF16)
    return jnp.concatenate([wr_hi, wr_lo], axis=1), br


def _moe(lay, layer, n_tiles, x, mod_l, g2, rec, rec_t, cnt, wg, wu, wd):
    d = x.shape[1]
    t_tok = n_tiles * TM
    rec_t = rec_t[:, :t_tok]

    rb = MOE_ROW_BLOCK
    eid = rec_t[R_EID:R_EID + 2].astype(I32)
    rank = rec_t[R_RANK:R_RANK + 2].astype(I32)
    counts = cnt[0, :MOE_EXPERTS].astype(I32)
    padded = (counts + rb - 1) // rb * rb
    pad_ends = jnp.cumsum(padded)
    pad_starts = pad_ends - padded
    experts = jnp.arange(MOE_EXPERTS, dtype=I32)[:, None, None]
    dest = jnp.sum(jnp.where(eid[None] == experts, pad_starts[:, None, None], 0), axis=0) + rank
    dest = dest.reshape(2, n_tiles, TM).transpose(1, 0, 2).reshape(n_tiles, 2 * TM)
    dest = jnp.pad(dest, ((0, 0), (0, DEST_ROW - 2 * TM)))
    n_blocks = -(-(2 * t_tok) // rb) + MOE_EXPERTS
    cap = n_blocks * rb
    used = pad_ends[-1] // rb
    idx = jnp.arange(n_blocks, dtype=I32)
    block_src = jnp.clip(idx, 0, jnp.maximum(used - 1, 0))
    nonempty = counts > 0
    e_ids = jnp.arange(MOE_EXPERTS, dtype=I32)
    later = jnp.where((e_ids[None, :] > e_ids[:, None]) & nonempty[None, :], e_ids[None, :], MOE_EXPERTS)
    nxt_of_e = jnp.min(later, axis=1)
    nxt_of_e = jnp.where(nxt_of_e == MOE_EXPERTS, -1, nxt_of_e)
    grp_of_e = jnp.cumsum(nonempty) - nonempty
    start = idx * rb
    owner = ((pad_starts[None, :] <= start[:, None]) & (start[:, None] < pad_ends[None, :])).astype(F32)
    per_e = jnp.stack([e_ids, pad_starts + counts, grp_of_e, nxt_of_e], axis=1).astype(F32)
    per_b = jnp.dot(owner, per_e, precision=HIGHEST).astype(I32)
    block_e, block_grp, block_nxt = per_b[:, 0], per_b[:, 2], per_b[:, 3]
    block_nv = jnp.where(idx < used, jnp.clip(per_b[:, 1] - start, 0, rb), 0)

    fill = jnp.concatenate([pad_starts + counts, pad_ends[-1:]]).astype(I32)
    plen = jnp.concatenate([padded - counts, cap - pad_ends[-1:]]).astype(I32)
    xs = _dispatch(lay, fill, plen, dest, x, mod_l, g2, cap)
    ys = _experts(layer, block_e, block_nv, block_src, block_grp, block_nxt, xs, wg, wu, wd)
    return _combine(lay, n_tiles, dest, x, rec, mod_l, ys)


def _rope_tables(lay):
    half = ATT_HEAD_DIM // 2
    pos = jnp.arange(lay.seq)
    rowp = (pos // GRID_W).astype(F32)
    colp = (pos % GRID_W).astype(F32)
    freqs = ROPE_THETA ** (-jnp.arange(0, half, 2, dtype=F32) / half)
    ang = jnp.concatenate([rowp[:, None] * freqs, colp[:, None] * freqs], axis=-1)
    cos = jnp.concatenate([jnp.cos(ang), jnp.ones((lay.n_ctx, half), F32)], axis=0)
    sin = jnp.concatenate([jnp.sin(ang), jnp.zeros((lay.n_ctx, half), F32)], axis=0)
    cos_d = jnp.repeat(cos, 2, axis=1)
    sin_d = jnp.repeat(sin, 2, axis=1) * jnp.tile(jnp.asarray([-1.0, 1.0], F32), half)
    reps = LANES // ATT_HEAD_DIM
    return jnp.tile(cos_d, (1, reps)), jnp.tile(sin_d, (1, reps)), cos_d.T, sin_d.T


def kernel(x, c, ctx, c_ctx, mod_w, mod_b, norm1_g, norm2_g, ssd_w_in, ssd_conv_w, ssd_conv_b, ssd_dt_bias,
           ssd_a_log, ssd_d, ssd_norm_g, ssd_w_out, att_w_qkv, att_q_gain, att_k_gain, att_w_o, moe_w_group,
           moe_b_group, moe_w_router, moe_b_router, moe_w_gate, moe_w_up, moe_w_down):
    b, n_lat, d = x.shape
    n_ctx = ctx.shape[1]
    depth = mod_w.shape[0]
    lay = _Layout(b, n_lat, n_ctx)
    assert b + 1 <= 8

    xs = jnp.concatenate([x.reshape(lay.t_lat, d), ctx.reshape(lay.t_ctx, d)], axis=0)
    c8 = jnp.concatenate([c, c_ctx[None, :], jnp.zeros((8 - b - 1, d), F32)], axis=0)
    mods = _mod_table(c8, mod_w, mod_b).reshape(depth, 8, N_MOD, d)

    nh = ssd_dt_bias.shape[2]
    di = nh * SSD_HEAD_DIM
    gn = SSD_N_GROUPS * SSD_D_STATE
    dc = di + 2 * gn
    expand = jnp.tile(jnp.repeat(jnp.eye(nh // SSD_N_GROUPS, dtype=BF16), SSD_HEAD_DIM, axis=1), (3, 1))

    hq = att_w_qkv.shape[2] // ATT_HEAD_DIM - 2 * ATT_KV_HEADS
    dq = hq * ATT_HEAD_DIM
    dkv = ATT_KV_HEADS * ATT_HEAD_DIM
    gsum = np.zeros((dq, LANES), np.float32)
    gsum[np.arange(dq), np.arange(dq) // ATT_HEAD_DIM] = 1.0
    gexp = np.concatenate([gsum.T, gsum.T], axis=0)
    vone = np.zeros((1, ATT_KV_HEADS * LANES), np.float32)
    vone[0, np.arange(ATT_KV_HEADS) * LANES + ATT_HEAD_DIM] = 1.0
    cos128, sin128, cos_t, sin_t = _rope_tables(lay)

    for layer in range(depth):
        last = layer == depth - 1
        mod_l = mods[layer]
        j = layer // 2
        g1 = norm1_g[layer][None, :]
        g2 = norm2_g[layer][None, :]
        n_tiles = lay.lat_tiles if last else lay.n_tiles
        wr, br = _router_weights(d, moe_w_group[layer], moe_b_group[layer], moe_w_router[layer], moe_b_router[layer])
        if layer % 2 == 0:
            w_in = ssd_w_in[j].astype(BF16)
            wz, wx, wdt = w_in[:, :di], w_in[:, di:di + dc], w_in[:, di + dc:]
            dtb = ssd_dt_bias[j].reshape(1, 2 * nh)
            z, xbc, dt, dtt = _ssd_inproj(lay, xs, mod_l, g1, wz, wx, wdt, wdt.T, dtb, dtb.T)
            xc, bm, cm = _ssd_conv(lay, xbc, ssd_conv_w[j], ssd_conv_b[j][None, :], di, gn)
            yf, yb = _ssd_scan(lay, xc, bm, cm, dt, dtt, ssd_a_log[j], expand)
            dexp = jnp.repeat(ssd_d[j], SSD_HEAD_DIM)[None, :]
            xs, rec, rec_t, cnt = _ssd_outproj(lay, n_tiles, yf, yb, xc, z, dexp, ssd_norm_g[j][None, :],
                                               ssd_w_out[j].astype(BF16), xs, mod_l, g2, wr, br)
        else:
            w = att_w_qkv[j]
            wq = w[:, :dq].astype(BF16)
            wkt = w[:, dq:dq + dkv].T.astype(BF16)
            wv = w[:, dq + dkv:].reshape(d, ATT_KV_HEADS, ATT_HEAD_DIM)
            wv2 = jnp.pad(wv, ((0, 0), (0, 0), (0, LANES - ATT_HEAD_DIM))).reshape(d, ATT_KV_HEADS * LANES).astype(BF16)
            qg = (jnp.tile(att_q_gain[j], hq) * (ATT_HEAD_DIM ** -0.5 * LOG2E))[None, :]
            kgt = att_k_gain[j][:, None]
            bound = (1.01 * ATT_HEAD_DIM * jnp.max(jnp.abs(qg)) * jnp.max(jnp.abs(kgt))).reshape(1, 1)
            qn, kt, v = _att_inproj(lay, xs, mod_l, g1, wq, wkt, wv2, jnp.asarray(vone), qg, kgt, bound, jnp.asarray(gsum, BF16),
                                    jnp.asarray(gexp, BF16), cos128, sin128, cos_t, sin_t)
            o = _att_core(lay, qn, kt, v)
            xs, rec, rec_t, cnt = _att_outproj(lay, n_tiles, o, att_w_o[j].astype(BF16), xs, mod_l, g2, wr, br)
        xs = _moe(lay, layer, n_tiles, xs, mod_l, g2, rec, rec_t, cnt, moe_w_gate, moe_w_up, moe_w_down)
    return xs[:lay.t_lat].reshape(b, n_lat, d)
```

```python
import functools

import numpy as np
import jax
import jax.numpy as jnp
from jax import lax
from jax.experimental import pallas as pl
from jax.experimental.pallas import tpu as pltpu

F32 = jnp.float32
BF16 = jnp.bfloat16
I32 = jnp.int32
HIGHEST = lax.Precision.HIGHEST

NORM_EPS = 1e-6
N_MOD = 6
GRID_W = 64
ROPE_THETA = 10000.0

SSD_HEAD_DIM = 64
SSD_N_GROUPS = 4
SSD_D_STATE = 128
SSD_CONV = 5
SSD_CHUNK = 128

ATT_HEAD_DIM = 64
ATT_KV_HEADS = 4
ATT_KEY_BLOCK = 4352
ATT_KV_PER_STEP = 2
LOG2E = 1.4426950408889634
ATT_MIN_DENOM = 2.0 ** -60

MOE_GROUPS = 4
MOE_EPG = 8
MOE_EXPERTS = MOE_GROUPS * MOE_EPG
MOE_ROW_BLOCK = 256
ZROWS = MOE_ROW_BLOCK // 2
DEST_ROW = 1024

TM = 256
LANES = 128
SUBLANES = 8
HALO = 16
VMEM_LIMIT = 56 * 1024 * 1024


def _cparams(sem):
    return pltpu.CompilerParams(dimension_semantics=sem, vmem_limit_bytes=VMEM_LIMIT)


def _silu(v):
    return v / (1.0 + jnp.exp(-v))


def _softplus(v):
    return jnp.maximum(v, 0.0) + jnp.log1p(jnp.exp(-jnp.abs(v)))


def _norm_mod(x, g, shift, scale):
    ms = jnp.mean(x * x, axis=-1, keepdims=True)
    y = x * lax.rsqrt(ms + NORM_EPS) * g
    return y * (1.0 + scale) + shift


class _Layout:
    def __init__(self, batch, seq, n_ctx):
        assert seq % TM == 0 and n_ctx % TM == 0
        self.batch, self.seq, self.n_ctx = batch, seq, n_ctx
        self.t_lat = batch * seq
        self.t_ctx = batch * n_ctx
        self.t = self.t_lat + self.t_ctx
        self.lat_tiles = self.t_lat // TM
        self.seq_tiles = seq // TM
        self.ctx_tiles = n_ctx // TM
        self.n_tiles = self.t // TM
        self.nk = seq + n_ctx

    def mod_row(self, i):
        return jnp.where(i < self.lat_tiles, i // self.seq_tiles, self.batch)

    def tile_batch(self, i):
        return jnp.where(i < self.lat_tiles, i // self.seq_tiles, (i - self.lat_tiles) // self.ctx_tiles)

    def tile_pos(self, i):
        return jnp.where(i < self.lat_tiles, i % self.seq_tiles,
                         self.seq_tiles + (i - self.lat_tiles) % self.ctx_tiles)

    def seg_first(self, i):
        return jnp.where(i < self.lat_tiles, i % self.seq_tiles == 0, (i - self.lat_tiles) % self.ctx_tiles == 0)

    def seg_last(self, i):
        return jnp.where(i < self.lat_tiles, i % self.seq_tiles == self.seq_tiles - 1,
                         (i - self.lat_tiles) % self.ctx_tiles == self.ctx_tiles - 1)


def _mod_kernel(c_ref, w_ref, b_ref, o_ref):
    s = _silu(c_ref[...])
    o_ref[...] = jnp.dot(s, w_ref[...], precision=HIGHEST, preferred_element_type=F32) + b_ref[...]


def _mod_table(c8, mod_w, mod_b):
    depth, d, n = mod_w.shape
    tn = 1536
    assert n % tn == 0
    return pl.pallas_call(
        _mod_kernel,
        grid=(depth, n // tn),
        in_specs=[pl.BlockSpec((8, d), lambda l, j: (0, 0)),
                  pl.BlockSpec((None, d, tn), lambda l, j: (l, 0, j)),
                  pl.BlockSpec((None, 1, tn), lambda l, j: (l, 0, j))],
        out_specs=pl.BlockSpec((None, 8, tn), lambda l, j: (l, 0, j)),
        out_shape=jax.ShapeDtypeStruct((depth, 8, n), F32),
        compiler_params=_cparams(("parallel", "parallel")),
        name="mod_table",
    )(c8, mod_w, mod_b.reshape(depth, 1, n))


def _ssd_inproj_kernel(x_ref, mod_ref, g_ref, wz_ref, wx_ref, wdt_ref, wdtt_ref, dtb_ref, dtbt_ref,
                       z_ref, xbc_ref, dt_ref, dtt_ref):
    h = _norm_mod(x_ref[...], g_ref[...], mod_ref[0:1, :], mod_ref[1:2, :]).astype(BF16)
    z_ref[...] = jnp.dot(h, wz_ref[...], preferred_element_type=F32).astype(BF16)
    xbc_ref[...] = jnp.dot(h, wx_ref[...], preferred_element_type=F32).astype(BF16)
    nh = dtb_ref.shape[1] // 2
    dt = _softplus(jnp.dot(h, wdt_ref[...], preferred_element_type=F32) + dtb_ref[...])
    dt_ref[0] = dt[:, :nh]
    dt_ref[1] = dt[:, nh:]
    dtt = lax.dot_general(wdtt_ref[...], h, (((1,), (1,)), ((), ())), preferred_element_type=F32)
    dtt = _softplus(dtt + dtbt_ref[...])
    dtt_ref[0] = dtt[:nh, :]
    dtt_ref[1] = dtt[nh:, :]


def _ssd_inproj(lay, x, mod_l, g, wz, wx, wdt, wdtt, dtb, dtbt):
    t, d = x.shape
    di, dc, nh2 = wz.shape[1], wx.shape[1], wdt.shape[1]
    nh = nh2 // 2
    full = lambda a: pl.BlockSpec(a.shape, lambda i: (0,) * a.ndim)
    return pl.pallas_call(
        _ssd_inproj_kernel,
        grid=(lay.n_tiles,),
        in_specs=[pl.BlockSpec((TM, d), lambda i: (i, 0)),
                  pl.BlockSpec((None, N_MOD, d), lambda i: (lay.mod_row(i), 0, 0)),
                  full(g), full(wz), full(wx), full(wdt), full(wdtt), full(dtb), full(dtbt)],
        out_specs=[pl.BlockSpec((TM, di), lambda i: (i, 0)),
                   pl.BlockSpec((TM, dc), lambda i: (i, 0)),
                   pl.BlockSpec((2, TM, nh), lambda i: (0, i, 0)),
                   pl.BlockSpec((2, nh, TM), lambda i: (0, 0, i))],
        out_shape=[jax.ShapeDtypeStruct((t, di), BF16),
                   jax.ShapeDtypeStruct((t, dc), BF16),
                   jax.ShapeDtypeStruct((2, t, nh), F32),
                   jax.ShapeDtypeStruct((2, nh, t), F32)],
        compiler_params=_cparams(("parallel",)),
        name="ssd_inproj",
    )(x, mod_l, g, wz, wx, wdt, wdtt, dtb, dtbt)


def _ssd_conv_kernel(lay, di, gn, prev_ref, main_ref, next_ref, w_ref, b_ref, sh_ref, sht_ref, shb_ref,
                     xs_ref, bm_ref, cm_ref):
    i = pl.program_id(0)
    dc = main_ref.shape[1]
    half = SSD_CONV // 2
    zero_halo = jnp.zeros((HALO, dc), BF16)
    prev = jnp.where(lay.seg_first(i), zero_halo, prev_ref[...])
    nxt = jnp.where(lay.seg_last(i), zero_halo, next_ref[...])
    cw = 256
    for c0 in range(0, dc, cw):
        u = main_ref[:, c0:c0 + cw]
        acc = b_ref[:, c0:c0 + cw] + u.astype(F32) * w_ref[half:half + 1, c0:c0 + cw]
        top = jnp.zeros((SUBLANES, cw), F32)
        bot = jnp.zeros((SUBLANES, cw), F32)
        for k in range(SSD_CONV):
            if k == half:
                continue
            wk = w_ref[k:k + 1, c0:c0 + cw]
            acc = acc + jnp.dot(sh_ref[k], u, preferred_element_type=F32) * wk
            if k < half:
                top = top + jnp.dot(sht_ref[k], prev[:, c0:c0 + cw], preferred_element_type=F32)[:SUBLANES] * wk
            else:
                bot = bot + jnp.dot(shb_ref[k], nxt[:, c0:c0 + cw], preferred_element_type=F32)[:SUBLANES] * wk
        acc = jnp.concatenate([acc[:SUBLANES] + top, acc[SUBLANES:TM - SUBLANES], acc[TM - SUBLANES:] + bot], axis=0)
        y = _silu(acc).astype(BF16)
        if c0 < di:
            xs_ref[:, c0:c0 + cw] = y
        elif c0 < di + gn:
            bm_ref[:, c0 - di:c0 - di + cw] = y
        else:
            cm_ref[:, c0 - di - gn:c0 - di - gn + cw] = y


def _conv_shift_matrices():
    half = SSD_CONV // 2
    sh = np.zeros((SSD_CONV, TM, TM), np.float32)
    sht = np.zeros((SSD_CONV, HALO, HALO), np.float32)
    shb = np.zeros((SSD_CONV, HALO, HALO), np.float32)
    for k in range(SSD_CONV):
        s = k - half
        sh[k] = np.eye(TM, k=s)
        for r in range(SUBLANES):
            if r + s < 0:
                sht[k, r, HALO + r + s] = 1.0
            if r - SUBLANES + s >= 0:
                shb[k, r, r - SUBLANES + s] = 1.0
    return jnp.asarray(sh, BF16), jnp.asarray(sht, BF16), jnp.asarray(shb, BF16)


def _ssd_conv(lay, xbc, conv_w, conv_b, di, gn):
    t, dc = xbc.shape
    sh, sht, shb = _conv_shift_matrices()
    hb = TM // HALO
    nhb = t // HALO
    assert gn == 512 and di % 512 == 0
    return pl.pallas_call(
        functools.partial(_ssd_conv_kernel, lay, di, gn),
        grid=(lay.n_tiles,),
        in_specs=[pl.BlockSpec((HALO, dc), lambda i: (jnp.maximum(i * hb - 1, 0), 0)),
                  pl.BlockSpec((TM, dc), lambda i: (i, 0)),
                  pl.BlockSpec((HALO, dc), lambda i: (jnp.minimum((i + 1) * hb, nhb - 1), 0)),
                  pl.BlockSpec(conv_w.shape, lambda i: (0, 0)),
                  pl.BlockSpec(conv_b.shape, lambda i: (0, 0)),
                  pl.BlockSpec(sh.shape, lambda i: (0, 0, 0)),
                  pl.BlockSpec(sht.shape, lambda i: (0, 0, 0)),
                  pl.BlockSpec(shb.shape, lambda i: (0, 0, 0))],
        out_specs=[pl.BlockSpec((TM, di), lambda i: (i, 0)),
                   pl.BlockSpec((TM, gn), lambda i: (i, 0)),
                   pl.BlockSpec((TM, gn), lambda i: (i, 0))],
        out_shape=[jax.ShapeDtypeStruct((t, di), BF16),
                   jax.ShapeDtypeStruct((t, gn), BF16),
                   jax.ShapeDtypeStruct((t, gn), BF16)],
        compiler_params=_cparams(("parallel",)),
        name="ssd_conv",
    )(xbc, xbc, xbc, conv_w, conv_b, sh, sht, shb)


def _ssd_scan_kernel(xf_ref, bf_ref, cf_ref, dtf_ref, dttf_ref, xb_ref, bb_ref, cb_ref, dtb_ref, dttb_ref,
                     alr_ref, alc_ref, e_ref, yf_ref, yb_ref, state_ref):
    step = pl.program_id(1)

    @pl.when(step == 0)
    def _():
        state_ref[...] = jnp.zeros_like(state_ref)

    _scan_chunk(0, step, xf_ref, bf_ref, cf_ref, dtf_ref, dttf_ref, alr_ref.at[0], alc_ref.at[0], e_ref,
                yf_ref, state_ref.at[0])
    _scan_chunk(1, step, xb_ref, bb_ref, cb_ref, dtb_ref, dttb_ref, alr_ref.at[1], alc_ref.at[1], e_ref,
                yb_ref, state_ref.at[1])


def _scan_chunk(d, step, xs_ref, bm_ref, cm_ref, dt_ref, dtt_ref, alr_ref, alc_ref, e_ref, y_ref, state_ref):
    q = SSD_CHUNK
    ng = SSD_N_GROUPS
    ns = SSD_D_STATE
    gw = xs_ref.shape[1] // ng
    hpg = gw // SSD_HEAD_DIM

    row = lax.broadcasted_iota(I32, (q, q), 0)
    col = lax.broadcasted_iota(I32, (q, q), 1)
    lmask = row >= col if d == 0 else row <= col
    lmask_t = col >= row if d == 0 else col <= row

    dt = dt_ref[...]
    dtt = dtt_ref[...]
    da = dt * (-jnp.exp(alr_ref[...]))
    dat = dtt * (-jnp.exp(alc_ref[...]))
    a_cum = jnp.dot(lmask.astype(F32), da, precision=HIGHEST, preferred_element_type=F32)
    a_cum_t = jnp.dot(dat, lmask_t.astype(F32), precision=HIGHEST, preferred_element_type=F32)
    a_end = jnp.sum(da, axis=0, keepdims=True)
    w_end = jnp.exp(a_end - a_cum) * dt

    e3 = e_ref[...]

    def expand(v):
        hi = v.astype(BF16)
        r1 = v - hi.astype(F32)
        mid = r1.astype(BF16)
        lo = (r1 - mid.astype(F32)).astype(BF16)
        return jnp.dot(jnp.concatenate([hi, mid, lo], axis=1), e3, preferred_element_type=F32)

    a_end8 = jnp.broadcast_to(a_end, (SUBLANES, a_end.shape[1]))
    lane = lax.broadcasted_iota(I32, (q, LANES), 1)
    for g in range(ng):
        hs = slice(g * hpg, (g + 1) * hpg)
        decay_in = jnp.exp(expand(a_cum[:, hs]))
        chunk_decay = jnp.exp(expand(a_end8[:, hs])[0:1, :])
        xw = (xs_ref[:, g * gw:(g + 1) * gw].astype(F32) * expand(w_end[:, hs])).astype(BF16)
        bg = bm_ref[:, g * ns:(g + 1) * ns]
        cg = cm_ref[:, g * ns:(g + 1) * ns]
        cb = lax.dot_general(cg, bg, (((1,), (1,)), ((), ())), preferred_element_type=F32)
        cb = jnp.where(lmask, cb, 0.0)
        s_in = state_ref[g]
        y_off = jnp.dot(cg, s_in.astype(BF16), preferred_element_type=F32)
        new_s = lax.dot_general(bg, xw, (((0,), (0,)), ((), ())), preferred_element_type=F32)
        state_ref[g] = s_in * chunk_decay + new_s
        for pr in range(hpg // 2):
            mixes = []
            for hh in range(2):
                h = g * hpg + 2 * pr + hh
                seg = a_cum[:, h:h + 1] - a_cum_t[h:h + 1, :]
                lh = jnp.exp(jnp.minimum(seg, 0.0))
                mixes.append((cb * lh * dtt[h:h + 1, :]).astype(BF16))
            lhs = jnp.concatenate(mixes, axis=1)
            l0 = g * gw + pr * LANES
            xp = xs_ref[:, l0:l0 + LANES]
            zero = jnp.zeros_like(xp)
            rhs = jnp.concatenate([jnp.where(lane < SSD_HEAD_DIM, xp, zero),
                                   jnp.where(lane >= SSD_HEAD_DIM, xp, zero)], axis=0)
            y_pair = jnp.dot(lhs, rhs, preferred_element_type=F32)
            y_pair = y_pair + y_off[:, pr * LANES:(pr + 1) * LANES] * decay_in[:, pr * LANES:(pr + 1) * LANES]
            y_ref[:, l0:l0 + LANES] = y_pair.astype(BF16)


def _ssd_scan(lay, xs, bm, cm, dt, dtt, a_log, expand):
    t, di = xs.shape
    gn = bm.shape[1]
    nh = dt.shape[2]
    q = SSD_CHUNK
    nct, nlt = lay.n_ctx // q, lay.seq // q
    nc = nct + nlt
    ctx_base = lay.t_lat // q

    def blk(d, b, s):
        j_ctx = jnp.where(d == 0, s, nct - 1 - s)
        j_lat = jnp.where(d == 0, s - nct, nlt - 1 - (s - nct))
        return jnp.where(s < nct, ctx_base + b * nct + j_ctx, b * nlt + j_lat)

    def chunk_specs(d):
        return [pl.BlockSpec((q, di), lambda b, s: (blk(d, b, s), 0)),
                pl.BlockSpec((q, gn), lambda b, s: (blk(d, b, s), 0)),
                pl.BlockSpec((q, gn), lambda b, s: (blk(d, b, s), 0)),
                pl.BlockSpec((None, q, nh), lambda b, s: (d, blk(d, b, s), 0)),
                pl.BlockSpec((None, nh, q), lambda b, s: (d, 0, blk(d, b, s)))]

    return pl.pallas_call(
        _ssd_scan_kernel,
        grid=(lay.batch, nc),
        in_specs=chunk_specs(0) + chunk_specs(1) + [
            pl.BlockSpec((2, 1, nh), lambda b, s: (0, 0, 0)),
            pl.BlockSpec((2, nh, 1), lambda b, s: (0, 0, 0)),
            pl.BlockSpec(expand.shape, lambda b, s: (0, 0))],
        out_specs=[pl.BlockSpec((q, di), lambda b, s: (blk(0, b, s), 0)),
                   pl.BlockSpec((q, di), lambda b, s: (blk(1, b, s), 0))],
        out_shape=[jax.ShapeDtypeStruct((t, di), BF16), jax.ShapeDtypeStruct((t, di), BF16)],
        scratch_shapes=[pltpu.VMEM((2, SSD_N_GROUPS, SSD_D_STATE, di // SSD_N_GROUPS), F32)],
        compiler_params=_cparams(("arbitrary", "arbitrary")),
        name="ssd_scan",
    )(xs, bm, cm, dt, dtt, xs, bm, cm, dt, dtt, a_log.reshape(2, 1, nh), a_log.reshape(2, nh, 1), expand)


def _ssd_outproj_kernel(n_route, yf_ref, yb_ref, xs_ref, z_ref, dexp_ref, ng_ref, w_ref, x_ref, mod_ref,
                        g2_ref, wr_ref, br_ref, o_ref, r_ref, rt_ref, cnt_ref, base_ref):
    gw = xs_ref.shape[1] // SSD_N_GROUPS
    parts = []
    for g in range(SSD_N_GROUPS):
        gs = slice(g * gw, (g + 1) * gw)
        yg = yf_ref[:, gs].astype(F32) + yb_ref[:, gs].astype(F32) + xs_ref[:, gs].astype(F32) * dexp_ref[:, gs]
        yg = yg * _silu(z_ref[:, gs].astype(F32))
        ms = jnp.mean(yg * yg, axis=-1, keepdims=True)
        parts.append((yg * lax.rsqrt(ms + NORM_EPS) * ng_ref[:, gs]).astype(BF16))
    yn = jnp.concatenate(parts, axis=1)
    m = jnp.dot(yn, w_ref[...], preferred_element_type=F32)
    x_new = x_ref[...] + mod_ref[2:3, :] * m
    o_ref[...] = x_new
    _route(pl.program_id(0), n_route, x_new, mod_ref, g2_ref, wr_ref, br_ref, r_ref, rt_ref, cnt_ref, base_ref)


def _ssd_outproj(lay, n_route, yf, yb, xs, z, dexp, ng, w, x, mod_l, g2, wr, br):
    t, d = x.shape
    di = xs.shape[1]
    full = lambda a: pl.BlockSpec(a.shape, lambda i: (0,) * a.ndim)
    r_in, r_out, r_shape, r_scratch = _route_specs(lay, g2, wr, br)
    return pl.pallas_call(
        functools.partial(_ssd_outproj_kernel, n_route),
        grid=(lay.n_tiles,),
        in_specs=[pl.BlockSpec((TM, di), lambda i: (i, 0)),
                  pl.BlockSpec((TM, di), lambda i: (i, 0)),
                  pl.BlockSpec((TM, di), lambda i: (i, 0)),
                  pl.BlockSpec((TM, di), lambda i: (i, 0)),
                  full(dexp), full(ng), full(w),
                  pl.BlockSpec((TM, d), lambda i: (i, 0)),
                  pl.BlockSpec((None, N_MOD, d), lambda i: (lay.mod_row(i), 0, 0))] + r_in,
        out_specs=[pl.BlockSpec((TM, d), lambda i: (i, 0))] + r_out,
        out_shape=[jax.ShapeDtypeStruct((t, d), F32)] + r_shape,
        scratch_shapes=r_scratch,
        compiler_params=_cparams(("arbitrary",)),
        name="ssd_outproj",
    )(yf, yb, xs, z, dexp, ng, w, x, mod_l, g2, wr, br)


def _att_inproj_kernel(x_ref, mod_ref, g_ref, wq_ref, wkt_ref, wv_ref, vone_ref, qg_ref, kgt_ref, bound_ref, gsum_ref, gexp_ref,
                       cos_ref, sin_ref, cost_ref, sint_ref, q_ref, kt_ref, v_ref):
    h = _norm_mod(x_ref[...], g_ref[...], mod_ref[0:1, :], mod_ref[1:2, :]).astype(BF16)
    hd = ATT_HEAD_DIM
    q = jnp.dot(h, wq_ref[...], preferred_element_type=F32)
    ssum = jnp.dot((q * q).astype(BF16), gsum_ref[...], preferred_element_type=F32)
    r = lax.rsqrt(ssum * (1.0 / hd) + NORM_EPS)
    r_hi = r.astype(BF16)
    r_lo = (r - r_hi.astype(F32)).astype(BF16)
    r_exp = jnp.dot(jnp.concatenate([r_hi, r_lo], axis=1), gexp_ref[...], preferred_element_type=F32)
    qn = q * r_exp * qg_ref[...]
    nl = qn.shape[1]
    lane = lax.broadcasted_iota(I32, qn.shape, 1)
    partner = jnp.where((lane & 1) == 0, pltpu.roll(qn, nl - 1, 1), pltpu.roll(qn, 1, 1))
    reps = nl // LANES
    cos = jnp.concatenate([cos_ref[...]] * reps, axis=1)
    sin = jnp.concatenate([sin_ref[...]] * reps, axis=1)
    q_ref[...] = (qn * cos + partner * sin).astype(BF16)
    kt = lax.dot_general(wkt_ref[...], h, (((1,), (1,)), ((), ())), preferred_element_type=F32)
    ct, st = cost_ref[...], sint_ref[...]
    sub = lax.broadcasted_iota(I32, ct.shape, 0)
    for kh in range(ATT_KV_HEADS):
        blk = kt[kh * hd:(kh + 1) * hd]
        rk = lax.rsqrt(jnp.sum(blk * blk, axis=0, keepdims=True) * (1.0 / hd) + NORM_EPS)
        kn = blk * rk * kgt_ref[...]
        kpart = jnp.where((sub & 1) == 0, pltpu.roll(kn, hd - 1, 0), pltpu.roll(kn, 1, 0))
        kt_ref[kh, 0:hd, :] = (kn * ct + kpart * st).astype(BF16)
        kt_ref[kh, hd:2 * hd, :] = jnp.where(sub == 0, -bound_ref[...], 0.0).astype(BF16)
    v = (jnp.dot(h, wv_ref[...], preferred_element_type=F32) + vone_ref[...]).astype(BF16)
    for kh in range(ATT_KV_HEADS):
        v_ref[kh] = v[:, kh * LANES:(kh + 1) * LANES]


def _att_inproj(lay, x, mod_l, g, wq, wkt, wv2, vone, qg, kgt, bound, gsum, gexp, cos, sin, cost, sint):
    t, d = x.shape
    dq = wq.shape[1]
    hd = ATT_HEAD_DIM
    full = lambda a: pl.BlockSpec(a.shape, lambda i: (0,) * a.ndim)
    return pl.pallas_call(
        _att_inproj_kernel,
        grid=(lay.n_tiles,),
        in_specs=[pl.BlockSpec((TM, d), lambda i: (i, 0)),
                  pl.BlockSpec((None, N_MOD, d), lambda i: (lay.mod_row(i), 0, 0)),
                  full(g), full(wq), full(wkt), full(wv2), full(vone), full(qg), full(kgt), full(bound), full(gsum), full(gexp),
                  pl.BlockSpec((TM, LANES), lambda i: (lay.tile_pos(i), 0)),
                  pl.BlockSpec((TM, LANES), lambda i: (lay.tile_pos(i), 0)),
                  pl.BlockSpec((hd, TM), lambda i: (0, lay.tile_pos(i))),
                  pl.BlockSpec((hd, TM), lambda i: (0, lay.tile_pos(i)))],
        out_specs=[pl.BlockSpec((TM, dq), lambda i: (i, 0)),
                   pl.BlockSpec((None, ATT_KV_HEADS, 2 * hd, TM), lambda i: (lay.tile_batch(i), 0, 0, lay.tile_pos(i))),
                   pl.BlockSpec((None, ATT_KV_HEADS, TM, LANES), lambda i: (lay.tile_batch(i), 0, lay.tile_pos(i), 0))],
        out_shape=[jax.ShapeDtypeStruct((t, dq), BF16),
                   jax.ShapeDtypeStruct((lay.batch, ATT_KV_HEADS, 2 * hd, lay.nk), BF16),
                   jax.ShapeDtypeStruct((lay.batch, ATT_KV_HEADS, lay.nk, LANES), BF16)],
        compiler_params=_cparams(("parallel",)),
        name="att_inproj",
    )(x, mod_l, g, wq, wkt, wv2, vone, qg, kgt, bound, gsum, gexp, cos, sin, cost, sint)


def _attend(q_ref, kt_ref, v_ref, o_ref, k0, n_keys):
    hd = ATT_HEAD_DIM
    tm = q_ref.shape[0]
    heads = q_ref.shape[1] // hd
    qs = [q_ref[:, u * hd:(u + 1) * hd] for u in range(heads)]

    bk = min(ATT_KEY_BLOCK, n_keys)
    assert n_keys % bk == 0 and k0 % LANES == 0 and bk % LANES == 0

    def body(j, carry):
        start = pl.multiple_of(k0 + j * bk, LANES)
        kb = kt_ref[0:hd, pl.ds(start, bk)]
        vb = v_ref[pl.ds(start, bk), :]
        out = []
        for u in range(heads):
            m_old, acc = carry[u]
            s = jnp.dot(qs[u], kb, preferred_element_type=F32)
            m_new = jnp.maximum(m_old, jnp.max(s, axis=-1, keepdims=True))
            p = jnp.exp2((s - m_new).astype(BF16))
            acc = jnp.exp2(m_old - m_new) * acc + jnp.dot(p, vb, preferred_element_type=F32)
            out.append((m_new, acc))
        return tuple(out)

    init = tuple((jnp.full((tm, 1), -jnp.inf, F32), jnp.zeros((tm, LANES), F32)) for _ in range(heads))
    final = lax.fori_loop(0, n_keys // bk, body, init) if n_keys > bk else body(0, init)
    outs = [acc[:, :hd] / acc[:, hd:hd + 1] for _, acc in final]
    o_ref[...] = jnp.concatenate(outs, axis=1).astype(BF16)


def _attend_bounded(q_ref, kt_ref, v_ref, o_ref):
    hd = ATT_HEAD_DIM
    tm = q_ref.shape[0]
    k_lo = kt_ref[...]
    k_hi = jnp.concatenate([k_lo[hd:], k_lo[:hd]], axis=0)
    vb = v_ref[...]
    lane = lax.broadcasted_iota(I32, (tm, LANES), 1)
    outs, dens = [], []
    for pr in range(q_ref.shape[1] // LANES):
        qp = q_ref[:, pr * LANES:(pr + 1) * LANES]
        q_lo = jnp.where(lane < hd, qp, (lane == hd).astype(BF16))
        q_hi = jnp.where(lane >= hd, qp, (lane == 0).astype(BF16))
        for qa, ka in ((q_lo, k_lo), (q_hi, k_hi)):
            s = jnp.dot(qa, ka, preferred_element_type=F32)
            acc = jnp.dot(jnp.exp2(s).astype(BF16), vb, preferred_element_type=F32)
            den = acc[:, hd:hd + 1]
            outs.append(acc[:, :hd] / den)
            dens.append(den)
    o_ref[...] = jnp.concatenate(outs, axis=1).astype(BF16)
    return jnp.min(jnp.concatenate(dens, axis=1))


def _att_core_kernel(lay, q_ref, kt_ref, v_ref, o_ref):
    qi = pl.program_id(2)
    qw = q_ref.shape[1] // ATT_KV_PER_STEP
    heads = [(q_ref.at[:, j * qw:(j + 1) * qw], kt_ref.at[j], v_ref.at[j], o_ref.at[:, j * qw:(j + 1) * qw])
             for j in range(ATT_KV_PER_STEP)]

    @pl.when(qi < lay.seq_tiles)
    def _():
        smallest = [_attend_bounded(*h) for h in heads]
        for h, small in zip(heads, smallest):
            @pl.when(jnp.logical_not(small > ATT_MIN_DENOM))
            def _():
                _attend(*h, 0, lay.nk)

    @pl.when(qi >= lay.seq_tiles)
    def _():
        for h in heads:
            _attend(*h, lay.seq, lay.n_ctx)


def _att_core(lay, qn, kt, v):
    t, dq = qn.shape
    hd = ATT_HEAD_DIM
    kvs = ATT_KV_PER_STEP
    qw = dq // ATT_KV_HEADS * kvs
    per_b = lay.seq_tiles + lay.ctx_tiles

    def row_tile(b, qi):
        return jnp.where(qi < lay.seq_tiles, b * lay.seq_tiles + qi,
                         lay.lat_tiles + b * lay.ctx_tiles + (qi - lay.seq_tiles))

    return pl.pallas_call(
        functools.partial(_att_core_kernel, lay),
        grid=(lay.batch, ATT_KV_HEADS // kvs, per_b),
        in_specs=[pl.BlockSpec((TM, qw), lambda b, kh, qi: (row_tile(b, qi), kh)),
                  pl.BlockSpec((None, kvs, 2 * hd, lay.nk), lambda b, kh, qi: (b, kh, 0, 0)),
                  pl.BlockSpec((None, kvs, lay.nk, LANES), lambda b, kh, qi: (b, kh, 0, 0))],
        out_specs=pl.BlockSpec((TM, qw), lambda b, kh, qi: (row_tile(b, qi), kh)),
        out_shape=jax.ShapeDtypeStruct((t, dq), BF16),
        compiler_params=_cparams(("parallel", "parallel", "arbitrary")),
        name="att_core",
    )(qn, kt, v)


def _att_outproj_kernel(n_route, o_ref, w_ref, x_ref, mod_ref, g2_ref, wr_ref, br_ref,
                        out_ref, r_ref, rt_ref, cnt_ref, base_ref):
    m = jnp.dot(o_ref[...], w_ref[...], preferred_element_type=F32)
    x_new = x_ref[...] + mod_ref[2:3, :] * m
    out_ref[...] = x_new
    _route(pl.program_id(0), n_route, x_new, mod_ref, g2_ref, wr_ref, br_ref, r_ref, rt_ref, cnt_ref, base_ref)


def _att_outproj(lay, n_route, o, w, x, mod_l, g2, wr, br):
    t, d = x.shape
    r_in, r_out, r_shape, r_scratch = _route_specs(lay, g2, wr, br)
    return pl.pallas_call(
        functools.partial(_att_outproj_kernel, n_route),
        grid=(lay.n_tiles,),
        in_specs=[pl.BlockSpec((TM, o.shape[1]), lambda i: (i, 0)),
                  pl.BlockSpec(w.shape, lambda i: (0, 0)),
                  pl.BlockSpec((TM, d), lambda i: (i, 0)),
                  pl.BlockSpec((None, N_MOD, d), lambda i: (lay.mod_row(i), 0, 0))] + r_in,
        out_specs=[pl.BlockSpec((TM, d), lambda i: (i, 0))] + r_out,
        out_shape=[jax.ShapeDtypeStruct((t, d), F32)] + r_shape,
        scratch_shapes=r_scratch,
        compiler_params=_cparams(("arbitrary",)),
        name="att_outproj",
    )(o, w, x, mod_l, g2, wr, br)


R_EID, R_RANK, R_W = 0, 2, 4
GROUP_LANE0 = MOE_EXPERTS


def _store_token_rows(ref, val):
    rows, d = val.shape
    for s in range(d // LANES):
        ref[pl.ds(s, rows, stride=d // LANES), :] = val[:, s * LANES:(s + 1) * LANES]


def _load_token_rows(ref, d):
    n = d // LANES
    rows = ref.shape[0] // n
    return jnp.concatenate([ref[pl.ds(s, rows, stride=n), :] for s in range(n)], axis=1)


def _token(ref, r, n):
    return ref.at[pl.ds(pl.multiple_of(r * n, n), n)]


def _route(i, n_route, x, mod_ref, g_ref, wr_ref, br_ref, r_ref, rt_ref, cnt_ref, base_ref):
    @pl.when(i == 0)
    def _():
        base_ref[...] = jnp.zeros_like(base_ref)

    f = _norm_mod(x, g_ref[...], mod_ref[3:4, :], mod_ref[4:5, :])
    f_hi = f.astype(BF16)
    f_lo = (f - f_hi.astype(F32)).astype(BF16)
    part = jnp.dot(f_hi, wr_ref[...], preferred_element_type=F32)
    part_lo = jnp.dot(f_lo, wr_ref[:, :LANES], preferred_element_type=F32)
    logits = part[:, :LANES] + part[:, LANES:] + part_lo + br_ref[...]
    lane = lax.broadcasted_iota(I32, logits.shape, 1)
    neg = jnp.float32(-jnp.inf)
    big = jnp.int32(LANES)

    def first_max(vals):
        top = jnp.max(vals, axis=-1, keepdims=True)
        idx = jnp.min(jnp.where(vals == top, lane, big), axis=-1, keepdims=True)
        return top, idx

    g_mask = (lane >= GROUP_LANE0) & (lane < GROUP_LANE0 + MOE_GROUPS)
    glog = jnp.where(g_mask, logits, neg)
    g_top, g_idx = first_max(glog)
    g_w = 1.0 / jnp.sum(jnp.exp(glog - g_top), axis=-1, keepdims=True)
    e0 = (g_idx - GROUP_LANE0) * MOE_EPG
    elog = jnp.where((lane >= e0) & (lane < e0 + MOE_EPG), logits, neg)
    v1, i1 = first_max(elog)
    v2, i2 = first_max(jnp.where(lane == i1, neg, elog))
    ex = jnp.exp(v2 - v1)
    w1 = g_w / (1.0 + ex)
    w2 = g_w * ex / (1.0 + ex)

    oh1 = (lane == i1).astype(F32)
    oh2 = (lane == i2).astype(F32)
    tm = logits.shape[0]
    rr = lax.broadcasted_iota(I32, (tm, tm), 0)
    cc = lax.broadcasted_iota(I32, (tm, tm), 1)
    before = (cc < rr).astype(BF16)
    cum1 = jnp.dot(before, oh1.astype(BF16), preferred_element_type=F32)
    cum2 = jnp.dot(before, oh2.astype(BF16), preferred_element_type=F32)
    tot1 = jnp.sum(oh1, axis=0, keepdims=True)
    tot2 = jnp.sum(oh2, axis=0, keepdims=True)
    base = base_ref[...]
    rank1 = jnp.sum(oh1 * (cum1 + base), axis=-1, keepdims=True)
    rank2 = jnp.sum(oh2 * (cum2 + base + tot1), axis=-1, keepdims=True)
    base = base + jnp.where(i < n_route, tot1 + tot2, 0.0)
    base_ref[...] = base
    cnt_ref[...] = base

    rec = jnp.zeros(logits.shape, F32)
    for k, val in ((R_EID, i1.astype(F32)), (R_EID + 1, i2.astype(F32)), (R_RANK, rank1), (R_RANK + 1, rank2),
                   (R_W, w1), (R_W + 1, w2)):
        rec = jnp.where(lane == k, val, rec)
    r_ref[...] = rec
    rt_ref[...] = rec.T[:SUBLANES, :]


def _route_specs(lay, g2, wr, br):
    t = lay.n_tiles * TM
    ins = [pl.BlockSpec(g2.shape, lambda i: (0, 0)),
           pl.BlockSpec(wr.shape, lambda i: (0, 0)),
           pl.BlockSpec(br.shape, lambda i: (0, 0))]
    outs = [pl.BlockSpec((TM, LANES), lambda i: (i, 0)),
            pl.BlockSpec((SUBLANES, TM), lambda i: (0, i)),
            pl.BlockSpec((1, LANES), lambda i: (0, 0))]
    shapes = [jax.ShapeDtypeStruct((t, LANES), F32),
              jax.ShapeDtypeStruct((SUBLANES, t), F32),
              jax.ShapeDtypeStruct((1, LANES), F32)]
    return ins, outs, shapes, [pltpu.VMEM((1, LANES), F32)]


def _dispatch_kernel(fill_ref, plen_ref, dest_hbm, x_ref, mod_ref, g_ref, xs_hbm, dest_s, fbuf, zbuf, dsem, fsem, ssem):
    i = pl.program_id(0)
    n = pl.num_programs(0)
    tn = x_ref.shape[1] // LANES

    def drain(s):
        for k in range(2):
            pltpu.make_async_copy(fbuf.at[s], xs_hbm.at[pl.ds(0, TM * tn)], ssem.at[s]).wait()

    def dest_copy(j):
        return pltpu.make_async_copy(dest_hbm.at[j], dest_s.at[pl.ds((j % 2) * DEST_ROW, DEST_ROW)], dsem.at[j % 2])

    def pad_pieces(e, fn):
        plen = plen_ref[e]
        for bit in range(ZROWS.bit_length() - 1, -1, -1):
            size = 1 << bit

            @pl.when((plen & size) != 0)
            def _():
                start = fill_ref[e] + ((plen >> (bit + 1)) << (bit + 1))
                fn(pltpu.make_async_copy(zbuf.at[pl.ds(0, size * tn)],
                                         xs_hbm.at[pl.ds(pl.multiple_of(start * tn, tn), size * tn)], fsem.at[0]))

    @pl.when(i == 0)
    def _():
        zbuf[...] = jnp.zeros_like(zbuf)
        dest_copy(0).start()

        def start_e(e, c):
            pad_pieces(e, lambda cp: cp.start())
            return c

        def wait_e(e, c):
            pad_pieces(e, lambda cp: cp.wait())
            return c

        def tail_pieces(fn):
            def piece(t, c):
                start = fill_ref[MOE_EXPERTS] + t * ZROWS
                fn(pltpu.make_async_copy(zbuf, xs_hbm.at[pl.ds(pl.multiple_of(start * tn, tn), ZROWS * tn)],
                                         fsem.at[0]))
                return c
            lax.fori_loop(0, plen_ref[MOE_EXPERTS] // ZROWS, piece, 0)

        lax.fori_loop(0, MOE_EXPERTS, start_e, 0)
        tail_pieces(lambda cp: cp.start())
        lax.fori_loop(0, MOE_EXPERTS, wait_e, 0)
        tail_pieces(lambda cp: cp.wait())

    @pl.when(i + 1 < n)
    def _():
        dest_copy(i + 1).start()

    dest_copy(i).wait()
    slot = i % 2

    @pl.when(i >= 2)
    def _():
        drain(slot)

    f = _norm_mod(x_ref[...], g_ref[...], mod_ref[3:4, :], mod_ref[4:5, :])
    _store_token_rows(fbuf.at[slot], f)

    def body(r, c):
        for k in range(2):
            row = dest_s[slot * DEST_ROW + k * TM + r]
            pltpu.make_async_copy(_token(fbuf.at[slot], r, tn), _token(xs_hbm, row, tn),
                                  ssem.at[slot]).start(priority=k)
        return c

    lax.fori_loop(0, TM, body, 0, unroll=8)

    @pl.when(i == n - 1)
    def _():
        drain(slot)

        @pl.when(n > 1)
        def _():
            drain(1 - slot)


def _dispatch(lay, fill, plen, dest, x, mod_l, g, cap):
    n_tiles = dest.shape[0]
    d = x.shape[1]
    tn = d // LANES
    grid_spec = pltpu.PrefetchScalarGridSpec(
        num_scalar_prefetch=2,
        grid=(n_tiles,),
        in_specs=[pl.BlockSpec(memory_space=pl.ANY),
                  pl.BlockSpec((TM, d), lambda i, fill, plen: (i, 0)),
                  pl.BlockSpec((None, N_MOD, d), lambda i, fill, plen: (lay.mod_row(i), 0, 0)),
                  pl.BlockSpec(g.shape, lambda i, fill, plen: (0, 0))],
        out_specs=pl.BlockSpec(memory_space=pl.ANY),
        scratch_shapes=[pltpu.SMEM((2 * DEST_ROW,), I32),
                        pltpu.VMEM((2, TM * tn, LANES), F32),
                        pltpu.VMEM((ZROWS * tn, LANES), F32),
                        pltpu.SemaphoreType.DMA((2,)),
                        pltpu.SemaphoreType.DMA((1,)),
                        pltpu.SemaphoreType.DMA((2,))],
    )
    return pl.pallas_call(
        _dispatch_kernel,
        grid_spec=grid_spec,
        out_shape=jax.ShapeDtypeStruct((cap * tn, LANES), F32),
        compiler_params=_cparams(("arbitrary",)),
        name="moe_dispatch",
    )(fill, plen, dest, x, mod_l, g)


def _expert_kernel(layer, be_ref, nv_ref, blk_ref, grp_ref, nxt_ref, x_ref, wg_hbm, wu_hbm, wd_hbm, y_ref,
                   wgf, wuf, wdf, wgb, wub, wdb, wsem):
    i = pl.program_id(0)
    live = nv_ref[i] > 0
    first = jnp.logical_or(i == 0, be_ref[i] != be_ref[jnp.maximum(i - 1, 0)])
    slot = grp_ref[i] % 2

    def fetch(e, s):
        return (pltpu.make_async_copy(wg_hbm.at[layer, e], wgf.at[s], wsem.at[s]),
                pltpu.make_async_copy(wu_hbm.at[layer, e], wuf.at[s], wsem.at[s]),
                pltpu.make_async_copy(wd_hbm.at[layer, e], wdf.at[s], wsem.at[s]))

    @pl.when(i == 0)
    def _():
        for cp in fetch(be_ref[0], 0):
            cp.start()

    @pl.when(jnp.logical_and(live, first))
    def _():
        for cp in fetch(be_ref[i], slot):
            cp.wait()

        @pl.when(nxt_ref[i] >= 0)
        def _():
            for cp in fetch(nxt_ref[i], 1 - slot):
                cp.start()

        wgb[...] = wgf[slot].astype(BF16)
        wub[...] = wuf[slot].astype(BF16)
        wdb[...] = wdf[slot].astype(BF16)

    @pl.when(live)
    def _():
        xb = _load_token_rows(x_ref, wgb.shape[0]).astype(BF16)
        hg = jnp.dot(xb, wgb[...], preferred_element_type=F32)
        hu = jnp.dot(xb, wub[...], preferred_element_type=F32)
        act = (_silu(hg) * hu).astype(BF16)
        _store_token_rows(y_ref, jnp.dot(act, wdb[...], preferred_element_type=F32))

    @pl.when(jnp.logical_not(live))
    def _():
        y_ref[...] = jnp.zeros_like(y_ref)


def _experts(layer, block_e, block_nv, block_src, block_grp, block_nxt, xs, wg, wu, wd):
    n_blocks = block_e.shape[0]
    d, dff = wg.shape[2:]
    rb = MOE_ROW_BLOCK
    blk = (rb * (d // LANES), LANES)
    grid_spec = pltpu.PrefetchScalarGridSpec(
        num_scalar_prefetch=5,
        grid=(n_blocks,),
        in_specs=[pl.BlockSpec(blk, lambda i, be, nv, src, grp, nxt: (src[i], 0)),
                  pl.BlockSpec(memory_space=pl.ANY),
                  pl.BlockSpec(memory_space=pl.ANY),
                  pl.BlockSpec(memory_space=pl.ANY)],
        out_specs=pl.BlockSpec(blk, lambda i, be, nv, src, grp, nxt: (i, 0)),
        scratch_shapes=[pltpu.VMEM((2, d, dff), F32),
                        pltpu.VMEM((2, d, dff), F32),
                        pltpu.VMEM((2, dff, d), F32),
                        pltpu.VMEM((d, dff), BF16),
                        pltpu.VMEM((d, dff), BF16),
                        pltpu.VMEM((dff, d), BF16),
                        pltpu.SemaphoreType.DMA((2,))],
    )
    return pl.pallas_call(
        functools.partial(_expert_kernel, layer),
        grid_spec=grid_spec,
        out_shape=jax.ShapeDtypeStruct(xs.shape, F32),
        compiler_params=_cparams(("arbitrary",)),
        name="moe_experts",
    )(block_e, block_nv, block_src, block_grp, block_nxt, xs, wg, wu, wd)


def _combine_kernel(dest_hbm, x_ref, r_ref, mod_ref, ys_hbm, o_ref, dest_s, gbuf, dsem, gsem):
    i = pl.program_id(0)
    n = pl.num_programs(0)
    tn = x_ref.shape[1] // LANES

    def dest_copy(j):
        return pltpu.make_async_copy(dest_hbm.at[j], dest_s.at[pl.ds((j % 3) * DEST_ROW, DEST_ROW)], dsem.at[j % 3])

    def gather_tile(j):
        def body(r, c):
            for k in range(2):
                row = dest_s[(j % 3) * DEST_ROW + k * TM + r]
                pltpu.make_async_copy(_token(ys_hbm, row, tn), _token(gbuf.at[j % 2, k], r, tn),
                                      gsem.at[j % 2]).start(priority=k)
            return c
        lax.fori_loop(0, TM, body, 0, unroll=8)

    @pl.when(i == 0)
    def _():
        dest_copy(0).start()
        dest_copy(0).wait()
        gather_tile(0)

        @pl.when(n > 1)
        def _():
            dest_copy(1).start()

    @pl.when(i + 1 < n)
    def _():
        dest_copy(i + 1).wait()

        @pl.when(i + 2 < n)
        def _():
            dest_copy(i + 2).start()

        gather_tile(i + 1)

    slot = i % 2
    for k in range(2):
        pltpu.make_async_copy(ys_hbm.at[pl.ds(0, TM * tn)], gbuf.at[slot, k], gsem.at[slot]).wait()
    w1 = r_ref[:, R_W:R_W + 1]
    w2 = r_ref[:, R_W + 1:R_W + 2]
    d = x_ref.shape[1]
    y = _load_token_rows(gbuf.at[slot, 0], d) * w1 + _load_token_rows(gbuf.at[slot, 1], d) * w2
    o_ref[...] = x_ref[...] + mod_ref[5:6, :] * y


def _combine(lay, n_tiles, dest, x, rec, mod_l, ys):
    d = x.shape[1]
    return pl.pallas_call(
        _combine_kernel,
        grid=(n_tiles,),
        in_specs=[pl.BlockSpec(memory_space=pl.ANY),
                  pl.BlockSpec((TM, d), lambda i: (i, 0)),
                  pl.BlockSpec((TM, LANES), lambda i: (i, 0)),
                  pl.BlockSpec((None, N_MOD, d), lambda i: (lay.mod_row(i), 0, 0)),
                  pl.BlockSpec(memory_space=pl.ANY)],
        out_specs=pl.BlockSpec((TM, d), lambda i: (i, 0)),
        out_shape=jax.ShapeDtypeStruct((n_tiles * TM, d), F32),
        scratch_shapes=[pltpu.SMEM((3 * DEST_ROW,), I32),
                        pltpu.VMEM((2, 2, TM * (d // LANES), LANES), F32),
                        pltpu.SemaphoreType.DMA((3,)),
                        pltpu.SemaphoreType.DMA((2,))],
        compiler_params=_cparams(("arbitrary",)),
        name="moe_combine",
    )(dest, x, rec, mod_l, ys)


def _router_weights(d, w_group, b_group, w_router, b_router):
    pad = LANES - MOE_EXPERTS - MOE_GROUPS
    wr = jnp.concatenate([w_router, w_group, jnp.zeros((d, pad), F32)], axis=1)
    br = jnp.concatenate([b_router, b_group, jnp.zeros((pad,), F32)])[None, :]
    wr_hi = wr.astype(BF16)
    wr_lo = (wr - wr_hi.astype(F32)).astype(BF16)
    return jnp.concatenate([wr_hi, wr_lo], axis=1), br


def _moe(lay, layer, n_tiles, x, mod_l, g2, rec, rec_t, cnt, wg, wu, wd):
    d = x.shape[1]
    t_tok = n_tiles * TM
    rec_t = rec_t[:, :t_tok]

    rb = MOE_ROW_BLOCK
    eid = rec_t[R_EID:R_EID + 2].astype(I32)
    rank = rec_t[R_RANK:R_RANK + 2].astype(I32)
    counts = cnt[0, :MOE_EXPERTS].astype(I32)
    padded = (counts + rb - 1) // rb * rb
    pad_ends = jnp.cumsum(padded)
    pad_starts = pad_ends - padded
    experts = jnp.arange(MOE_EXPERTS, dtype=I32)[:, None, None]
    dest = jnp.sum(jnp.where(eid[None] == experts, pad_starts[:, None, None], 0), axis=0) + rank
    dest = dest.reshape(2, n_tiles, TM).transpose(1, 0, 2).reshape(n_tiles, 2 * TM)
    dest = jnp.pad(dest, ((0, 0), (0, DEST_ROW - 2 * TM)))
    n_blocks = -(-(2 * t_tok) // rb) + MOE_EXPERTS
    cap = n_blocks * rb
    used = pad_ends[-1] // rb
    idx = jnp.arange(n_blocks, dtype=I32)
    block_src = jnp.clip(idx, 0, jnp.maximum(used - 1, 0))
    nonempty = counts > 0
    e_ids = jnp.arange(MOE_EXPERTS, dtype=I32)
    later = jnp.where((e_ids[None, :] > e_ids[:, None]) & nonempty[None, :], e_ids[None, :], MOE_EXPERTS)
    nxt_of_e = jnp.min(later, axis=1)
    nxt_of_e = jnp.where(nxt_of_e == MOE_EXPERTS, -1, nxt_of_e)
    grp_of_e = jnp.cumsum(nonempty) - nonempty
    start = idx * rb
    owner = ((pad_starts[None, :] <= start[:, None]) & (start[:, None] < pad_ends[None, :])).astype(F32)
    per_e = jnp.stack([e_ids, pad_starts + counts, grp_of_e, nxt_of_e], axis=1).astype(F32)
    per_b = jnp.dot(owner, per_e, precision=HIGHEST).astype(I32)
    block_e, block_grp, block_nxt = per_b[:, 0], per_b[:, 2], per_b[:, 3]
    block_nv = jnp.where(idx < used, jnp.clip(per_b[:, 1] - start, 0, rb), 0)

    fill = jnp.concatenate([pad_starts + counts, pad_ends[-1:]]).astype(I32)
    plen = jnp.concatenate([padded - counts, cap - pad_ends[-1:]]).astype(I32)
    xs = _dispatch(lay, fill, plen, dest, x, mod_l, g2, cap)
    ys = _experts(layer, block_e, block_nv, block_src, block_grp, block_nxt, xs, wg, wu, wd)
    return _combine(lay, n_tiles, dest, x, rec, mod_l, ys)


def _rope_tables(lay):
    half = ATT_HEAD_DIM // 2
    pos = jnp.arange(lay.seq)
    rowp = (pos // GRID_W).astype(F32)
    colp = (pos % GRID_W).astype(F32)
    freqs = ROPE_THETA ** (-jnp.arange(0, half, 2, dtype=F32) / half)
    ang = jnp.concatenate([rowp[:, None] * freqs, colp[:, None] * freqs], axis=-1)
    cos = jnp.concatenate([jnp.cos(ang), jnp.ones((lay.n_ctx, half), F32)], axis=0)
    sin = jnp.concatenate([jnp.sin(ang), jnp.zeros((lay.n_ctx, half), F32)], axis=0)
    cos_d = jnp.repeat(cos, 2, axis=1)
    sin_d = jnp.repeat(sin, 2, axis=1) * jnp.tile(jnp.asarray([-1.0, 1.0], F32), half)
    reps = LANES // ATT_HEAD_DIM
    return jnp.tile(cos_d, (1, reps)), jnp.tile(sin_d, (1, reps)), cos_d.T, sin_d.T


def kernel(x, c, ctx, c_ctx, mod_w, mod_b, norm1_g, norm2_g, ssd_w_in, ssd_conv_w, ssd_conv_b, ssd_dt_bias,
           ssd_a_log, ssd_d, ssd_norm_g, ssd_w_out, att_w_qkv, att_q_gain, att_k_gain, att_w_o, moe_w_group,
           moe_b_group, moe_w_router, moe_b_router, moe_w_gate, moe_w_up, moe_w_down):
    b, n_lat, d = x.shape
    n_ctx = ctx.shape[1]
    depth = mod_w.shape[0]
    lay = _Layout(b, n_lat, n_ctx)
    assert b + 1 <= 8

    xs = jnp.concatenate([x.reshape(lay.t_lat, d), ctx.reshape(lay.t_ctx, d)], axis=0)
    c8 = jnp.concatenate([c, c_ctx[None, :], jnp.zeros((8 - b - 1, d), F32)], axis=0)
    mods = _mod_table(c8, mod_w, mod_b).reshape(depth, 8, N_MOD, d)

    nh = ssd_dt_bias.shape[2]
    di = nh * SSD_HEAD_DIM
    gn = SSD_N_GROUPS * SSD_D_STATE
    dc = di + 2 * gn
    expand = jnp.tile(jnp.repeat(jnp.eye(nh // SSD_N_GROUPS, dtype=BF16), SSD_HEAD_DIM, axis=1), (3, 1))

    hq = att_w_qkv.shape[2] // ATT_HEAD_DIM - 2 * ATT_KV_HEADS
    dq = hq * ATT_HEAD_DIM
    dkv = ATT_KV_HEADS * ATT_HEAD_DIM
    gsum = np.zeros((dq, LANES), np.float32)
    gsum[np.arange(dq), np.arange(dq) // ATT_HEAD_DIM] = 1.0
    gexp = np.concatenate([gsum.T, gsum.T], axis=0)
    vone = np.zeros((1, ATT_KV_HEADS * LANES), np.float32)
    vone[0, np.arange(ATT_KV_HEADS) * LANES + ATT_HEAD_DIM] = 1.0
    cos128, sin128, cos_t, sin_t = _rope_tables(lay)

    for layer in range(depth):
        last = layer == depth - 1
        mod_l = mods[layer]
        j = layer // 2
        g1 = norm1_g[layer][None, :]
        g2 = norm2_g[layer][None, :]
        n_tiles = lay.lat_tiles if last else lay.n_tiles
        wr, br = _router_weights(d, moe_w_group[layer], moe_b_group[layer], moe_w_router[layer], moe_b_router[layer])
        if layer % 2 == 0:
            w_in = ssd_w_in[j].astype(BF16)
            wz, wx, wdt = w_in[:, :di], w_in[:, di:di + dc], w_in[:, di + dc:]
            dtb = ssd_dt_bias[j].reshape(1, 2 * nh)
            z, xbc, dt, dtt = _ssd_inproj(lay, xs, mod_l, g1, wz, wx, wdt, wdt.T, dtb, dtb.T)
            xc, bm, cm = _ssd_conv(lay, xbc, ssd_conv_w[j], ssd_conv_b[j][None, :], di, gn)
            yf, yb = _ssd_scan(lay, xc, bm, cm, dt, dtt, ssd_a_log[j], expand)
            dexp = jnp.repeat(ssd_d[j], SSD_HEAD_DIM)[None, :]
            xs, rec, rec_t, cnt = _ssd_outproj(lay, n_tiles, yf, yb, xc, z, dexp, ssd_norm_g[j][None, :],
                                               ssd_w_out[j].astype(BF16), xs, mod_l, g2, wr, br)
        else:
            w = att_w_qkv[j]
            wq = w[:, :dq].astype(BF16)
            wkt = w[:, dq:dq + dkv].T.astype(BF16)
            wv = w[:, dq + dkv:].reshape(d, ATT_KV_HEADS, ATT_HEAD_DIM)
            wv2 = jnp.pad(wv, ((0, 0), (0, 0), (0, LANES - ATT_HEAD_DIM))).reshape(d, ATT_KV_HEADS * LANES).astype(BF16)
            qg = (jnp.tile(att_q_gain[j], hq) * (ATT_HEAD_DIM ** -0.5 * LOG2E))[None, :]
            kgt = att_k_gain[j][:, None]
            bound = (1.01 * ATT_HEAD_DIM * jnp.max(jnp.abs(qg)) * jnp.max(jnp.abs(kgt))).reshape(1, 1)
            qn, kt, v = _att_inproj(lay, xs, mod_l, g1, wq, wkt, wv2, jnp.asarray(vone), qg, kgt, bound, jnp.asarray(gsum, BF16),
                                    jnp.asarray(gexp, BF16), cos128, sin128, cos_t, sin_t)
            o = _att_core(lay, qn, kt, v)
            xs, rec, rec_t, cnt = _att_outproj(lay, n_tiles, o, att_w_o[j].astype(BF16), xs, mod_l, g2, wr, br)
        xs = _moe(lay, layer, n_tiles, xs, mod_l, g2, rec, rec_t, cnt, moe_w_gate, moe_w_up, moe_w_down)
    return xs[:lay.t_lat].reshape(b, n_lat, d)
```

```python
import functools

import numpy as np
import jax
import jax.numpy as jnp
from jax import lax
from jax.experimental import pallas as pl
from jax.experimental.pallas import tpu as pltpu

F32 = jnp.float32
BF16 = jnp.bfloat16
I32 = jnp.int32
HIGHEST = lax.Precision.HIGHEST

NORM_EPS = 1e-6
N_MOD = 6
GRID_W = 64
ROPE_THETA = 10000.0

SSD_HEAD_DIM = 64
SSD_N_GROUPS = 4
SSD_D_STATE = 128
SSD_CONV = 5
SSD_CHUNK = 128

ATT_HEAD_DIM = 64
ATT_KV_HEADS = 4
ATT_KEY_BLOCK = 4352
ATT_KV_PER_STEP = 4
LOG2E = 1.4426950408889634
ATT_MIN_DENOM = 2.0 ** -60

MOE_GROUPS = 4
MOE_EPG = 8
MOE_EXPERTS = MOE_GROUPS * MOE_EPG
MOE_ROW_BLOCK = 256
ZROWS = MOE_ROW_BLOCK // 2
DEST_ROW = 1024

TM = 256
LANES = 128
SUBLANES = 8
HALO = 16
VMEM_LIMIT = 56 * 1024 * 1024


def _cparams(sem):
    return pltpu.CompilerParams(dimension_semantics=sem, vmem_limit_bytes=VMEM_LIMIT)


def _silu(v):
    return v / (1.0 + jnp.exp(-v))


def _softplus(v):
    return jnp.maximum(v, 0.0) + jnp.log1p(jnp.exp(-jnp.abs(v)))


def _norm_mod(x, g, shift, scale):
    ms = jnp.mean(x * x, axis=-1, keepdims=True)
    y = x * lax.rsqrt(ms + NORM_EPS) * g
    return y * (1.0 + scale) + shift


class _Layout:
    def __init__(self, batch, seq, n_ctx):
        assert seq % TM == 0 and n_ctx % TM == 0
        self.batch, self.seq, self.n_ctx = batch, seq, n_ctx
        self.t_lat = batch * seq
        self.t_ctx = batch * n_ctx
        self.t = self.t_lat + self.t_ctx
        self.lat_tiles = self.t_lat // TM
        self.seq_tiles = seq // TM
        self.ctx_tiles = n_ctx // TM
        self.n_tiles = self.t // TM
        self.nk = seq + n_ctx

    def mod_row(self, i):
        return jnp.where(i < self.lat_tiles, i // self.seq_tiles, self.batch)

    def tile_batch(self, i):
        return jnp.where(i < self.lat_tiles, i // self.seq_tiles, (i - self.lat_tiles) // self.ctx_tiles)

    def tile_pos(self, i):
        return jnp.where(i < self.lat_tiles, i % self.seq_tiles,
                         self.seq_tiles + (i - self.lat_tiles) % self.ctx_tiles)

    def seg_first(self, i):
        return jnp.where(i < self.lat_tiles, i % self.seq_tiles == 0, (i - self.lat_tiles) % self.ctx_tiles == 0)

    def seg_last(self, i):
        return jnp.where(i < self.lat_tiles, i % self.seq_tiles == self.seq_tiles - 1,
                         (i - self.lat_tiles) % self.ctx_tiles == self.ctx_tiles - 1)


def _mod_kernel(c_ref, w_ref, b_ref, o_ref):
    s = _silu(c_ref[...])
    o_ref[...] = jnp.dot(s, w_ref[...], precision=HIGHEST, preferred_element_type=F32) + b_ref[...]


def _mod_table(c8, mod_w, mod_b):
    depth, d, n = mod_w.shape
    tn = 1536
    assert n % tn == 0
    return pl.pallas_call(
        _mod_kernel,
        grid=(depth, n // tn),
        in_specs=[pl.BlockSpec((8, d), lambda l, j: (0, 0)),
                  pl.BlockSpec((None, d, tn), lambda l, j: (l, 0, j)),
                  pl.BlockSpec((None, 1, tn), lambda l, j: (l, 0, j))],
        out_specs=pl.BlockSpec((None, 8, tn), lambda l, j: (l, 0, j)),
        out_shape=jax.ShapeDtypeStruct((depth, 8, n), F32),
        compiler_params=_cparams(("parallel", "parallel")),
        name="mod_table",
    )(c8, mod_w, mod_b.reshape(depth, 1, n))


def _ssd_inproj_kernel(x_ref, mod_ref, g_ref, wz_ref, wx_ref, wdt_ref, wdtt_ref, dtb_ref, dtbt_ref,
                       z_ref, xbc_ref, dt_ref, dtt_ref):
    h = _norm_mod(x_ref[...], g_ref[...], mod_ref[0:1, :], mod_ref[1:2, :]).astype(BF16)
    z_ref[...] = jnp.dot(h, wz_ref[...], preferred_element_type=F32).astype(BF16)
    xbc_ref[...] = jnp.dot(h, wx_ref[...], preferred_element_type=F32).astype(BF16)
    nh = dtb_ref.shape[1] // 2
    dt = _softplus(jnp.dot(h, wdt_ref[...], preferred_element_type=F32) + dtb_ref[...])
    dt_ref[0] = dt[:, :nh]
    dt_ref[1] = dt[:, nh:]
    dtt = lax.dot_general(wdtt_ref[...], h, (((1,), (1,)), ((), ())), preferred_element_type=F32)
    dtt = _softplus(dtt + dtbt_ref[...])
    dtt_ref[0] = dtt[:nh, :]
    dtt_ref[1] = dtt[nh:, :]


def _ssd_inproj(lay, x, mod_l, g, wz, wx, wdt, wdtt, dtb, dtbt):
    t, d = x.shape
    di, dc, nh2 = wz.shape[1], wx.shape[1], wdt.shape[1]
    nh = nh2 // 2
    full = lambda a: pl.BlockSpec(a.shape, lambda i: (0,) * a.ndim)
    return pl.pallas_call(
        _ssd_inproj_kernel,
        grid=(lay.n_tiles,),
        in_specs=[pl.BlockSpec((TM, d), lambda i: (i, 0)),
                  pl.BlockSpec((None, N_MOD, d), lambda i: (lay.mod_row(i), 0, 0)),
                  full(g), full(wz), full(wx), full(wdt), full(wdtt), full(dtb), full(dtbt)],
        out_specs=[pl.BlockSpec((TM, di), lambda i: (i, 0)),
                   pl.BlockSpec((TM, dc), lambda i: (i, 0)),
                   pl.BlockSpec((2, TM, nh), lambda i: (0, i, 0)),
                   pl.BlockSpec((2, nh, TM), lambda i: (0, 0, i))],
        out_shape=[jax.ShapeDtypeStruct((t, di), BF16),
                   jax.ShapeDtypeStruct((t, dc), BF16),
                   jax.ShapeDtypeStruct((2, t, nh), F32),
                   jax.ShapeDtypeStruct((2, nh, t), F32)],
        compiler_params=_cparams(("parallel",)),
        name="ssd_inproj",
    )(x, mod_l, g, wz, wx, wdt, wdtt, dtb, dtbt)


def _ssd_conv_kernel(lay, di, gn, prev_ref, main_ref, next_ref, w_ref, b_ref, sh_ref, sht_ref, shb_ref,
                     xs_ref, bm_ref, cm_ref):
    i = pl.program_id(0)
    dc = main_ref.shape[1]
    half = SSD_CONV // 2
    zero_halo = jnp.zeros((HALO, dc), BF16)
    prev = jnp.where(lay.seg_first(i), zero_halo, prev_ref[...])
    nxt = jnp.where(lay.seg_last(i), zero_halo, next_ref[...])
    cw = 256
    for c0 in range(0, dc, cw):
        u = main_ref[:, c0:c0 + cw]
        acc = b_ref[:, c0:c0 + cw] + u.astype(F32) * w_ref[half:half + 1, c0:c0 + cw]
        top = jnp.zeros((SUBLANES, cw), F32)
        bot = jnp.zeros((SUBLANES, cw), F32)
        for k in range(SSD_CONV):
            if k == half:
                continue
            wk = w_ref[k:k + 1, c0:c0 + cw]
            acc = acc + jnp.dot(sh_ref[k], u, preferred_element_type=F32) * wk
            if k < half:
                top = top + jnp.dot(sht_ref[k], prev[:, c0:c0 + cw], preferred_element_type=F32)[:SUBLANES] * wk
            else:
                bot = bot + jnp.dot(shb_ref[k], nxt[:, c0:c0 + cw], preferred_element_type=F32)[:SUBLANES] * wk
        acc = jnp.concatenate([acc[:SUBLANES] + top, acc[SUBLANES:TM - SUBLANES], acc[TM - SUBLANES:] + bot], axis=0)
        y = _silu(acc).astype(BF16)
        if c0 < di:
            xs_ref[:, c0:c0 + cw] = y
        elif c0 < di + gn:
            bm_ref[:, c0 - di:c0 - di + cw] = y
        else:
            cm_ref[:, c0 - di - gn:c0 - di - gn + cw] = y


def _conv_shift_matrices():
    half = SSD_CONV // 2
    sh = np.zeros((SSD_CONV, TM, TM), np.float32)
    sht = np.zeros((SSD_CONV, HALO, HALO), np.float32)
    shb = np.zeros((SSD_CONV, HALO, HALO), np.float32)
    for k in range(SSD_CONV):
        s = k - half
        sh[k] = np.eye(TM, k=s)
        for r in range(SUBLANES):
            if r + s < 0:
                sht[k, r, HALO + r + s] = 1.0
            if r - SUBLANES + s >= 0:
                shb[k, r, r - SUBLANES + s] = 1.0
    return jnp.asarray(sh, BF16), jnp.asarray(sht, BF16), jnp.asarray(shb, BF16)


def _ssd_conv(lay, xbc, conv_w, conv_b, di, gn):
    t, dc = xbc.shape
    sh, sht, shb = _conv_shift_matrices()
    hb = TM // HALO
    nhb = t // HALO
    assert gn == 512 and di % 512 == 0
    return pl.pallas_call(
        functools.partial(_ssd_conv_kernel, lay, di, gn),
        grid=(lay.n_tiles,),
        in_specs=[pl.BlockSpec((HALO, dc), lambda i: (jnp.maximum(i * hb - 1, 0), 0)),
                  pl.BlockSpec((TM, dc), lambda i: (i, 0)),
                  pl.BlockSpec((HALO, dc), lambda i: (jnp.minimum((i + 1) * hb, nhb - 1), 0)),
                  pl.BlockSpec(conv_w.shape, lambda i: (0, 0)),
                  pl.BlockSpec(conv_b.shape, lambda i: (0, 0)),
                  pl.BlockSpec(sh.shape, lambda i: (0, 0, 0)),
                  pl.BlockSpec(sht.shape, lambda i: (0, 0, 0)),
                  pl.BlockSpec(shb.shape, lambda i: (0, 0, 0))],
        out_specs=[pl.BlockSpec((TM, di), lambda i: (i, 0)),
                   pl.BlockSpec((TM, gn), lambda i: (i, 0)),
                   pl.BlockSpec((TM, gn), lambda i: (i, 0))],
        out_shape=[jax.ShapeDtypeStruct((t, di), BF16),
                   jax.ShapeDtypeStruct((t, gn), BF16),
                   jax.ShapeDtypeStruct((t, gn), BF16)],
        compiler_params=_cparams(("parallel",)),
        name="ssd_conv",
    )(xbc, xbc, xbc, conv_w, conv_b, sh, sht, shb)


def _ssd_scan_kernel(xf_ref, bf_ref, cf_ref, dtf_ref, dttf_ref, xb_ref, bb_ref, cb_ref, dtb_ref, dttb_ref,
                     alr_ref, alc_ref, e_ref, yf_ref, yb_ref, state_ref):
    step = pl.program_id(1)

    @pl.when(step == 0)
    def _():
        state_ref[...] = jnp.zeros_like(state_ref)

    _scan_chunk(0, step, xf_ref, bf_ref, cf_ref, dtf_ref, dttf_ref, alr_ref.at[0], alc_ref.at[0], e_ref,
                yf_ref, state_ref.at[0])
    _scan_chunk(1, step, xb_ref, bb_ref, cb_ref, dtb_ref, dttb_ref, alr_ref.at[1], alc_ref.at[1], e_ref,
                yb_ref, state_ref.at[1])


def _scan_chunk(d, step, xs_ref, bm_ref, cm_ref, dt_ref, dtt_ref, alr_ref, alc_ref, e_ref, y_ref, state_ref):
    q = SSD_CHUNK
    ng = SSD_N_GROUPS
    ns = SSD_D_STATE
    gw = xs_ref.shape[1] // ng
    hpg = gw // SSD_HEAD_DIM

    row = lax.broadcasted_iota(I32, (q, q), 0)
    col = lax.broadcasted_iota(I32, (q, q), 1)
    lmask = row >= col if d == 0 else row <= col
    lmask_t = col >= row if d == 0 else col <= row

    dt = dt_ref[...]
    dtt = dtt_ref[...]
    da = dt * (-jnp.exp(alr_ref[...]))
    dat = dtt * (-jnp.exp(alc_ref[...]))
    a_cum = jnp.dot(lmask.astype(F32), da, precision=HIGHEST, preferred_element_type=F32)
    a_cum_t = jnp.dot(dat, lmask_t.astype(F32), precision=HIGHEST, preferred_element_type=F32)
    a_end = jnp.sum(da, axis=0, keepdims=True)
    w_end = jnp.exp(a_end - a_cum) * dt

    e3 = e_ref[...]

    def expand(v):
        hi = v.astype(BF16)
        r1 = v - hi.astype(F32)
        mid = r1.astype(BF16)
        lo = (r1 - mid.astype(F32)).astype(BF16)
        return jnp.dot(jnp.concatenate([hi, mid, lo], axis=1), e3, preferred_element_type=F32)

    a_end8 = jnp.broadcast_to(a_end, (SUBLANES, a_end.shape[1]))
    lane = lax.broadcasted_iota(I32, (q, LANES), 1)
    for g in range(ng):
        hs = slice(g * hpg, (g + 1) * hpg)
        decay_in = jnp.exp(expand(a_cum[:, hs]))
        chunk_decay = jnp.exp(expand(a_end8[:, hs])[0:1, :])
        xw = (xs_ref[:, g * gw:(g + 1) * gw].astype(F32) * expand(w_end[:, hs])).astype(BF16)
        bg = bm_ref[:, g * ns:(g + 1) * ns]
        cg = cm_ref[:, g * ns:(g + 1) * ns]
        cb = lax.dot_general(cg, bg, (((1,), (1,)), ((), ())), preferred_element_type=F32)
        cb = jnp.where(lmask, cb, 0.0)
        s_in = state_ref[g]
        y_off = jnp.dot(cg, s_in.astype(BF16), preferred_element_type=F32)
        new_s = lax.dot_general(bg, xw, (((0,), (0,)), ((), ())), preferred_element_type=F32)
        state_ref[g] = s_in * chunk_decay + new_s
        for pr in range(hpg // 2):
            mixes = []
            for hh in range(2):
                h = g * hpg + 2 * pr + hh
                seg = a_cum[:, h:h + 1] - a_cum_t[h:h + 1, :]
                lh = jnp.exp(jnp.minimum(seg, 0.0))
                mixes.append((cb * lh * dtt[h:h + 1, :]).astype(BF16))
            lhs = jnp.concatenate(mixes, axis=1)
            l0 = g * gw + pr * LANES
            xp = xs_ref[:, l0:l0 + LANES]
            zero = jnp.zeros_like(xp)
            rhs = jnp.concatenate([jnp.where(lane < SSD_HEAD_DIM, xp, zero),
                                   jnp.where(lane >= SSD_HEAD_DIM, xp, zero)], axis=0)
            y_pair = jnp.dot(lhs, rhs, preferred_element_type=F32)
            y_pair = y_pair + y_off[:, pr * LANES:(pr + 1) * LANES] * decay_in[:, pr * LANES:(pr + 1) * LANES]
            y_ref[:, l0:l0 + LANES] = y_pair.astype(BF16)


def _ssd_scan(lay, xs, bm, cm, dt, dtt, a_log, expand):
    t, di = xs.shape
    gn = bm.shape[1]
    nh = dt.shape[2]
    q = SSD_CHUNK
    nct, nlt = lay.n_ctx // q, lay.seq // q
    nc = nct + nlt
    ctx_base = lay.t_lat // q

    def blk(d, b, s):
        j_ctx = jnp.where(d == 0, s, nct - 1 - s)
        j_lat = jnp.where(d == 0, s - nct, nlt - 1 - (s - nct))
        return jnp.where(s < nct, ctx_base + b * nct + j_ctx, b * nlt + j_lat)

    def chunk_specs(d):
        return [pl.BlockSpec((q, di), lambda b, s: (blk(d, b, s), 0)),
                pl.BlockSpec((q, gn), lambda b, s: (blk(d, b, s), 0)),
                pl.BlockSpec((q, gn), lambda b, s: (blk(d, b, s), 0)),
                pl.BlockSpec((None, q, nh), lambda b, s: (d, blk(d, b, s), 0)),
                pl.BlockSpec((None, nh, q), lambda b, s: (d, 0, blk(d, b, s)))]

    return pl.pallas_call(
        _ssd_scan_kernel,
        grid=(lay.batch, nc),
        in_specs=chunk_specs(0) + chunk_specs(1) + [
            pl.BlockSpec((2, 1, nh), lambda b, s: (0, 0, 0)),
            pl.BlockSpec((2, nh, 1), lambda b, s: (0, 0, 0)),
            pl.BlockSpec(expand.shape, lambda b, s: (0, 0))],
        out_specs=[pl.BlockSpec((q, di), lambda b, s: (blk(0, b, s), 0)),
                   pl.BlockSpec((q, di), lambda b, s: (blk(1, b, s), 0))],
        out_shape=[jax.ShapeDtypeStruct((t, di), BF16), jax.ShapeDtypeStruct((t, di), BF16)],
        scratch_shapes=[pltpu.VMEM((2, SSD_N_GROUPS, SSD_D_STATE, di // SSD_N_GROUPS), F32)],
        compiler_params=_cparams(("arbitrary", "arbitrary")),
        name="ssd_scan",
    )(xs, bm, cm, dt, dtt, xs, bm, cm, dt, dtt, a_log.reshape(2, 1, nh), a_log.reshape(2, nh, 1), expand)


def _ssd_outproj_kernel(n_route, yf_ref, yb_ref, xs_ref, z_ref, dexp_ref, ng_ref, w_ref, x_ref, mod_ref,
                        g2_ref, wr_ref, br_ref, o_ref, r_ref, rt_ref, cnt_ref, base_ref):
    gw = xs_ref.shape[1] // SSD_N_GROUPS
    parts = []
    for g in range(SSD_N_GROUPS):
        gs = slice(g * gw, (g + 1) * gw)
        yg = yf_ref[:, gs].astype(F32) + yb_ref[:, gs].astype(F32) + xs_ref[:, gs].astype(F32) * dexp_ref[:, gs]
        yg = yg * _silu(z_ref[:, gs].astype(F32))
        ms = jnp.mean(yg * yg, axis=-1, keepdims=True)
        parts.append((yg * lax.rsqrt(ms + NORM_EPS) * ng_ref[:, gs]).astype(BF16))
    yn = jnp.concatenate(parts, axis=1)
    m = jnp.dot(yn, w_ref[...], preferred_element_type=F32)
    x_new = x_ref[...] + mod_ref[2:3, :] * m
    o_ref[...] = x_new
    _route(pl.program_id(0), n_route, x_new, mod_ref, g2_ref, wr_ref, br_ref, r_ref, rt_ref, cnt_ref, base_ref)


def _ssd_outproj(lay, n_route, yf, yb, xs, z, dexp, ng, w, x, mod_l, g2, wr, br):
    t, d = x.shape
    di = xs.shape[1]
    full = lambda a: pl.BlockSpec(a.shape, lambda i: (0,) * a.ndim)
    r_in, r_out, r_shape, r_scratch = _route_specs(lay, g2, wr, br)
    return pl.pallas_call(
        functools.partial(_ssd_outproj_kernel, n_route),
        grid=(lay.n_tiles,),
        in_specs=[pl.BlockSpec((TM, di), lambda i: (i, 0)),
                  pl.BlockSpec((TM, di), lambda i: (i, 0)),
                  pl.BlockSpec((TM, di), lambda i: (i, 0)),
                  pl.BlockSpec((TM, di), lambda i: (i, 0)),
                  full(dexp), full(ng), full(w),
                  pl.BlockSpec((TM, d), lambda i: (i, 0)),
                  pl.BlockSpec((None, N_MOD, d), lambda i: (lay.mod_row(i), 0, 0))] + r_in,
        out_specs=[pl.BlockSpec((TM, d), lambda i: (i, 0))] + r_out,
        out_shape=[jax.ShapeDtypeStruct((t, d), F32)] + r_shape,
        scratch_shapes=r_scratch,
        compiler_params=_cparams(("arbitrary",)),
        name="ssd_outproj",
    )(yf, yb, xs, z, dexp, ng, w, x, mod_l, g2, wr, br)


def _att_inproj_kernel(x_ref, mod_ref, g_ref, wq_ref, wkt_ref, wv_ref, vone_ref, qg_ref, kgt_ref, bound_ref, gsum_ref, gexp_ref,
                       cos_ref, sin_ref, cost_ref, sint_ref, q_ref, kt_ref, v_ref):
    h = _norm_mod(x_ref[...], g_ref[...], mod_ref[0:1, :], mod_ref[1:2, :]).astype(BF16)
    hd = ATT_HEAD_DIM
    q = jnp.dot(h, wq_ref[...], preferred_element_type=F32)
    ssum = jnp.dot((q * q).astype(BF16), gsum_ref[...], preferred_element_type=F32)
    r = lax.rsqrt(ssum * (1.0 / hd) + NORM_EPS)
    r_hi = r.astype(BF16)
    r_lo = (r - r_hi.astype(F32)).astype(BF16)
    r_exp = jnp.dot(jnp.concatenate([r_hi, r_lo], axis=1), gexp_ref[...], preferred_element_type=F32)
    qn = q * r_exp * qg_ref[...]
    nl = qn.shape[1]
    lane = lax.broadcasted_iota(I32, qn.shape, 1)
    partner = jnp.where((lane & 1) == 0, pltpu.roll(qn, nl - 1, 1), pltpu.roll(qn, 1, 1))
    reps = nl // LANES
    cos = jnp.concatenate([cos_ref[...]] * reps, axis=1)
    sin = jnp.concatenate([sin_ref[...]] * reps, axis=1)
    q_ref[...] = (qn * cos + partner * sin).astype(BF16)
    kt = lax.dot_general(wkt_ref[...], h, (((1,), (1,)), ((), ())), preferred_element_type=F32)
    ct, st = cost_ref[...], sint_ref[...]
    sub = lax.broadcasted_iota(I32, ct.shape, 0)
    for kh in range(ATT_KV_HEADS):
        blk = kt[kh * hd:(kh + 1) * hd]
        rk = lax.rsqrt(jnp.sum(blk * blk, axis=0, keepdims=True) * (1.0 / hd) + NORM_EPS)
        kn = blk * rk * kgt_ref[...]
        kpart = jnp.where((sub & 1) == 0, pltpu.roll(kn, hd - 1, 0), pltpu.roll(kn, 1, 0))
        kt_ref[kh, 0:hd, :] = (kn * ct + kpart * st).astype(BF16)
        kt_ref[kh, hd:2 * hd, :] = jnp.where(sub == 0, -bound_ref[...], 0.0).astype(BF16)
    v = (jnp.dot(h, wv_ref[...], preferred_element_type=F32) + vone_ref[...]).astype(BF16)
    for kh in range(ATT_KV_HEADS):
        v_ref[kh] = v[:, kh * LANES:(kh + 1) * LANES]


def _att_inproj(lay, x, mod_l, g, wq, wkt, wv2, vone, qg, kgt, bound, gsum, gexp, cos, sin, cost, sint):
    t, d = x.shape
    dq = wq.shape[1]
    hd = ATT_HEAD_DIM
    full = lambda a: pl.BlockSpec(a.shape, lambda i: (0,) * a.ndim)
    return pl.pallas_call(
        _att_inproj_kernel,
        grid=(lay.n_tiles,),
        in_specs=[pl.BlockSpec((TM, d), lambda i: (i, 0)),
                  pl.BlockSpec((None, N_MOD, d), lambda i: (lay.mod_row(i), 0, 0)),
                  full(g), full(wq), full(wkt), full(wv2), full(vone), full(qg), full(kgt), full(bound), full(gsum), full(gexp),
                  pl.BlockSpec((TM, LANES), lambda i: (lay.tile_pos(i), 0)),
                  pl.BlockSpec((TM, LANES), lambda i: (lay.tile_pos(i), 0)),
                  pl.BlockSpec((hd, TM), lambda i: (0, lay.tile_pos(i))),
                  pl.BlockSpec((hd, TM), lambda i: (0, lay.tile_pos(i)))],
        out_specs=[pl.BlockSpec((TM, dq), lambda i: (i, 0)),
                   pl.BlockSpec((None, ATT_KV_HEADS, 2 * hd, TM), lambda i: (lay.tile_batch(i), 0, 0, lay.tile_pos(i))),
                   pl.BlockSpec((None, ATT_KV_HEADS, TM, LANES), lambda i: (lay.tile_batch(i), 0, lay.tile_pos(i), 0))],
        out_shape=[jax.ShapeDtypeStruct((t, dq), BF16),
                   jax.ShapeDtypeStruct((lay.batch, ATT_KV_HEADS, 2 * hd, lay.nk), BF16),
                   jax.ShapeDtypeStruct((lay.batch, ATT_KV_HEADS, lay.nk, LANES), BF16)],
        compiler_params=_cparams(("parallel",)),
        name="att_inproj",
    )(x, mod_l, g, wq, wkt, wv2, vone, qg, kgt, bound, gsum, gexp, cos, sin, cost, sint)


def _attend(q_ref, kt_ref, v_ref, o_ref, k0, n_keys):
    hd = ATT_HEAD_DIM
    tm = q_ref.shape[0]
    heads = q_ref.shape[1] // hd
    qs = [q_ref[:, u * hd:(u + 1) * hd] for u in range(heads)]

    bk = min(ATT_KEY_BLOCK, n_keys)
    assert n_keys % bk == 0 and k0 % LANES == 0 and bk % LANES == 0

    def body(j, carry):
        start = pl.multiple_of(k0 + j * bk, LANES)
        kb = kt_ref[0:hd, pl.ds(start, bk)]
        vb = v_ref[pl.ds(start, bk), :]
        out = []
        for u in range(heads):
            m_old, acc = carry[u]
            s = jnp.dot(qs[u], kb, preferred_element_type=F32)
            m_new = jnp.maximum(m_old, jnp.max(s, axis=-1, keepdims=True))
            p = jnp.exp2((s - m_new).astype(BF16))
            acc = jnp.exp2(m_old - m_new) * acc + jnp.dot(p, vb, preferred_element_type=F32)
            out.append((m_new, acc))
        return tuple(out)

    init = tuple((jnp.full((tm, 1), -jnp.inf, F32), jnp.zeros((tm, LANES), F32)) for _ in range(heads))
    final = lax.fori_loop(0, n_keys // bk, body, init) if n_keys > bk else body(0, init)
    outs = [acc[:, :hd] / acc[:, hd:hd + 1] for _, acc in final]
    o_ref[...] = jnp.concatenate(outs, axis=1).astype(BF16)


def _attend_bounded(q_ref, kt_ref, v_ref, o_ref):
    hd = ATT_HEAD_DIM
    tm = q_ref.shape[0]
    k_lo = kt_ref[...]
    k_hi = jnp.concatenate([k_lo[hd:], k_lo[:hd]], axis=0)
    vb = v_ref[...]
    lane = lax.broadcasted_iota(I32, (tm, LANES), 1)
    outs, dens = [], []
    for pr in range(q_ref.shape[1] // LANES):
        qp = q_ref[:, pr * LANES:(pr + 1) * LANES]
        q_lo = jnp.where(lane < hd, qp, (lane == hd).astype(BF16))
        q_hi = jnp.where(lane >= hd, qp, (lane == 0).astype(BF16))
        for qa, ka in ((q_lo, k_lo), (q_hi, k_hi)):
            s = jnp.dot(qa, ka, preferred_element_type=F32)
            acc = jnp.dot(jnp.exp2(s).astype(BF16), vb, preferred_element_type=F32)
            den = acc[:, hd:hd + 1]
            outs.append(acc[:, :hd] / den)
            dens.append(den)
    o_ref[...] = jnp.concatenate(outs, axis=1).astype(BF16)
    return jnp.min(jnp.concatenate(dens, axis=1))


def _att_core_kernel(lay, q_ref, kt_ref, v_ref, o_ref):
    qi = pl.program_id(2)
    qw = q_ref.shape[1] // ATT_KV_PER_STEP
    heads = [(q_ref.at[:, j * qw:(j + 1) * qw], kt_ref.at[j], v_ref.at[j], o_ref.at[:, j * qw:(j + 1) * qw])
             for j in range(ATT_KV_PER_STEP)]

    @pl.when(qi < lay.seq_tiles)
    def _():
        smallest = [_attend_bounded(*h) for h in heads]
        for h, small in zip(heads, smallest):
            @pl.when(jnp.logical_not(small > ATT_MIN_DENOM))
            def _():
                _attend(*h, 0, lay.nk)

    @pl.when(qi >= lay.seq_tiles)
    def _():
        for h in heads:
            _attend(*h, lay.seq, lay.n_ctx)


def _att_core(lay, qn, kt, v):
    t, dq = qn.shape
    hd = ATT_HEAD_DIM
    kvs = ATT_KV_PER_STEP
    qw = dq // ATT_KV_HEADS * kvs
    per_b = lay.seq_tiles + lay.ctx_tiles

    def row_tile(b, qi):
        return jnp.where(qi < lay.seq_tiles, b * lay.seq_tiles + qi,
                         lay.lat_tiles + b * lay.ctx_tiles + (qi - lay.seq_tiles))

    return pl.pallas_call(
        functools.partial(_att_core_kernel, lay),
        grid=(lay.batch, ATT_KV_HEADS // kvs, per_b),
        in_specs=[pl.BlockSpec((TM, qw), lambda b, kh, qi: (row_tile(b, qi), kh)),
                  pl.BlockSpec((None, kvs, 2 * hd, lay.nk), lambda b, kh, qi: (b, kh, 0, 0)),
                  pl.BlockSpec((None, kvs, lay.nk, LANES), lambda b, kh, qi: (b, kh, 0, 0))],
        out_specs=pl.BlockSpec((TM, qw), lambda b, kh, qi: (row_tile(b, qi), kh)),
        out_shape=jax.ShapeDtypeStruct((t, dq), BF16),
        compiler_params=_cparams(("parallel", "parallel", "arbitrary")),
        name="att_core",
    )(qn, kt, v)


def _att_outproj_kernel(n_route, o_ref, w_ref, x_ref, mod_ref, g2_ref, wr_ref, br_ref,
                        out_ref, r_ref, rt_ref, cnt_ref, base_ref):
    m = jnp.dot(o_ref[...], w_ref[...], preferred_element_type=F32)
    x_new = x_ref[...] + mod_ref[2:3, :] * m
    out_ref[...] = x_new
    _route(pl.program_id(0), n_route, x_new, mod_ref, g2_ref, wr_ref, br_ref, r_ref, rt_ref, cnt_ref, base_ref)


def _att_outproj(lay, n_route, o, w, x, mod_l, g2, wr, br):
    t, d = x.shape
    r_in, r_out, r_shape, r_scratch = _route_specs(lay, g2, wr, br)
    return pl.pallas_call(
        functools.partial(_att_outproj_kernel, n_route),
        grid=(lay.n_tiles,),
        in_specs=[pl.BlockSpec((TM, o.shape[1]), lambda i: (i, 0)),
                  pl.BlockSpec(w.shape, lambda i: (0, 0)),
                  pl.BlockSpec((TM, d), lambda i: (i, 0)),
                  pl.BlockSpec((None, N_MOD, d), lambda i: (lay.mod_row(i), 0, 0))] + r_in,
        out_specs=[pl.BlockSpec((TM, d), lambda i: (i, 0))] + r_out,
        out_shape=[jax.ShapeDtypeStruct((t, d), F32)] + r_shape,
        scratch_shapes=r_scratch,
        compiler_params=_cparams(("arbitrary",)),
        name="att_outproj",
    )(o, w, x, mod_l, g2, wr, br)


R_EID, R_RANK, R_W = 0, 2, 4
GROUP_LANE0 = MOE_EXPERTS


def _store_token_rows(ref, val):
    rows, d = val.shape
    for s in range(d // LANES):
        ref[pl.ds(s, rows, stride=d // LANES), :] = val[:, s * LANES:(s + 1) * LANES]


def _load_token_rows(ref, d):
    n = d // LANES
    rows = ref.shape[0] // n
    return jnp.concatenate([ref[pl.ds(s, rows, stride=n), :] for s in range(n)], axis=1)


def _token(ref, r, n):
    return ref.at[pl.ds(pl.multiple_of(r * n, n), n)]


def _route(i, n_route, x, mod_ref, g_ref, wr_ref, br_ref, r_ref, rt_ref, cnt_ref, base_ref):
    @pl.when(i == 0)
    def _():
        base_ref[...] = jnp.zeros_like(base_ref)

    f = _norm_mod(x, g_ref[...], mod_ref[3:4, :], mod_ref[4:5, :])
    f_hi = f.astype(BF16)
    f_lo = (f - f_hi.astype(F32)).astype(BF16)
    part = jnp.dot(f_hi, wr_ref[...], preferred_element_type=F32)
    part_lo = jnp.dot(f_lo, wr_ref[:, :LANES], preferred_element_type=F32)
    logits = part[:, :LANES] + part[:, LANES:] + part_lo + br_ref[...]
    lane = lax.broadcasted_iota(I32, logits.shape, 1)
    neg = jnp.float32(-jnp.inf)
    big = jnp.int32(LANES)

    def first_max(vals):
        top = jnp.max(vals, axis=-1, keepdims=True)
        idx = jnp.min(jnp.where(vals == top, lane, big), axis=-1, keepdims=True)
        return top, idx

    g_mask = (lane >= GROUP_LANE0) & (lane < GROUP_LANE0 + MOE_GROUPS)
    glog = jnp.where(g_mask, logits, neg)
    g_top, g_idx = first_max(glog)
    g_w = 1.0 / jnp.sum(jnp.exp(glog - g_top), axis=-1, keepdims=True)
    e0 = (g_idx - GROUP_LANE0) * MOE_EPG
    elog = jnp.where((lane >= e0) & (lane < e0 + MOE_EPG), logits, neg)
    v1, i1 = first_max(elog)
    v2, i2 = first_max(jnp.where(lane == i1, neg, elog))
    ex = jnp.exp(v2 - v1)
    w1 = g_w / (1.0 + ex)
    w2 = g_w * ex / (1.0 + ex)

    oh1 = (lane == i1).astype(F32)
    oh2 = (lane == i2).astype(F32)
    tm = logits.shape[0]
    rr = lax.broadcasted_iota(I32, (tm, tm), 0)
    cc = lax.broadcasted_iota(I32, (tm, tm), 1)
    before = (cc < rr).astype(BF16)
    cum1 = jnp.dot(before, oh1.astype(BF16), preferred_element_type=F32)
    cum2 = jnp.dot(before, oh2.astype(BF16), preferred_element_type=F32)
    tot1 = jnp.sum(oh1, axis=0, keepdims=True)
    tot2 = jnp.sum(oh2, axis=0, keepdims=True)
    base = base_ref[...]
    rank1 = jnp.sum(oh1 * (cum1 + base), axis=-1, keepdims=True)
    rank2 = jnp.sum(oh2 * (cum2 + base + tot1), axis=-1, keepdims=True)
    base = base + jnp.where(i < n_route, tot1 + tot2, 0.0)
    base_ref[...] = base
    cnt_ref[...] = base

    rec = jnp.zeros(logits.shape, F32)
    for k, val in ((R_EID, i1.astype(F32)), (R_EID + 1, i2.astype(F32)), (R_RANK, rank1), (R_RANK + 1, rank2),
                   (R_W, w1), (R_W + 1, w2)):
        rec = jnp.where(lane == k, val, rec)
    r_ref[...] = rec
    rt_ref[...] = rec.T[:SUBLANES, :]


def _route_specs(lay, g2, wr, br):
    t = lay.n_tiles * TM
    ins = [pl.BlockSpec(g2.shape, lambda i: (0, 0)),
           pl.BlockSpec(wr.shape, lambda i: (0, 0)),
           pl.BlockSpec(br.shape, lambda i: (0, 0))]
    outs = [pl.BlockSpec((TM, LANES), lambda i: (i, 0)),
            pl.BlockSpec((SUBLANES, TM), lambda i: (0, i)),
            pl.BlockSpec((1, LANES), lambda i: (0, 0))]
    shapes = [jax.ShapeDtypeStruct((t, LANES), F32),
              jax.ShapeDtypeStruct((SUBLANES, t), F32),
              jax.ShapeDtypeStruct((1, LANES), F32)]
    return ins, outs, shapes, [pltpu.VMEM((1, LANES), F32)]


def _dispatch_kernel(fill_ref, plen_ref, dest_hbm, x_ref, mod_ref, g_ref, xs_hbm, dest_s, fbuf, zbuf, dsem, fsem, ssem):
    i = pl.program_id(0)
    n = pl.num_programs(0)
    tn = x_ref.shape[1] // LANES

    def drain(s):
        for k in range(2):
            pltpu.make_async_copy(fbuf.at[s], xs_hbm.at[pl.ds(0, TM * tn)], ssem.at[s]).wait()

    def dest_copy(j):
        return pltpu.make_async_copy(dest_hbm.at[j], dest_s.at[pl.ds((j % 2) * DEST_ROW, DEST_ROW)], dsem.at[j % 2])

    def pad_pieces(e, fn):
        plen = plen_ref[e]
        for bit in range(ZROWS.bit_length() - 1, -1, -1):
            size = 1 << bit

            @pl.when((plen & size) != 0)
            def _():
                start = fill_ref[e] + ((plen >> (bit + 1)) << (bit + 1))
                fn(pltpu.make_async_copy(zbuf.at[pl.ds(0, size * tn)],
                                         xs_hbm.at[pl.ds(pl.multiple_of(start * tn, tn), size * tn)], fsem.at[0]))

    @pl.when(i == 0)
    def _():
        zbuf[...] = jnp.zeros_like(zbuf)
        dest_copy(0).start()

        def start_e(e, c):
            pad_pieces(e, lambda cp: cp.start())
            return c

        def wait_e(e, c):
            pad_pieces(e, lambda cp: cp.wait())
            return c

        def tail_pieces(fn):
            def piece(t, c):
                start = fill_ref[MOE_EXPERTS] + t * ZROWS
                fn(pltpu.make_async_copy(zbuf, xs_hbm.at[pl.ds(pl.multiple_of(start * tn, tn), ZROWS * tn)],
                                         fsem.at[0]))
                return c
            lax.fori_loop(0, plen_ref[MOE_EXPERTS] // ZROWS, piece, 0)

        lax.fori_loop(0, MOE_EXPERTS, start_e, 0)
        tail_pieces(lambda cp: cp.start())
        lax.fori_loop(0, MOE_EXPERTS, wait_e, 0)
        tail_pieces(lambda cp: cp.wait())

    @pl.when(i + 1 < n)
    def _():
        dest_copy(i + 1).start()

    dest_copy(i).wait()
    slot = i % 2

    @pl.when(i >= 2)
    def _():
        drain(slot)

    f = _norm_mod(x_ref[...], g_ref[...], mod_ref[3:4, :], mod_ref[4:5, :])
    _store_token_rows(fbuf.at[slot], f)

    def body(r, c):
        for k in range(2):
            row = dest_s[slot * DEST_ROW + k * TM + r]
            pltpu.make_async_copy(_token(fbuf.at[slot], r, tn), _token(xs_hbm, row, tn),
                                  ssem.at[slot]).start(priority=k)
        return c

    lax.fori_loop(0, TM, body, 0, unroll=8)

    @pl.when(i == n - 1)
    def _():
        drain(slot)

        @pl.when(n > 1)
        def _():
            drain(1 - slot)


def _dispatch(lay, fill, plen, dest, x, mod_l, g, cap):
    n_tiles = dest.shape[0]
    d = x.shape[1]
    tn = d // LANES
    grid_spec = pltpu.PrefetchScalarGridSpec(
        num_scalar_prefetch=2,
        grid=(n_tiles,),
        in_specs=[pl.BlockSpec(memory_space=pl.ANY),
                  pl.BlockSpec((TM, d), lambda i, fill, plen: (i, 0)),
                  pl.BlockSpec((None, N_MOD, d), lambda i, fill, plen: (lay.mod_row(i), 0, 0)),
                  pl.BlockSpec(g.shape, lambda i, fill, plen: (0, 0))],
        out_specs=pl.BlockSpec(memory_space=pl.ANY),
        scratch_shapes=[pltpu.SMEM((2 * DEST_ROW,), I32),
                        pltpu.VMEM((2, TM * tn, LANES), F32),
                        pltpu.VMEM((ZROWS * tn, LANES), F32),
                        pltpu.SemaphoreType.DMA((2,)),
                        pltpu.SemaphoreType.DMA((1,)),
                        pltpu.SemaphoreType.DMA((2,))],
    )
    return pl.pallas_call(
        _dispatch_kernel,
        grid_spec=grid_spec,
        out_shape=jax.ShapeDtypeStruct((cap * tn, LANES), F32),
        compiler_params=_cparams(("arbitrary",)),
        name="moe_dispatch",
    )(fill, plen, dest, x, mod_l, g)


def _expert_kernel(layer, be_ref, nv_ref, blk_ref, grp_ref, nxt_ref, x_ref, wg_hbm, wu_hbm, wd_hbm, y_ref,
                   wgf, wuf, wdf, wgb, wub, wdb, wsem):
    i = pl.program_id(0)
    live = nv_ref[i] > 0
    first = jnp.logical_or(i == 0, be_ref[i] != be_ref[jnp.maximum(i - 1, 0)])
    slot = grp_ref[i] % 2

    def fetch(e, s):
        return (pltpu.make_async_copy(wg_hbm.at[layer, e], wgf.at[s], wsem.at[s]),
                pltpu.make_async_copy(wu_hbm.at[layer, e], wuf.at[s], wsem.at[s]),
                pltpu.make_async_copy(wd_hbm.at[layer, e], wdf.at[s], wsem.at[s]))

    @pl.when(i == 0)
    def _():
        for cp in fetch(be_ref[0], 0):
            cp.start()

    @pl.when(jnp.logical_and(live, first))
    def _():
        for cp in fetch(be_ref[i], slot):
            cp.wait()

        @pl.when(nxt_ref[i] >= 0)
        def _():
            for cp in fetch(nxt_ref[i], 1 - slot):
                cp.start()

        wgb[...] = wgf[slot].astype(BF16)
        wub[...] = wuf[slot].astype(BF16)
        wdb[...] = wdf[slot].astype(BF16)

    @pl.when(live)
    def _():
        xb = _load_token_rows(x_ref, wgb.shape[0]).astype(BF16)
        hg = jnp.dot(xb, wgb[...], preferred_element_type=F32)
        hu = jnp.dot(xb, wub[...], preferred_element_type=F32)
        act = (_silu(hg) * hu).astype(BF16)
        _store_token_rows(y_ref, jnp.dot(act, wdb[...], preferred_element_type=F32))

    @pl.when(jnp.logical_not(live))
    def _():
        y_ref[...] = jnp.zeros_like(y_ref)


def _experts(layer, block_e, block_nv, block_src, block_grp, block_nxt, xs, wg, wu, wd):
    n_blocks = block_e.shape[0]
    d, dff = wg.shape[2:]
    rb = MOE_ROW_BLOCK
    blk = (rb * (d // LANES), LANES)
    grid_spec = pltpu.PrefetchScalarGridSpec(
        num_scalar_prefetch=5,
        grid=(n_blocks,),
        in_specs=[pl.BlockSpec(blk, lambda i, be, nv, src, grp, nxt: (src[i], 0)),
                  pl.BlockSpec(memory_space=pl.ANY),
                  pl.BlockSpec(memory_space=pl.ANY),
                  pl.BlockSpec(memory_space=pl.ANY)],
        out_specs=pl.BlockSpec(blk, lambda i, be, nv, src, grp, nxt: (i, 0)),
        scratch_shapes=[pltpu.VMEM((2, d, dff), F32),
                        pltpu.VMEM((2, d, dff), F32),
                        pltpu.VMEM((2, dff, d), F32),
                        pltpu.VMEM((d, dff), BF16),
                        pltpu.VMEM((d, dff), BF16),
                        pltpu.VMEM((dff, d), BF16),
                        pltpu.SemaphoreType.DMA((2,))],
    )
    return pl.pallas_call(
        functools.partial(_expert_kernel, layer),
        grid_spec=grid_spec,
        out_shape=jax.ShapeDtypeStruct(xs.shape, F32),
        compiler_params=_cparams(("arbitrary",)),
        name="moe_experts",
    )(block_e, block_nv, block_src, block_grp, block_nxt, xs, wg, wu, wd)


def _combine_kernel(dest_hbm, x_ref, r_ref, mod_ref, ys_hbm, o_ref, dest_s, gbuf, dsem, gsem):
    i = pl.program_id(0)
    n = pl.num_programs(0)
    tn = x_ref.shape[1] // LANES

    def dest_copy(j):
        return pltpu.make_async_copy(dest_hbm.at[j], dest_s.at[pl.ds((j % 3) * DEST_ROW, DEST_ROW)], dsem.at[j % 3])

    def gather_tile(j):
        def body(r, c):
            for k in range(2):
                row = dest_s[(j % 3) * DEST_ROW + k * TM + r]
                pltpu.make_async_copy(_token(ys_hbm, row, tn), _token(gbuf.at[j % 2, k], r, tn),
                                      gsem.at[j % 2]).start(priority=k)
            return c
        lax.fori_loop(0, TM, body, 0, unroll=8)

    @pl.when(i == 0)
    def _():
        dest_copy(0).start()
        dest_copy(0).wait()
        gather_tile(0)

        @pl.when(n > 1)
        def _():
            dest_copy(1).start()

    @pl.when(i + 1 < n)
    def _():
        dest_copy(i + 1).wait()

        @pl.when(i + 2 < n)
        def _():
            dest_copy(i + 2).start()

        gather_tile(i + 1)

    slot = i % 2
    for k in range(2):
        pltpu.make_async_copy(ys_hbm.at[pl.ds(0, TM * tn)], gbuf.at[slot, k], gsem.at[slot]).wait()
    w1 = r_ref[:, R_W:R_W + 1]
    w2 = r_ref[:, R_W + 1:R_W + 2]
    d = x_ref.shape[1]
    y = _load_token_rows(gbuf.at[slot, 0], d) * w1 + _load_token_rows(gbuf.at[slot, 1], d) * w2
    o_ref[...] = x_ref[...] + mod_ref[5:6, :] * y


def _combine(lay, n_tiles, dest, x, rec, mod_l, ys):
    d = x.shape[1]
    return pl.pallas_call(
        _combine_kernel,
        grid=(n_tiles,),
        in_specs=[pl.BlockSpec(memory_space=pl.ANY),
                  pl.BlockSpec((TM, d), lambda i: (i, 0)),
                  pl.BlockSpec((TM, LANES), lambda i: (i, 0)),
                  pl.BlockSpec((None, N_MOD, d), lambda i: (lay.mod_row(i), 0, 0)),
                  pl.BlockSpec(memory_space=pl.ANY)],
        out_specs=pl.BlockSpec((TM, d), lambda i: (i, 0)),
        out_shape=jax.ShapeDtypeStruct((n_tiles * TM, d), F32),
        scratch_shapes=[pltpu.SMEM((3 * DEST_ROW,), I32),
                        pltpu.VMEM((2, 2, TM * (d // LANES), LANES), F32),
                        pltpu.SemaphoreType.DMA((3,)),
                        pltpu.SemaphoreType.DMA((2,))],
        compiler_params=_cparams(("arbitrary",)),
        name="moe_combine",
    )(dest, x, rec, mod_l, ys)


def _router_weights(d, w_group, b_group, w_router, b_router):
    pad = LANES - MOE_EXPERTS - MOE_GROUPS
    wr = jnp.concatenate([w_router, w_group, jnp.zeros((d, pad), F32)], axis=1)
    br = jnp.concatenate([b_router, b_group, jnp.zeros((pad,), F32)])[None, :]
    wr_hi = wr.astype(BF16)
    wr_lo = (wr - wr_hi.astype(F32)).astype(BF16)
    return jnp.concatenate([wr_hi, wr_lo], axis=1), br


def _moe(lay, layer, n_tiles, x, mod_l, g2, rec, rec_t, cnt, wg, wu, wd):
    d = x.shape[1]
    t_tok = n_tiles * TM
    rec_t = rec_t[:, :t_tok]

    rb = MOE_ROW_BLOCK
    eid = rec_t[R_EID:R_EID + 2].astype(I32)
    rank = rec_t[R_RANK:R_RANK + 2].astype(I32)
    counts = cnt[0, :MOE_EXPERTS].astype(I32)
    padded = (counts + rb - 1) // rb * rb
    pad_ends = jnp.cumsum(padded)
    pad_starts = pad_ends - padded
    experts = jnp.arange(MOE_EXPERTS, dtype=I32)[:, None, None]
    dest = jnp.sum(jnp.where(eid[None] == experts, pad_starts[:, None, None], 0), axis=0) + rank
    dest = dest.reshape(2, n_tiles, TM).transpose(1, 0, 2).reshape(n_tiles, 2 * TM)
    dest = jnp.pad(dest, ((0, 0), (0, DEST_ROW - 2 * TM)))
    n_blocks = -(-(2 * t_tok) // rb) + MOE_EXPERTS
    cap = n_blocks * rb
    used = pad_ends[-1] // rb
    idx = jnp.arange(n_blocks, dtype=I32)
    block_src = jnp.clip(idx, 0, jnp.maximum(used - 1, 0))
    nonempty = counts > 0
    e_ids = jnp.arange(MOE_EXPERTS, dtype=I32)
    later = jnp.where((e_ids[None, :] > e_ids[:, None]) & nonempty[None, :], e_ids[None, :], MOE_EXPERTS)
    nxt_of_e = jnp.min(later, axis=1)
    nxt_of_e = jnp.where(nxt_of_e == MOE_EXPERTS, -1, nxt_of_e)
    grp_of_e = jnp.cumsum(nonempty) - nonempty
    start = idx * rb
    owner = ((pad_starts[None, :] <= start[:, None]) & (start[:, None] < pad_ends[None, :])).astype(F32)
    per_e = jnp.stack([e_ids, pad_starts + counts, grp_of_e, nxt_of_e], axis=1).astype(F32)
    per_b = jnp.dot(owner, per_e, precision=HIGHEST).astype(I32)
    block_e, block_grp, block_nxt = per_b[:, 0], per_b[:, 2], per_b[:, 3]
    block_nv = jnp.where(idx < used, jnp.clip(per_b[:, 1] - start, 0, rb), 0)

    fill = jnp.concatenate([pad_starts + counts, pad_ends[-1:]]).astype(I32)
    plen = jnp.concatenate([padded - counts, cap - pad_ends[-1:]]).astype(I32)
    xs = _dispatch(lay, fill, plen, dest, x, mod_l, g2, cap)
    ys = _experts(layer, block_e, block_nv, block_src, block_grp, block_nxt, xs, wg, wu, wd)
    return _combine(lay, n_tiles, dest, x, rec, mod_l, ys)


def _rope_tables(lay):
    half = ATT_HEAD_DIM // 2
    pos = jnp.arange(lay.seq)
    rowp = (pos // GRID_W).astype(F32)
    colp = (pos % GRID_W).astype(F32)
    freqs = ROPE_THETA ** (-jnp.arange(0, half, 2, dtype=F32) / half)
    ang = jnp.concatenate([rowp[:, None] * freqs, colp[:, None] * freqs], axis=-1)
    cos = jnp.concatenate([jnp.cos(ang), jnp.ones((lay.n_ctx, half), F32)], axis=0)
    sin = jnp.concatenate([jnp.sin(ang), jnp.zeros((lay.n_ctx, half), F32)], axis=0)
    cos_d = jnp.repeat(cos, 2, axis=1)
    sin_d = jnp.repeat(sin, 2, axis=1) * jnp.tile(jnp.asarray([-1.0, 1.0], F32), half)
    reps = LANES // ATT_HEAD_DIM
    return jnp.tile(cos_d, (1, reps)), jnp.tile(sin_d, (1, reps)), cos_d.T, sin_d.T


def kernel(x, c, ctx, c_ctx, mod_w, mod_b, norm1_g, norm2_g, ssd_w_in, ssd_conv_w, ssd_conv_b, ssd_dt_bias,
           ssd_a_log, ssd_d, ssd_norm_g, ssd_w_out, att_w_qkv, att_q_gain, att_k_gain, att_w_o, moe_w_group,
           moe_b_group, moe_w_router, moe_b_router, moe_w_gate, moe_w_up, moe_w_down):
    b, n_lat, d = x.shape
    n_ctx = ctx.shape[1]
    depth = mod_w.shape[0]
    lay = _Layout(b, n_lat, n_ctx)
    assert b + 1 <= 8

    xs = jnp.concatenate([x.reshape(lay.t_lat, d), ctx.reshape(lay.t_ctx, d)], axis=0)
    c8 = jnp.concatenate([c, c_ctx[None, :], jnp.zeros((8 - b - 1, d), F32)], axis=0)
    mods = _mod_table(c8, mod_w, mod_b).reshape(depth, 8, N_MOD, d)

    nh = ssd_dt_bias.shape[2]
    di = nh * SSD_HEAD_DIM
    gn = SSD_N_GROUPS * SSD_D_STATE
    dc = di + 2 * gn
    expand = jnp.tile(jnp.repeat(jnp.eye(nh // SSD_N_GROUPS, dtype=BF16), SSD_HEAD_DIM, axis=1), (3, 1))

    hq = att_w_qkv.shape[2] // ATT_HEAD_DIM - 2 * ATT_KV_HEADS
    dq = hq * ATT_HEAD_DIM
    dkv = ATT_KV_HEADS * ATT_HEAD_DIM
    gsum = np.zeros((dq, LANES), np.float32)
    gsum[np.arange(dq), np.arange(dq) // ATT_HEAD_DIM] = 1.0
    gexp = np.concatenate([gsum.T, gsum.T], axis=0)
    vone = np.zeros((1, ATT_KV_HEADS * LANES), np.float32)
    vone[0, np.arange(ATT_KV_HEADS) * LANES + ATT_HEAD_DIM] = 1.0
    cos128, sin128, cos_t, sin_t = _rope_tables(lay)

    for layer in range(depth):
        last = layer == depth - 1
        mod_l = mods[layer]
        j = layer // 2
        g1 = norm1_g[layer][None, :]
        g2 = norm2_g[layer][None, :]
        n_tiles = lay.lat_tiles if last else lay.n_tiles
        wr, br = _router_weights(d, moe_w_group[layer], moe_b_group[layer], moe_w_router[layer], moe_b_router[layer])
        if layer % 2 == 0:
            w_in = ssd_w_in[j].astype(BF16)
            wz, wx, wdt = w_in[:, :di], w_in[:, di:di + dc], w_in[:, di + dc:]
            dtb = ssd_dt_bias[j].reshape(1, 2 * nh)
            z, xbc, dt, dtt = _ssd_inproj(lay, xs, mod_l, g1, wz, wx, wdt, wdt.T, dtb, dtb.T)
            xc, bm, cm = _ssd_conv(lay, xbc, ssd_conv_w[j], ssd_conv_b[j][None, :], di, gn)
            yf, yb = _ssd_scan(lay, xc, bm, cm, dt, dtt, ssd_a_log[j], expand)
            dexp = jnp.repeat(ssd_d[j], SSD_HEAD_DIM)[None, :]
            xs, rec, rec_t, cnt = _ssd_outproj(lay, n_tiles, yf, yb, xc, z, dexp, ssd_norm_g[j][None, :],
                                               ssd_w_out[j].astype(BF16), xs, mod_l, g2, wr, br)
        else:
            w = att_w_qkv[j]
            wq = w[:, :dq].astype(BF16)
            wkt = w[:, dq:dq + dkv].T.astype(BF16)
            wv = w[:, dq + dkv:].reshape(d, ATT_KV_HEADS, ATT_HEAD_DIM)
            wv2 = jnp.pad(wv, ((0, 0), (0, 0), (0, LANES - ATT_HEAD_DIM))).reshape(d, ATT_KV_HEADS * LANES).astype(BF16)
            qg = (jnp.tile(att_q_gain[j], hq) * (ATT_HEAD_DIM ** -0.5 * LOG2E))[None, :]
            kgt = att_k_gain[j][:, None]
            bound = (1.01 * ATT_HEAD_DIM * jnp.max(jnp.abs(qg)) * jnp.max(jnp.abs(kgt))).reshape(1, 1)
            qn, kt, v = _att_inproj(lay, xs, mod_l, g1, wq, wkt, wv2, jnp.asarray(vone), qg, kgt, bound, jnp.asarray(gsum, BF16),
                                    jnp.asarray(gexp, BF16), cos128, sin128, cos_t, sin_t)
            o = _att_core(lay, qn, kt, v)
            xs, rec, rec_t, cnt = _att_outproj(lay, n_tiles, o, att_w_o[j].astype(BF16), xs, mod_l, g2, wr, br)
        xs = _moe(lay, layer, n_tiles, xs, mod_l, g2, rec, rec_t, cnt, moe_w_gate, moe_w_up, moe_w_down)
    return xs[:lay.t_lat].reshape(b, n_lat, d)
```
